```python
import math
import jax, jax.numpy as jnp
from jax import lax
import numpy as np

D_MODEL = 1024
BATCH = 32
SEQ = 2048
DEPTH = 4

RET_HEADS = 4
RET_QK_DIM = 32
RET_V_DIM = 64
RET_CHUNK = 128
RET_QK_WIDTH = RET_HEADS * RET_QK_DIM
RET_V_WIDTH = RET_HEADS * RET_V_DIM
ROPE_BASE = 10000.0
GMLP_GROUPS = 4
GMLP_GROUP_CH = 64
GMLP_WIDTH = GMLP_GROUPS * GMLP_GROUP_CH
GMLP_CHUNK = 128
CONV_WIDTH = 256
CONV_K = 3
N_BRANCH = 3
BRANCH_WIDTH = 256
GATE_RANK = 128
IN_SPLITS = [RET_QK_WIDTH, RET_QK_WIDTH, RET_V_WIDTH, RET_V_WIDTH,
             GMLP_WIDTH, GMLP_WIDTH,
             CONV_WIDTH, CONV_WIDTH, CONV_WIDTH,
             N_BRANCH * GATE_RANK]
IN_WIDTH = sum(IN_SPLITS)
N_EXPERTS = 32
TOP_K = 4
D_FF = 256
SWIGLU_LIMIT = 7.0
SWIGLU_ALPHA = 1.702
MOE_BLOCK = 512
DEEPNORM_ALPHA = (2.0 * DEPTH) ** 0.25
DEEPNORM_BETA = (8.0 * DEPTH) ** -0.25
LN_EPS = 1e-5

kernel_name = "hybrid_retention_gmlp_shortconv_moe_deepnorm"


def _layernorm(x, g, b):
    xf = x.astype(jnp.float32)
    mu = jnp.mean(xf, axis=-1, keepdims=True)
    var = jnp.mean(jnp.square(xf - mu), axis=-1, keepdims=True)
    y = (xf - mu) * lax.rsqrt(var + LN_EPS) * g.astype(jnp.float32) + b.astype(jnp.float32)
    return y.astype(x.dtype)


def _rotary(t, positions):
    dh = t.shape[-1]
    inv_freq = ROPE_BASE ** (-jnp.arange(0, dh, 2, dtype=jnp.float32) / dh)
    ang = positions.astype(jnp.float32)[..., None] * inv_freq
    cos = jnp.cos(ang)[:, :, None, :]
    sin = jnp.sin(ang)[:, :, None, :]
    t1, t2 = jnp.split(t, 2, axis=-1)
    return jnp.concatenate([t1 * cos - t2 * sin, t2 * cos + t1 * sin], axis=-1)


def _retention(q, k, v, g, positions):
    out_dtype = v.dtype
    bsz, seq, _ = q.shape
    n_chunks = seq // RET_CHUNK
    f32 = jnp.float32
    q = _rotary(q.astype(f32).reshape(bsz, seq, RET_HEADS, RET_QK_DIM), positions)
    k = _rotary(k.astype(f32).reshape(bsz, seq, RET_HEADS, RET_QK_DIM), positions) * (RET_QK_DIM ** -0.5)
    v = v.astype(f32).reshape(bsz, seq, RET_HEADS, RET_V_DIM)
    qc = q.reshape(bsz, n_chunks, RET_CHUNK, RET_HEADS, RET_QK_DIM)
    kc = k.reshape(bsz, n_chunks, RET_CHUNK, RET_HEADS, RET_QK_DIM)
    vc = v.reshape(bsz, n_chunks, RET_CHUNK, RET_HEADS, RET_V_DIM)

    log_gamma = jnp.log(1.0 - 2.0 ** (-5.0 - jnp.arange(RET_HEADS, dtype=f32)))
    idx = jnp.arange(RET_CHUNK, dtype=f32)
    diff = idx[:, None] - idx[None, :]
    decay_mask = jnp.where(diff[None] >= 0,
                           jnp.exp(log_gamma[:, None, None] * jnp.maximum(diff, 0.0)[None]),
                           0.0)
    scores = jnp.einsum('bnchk,bnmhk->bnhcm', qc, kc) * decay_mask[None, None]
    o_intra = jnp.einsum('bnhcm,bnmhv->bnchv', scores, vc)
    zeta = jnp.exp(log_gamma[:, None] * (RET_CHUNK - 1.0 - idx)[None])
    kv = jnp.einsum('bnmhk,bnmhv,hm->bnhkv', kc, vc, zeta)
    chunk_decay = jnp.exp(log_gamma * RET_CHUNK)[None, :, None, None]
    kv_t = jnp.moveaxis(kv, 1, 0)

    def step(state, kv_n):
        return state * chunk_decay + kv_n, state

    _, r_prev = lax.scan(step, jnp.zeros_like(kv_t[0]), kv_t)
    r_prev = jnp.moveaxis(r_prev, 0, 1)
    xi = jnp.exp(log_gamma[:, None] * (idx + 1.0)[None]).T
    o_cross = jnp.einsum('bnchk,bnhkv->bnchv', qc, r_prev) * xi[None, None, :, :, None]
    o = (o_intra + o_cross).reshape(bsz, seq, RET_HEADS, RET_V_DIM)
    mu = jnp.mean(o, axis=-1, keepdims=True)
    var = jnp.mean(jnp.square(o - mu), axis=-1, keepdims=True)
    o = ((o - mu) * lax.rsqrt(var + LN_EPS)).reshape(bsz, seq, RET_V_WIDTH)
    return (jax.nn.silu(g.astype(f32)) * o).astype(out_dtype)


def _gmlp_spatial(zu, zv, ln_g, ln_b, w_spatial, b_spatial):
    bsz, seq, _ = zu.shape
    n_chunks = seq // GMLP_CHUNK
    u = jax.nn.gelu(zu)
    v = _layernorm(jax.nn.gelu(zv), ln_g, ln_b)
    vc = v.reshape(bsz, n_chunks, GMLP_CHUNK, GMLP_GROUPS, GMLP_GROUP_CH)
    causal = jnp.tril(jnp.ones((GMLP_CHUNK, GMLP_CHUNK), dtype=w_spatial.dtype))
    w_masked = w_spatial * causal[None]
    z = jnp.einsum('gcm,bnmgx->bncgx', w_masked, vc) + b_spatial.T[None, None, :, :, None]
    return u * z.reshape(bsz, seq, GMLP_WIDTH).astype(u.dtype)


def _short_conv(gate_b, gate_c, xin, conv_w):
    z = gate_c * xin
    rhs = conv_w.astype(z.dtype)[:, None, :]
    conv = lax.conv_general_dilated(z, rhs, window_strides=(1,), padding=[(CONV_K - 1, 0)],
                                    dimension_numbers=('NWC', 'WIO', 'NWC'),
                                    feature_group_count=CONV_WIDTH)
    return gate_b * conv


def _hybrid_mixer(h, positions, w_in, gmlp_ln_g, gmlp_ln_b, w_spatial, b_spatial,
                  conv_w, w_branch, w_gate_up, b_gate, w_out):
    bsz, seq, _ = h.shape
    proj = h @ w_in
    offs = [int(o) for o in np.cumsum(IN_SPLITS)[:-1]]
    (rq, rk, rv, rg, gu, gv, cb, cc, cx, gate_code) = jnp.split(proj, offs, axis=-1)
    r = _retention(rq, rk, rv, rg, positions)
    s = _gmlp_spatial(gu, gv, gmlp_ln_g, gmlp_ln_b, w_spatial, b_spatial)
    k = _short_conv(cb, cc, cx, conv_w)
    branches = jnp.stack([r, s.astype(r.dtype), k.astype(r.dtype)], axis=2)
    y = jnp.einsum('bsnw,nwd->bsnd', branches, w_branch)
    gate_code = gate_code.reshape(bsz, seq, N_BRANCH, GATE_RANK)
    g = jax.nn.sigmoid(jnp.einsum('bsnr,nrd->bsnd', gate_code, w_gate_up) + b_gate)
    merged = jnp.sum(g * y, axis=2)
    return merged @ w_out


def _moe(h, w_router, b_router, w_gu, b_gu, w_down, b_down):
    bsz, seq, d = h.shape
    n_tok = bsz * seq
    n_asg = n_tok * TOP_K
    n_blocks = n_asg // MOE_BLOCK + N_EXPERTS
    n_rows = n_blocks * MOE_BLOCK
    ht = h.reshape(n_tok, d)
    logits = (ht @ w_router + b_router).astype(jnp.float32)
    top_v, top_i = lax.top_k(logits, TOP_K)
    weights = jax.nn.softmax(top_v, axis=-1)
    flat_e = top_i.reshape(-1).astype(jnp.int32)
    order = jnp.argsort(flat_e)
    e_sorted = flat_e[order]
    tok_sorted = (order // TOP_K).astype(jnp.int32)
    w_sorted = weights.reshape(-1)[order]
    sizes = jnp.bincount(flat_e, length=N_EXPERTS).astype(jnp.int32)
    start = jnp.cumsum(sizes) - sizes
    pad_sizes = (sizes + MOE_BLOCK - 1) // MOE_BLOCK * MOE_BLOCK
    pad_end = jnp.cumsum(pad_sizes)
    pad_start = pad_end - pad_sizes
    dest = pad_start[e_sorted] + jnp.arange(n_asg, dtype=jnp.int32) - start[e_sorted]
    row_tok = jnp.full((n_rows,), n_tok, dtype=jnp.int32).at[dest].set(tok_sorted)
    row_w = jnp.zeros((n_rows,), jnp.float32).at[dest].set(w_sorted)
    block_start = jnp.arange(n_blocks, dtype=jnp.int32) * MOE_BLOCK
    block_e = jnp.minimum(jnp.searchsorted(pad_end, block_start, side='right'),
                          N_EXPERTS - 1).astype(jnp.int32)
    h_pad = jnp.concatenate([ht, jnp.zeros((1, d), ht.dtype)], axis=0)
    xs = h_pad[row_tok].reshape(n_blocks, MOE_BLOCK, d).astype(w_gu.dtype)
    gu = jnp.einsum('nmd,ndf->nmf', xs, w_gu[block_e]) + b_gu[block_e][:, None, :]
    gate = jnp.minimum(gu[..., :D_FF], SWIGLU_LIMIT)
    up = jnp.clip(gu[..., D_FF:], -SWIGLU_LIMIT, SWIGLU_LIMIT)
    act = (up + 1.0) * (gate * jax.nn.sigmoid(SWIGLU_ALPHA * gate))
    out = jnp.einsum('nmf,nfd->nmd', act.astype(w_down.dtype), w_down[block_e]) + b_down[block_e][:, None, :]
    out = out.reshape(n_rows, d).astype(jnp.float32) * row_w[:, None]
    y = jax.ops.segment_sum(out, row_tok, num_segments=n_tok + 1)[:n_tok]
    return y.reshape(bsz, seq, d).astype(h.dtype)


def setup_inputs(seed: int = 0) -> dict:
    key = jax.random.key(seed)
    ks = jax.random.split(key, 26)
    f32 = jnp.float32
    L, D, E, F = DEPTH, D_MODEL, N_EXPERTS, D_FF
    nrm = lambda k, shape, s: jax.random.normal(k, shape, f32) * s
    return {
        "x": nrm(ks[0], (BATCH, SEQ, D), 1.0),
        "c": nrm(ks[1], (BATCH, D), 1.0),
        "positions": jnp.broadcast_to(jnp.arange(SEQ, dtype=jnp.int32), (BATCH, SEQ)),
        "w_ada": nrm(ks[2], (L, D, 6 * D), 0.5 * D ** -0.5),
        "b_ada": nrm(ks[3], (L, 6 * D), 0.01),
        "w_in": nrm(ks[4], (L, D, IN_WIDTH), D ** -0.5),
        "gmlp_ln_g": 1.0 + nrm(ks[5], (L, GMLP_WIDTH), 0.01),
        "gmlp_ln_b": nrm(ks[6], (L, GMLP_WIDTH), 0.01),
        "w_spatial": nrm(ks[7], (L, GMLP_GROUPS, GMLP_CHUNK, GMLP_CHUNK), GMLP_CHUNK ** -0.5),
        "b_spatial": 1.0 + nrm(ks[8], (L, GMLP_GROUPS, GMLP_CHUNK), 0.01),
        "conv_w": nrm(ks[9], (L, CONV_K, CONV_WIDTH), CONV_K ** -0.5),
        "w_branch": nrm(ks[10], (L, N_BRANCH, BRANCH_WIDTH, D), DEEPNORM_BETA * BRANCH_WIDTH ** -0.5),
        "w_gate_up": nrm(ks[11], (L, N_BRANCH, GATE_RANK, D), GATE_RANK ** -0.5),
        "b_gate": nrm(ks[12], (L, N_BRANCH, D), 0.01),
        "w_out": nrm(ks[13], (L, D, D), DEEPNORM_BETA * D ** -0.5),
        "ln1_g": 1.0 + nrm(ks[14], (L, D), 0.01),
        "ln1_b": nrm(ks[15], (L, D), 0.01),
        "w_router": nrm(ks[16], (L, D, E), D ** -0.5),
        "b_router": nrm(ks[17], (L, E), 0.01),
        "w_gu": nrm(ks[18], (L, E, D, 2 * F), D ** -0.5),
        "b_gu": nrm(ks[19], (L, E, 2 * F), 0.01),
        "w_down": nrm(ks[20], (L, E, F, D), DEEPNORM_BETA * F ** -0.5),
        "b_down": nrm(ks[21], (L, E, D), 0.01),
        "ln2_g": 1.0 + nrm(ks[22], (L, D), 0.01),
        "ln2_b": nrm(ks[23], (L, D), 0.01),
    }


def reference(x, c, positions, w_ada, b_ada, w_in, gmlp_ln_g, gmlp_ln_b, w_spatial, b_spatial,
              conv_w, w_branch, w_gate_up, b_gate, w_out, ln1_g, ln1_b, w_router, b_router,
              w_gu, b_gu, w_down, b_down, ln2_g, ln2_b):
    out_dtype = x.dtype
    c_act = jax.nn.silu(c)
    for l in range(DEPTH):
        ada = c_act @ w_ada[l] + b_ada[l]
        sh1, sc1, gt1, sh2, sc2, gt2 = [a[:, None, :] for a in jnp.split(ada, 6, axis=-1)]
        h = x * (1.0 + sc1) + sh1
        mix = _hybrid_mixer(h, positions, w_in[l], gmlp_ln_g[l], gmlp_ln_b[l], w_spatial[l],
                            b_spatial[l], conv_w[l], w_branch[l], w_gate_up[l], b_gate[l], w_out[l])
        x = _layernorm(DEEPNORM_ALPHA * x + (1.0 + gt1) * mix, ln1_g[l], ln1_b[l])
        h2 = x * (1.0 + sc2) + sh2
        ffn = _moe(h2, w_router[l], b_router[l], w_gu[l], b_gu[l], w_down[l], b_down[l])
        x = _layernorm(DEEPNORM_ALPHA * x + (1.0 + gt2) * ffn, ln2_g[l], ln2_b[l])
    return x.astype(out_dtype)
```

```python
import functools

import jax
import jax.numpy as jnp
from jax import lax
from jax.experimental import pallas as pl
from jax.experimental.pallas import tpu as pltpu

F32 = jnp.float32
MXU_DTYPE = jnp.bfloat16

DEPTH = 4
RET_HEADS = 4
RET_QK_DIM = 32
RET_V_DIM = 64
RET_QK_WIDTH = RET_HEADS * RET_QK_DIM
RET_V_WIDTH = RET_HEADS * RET_V_DIM
CHUNK = 128
ROPE_BASE = 10000.0
GMLP_GROUPS = 4
GMLP_WIDTH = 256
CONV_WIDTH = 256
GATE_RANK = 128
N_BRANCH = 3
N_EXPERTS = 32
TOP_K = 4
D_FF = 256
SWIGLU_LIMIT = 7.0
SWIGLU_ALPHA = 1.702
DEEPNORM_ALPHA = (2.0 * DEPTH) ** 0.25
LN_EPS = 1e-5

_O_QK = 0
_O_VG = 2 * RET_QK_WIDTH
_O_GMLP = _O_VG + 2 * RET_V_WIDTH
_O_CONV = _O_GMLP + 2 * GMLP_WIDTH
_O_CODE = _O_CONV + 3 * CONV_WIDTH
_IN_WIDTH = _O_CODE + N_BRANCH * GATE_RANK

LANES = 128
NEG_BIG = -1e30

SEQ_TILE = 256
MOE_BLOCK = 512
COMBINE_TILE = 128
VMEM_LIMIT = 56 * 1024 * 1024


def _dot(a, b):
    return jnp.dot(a.astype(MXU_DTYPE), b.astype(MXU_DTYPE), preferred_element_type=F32)


def _dot_nt(a, b):
    return lax.dot_general(a.astype(MXU_DTYPE), b.astype(MXU_DTYPE),
                           (((1,), (1,)), ((), ())), preferred_element_type=F32)


def _split_dot(x, w):
    hi = x.astype(MXU_DTYPE)
    lo = (x - hi.astype(F32)).astype(MXU_DTYPE)
    return (jnp.dot(hi, w, preferred_element_type=F32)
            + jnp.dot(lo, w, preferred_element_type=F32))


def _layernorm_rows(x, g, b):
    mu = jnp.mean(x, axis=-1, keepdims=True)
    d = x - mu
    var = jnp.mean(d * d, axis=-1, keepdims=True)
    return d * lax.rsqrt(var + LN_EPS) * g + b


def _ada_kernel(c_ref, w_ref, b_ref, o_ref):
    c_act = jax.nn.silu(c_ref[...])
    o_ref[...] = jnp.dot(c_act, w_ref[...], preferred_element_type=F32,
                         precision=lax.Precision.HIGHEST) + b_ref[...]


def _ada_call(c, w_ada, b_ada):
    depth, d, six_d = w_ada.shape
    bsz = c.shape[0]
    n_col = six_d // d
    return pl.pallas_call(
        _ada_kernel,
        grid=(depth, n_col),
        in_specs=[
            pl.BlockSpec((bsz, d), lambda l, j: (0, 0)),
            pl.BlockSpec((None, d, d), lambda l, j: (l, 0, j)),
            pl.BlockSpec((None, 1, d), lambda l, j: (l, 0, j)),
        ],
        out_specs=pl.BlockSpec((None, None, bsz, d), lambda l, j: (l, j, 0, 0)),
        out_shape=jax.ShapeDtypeStruct((depth, n_col, bsz, d), F32),
        compiler_params=pltpu.CompilerParams(
            dimension_semantics=("arbitrary", "arbitrary"), vmem_limit_bytes=VMEM_LIMIT),
        name="ada_ln",
    )(c, w_ada, b_ada.reshape(depth, 1, six_d))


def _mixer_kernel(x_ref, ada_ref, cos_ref, sin_ref, w_in_ref, lng_ref, lnb_ref, wsp_ref, bsp_ref,
                  convw_ref, wbr_ref, wgt_ref, bgt_ref, wout_ref, ln1g_ref, ln1b_ref, wr_ref, br_ref,
                  decay_ref, zeta_ref, xi_ref, cdec_ref, bmask_ref, gavg_ref,
                  x1_ref, h2_ref, sel_ref, cnt_ref,
                  state_ref, zc_ref, cntacc_ref):
    b = pl.program_id(0)
    s = pl.program_id(1)
    ts = x_ref.shape[0]

    @pl.when(s == 0)
    def _():
        state_ref[...] = jnp.zeros_like(state_ref)
        zc_ref[...] = jnp.zeros_like(zc_ref)

    @pl.when(jnp.logical_and(b == 0, s == 0))
    def _():
        cntacc_ref[...] = jnp.zeros_like(cntacc_ref)

    x = x_ref[...]
    ada = ada_ref[...]
    sh1, sc1, gt1, sh2, sc2 = ada[0:1], ada[1:2], ada[2:3], ada[3:4], ada[4:5]
    h = (x * (1.0 + sc1) + sh1).astype(MXU_DTYPE)

    def proj(lo, hi):
        return jnp.dot(h, w_in_ref[:, lo:hi], preferred_element_type=F32)

    lane_qk = lax.broadcasted_iota(jnp.int32, (1, RET_QK_WIDTH), 1)
    lane_v = lax.broadcasted_iota(jnp.int32, (1, RET_V_WIDTH), 1)
    qk_masks = [lane_qk // RET_QK_DIM == hd for hd in range(RET_HEADS)]
    v_masks = [lane_v // RET_V_DIM == hd for hd in range(RET_HEADS)]

    qk = proj(_O_QK, _O_VG)
    cos = cos_ref[...]
    sin = sin_ref[...]
    first_half = (lane_qk % RET_QK_DIM) < (RET_QK_DIM // 2)

    def rotary(t):
        swapped = jnp.where(first_half, pltpu.roll(t, RET_QK_WIDTH - RET_QK_DIM // 2, 1),
                            pltpu.roll(t, RET_QK_DIM // 2, 1))
        return t * cos + swapped * sin

    q = rotary(qk[:, :RET_QK_WIDTH])
    k = rotary(qk[:, RET_QK_WIDTH:]) * (RET_QK_DIM ** -0.5)
    vg = proj(_O_VG, _O_GMLP)
    v = vg[:, :RET_V_WIDTH]
    g = vg[:, RET_V_WIDTH:]

    decay = decay_ref[...]
    zeta = zeta_ref[...]
    xi = xi_ref[...]
    cdec = cdec_ref[...]
    bmask = bmask_ref[...]
    gavg = gavg_ref[...]

    o_chunks = []
    state = state_ref[...]
    for c in range(ts // CHUNK):
        rows = slice(c * CHUNK, (c + 1) * CHUNK)
        q_c, k_c, v_c = q[rows], k[rows], v[rows]
        v_m = v_c.astype(MXU_DTYPE)
        q_all = jnp.concatenate([jnp.where(m, q_c, 0.0) for m in qk_masks], axis=0)
        scores = _dot_nt(q_all, k_c) * decay
        o_all = _dot(scores, v_m)
        o = _dot(q_c, state) * xi
        for hd in range(RET_HEADS):
            o = o + jnp.where(v_masks[hd], o_all[hd * CHUNK:(hd + 1) * CHUNK], 0.0)
        kv = _dot((k_c * zeta).T, v_m) * bmask
        state = state * cdec + kv
        o_chunks.append(o)
    state_ref[...] = state
    o = jnp.concatenate(o_chunks, axis=0) if len(o_chunks) > 1 else o_chunks[0]
    mu = _split_dot(o, gavg)
    d = o - mu
    var = _split_dot(d * d, gavg)
    r_br = jax.nn.silu(g) * (d * lax.rsqrt(var + LN_EPS))

    guv = proj(_O_GMLP, _O_CONV)
    u = jax.nn.gelu(guv[:, :GMLP_WIDTH])
    vv = _layernorm_rows(jax.nn.gelu(guv[:, GMLP_WIDTH:]), lng_ref[...], lnb_ref[...]).astype(MXU_DTYPE)
    rr = lax.broadcasted_iota(jnp.int32, (GMLP_GROUPS * CHUNK, CHUNK), 0) % CHUNK
    cc_ = lax.broadcasted_iota(jnp.int32, (GMLP_GROUPS * CHUNK, CHUNK), 1)
    w_sp = jnp.where(cc_ <= rr, wsp_ref[...], 0.0).astype(MXU_DTYPE)
    bsp = bsp_ref[...]
    z_chunks = []
    for c in range(ts // CHUNK):
        z_all = jnp.dot(w_sp, vv[c * CHUNK:(c + 1) * CHUNK], preferred_element_type=F32)
        z = bsp
        for gi in range(GMLP_GROUPS):
            z = z + jnp.where(v_masks[gi], z_all[gi * CHUNK:(gi + 1) * CHUNK], 0.0)
        z_chunks.append(z)
    z = jnp.concatenate(z_chunks, axis=0) if len(z_chunks) > 1 else z_chunks[0]
    s_br = u * z

    cbcx = proj(_O_CONV, _O_CODE)
    gate_b = cbcx[:, :CONV_WIDTH]
    zc = cbcx[:, CONV_WIDTH:2 * CONV_WIDTH] * cbcx[:, 2 * CONV_WIDTH:]
    carry = zc_ref[...]
    prev1 = carry[7:8]
    prev2 = carry[6:7]
    row = lax.broadcasted_iota(jnp.int32, (ts, 1), 0)
    z1 = jnp.where(row == 0, prev1, pltpu.roll(zc, 1, 0))
    z2 = jnp.where(row == 0, prev2, jnp.where(row == 1, prev1, pltpu.roll(zc, 2, 0)))
    zc_ref[...] = zc[ts - 8:ts]
    cw = convw_ref[...]
    k_br = gate_b * (cw[0:1] * z2 + cw[1:2] * z1 + cw[2:3] * zc)

    code = proj(_O_CODE, _IN_WIDTH)
    merged = None
    for i, br in enumerate((r_br, s_br, k_br)):
        y = jnp.dot(br.astype(MXU_DTYPE), wbr_ref[i], preferred_element_type=F32)
        gl = jnp.dot(code[:, i * GATE_RANK:(i + 1) * GATE_RANK].astype(MXU_DTYPE), wgt_ref[i],
                     preferred_element_type=F32) + bgt_ref[i:i + 1]
        t = jax.nn.sigmoid(gl) * y
        merged = t if merged is None else merged + t
    mix = jnp.dot(merged.astype(MXU_DTYPE), wout_ref[...], preferred_element_type=F32)
    x1 = _layernorm_rows(DEEPNORM_ALPHA * x + (1.0 + gt1) * mix, ln1g_ref[...], ln1b_ref[...])
    x1_ref[...] = x1
    h2 = x1 * (1.0 + sc2) + sh2
    h2_ref[...] = h2

    logits = jnp.dot(h2.astype(MXU_DTYPE), wr_ref[...], preferred_element_type=F32) + br_ref[...]
    lane = lax.broadcasted_iota(jnp.int32, (1, LANES), 1).astype(F32)
    top_v, top_i = [], []
    work = logits
    for _ in range(TOP_K):
        m = jnp.max(work, axis=-1, keepdims=True)
        i_sel = jnp.min(jnp.where(work == m, lane, float(LANES)), axis=-1, keepdims=True)
        top_v.append(m)
        top_i.append(i_sel)
        work = jnp.where(lane == i_sel, -jnp.inf, work)
    exps = [jnp.exp(tv - top_v[0]) for tv in top_v]
    denom = exps[0] + exps[1] + exps[2] + exps[3]
    onehots = [lane == ti for ti in top_i]
    member = jnp.zeros((ts, LANES), F32)
    for oh in onehots:
        member = member + jnp.where(oh, 1.0, 0.0)
    tr = lax.broadcasted_iota(jnp.int32, (ts, ts), 0)
    tc = lax.broadcasted_iota(jnp.int32, (ts, ts), 1)
    before = jnp.where(tc < tr, 1.0, 0.0).astype(MXU_DTYPE)
    cnt = cntacc_ref[...]
    rank_all = jnp.dot(before, member.astype(MXU_DTYPE), preferred_element_type=F32) + cnt[0:1]
    sel = jnp.zeros((ts, LANES), F32)
    for kk in range(TOP_K):
        rank_k = jnp.sum(jnp.where(onehots[kk], rank_all, 0.0), axis=-1, keepdims=True)
        sel = sel + jnp.where(lane == float(kk), top_i[kk], 0.0)
        sel = sel + jnp.where(lane == float(TOP_K + kk), exps[kk] / denom, 0.0)
        sel = sel + jnp.where(lane == float(2 * TOP_K + kk), rank_k, 0.0)
    sel_ref[...] = sel
    cnt = cnt + jnp.sum(member, axis=0, keepdims=True)
    cntacc_ref[...] = cnt
    cnt_ref[...] = cnt


def _const_spec(shape):
    return pl.BlockSpec(shape, lambda b, s: (0,) * len(shape))


def _mixer_call(x, ada_l, cos_t, sin_t, lw, tables):
    bsz, seq, d = x.shape
    ts = min(SEQ_TILE, seq)
    n_tok = bsz * seq
    row3 = lambda b, s: (b, s, 0)
    in_specs = [
        pl.BlockSpec((None, ts, d), row3),
        pl.BlockSpec((None, 6, d), lambda b, s: (b, 0, 0)),
        pl.BlockSpec((None, ts, RET_QK_WIDTH), row3),
        pl.BlockSpec((None, ts, RET_QK_WIDTH), row3),
    ] + [_const_spec(a.shape) for a in lw] + [_const_spec(a.shape) for a in tables]
    tok_row = lambda b, s: (b * (seq // ts) + s, 0)
    out_specs = [
        pl.BlockSpec((ts, d), tok_row),
        pl.BlockSpec((ts, d), tok_row),
        pl.BlockSpec((ts, LANES), tok_row),
        pl.BlockSpec((8, LANES), lambda b, s: (0, 0)),
    ]
    out_shape = [
        jax.ShapeDtypeStruct((n_tok, d), F32),
        jax.ShapeDtypeStruct((n_tok, d), F32),
        jax.ShapeDtypeStruct((n_tok, LANES), F32),
        jax.ShapeDtypeStruct((8, LANES), F32),
    ]
    return pl.pallas_call(
        _mixer_kernel,
        grid=(bsz, seq // ts),
        in_specs=in_specs,
        out_specs=out_specs,
        out_shape=out_shape,
        scratch_shapes=[
            pltpu.VMEM((RET_QK_WIDTH, RET_V_WIDTH), F32),
            pltpu.VMEM((8, CONV_WIDTH), F32),
            pltpu.VMEM((8, LANES), F32),
        ],
        compiler_params=pltpu.CompilerParams(
            dimension_semantics=("arbitrary", "arbitrary"), vmem_limit_bytes=VMEM_LIMIT),
        name="mixer",
    )(x, ada_l, cos_t, sin_t, *lw, *tables)


def _expert_kernel(be_ref, nu_ref, tok_cur_ref, tok_nxt_ref, h2_hbm, wgu_ref, bgu_ref, wdn_ref, bdn_ref,
                   ys_ref, xbuf, sem):
    i = pl.program_id(0)
    n_used = nu_ref[0]
    blk = xbuf.shape[1]
    slot = i % 2

    def row_copy(tok_ref, r, dst_slot):
        t = tok_ref[0, 0, r]
        return pltpu.make_async_copy(h2_hbm.at[pl.ds(t, 1), :], xbuf.at[dst_slot, pl.ds(r, 1), :],
                                     sem.at[dst_slot])

    def issue(tok_ref, dst_slot):
        def body(r, carry):
            row_copy(tok_ref, r, dst_slot).start()
            return carry
        lax.fori_loop(0, blk, body, 0, unroll=8)

    @pl.when(i == 0)
    def _():
        issue(tok_cur_ref, 0)

    @pl.when(i + 1 < n_used)
    def _():
        issue(tok_nxt_ref, 1 - slot)

    @pl.when(i < n_used)
    def _():
        def wbody(r, carry):
            row_copy(tok_cur_ref, r, slot).wait()
            return carry
        lax.fori_loop(0, blk, wbody, 0, unroll=8)
        xs = xbuf[slot].astype(MXU_DTYPE)
        gu = jnp.dot(xs, wgu_ref[...], preferred_element_type=F32) + bgu_ref[...]
        gate = jnp.minimum(gu[:, :D_FF], SWIGLU_LIMIT)
        up = jnp.clip(gu[:, D_FF:], -SWIGLU_LIMIT, SWIGLU_LIMIT)
        act = (up + 1.0) * (gate * jax.nn.sigmoid(SWIGLU_ALPHA * gate))
        ys_ref[...] = jnp.dot(act.astype(MXU_DTYPE), wdn_ref[...], preferred_element_type=F32) + bdn_ref[...]

    @pl.when(i >= n_used)
    def _():
        ys_ref[...] = jnp.zeros_like(ys_ref)


def _expert_call(block_e, n_used, row_tok3, h2, w_gu, b_gu, w_down, b_down):
    n_blocks, _, blk = row_tok3.shape
    d = h2.shape[1]
    two_f = w_gu.shape[2]
    last = lambda nu: jnp.maximum(nu[0] - 1, 0)
    grid_spec = pltpu.PrefetchScalarGridSpec(
        num_scalar_prefetch=2,
        grid=(n_blocks,),
        in_specs=[
            pl.BlockSpec((1, 1, blk), lambda i, be, nu: (jnp.minimum(i, last(nu)), 0, 0),
                         memory_space=pltpu.SMEM),
            pl.BlockSpec((1, 1, blk), lambda i, be, nu: (jnp.minimum(i + 1, last(nu)), 0, 0),
                         memory_space=pltpu.SMEM),
            pl.BlockSpec(memory_space=pl.ANY),
            pl.BlockSpec((None, d, two_f), lambda i, be, nu: (be[i], 0, 0)),
            pl.BlockSpec((None, 1, two_f), lambda i, be, nu: (be[i], 0, 0)),
            pl.BlockSpec((None, two_f // 2, d), lambda i, be, nu: (be[i], 0, 0)),
            pl.BlockSpec((None, 1, d), lambda i, be, nu: (be[i], 0, 0)),
        ],
        out_specs=pl.BlockSpec((blk, d), lambda i, be, nu: (i, 0)),
        scratch_shapes=[pltpu.VMEM((2, blk, d), F32), pltpu.SemaphoreType.DMA((2,))],
    )
    n_exp = w_gu.shape[0]
    return pl.pallas_call(
        _expert_kernel,
        grid_spec=grid_spec,
        out_shape=jax.ShapeDtypeStruct((n_blocks * blk, d), F32),
        compiler_params=pltpu.CompilerParams(
            dimension_semantics=("arbitrary",), vmem_limit_bytes=VMEM_LIMIT),
        name="experts",
    )(block_e, n_used, row_tok3, row_tok3, h2, w_gu, b_gu.reshape(n_exp, 1, two_f), w_down,
      b_down.reshape(n_exp, 1, d))


def _combine_kernel(dest_cur_ref, dest_nxt_ref, ys_hbm, x1_ref, sel_ref, ada_ref, lng_ref, lnb_ref,
                    o_ref, gbuf, sem):
    i = pl.program_id(0)
    n_steps = pl.num_programs(0)
    tt = x1_ref.shape[0]
    slot = i % 2

    def row_copy(dest_ref, j, dst_slot):
        dst = dest_ref[0, 0, j]
        return pltpu.make_async_copy(ys_hbm.at[pl.ds(dst, 1), :],
                                     gbuf.at[dst_slot, j % TOP_K, pl.ds(j // TOP_K, 1), :],
                                     sem.at[dst_slot])

    def issue(dest_ref, dst_slot):
        def body(j, carry):
            row_copy(dest_ref, j, dst_slot).start()
            return carry
        lax.fori_loop(0, tt * TOP_K, body, 0, unroll=8)

    @pl.when(i == 0)
    def _():
        issue(dest_cur_ref, 0)

    @pl.when(i + 1 < n_steps)
    def _():
        issue(dest_nxt_ref, 1 - slot)

    def wbody(j, carry):
        row_copy(dest_cur_ref, j, slot).wait()
        return carry
    lax.fori_loop(0, tt * TOP_K, wbody, 0, unroll=8)

    sel = sel_ref[...]
    ffn = None
    for kk in range(TOP_K):
        t = sel[:, TOP_K + kk:TOP_K + kk + 1] * gbuf[slot, kk]
        ffn = t if ffn is None else ffn + t
    ada = ada_ref[...]
    gt2 = ada[5:6]
    o_ref[...] = _layernorm_rows(DEEPNORM_ALPHA * x1_ref[...] + (1.0 + gt2) * ffn, lng_ref[...], lnb_ref[...])


def _combine_call(dest3, ys, x1, sel, ada_l, ln_g, ln_b, seq):
    n_tok, d = x1.shape
    n_steps, _, per = dest3.shape
    tt = per // TOP_K
    tiles_per_seq = seq // tt
    return pl.pallas_call(
        _combine_kernel,
        grid=(n_steps,),
        in_specs=[
            pl.BlockSpec((1, 1, per), lambda i: (i, 0, 0), memory_space=pltpu.SMEM),
            pl.BlockSpec((1, 1, per), lambda i: (jnp.minimum(i + 1, n_steps - 1), 0, 0),
                         memory_space=pltpu.SMEM),
            pl.BlockSpec(memory_space=pl.ANY),
            pl.BlockSpec((tt, d), lambda i: (i, 0)),
            pl.BlockSpec((tt, LANES), lambda i: (i, 0)),
            pl.BlockSpec((None, 6, d), lambda i: (i // tiles_per_seq, 0, 0)),
            pl.BlockSpec((1, d), lambda i: (0, 0)),
            pl.BlockSpec((1, d), lambda i: (0, 0)),
        ],
        out_specs=pl.BlockSpec((tt, d), lambda i: (i, 0)),
        out_shape=jax.ShapeDtypeStruct((n_tok, d), F32),
        scratch_shapes=[pltpu.VMEM((2, TOP_K, tt, d), F32), pltpu.SemaphoreType.DMA((2,))],
        compiler_params=pltpu.CompilerParams(
            dimension_semantics=("arbitrary",), vmem_limit_bytes=VMEM_LIMIT),
        name="combine",
    )(dest3, dest3, ys, x1, sel, ada_l, ln_g.reshape(1, d), ln_b.reshape(1, d))


def _retention_tables():
    log_gamma = jnp.log(1.0 - 2.0 ** (-5.0 - jnp.arange(RET_HEADS, dtype=F32)))
    idx = jnp.arange(CHUNK, dtype=F32)
    diff = idx[:, None] - idx[None, :]
    decay = jnp.where(diff[None] >= 0,
                      jnp.exp(log_gamma[:, None, None] * jnp.maximum(diff, 0.0)[None]), 0.0)
    decay = decay.reshape(RET_HEADS * CHUNK, CHUNK)
    zeta = jnp.exp(log_gamma[:, None] * (CHUNK - 1.0 - idx)[None])
    zeta_t = jnp.repeat(zeta.T, RET_QK_DIM, axis=1)
    xi = jnp.exp(log_gamma[:, None] * (idx + 1.0)[None]).T
    xi_t = jnp.repeat(xi, RET_V_DIM, axis=1)
    cdec = jnp.repeat(jnp.exp(log_gamma * CHUNK), RET_V_DIM)[None, :]
    row_head = jnp.arange(RET_QK_WIDTH) // RET_QK_DIM
    col_head = jnp.arange(RET_V_WIDTH) // RET_V_DIM
    same_head = row_head[:, None] == col_head[None, :]
    bmask = same_head.astype(F32)
    gavg = ((col_head[:, None] == col_head[None, :]).astype(F32) / RET_V_DIM).astype(MXU_DTYPE)
    return decay, zeta_t, xi_t, cdec, bmask, gavg


def _rotary_tables(positions):
    inv_freq = ROPE_BASE ** (-jnp.arange(0, RET_QK_DIM, 2, dtype=F32) / RET_QK_DIM)
    ang = positions.astype(F32)[..., None] * inv_freq
    cos = jnp.cos(ang)
    sin = jnp.sin(ang)
    cos_t = jnp.tile(jnp.concatenate([cos, cos], axis=-1), (1, 1, RET_HEADS))
    sin_t = jnp.tile(jnp.concatenate([-sin, sin], axis=-1), (1, 1, RET_HEADS))
    return cos_t, sin_t


def _routing_tables(sel, counts, n_tok, blk):
    n_asg = n_tok * TOP_K
    n_blocks = n_asg // blk + N_EXPERTS
    sizes = counts.astype(jnp.int32)
    pad_sizes = (sizes + blk - 1) // blk * blk
    pad_end = jnp.cumsum(pad_sizes)
    pad_start = pad_end - pad_sizes
    idx = sel[:, 0:TOP_K].astype(jnp.int32)
    rank = sel[:, 2 * TOP_K:3 * TOP_K].astype(jnp.int32)
    dest = pad_start[idx] + rank
    tok = jnp.broadcast_to(jnp.arange(n_tok, dtype=jnp.int32)[:, None], (n_tok, TOP_K))
    row_tok = jnp.zeros((n_blocks * blk,), jnp.int32).at[dest.reshape(-1)].set(tok.reshape(-1))
    n_used = (pad_end[-1] // blk).astype(jnp.int32)
    block_start = jnp.arange(n_blocks, dtype=jnp.int32) * blk
    block_start = jnp.minimum(block_start, (n_used - 1) * blk)
    block_e = jnp.minimum(jnp.searchsorted(pad_end, block_start, side='right'),
                          N_EXPERTS - 1).astype(jnp.int32)
    return dest, row_tok.reshape(n_blocks, 1, blk), block_e, n_used.reshape(1)


def kernel(x, c, positions, w_ada, b_ada, w_in, gmlp_ln_g, gmlp_ln_b, w_spatial, b_spatial, conv_w,
           w_branch, w_gate_up, b_gate, w_out, ln1_g, ln1_b, w_router, b_router, w_gu, b_gu, w_down,
           b_down, ln2_g, ln2_b):
    bsz, seq, d = x.shape
    n_tok = bsz * seq
    depth = w_ada.shape[0]
    blk = min(MOE_BLOCK, n_tok * TOP_K // N_EXPERTS)
    tt = min(COMBINE_TILE, seq)

    ada = _ada_call(c, w_ada, b_ada)
    ada = jnp.transpose(ada, (0, 2, 1, 3))
    cos_t, sin_t = _rotary_tables(positions)
    tables = _retention_tables()

    mx = MXU_DTYPE
    w_in_m = w_in.astype(mx)
    w_branch_m = w_branch.astype(mx)
    w_gate_m = w_gate_up.astype(mx)
    w_out_m = w_out.astype(mx)
    w_gu_m = w_gu.astype(mx)
    w_down_m = w_down.astype(mx)
    n_exp = w_router.shape[-1]
    w_router_m = jnp.pad(w_router, ((0, 0), (0, 0), (0, LANES - n_exp))).astype(mx)
    b_router_p = jnp.pad(b_router, ((0, 0), (0, LANES - n_exp)), constant_values=NEG_BIG)

    for l in range(depth):
        bsp_t = jnp.repeat(b_spatial[l].T, GMLP_WIDTH // GMLP_GROUPS, axis=1)
        lw = (
            w_in_m[l],
            gmlp_ln_g[l].reshape(1, -1), gmlp_ln_b[l].reshape(1, -1),
            w_spatial[l].reshape(GMLP_GROUPS * CHUNK, CHUNK), bsp_t,
            jnp.pad(conv_w[l], ((0, 8 - conv_w.shape[1]), (0, 0))),
            w_branch_m[l], w_gate_m[l], b_gate[l], w_out_m[l],
            ln1_g[l].reshape(1, -1), ln1_b[l].reshape(1, -1),
            w_router_m[l], b_router_p[l].reshape(1, -1),
        )
        x1, h2, sel, cnt = _mixer_call(x, ada[l], cos_t, sin_t, lw, tables)
        dest, row_tok3, block_e, n_used = _routing_tables(sel, cnt[0, :n_exp], n_tok, blk)
        ys = _expert_call(block_e, n_used, row_tok3, h2, w_gu_m[l], b_gu[l], w_down_m[l], b_down[l])
        dest3 = dest.reshape(n_tok // tt, 1, tt * TOP_K)
        x = _combine_call(dest3, ys, x1, sel, ada[l], ln2_g[l], ln2_b[l], seq).reshape(bsz, seq, d)
    return x
```

```python
import functools

import jax
import jax.numpy as jnp
from jax import lax
from jax.experimental import pallas as pl
from jax.experimental.pallas import tpu as pltpu

F32 = jnp.float32
MXU_DTYPE = jnp.bfloat16

DEPTH = 4
RET_HEADS = 4
RET_QK_DIM = 32
RET_V_DIM = 64
RET_QK_WIDTH = RET_HEADS * RET_QK_DIM
RET_V_WIDTH = RET_HEADS * RET_V_DIM
CHUNK = 128
ROPE_BASE = 10000.0
GMLP_GROUPS = 4
GMLP_WIDTH = 256
CONV_WIDTH = 256
GATE_RANK = 128
N_BRANCH = 3
N_EXPERTS = 32
TOP_K = 4
D_FF = 256
SWIGLU_LIMIT = 7.0
SWIGLU_ALPHA = 1.702
DEEPNORM_ALPHA = (2.0 * DEPTH) ** 0.25
LN_EPS = 1e-5

_O_QK = 0
_O_VG = 2 * RET_QK_WIDTH
_O_GMLP = _O_VG + 2 * RET_V_WIDTH
_O_CONV = _O_GMLP + 2 * GMLP_WIDTH
_O_CODE = _O_CONV + 3 * CONV_WIDTH
_IN_WIDTH = _O_CODE + N_BRANCH * GATE_RANK

LANES = 128
NEG_BIG = -1e30

SEQ_TILE = 256
MOE_BLOCK = 512
COMBINE_TILE = 128
VMEM_LIMIT = 56 * 1024 * 1024


def _dot(a, b):
    return jnp.dot(a.astype(MXU_DTYPE), b.astype(MXU_DTYPE), preferred_element_type=F32)


def _dot_nt(a, b):
    return lax.dot_general(a.astype(MXU_DTYPE), b.astype(MXU_DTYPE),
                           (((1,), (1,)), ((), ())), preferred_element_type=F32)


def _split_dot(x, w):
    hi = x.astype(MXU_DTYPE)
    lo = (x - hi.astype(F32)).astype(MXU_DTYPE)
    return (jnp.dot(hi, w, preferred_element_type=F32)
            + jnp.dot(lo, w, preferred_element_type=F32))


def _layernorm_rows(x, g, b):
    mu = jnp.mean(x, axis=-1, keepdims=True)
    d = x - mu
    var = jnp.mean(d * d, axis=-1, keepdims=True)
    return d * lax.rsqrt(var + LN_EPS) * g + b


def _ada_kernel(c_ref, w_ref, b_ref, o_ref):
    c_act = jax.nn.silu(c_ref[...])
    o_ref[...] = jnp.dot(c_act, w_ref[...], preferred_element_type=F32,
                         precision=lax.Precision.HIGHEST) + b_ref[...]


def _ada_call(c, w_ada, b_ada):
    depth, d, six_d = w_ada.shape
    bsz = c.shape[0]
    n_col = six_d // d
    return pl.pallas_call(
        _ada_kernel,
        grid=(depth, n_col),
        in_specs=[
            pl.BlockSpec((bsz, d), lambda l, j: (0, 0)),
            pl.BlockSpec((None, d, d), lambda l, j: (l, 0, j)),
            pl.BlockSpec((None, 1, d), lambda l, j: (l, 0, j)),
        ],
        out_specs=pl.BlockSpec((None, None, bsz, d), lambda l, j: (l, j, 0, 0)),
        out_shape=jax.ShapeDtypeStruct((depth, n_col, bsz, d), F32),
        compiler_params=pltpu.CompilerParams(
            dimension_semantics=("arbitrary", "arbitrary"), vmem_limit_bytes=VMEM_LIMIT),
        name="ada_ln",
    )(c, w_ada, b_ada.reshape(depth, 1, six_d))


def _mixer_kernel(x_ref, ada_ref, cos_ref, sin_ref, w_in_ref, lng_ref, lnb_ref, wsp_ref, bsp_ref,
                  convw_ref, wbr_ref, wgt_ref, bgt_ref, wout_ref, ln1g_ref, ln1b_ref, wr_ref, br_ref,
                  decay_ref, zeta_ref, xi_ref, cdec_ref, bmask_ref, gavg_ref,
                  x1_ref, h2_ref, sel_ref, cnt_ref,
                  state_ref, zc_ref, cntacc_ref):
    b = pl.program_id(0)
    s = pl.program_id(1)
    ts = x_ref.shape[0]

    @pl.when(s == 0)
    def _():
        state_ref[...] = jnp.zeros_like(state_ref)
        zc_ref[...] = jnp.zeros_like(zc_ref)

    @pl.when(jnp.logical_and(b == 0, s == 0))
    def _():
        cntacc_ref[...] = jnp.zeros_like(cntacc_ref)

    x = x_ref[...]
    ada = ada_ref[...]
    sh1, sc1, gt1, sh2, sc2 = ada[0:1], ada[1:2], ada[2:3], ada[3:4], ada[4:5]
    h = (x * (1.0 + sc1) + sh1).astype(MXU_DTYPE)

    def proj(lo, hi):
        return jnp.dot(h, w_in_ref[:, lo:hi], preferred_element_type=F32)

    lane_qk = lax.broadcasted_iota(jnp.int32, (1, RET_QK_WIDTH), 1)
    lane_v = lax.broadcasted_iota(jnp.int32, (1, RET_V_WIDTH), 1)
    qk_masks = [lane_qk // RET_QK_DIM == hd for hd in range(RET_HEADS)]
    v_masks = [lane_v // RET_V_DIM == hd for hd in range(RET_HEADS)]

    qk = proj(_O_QK, _O_VG)
    cos = cos_ref[...]
    sin = sin_ref[...]
    first_half = (lane_qk % RET_QK_DIM) < (RET_QK_DIM // 2)

    def rotary(t):
        swapped = jnp.where(first_half, pltpu.roll(t, RET_QK_WIDTH - RET_QK_DIM // 2, 1),
                            pltpu.roll(t, RET_QK_DIM // 2, 1))
        return t * cos + swapped * sin

    q = rotary(qk[:, :RET_QK_WIDTH])
    k = rotary(qk[:, RET_QK_WIDTH:]) * (RET_QK_DIM ** -0.5)
    vg = proj(_O_VG, _O_GMLP)
    v = vg[:, :RET_V_WIDTH]
    g = vg[:, RET_V_WIDTH:]

    decay = decay_ref[...]
    zeta = zeta_ref[...]
    xi = xi_ref[...]
    cdec = cdec_ref[...]
    bmask = bmask_ref[...]
    gavg = gavg_ref[...]

    o_chunks = []
    state = state_ref[...]
    for c in range(ts // CHUNK):
        rows = slice(c * CHUNK, (c + 1) * CHUNK)
        q_c, k_c, v_c = q[rows], k[rows], v[rows]
        v_m = v_c.astype(MXU_DTYPE)
        q_all = jnp.concatenate([jnp.where(m, q_c, 0.0) for m in qk_masks], axis=0)
        scores = _dot_nt(q_all, k_c) * decay
        o_all = _dot(scores, v_m)
        o = _dot(q_c, state) * xi
        for hd in range(RET_HEADS):
            o = o + jnp.where(v_masks[hd], o_all[hd * CHUNK:(hd + 1) * CHUNK], 0.0)
        kv = _dot((k_c * zeta).T, v_m) * bmask
        state = state * cdec + kv
        o_chunks.append(o)
    state_ref[...] = state
    o = jnp.concatenate(o_chunks, axis=0) if len(o_chunks) > 1 else o_chunks[0]
    mu = _split_dot(o, gavg)
    d = o - mu
    var = _split_dot(d * d, gavg)
    r_br = jax.nn.silu(g) * (d * lax.rsqrt(var + LN_EPS))

    guv = proj(_O_GMLP, _O_CONV)
    u = jax.nn.gelu(guv[:, :GMLP_WIDTH])
    vv = _layernorm_rows(jax.nn.gelu(guv[:, GMLP_WIDTH:]), lng_ref[...], lnb_ref[...]).astype(MXU_DTYPE)
    rr = lax.broadcasted_iota(jnp.int32, (GMLP_GROUPS * CHUNK, CHUNK), 0) % CHUNK
    cc_ = lax.broadcasted_iota(jnp.int32, (GMLP_GROUPS * CHUNK, CHUNK), 1)
    w_sp = jnp.where(cc_ <= rr, wsp_ref[...], 0.0).astype(MXU_DTYPE)
    bsp = bsp_ref[...]
    z_chunks = []
    for c in range(ts // CHUNK):
        z_all = jnp.dot(w_sp, vv[c * CHUNK:(c + 1) * CHUNK], preferred_element_type=F32)
        z = bsp
        for gi in range(GMLP_GROUPS):
            z = z + jnp.where(v_masks[gi], z_all[gi * CHUNK:(gi + 1) * CHUNK], 0.0)
        z_chunks.append(z)
    z = jnp.concatenate(z_chunks, axis=0) if len(z_chunks) > 1 else z_chunks[0]
    s_br = u * z

    cbcx = proj(_O_CONV, _O_CODE)
    gate_b = cbcx[:, :CONV_WIDTH]
    zc = cbcx[:, CONV_WIDTH:2 * CONV_WIDTH] * cbcx[:, 2 * CONV_WIDTH:]
    carry = zc_ref[...]
    prev1 = carry[7:8]
    prev2 = carry[6:7]
    row = lax.broadcasted_iota(jnp.int32, (ts, 1), 0)
    z1 = jnp.where(row == 0, prev1, pltpu.roll(zc, 1, 0))
    z2 = jnp.where(row == 0, prev2, jnp.where(row == 1, prev1, pltpu.roll(zc, 2, 0)))
    zc_ref[...] = zc[ts - 8:ts]
    cw = convw_ref[...]
    k_br = gate_b * (cw[0:1] * z2 + cw[1:2] * z1 + cw[2:3] * zc)

    code = proj(_O_CODE, _IN_WIDTH)
    merged = None
    for i, br in enumerate((r_br, s_br, k_br)):
        y = jnp.dot(br.astype(MXU_DTYPE), wbr_ref[i], preferred_element_type=F32)
        gl = jnp.dot(code[:, i * GATE_RANK:(i + 1) * GATE_RANK].astype(MXU_DTYPE), wgt_ref[i],
                     preferred_element_type=F32) + bgt_ref[i:i + 1]
        t = jax.nn.sigmoid(gl) * y
        merged = t if merged is None else merged + t
    mix = jnp.dot(merged.astype(MXU_DTYPE), wout_ref[...], preferred_element_type=F32)
    x1 = _layernorm_rows(DEEPNORM_ALPHA * x + (1.0 + gt1) * mix, ln1g_ref[...], ln1b_ref[...])
    x1_ref[...] = x1
    h2 = x1 * (1.0 + sc2) + sh2
    h2_ref[...] = h2

    logits = jnp.dot(h2.astype(MXU_DTYPE), wr_ref[...], preferred_element_type=F32) + br_ref[...]
    lane = lax.broadcasted_iota(jnp.int32, (1, LANES), 1).astype(F32)
    top_v, top_i = [], []
    work = logits
    for _ in range(TOP_K):
        m = jnp.max(work, axis=-1, keepdims=True)
        i_sel = jnp.min(jnp.where(work == m, lane, float(LANES)), axis=-1, keepdims=True)
        top_v.append(m)
        top_i.append(i_sel)
        work = jnp.where(lane == i_sel, -jnp.inf, work)
    exps = [jnp.exp(tv - top_v[0]) for tv in top_v]
    denom = exps[0] + exps[1] + exps[2] + exps[3]
    onehots = [lane == ti for ti in top_i]
    member = jnp.zeros((ts, LANES), F32)
    for oh in onehots:
        member = member + jnp.where(oh, 1.0, 0.0)
    tr = lax.broadcasted_iota(jnp.int32, (ts, ts), 0)
    tc = lax.broadcasted_iota(jnp.int32, (ts, ts), 1)
    before = jnp.where(tc < tr, 1.0, 0.0).astype(MXU_DTYPE)
    cnt = cntacc_ref[...]
    rank_all = jnp.dot(before, member.astype(MXU_DTYPE), preferred_element_type=F32) + cnt[0:1]
    sel = jnp.zeros((ts, LANES), F32)
    for kk in range(TOP_K):
        rank_k = jnp.sum(jnp.where(onehots[kk], rank_all, 0.0), axis=-1, keepdims=True)
        sel = sel + jnp.where(lane == float(kk), top_i[kk], 0.0)
        sel = sel + jnp.where(lane == float(TOP_K + kk), exps[kk] / denom, 0.0)
        sel = sel + jnp.where(lane == float(2 * TOP_K + kk), rank_k, 0.0)
    sel_ref[...] = sel
    cnt = cnt + jnp.sum(member, axis=0, keepdims=True)
    cntacc_ref[...] = cnt
    cnt_ref[...] = cnt


def _const_spec(shape):
    return pl.BlockSpec(shape, lambda b, s: (0,) * len(shape))


def _mixer_call(x, ada_l, cos_t, sin_t, lw, tables):
    bsz, seq, d = x.shape
    ts = min(SEQ_TILE, seq)
    n_tok = bsz * seq
    row3 = lambda b, s: (b, s, 0)
    in_specs = [
        pl.BlockSpec((None, ts, d), row3),
        pl.BlockSpec((None, 6, d), lambda b, s: (b, 0, 0)),
        pl.BlockSpec((None, ts, RET_QK_WIDTH), row3),
        pl.BlockSpec((None, ts, RET_QK_WIDTH), row3),
    ] + [_const_spec(a.shape) for a in lw] + [_const_spec(a.shape) for a in tables]
    tok_row = lambda b, s: (b * (seq // ts) + s, 0)
    out_specs = [
        pl.BlockSpec((ts, d), tok_row),
        pl.BlockSpec((ts, d), tok_row),
        pl.BlockSpec((ts, LANES), tok_row),
        pl.BlockSpec((8, LANES), lambda b, s: (0, 0)),
    ]
    out_shape = [
        jax.ShapeDtypeStruct((n_tok, d), F32),
        jax.ShapeDtypeStruct((n_tok, d), F32),
        jax.ShapeDtypeStruct((n_tok, LANES), F32),
        jax.ShapeDtypeStruct((8, LANES), F32),
    ]
    return pl.pallas_call(
        _mixer_kernel,
        grid=(bsz, seq // ts),
        in_specs=in_specs,
        out_specs=out_specs,
        out_shape=out_shape,
        scratch_shapes=[
            pltpu.VMEM((RET_QK_WIDTH, RET_V_WIDTH), F32),
            pltpu.VMEM((8, CONV_WIDTH), F32),
            pltpu.VMEM((8, LANES), F32),
        ],
        compiler_params=pltpu.CompilerParams(
            dimension_semantics=("arbitrary", "arbitrary"), vmem_limit_bytes=VMEM_LIMIT),
        name="mixer",
    )(x, ada_l, cos_t, sin_t, *lw, *tables)


ROW_GROUP = 8


def _dispatch_kernel(pe_ref, ps_ref, dest_ref, h2_ref, xs_hbm, zbuf, sem, zsem):
    i = pl.program_id(0)
    ts = h2_ref.shape[0]
    blk = zbuf.shape[0]
    n_exp = pe_ref.shape[0]

    @pl.when(i == 0)
    def _():
        zbuf[...] = jnp.zeros_like(zbuf)

        def tail_copy(e):
            start = pl.multiple_of(jnp.maximum(pe_ref[e] - blk, 0), blk)
            return pltpu.make_async_copy(zbuf, xs_hbm.at[pl.ds(start, blk), :], zsem)

        def zstart(e, carry):
            @pl.when(pe_ref[e] > ps_ref[e])
            def _():
                tail_copy(e).start()
            return carry

        def zwait(e, carry):
            @pl.when(pe_ref[e] > ps_ref[e])
            def _():
                tail_copy(e).wait()
            return carry

        lax.fori_loop(0, n_exp, zstart, 0)
        lax.fori_loop(0, n_exp, zwait, 0)

    def row_copy(g, u, kk):
        r = pl.multiple_of(g * ROW_GROUP, ROW_GROUP) + u
        dst = dest_ref[0, 0, (g * ROW_GROUP + u) * TOP_K + kk]
        return pltpu.make_async_copy(h2_ref.at[pl.ds(r, 1), :], xs_hbm.at[pl.ds(dst, 1), :], sem)

    def issue(g, carry):
        for u in range(ROW_GROUP):
            for kk in range(TOP_K):
                row_copy(g, u, kk).start()
        return carry

    def drain(g, carry):
        for u in range(ROW_GROUP):
            for kk in range(TOP_K):
                row_copy(g, u, kk).wait()
        return carry

    lax.fori_loop(0, ts // ROW_GROUP, issue, 0)
    lax.fori_loop(0, ts // ROW_GROUP, drain, 0)


def _dispatch_call(pad_end, pad_start, dest3, h2, n_rows, blk):
    n_tok, d = h2.shape
    n_steps, _, per = dest3.shape
    ts = per // TOP_K
    grid_spec = pltpu.PrefetchScalarGridSpec(
        num_scalar_prefetch=2,
        grid=(n_steps,),
        in_specs=[
            pl.BlockSpec((1, 1, per), lambda i, pe, ps: (i, 0, 0), memory_space=pltpu.SMEM),
            pl.BlockSpec((ts, d), lambda i, pe, ps: (i, 0)),
        ],
        out_specs=pl.BlockSpec(memory_space=pl.ANY),
        scratch_shapes=[pltpu.VMEM((blk, d), F32), pltpu.SemaphoreType.DMA(()),
                        pltpu.SemaphoreType.DMA(())],
    )
    return pl.pallas_call(
        _dispatch_kernel,
        grid_spec=grid_spec,
        out_shape=jax.ShapeDtypeStruct((n_rows, d), F32),
        compiler_params=pltpu.CompilerParams(
            dimension_semantics=("arbitrary",), vmem_limit_bytes=VMEM_LIMIT),
        name="dispatch",
    )(pad_end, pad_start, dest3, h2)


def _expert_kernel(be_ref, nu_ref, xs_ref, wgu_ref, bgu_ref, wdn_ref, bdn_ref, ys_ref):
    i = pl.program_id(0)
    n_used = nu_ref[0]

    @pl.when(i < n_used)
    def _():
        xs = xs_ref[...].astype(MXU_DTYPE)
        gu = jnp.dot(xs, wgu_ref[...], preferred_element_type=F32) + bgu_ref[...]
        gate = jnp.minimum(gu[:, :D_FF], SWIGLU_LIMIT)
        up = jnp.clip(gu[:, D_FF:], -SWIGLU_LIMIT, SWIGLU_LIMIT)
        act = (up + 1.0) * (gate * jax.nn.sigmoid(SWIGLU_ALPHA * gate))
        ys_ref[...] = jnp.dot(act.astype(MXU_DTYPE), wdn_ref[...], preferred_element_type=F32) + bdn_ref[...]

    @pl.when(i >= n_used)
    def _():
        ys_ref[...] = jnp.zeros_like(ys_ref)


def _expert_call(block_e, n_used, xs, w_gu, b_gu, w_down, b_down, blk):
    n_rows, d = xs.shape
    n_blocks = n_rows // blk
    two_f = w_gu.shape[2]
    last = lambda nu: jnp.maximum(nu[0] - 1, 0)
    grid_spec = pltpu.PrefetchScalarGridSpec(
        num_scalar_prefetch=2,
        grid=(n_blocks,),
        in_specs=[
            pl.BlockSpec((blk, d), lambda i, be, nu: (jnp.minimum(i, last(nu)), 0)),
            pl.BlockSpec((None, d, two_f), lambda i, be, nu: (be[i], 0, 0)),
            pl.BlockSpec((None, 1, two_f), lambda i, be, nu: (be[i], 0, 0)),
            pl.BlockSpec((None, two_f // 2, d), lambda i, be, nu: (be[i], 0, 0)),
            pl.BlockSpec((None, 1, d), lambda i, be, nu: (be[i], 0, 0)),
        ],
        out_specs=pl.BlockSpec((blk, d), lambda i, be, nu: (i, 0)),
    )
    n_exp = w_gu.shape[0]
    return pl.pallas_call(
        _expert_kernel,
        grid_spec=grid_spec,
        out_shape=jax.ShapeDtypeStruct((n_rows, d), F32),
        compiler_params=pltpu.CompilerParams(
            dimension_semantics=("arbitrary",), vmem_limit_bytes=VMEM_LIMIT),
        name="experts",
    )(block_e, n_used, xs, w_gu, b_gu.reshape(n_exp, 1, two_f), w_down, b_down.reshape(n_exp, 1, d))


def _combine_kernel(dest_cur_ref, dest_nxt_ref, ys_hbm, x1_ref, sel_ref, ada_ref, lng_ref, lnb_ref,
                    o_ref, gbuf, sem):
    i = pl.program_id(0)
    n_steps = pl.num_programs(0)
    tt = x1_ref.shape[0]
    slot = i % 2

    def row_copy(dest_ref, g, u, kk, dst_slot):
        r = pl.multiple_of(g * ROW_GROUP, ROW_GROUP) + u
        src = dest_ref[0, 0, (g * ROW_GROUP + u) * TOP_K + kk]
        return pltpu.make_async_copy(ys_hbm.at[pl.ds(src, 1), :], gbuf.at[dst_slot, kk, pl.ds(r, 1), :],
                                     sem.at[dst_slot])

    def issue(dest_ref, dst_slot):
        def body(g, carry):
            for u in range(ROW_GROUP):
                for kk in range(TOP_K):
                    row_copy(dest_ref, g, u, kk, dst_slot).start()
            return carry
        lax.fori_loop(0, tt // ROW_GROUP, body, 0)

    @pl.when(i == 0)
    def _():
        issue(dest_cur_ref, 0)

    @pl.when(i + 1 < n_steps)
    def _():
        issue(dest_nxt_ref, 1 - slot)

    def drain(g, carry):
        for u in range(ROW_GROUP):
            for kk in range(TOP_K):
                row_copy(dest_cur_ref, g, u, kk, slot).wait()
        return carry
    lax.fori_loop(0, tt // ROW_GROUP, drain, 0)

    sel = sel_ref[...]
    ffn = None
    for kk in range(TOP_K):
        t = sel[:, TOP_K + kk:TOP_K + kk + 1] * gbuf[slot, kk]
        ffn = t if ffn is None else ffn + t
    ada = ada_ref[...]
    gt2 = ada[5:6]
    o_ref[...] = _layernorm_rows(DEEPNORM_ALPHA * x1_ref[...] + (1.0 + gt2) * ffn, lng_ref[...], lnb_ref[...])


def _combine_call(dest3, ys, x1, sel, ada_l, ln_g, ln_b, seq):
    n_tok, d = x1.shape
    n_steps, _, per = dest3.shape
    tt = per // TOP_K
    tiles_per_seq = seq // tt
    return pl.pallas_call(
        _combine_kernel,
        grid=(n_steps,),
        in_specs=[
            pl.BlockSpec((1, 1, per), lambda i: (i, 0, 0), memory_space=pltpu.SMEM),
            pl.BlockSpec((1, 1, per), lambda i: (jnp.minimum(i + 1, n_steps - 1), 0, 0),
                         memory_space=pltpu.SMEM),
            pl.BlockSpec(memory_space=pl.ANY),
            pl.BlockSpec((tt, d), lambda i: (i, 0)),
            pl.BlockSpec((tt, LANES), lambda i: (i, 0)),
            pl.BlockSpec((None, 6, d), lambda i: (i // tiles_per_seq, 0, 0)),
            pl.BlockSpec((1, d), lambda i: (0, 0)),
            pl.BlockSpec((1, d), lambda i: (0, 0)),
        ],
        out_specs=pl.BlockSpec((tt, d), lambda i: (i, 0)),
        out_shape=jax.ShapeDtypeStruct((n_tok, d), F32),
        scratch_shapes=[pltpu.VMEM((2, TOP_K, tt, d), F32), pltpu.SemaphoreType.DMA((2,))],
        compiler_params=pltpu.CompilerParams(
            dimension_semantics=("arbitrary",), vmem_limit_bytes=VMEM_LIMIT),
        name="combine",
    )(dest3, dest3, ys, x1, sel, ada_l, ln_g.reshape(1, d), ln_b.reshape(1, d))


def _retention_tables():
    log_gamma = jnp.log(1.0 - 2.0 ** (-5.0 - jnp.arange(RET_HEADS, dtype=F32)))
    idx = jnp.arange(CHUNK, dtype=F32)
    diff = idx[:, None] - idx[None, :]
    decay = jnp.where(diff[None] >= 0,
                      jnp.exp(log_gamma[:, None, None] * jnp.maximum(diff, 0.0)[None]), 0.0)
    decay = decay.reshape(RET_HEADS * CHUNK, CHUNK)
    zeta = jnp.exp(log_gamma[:, None] * (CHUNK - 1.0 - idx)[None])
    zeta_t = jnp.repeat(zeta.T, RET_QK_DIM, axis=1)
    xi = jnp.exp(log_gamma[:, None] * (idx + 1.0)[None]).T
    xi_t = jnp.repeat(xi, RET_V_DIM, axis=1)
    cdec = jnp.repeat(jnp.exp(log_gamma * CHUNK), RET_V_DIM)[None, :]
    row_head = jnp.arange(RET_QK_WIDTH) // RET_QK_DIM
    col_head = jnp.arange(RET_V_WIDTH) // RET_V_DIM
    same_head = row_head[:, None] == col_head[None, :]
    bmask = same_head.astype(F32)
    gavg = ((col_head[:, None] == col_head[None, :]).astype(F32) / RET_V_DIM).astype(MXU_DTYPE)
    return decay, zeta_t, xi_t, cdec, bmask, gavg


def _rotary_tables(positions):
    inv_freq = ROPE_BASE ** (-jnp.arange(0, RET_QK_DIM, 2, dtype=F32) / RET_QK_DIM)
    ang = positions.astype(F32)[..., None] * inv_freq
    cos = jnp.cos(ang)
    sin = jnp.sin(ang)
    cos_t = jnp.tile(jnp.concatenate([cos, cos], axis=-1), (1, 1, RET_HEADS))
    sin_t = jnp.tile(jnp.concatenate([-sin, sin], axis=-1), (1, 1, RET_HEADS))
    return cos_t, sin_t


def _routing_tables(sel, counts, n_tok, blk):
    n_asg = n_tok * TOP_K
    n_blocks = n_asg // blk + N_EXPERTS
    sizes = counts.astype(jnp.int32)
    pad_sizes = (sizes + blk - 1) // blk * blk
    pad_end = jnp.cumsum(pad_sizes)
    pad_start = pad_end - pad_sizes
    idx = sel[:, 0:TOP_K].astype(jnp.int32)
    rank = sel[:, 2 * TOP_K:3 * TOP_K].astype(jnp.int32)
    onehot = idx[:, :, None] == jnp.arange(N_EXPERTS, dtype=jnp.int32)
    dest = jnp.sum(jnp.where(onehot, pad_start, 0), axis=-1) + rank
    n_used = (pad_end[-1] // blk).astype(jnp.int32)
    block_start = jnp.arange(n_blocks, dtype=jnp.int32) * blk
    block_start = jnp.minimum(block_start, (n_used - 1) * blk)
    block_e = jnp.sum((pad_end[None, :] <= block_start[:, None]).astype(jnp.int32), axis=-1)
    block_e = jnp.minimum(block_e, N_EXPERTS - 1)
    return dest, pad_end.astype(jnp.int32), pad_start.astype(jnp.int32), block_e, n_used.reshape(1)


def kernel(x, c, positions, w_ada, b_ada, w_in, gmlp_ln_g, gmlp_ln_b, w_spatial, b_spatial, conv_w,
           w_branch, w_gate_up, b_gate, w_out, ln1_g, ln1_b, w_router, b_router, w_gu, b_gu, w_down,
           b_down, ln2_g, ln2_b):
    bsz, seq, d = x.shape
    n_tok = bsz * seq
    depth = w_ada.shape[0]
    blk = min(MOE_BLOCK, n_tok * TOP_K // N_EXPERTS)
    tt = min(COMBINE_TILE, seq)
    ts = min(SEQ_TILE, seq)

    ada = _ada_call(c, w_ada, b_ada)
    ada = jnp.transpose(ada, (0, 2, 1, 3))
    cos_t, sin_t = _rotary_tables(positions)
    tables = _retention_tables()

    mx = MXU_DTYPE
    w_in_m = w_in.astype(mx)
    w_branch_m = w_branch.astype(mx)
    w_gate_m = w_gate_up.astype(mx)
    w_out_m = w_out.astype(mx)
    w_gu_m = w_gu.astype(mx)
    w_down_m = w_down.astype(mx)
    n_exp = w_router.shape[-1]
    w_router_m = jnp.pad(w_router, ((0, 0), (0, 0), (0, LANES - n_exp))).astype(mx)
    b_router_p = jnp.pad(b_router, ((0, 0), (0, LANES - n_exp)), constant_values=NEG_BIG)

    for l in range(depth):
        bsp_t = jnp.repeat(b_spatial[l].T, GMLP_WIDTH // GMLP_GROUPS, axis=1)
        lw = (
            w_in_m[l],
            gmlp_ln_g[l].reshape(1, -1), gmlp_ln_b[l].reshape(1, -1),
            w_spatial[l].reshape(GMLP_GROUPS * CHUNK, CHUNK), bsp_t,
            jnp.pad(conv_w[l], ((0, 8 - conv_w.shape[1]), (0, 0))),
            w_branch_m[l], w_gate_m[l], b_gate[l], w_out_m[l],
            ln1_g[l].reshape(1, -1), ln1_b[l].reshape(1, -1),
            w_router_m[l], b_router_p[l].reshape(1, -1),
        )
        x1, h2, sel, cnt = _mixer_call(x, ada[l], cos_t, sin_t, lw, tables)
        dest, pad_end, pad_start, block_e, n_used = _routing_tables(sel, cnt[0, :n_exp], n_tok, blk)
        n_rows = (n_tok * TOP_K // blk + n_exp) * blk
        xs = _dispatch_call(pad_end, pad_start, dest.reshape(n_tok // ts, 1, ts * TOP_K), h2, n_rows, blk)
        ys = _expert_call(block_e, n_used, xs, w_gu_m[l], b_gu[l], w_down_m[l], b_down[l], blk)
        dest3 = dest.reshape(n_tok // tt, 1, tt * TOP_K)
        x = _combine_call(dest3, ys, x1, sel, ada[l], ln2_g[l], ln2_b[l], seq).reshape(bsz, seq, d)
    return x
```

```python
import functools

import jax
import jax.numpy as jnp
from jax import lax
from jax.experimental import pallas as pl
from jax.experimental.pallas import tpu as pltpu

F32 = jnp.float32
MXU_DTYPE = jnp.bfloat16

DEPTH = 4
RET_HEADS = 4
RET_QK_DIM = 32
RET_V_DIM = 64
RET_QK_WIDTH = RET_HEADS * RET_QK_DIM
RET_V_WIDTH = RET_HEADS * RET_V_DIM
CHUNK = 128
ROPE_BASE = 10000.0
GMLP_GROUPS = 4
GMLP_WIDTH = 256
CONV_WIDTH = 256
GATE_RANK = 128
N_BRANCH = 3
N_EXPERTS = 32
TOP_K = 4
D_FF = 256
SWIGLU_LIMIT = 7.0
SWIGLU_ALPHA = 1.702
DEEPNORM_ALPHA = (2.0 * DEPTH) ** 0.25
LN_EPS = 1e-5

_O_QK = 0
_O_VG = 2 * RET_QK_WIDTH
_O_GMLP = _O_VG + 2 * RET_V_WIDTH
_O_CONV = _O_GMLP + 2 * GMLP_WIDTH
_O_CODE = _O_CONV + 3 * CONV_WIDTH
_IN_WIDTH = _O_CODE + N_BRANCH * GATE_RANK

LANES = 128
NEG_BIG = -1e30

SEQ_TILE = 256
SLAB_CAP = 64
SLAB_TILES = 8
MOE_BLOCK = 512
COMBINE_TILE = 128
VMEM_LIMIT = 56 * 1024 * 1024


def _dot(a, b):
    return jnp.dot(a.astype(MXU_DTYPE), b.astype(MXU_DTYPE), preferred_element_type=F32)


def _dot_nt(a, b):
    return lax.dot_general(a.astype(MXU_DTYPE), b.astype(MXU_DTYPE),
                           (((1,), (1,)), ((), ())), preferred_element_type=F32)


def _split_dot(x, w):
    hi = x.astype(MXU_DTYPE)
    lo = (x - hi.astype(F32)).astype(MXU_DTYPE)
    return (jnp.dot(hi, w, preferred_element_type=F32)
            + jnp.dot(lo, w, preferred_element_type=F32))


def _layernorm_rows(x, g, b):
    mu = jnp.mean(x, axis=-1, keepdims=True)
    d = x - mu
    var = jnp.mean(d * d, axis=-1, keepdims=True)
    return d * lax.rsqrt(var + LN_EPS) * g + b


def _ada_kernel(c_ref, w_ref, b_ref, o_ref):
    c_act = jax.nn.silu(c_ref[...])
    o_ref[...] = jnp.dot(c_act, w_ref[...], preferred_element_type=F32,
                         precision=lax.Precision.HIGHEST) + b_ref[...]


def _ada_call(c, w_ada, b_ada):
    depth, d, six_d = w_ada.shape
    bsz = c.shape[0]
    n_col = six_d // d
    return pl.pallas_call(
        _ada_kernel,
        grid=(depth, n_col),
        in_specs=[
            pl.BlockSpec((bsz, d), lambda l, j: (0, 0)),
            pl.BlockSpec((None, d, d), lambda l, j: (l, 0, j)),
            pl.BlockSpec((None, 1, d), lambda l, j: (l, 0, j)),
        ],
        out_specs=pl.BlockSpec((None, None, bsz, d), lambda l, j: (l, j, 0, 0)),
        out_shape=jax.ShapeDtypeStruct((depth, n_col, bsz, d), F32),
        compiler_params=pltpu.CompilerParams(
            dimension_semantics=("arbitrary", "arbitrary"), vmem_limit_bytes=VMEM_LIMIT),
        name="ada_ln",
    )(c, w_ada, b_ada.reshape(depth, 1, six_d))


def _mixer_kernel(x_ref, ada_ref, cos_ref, sin_ref, w_in_ref, lng_ref, lnb_ref, wsp_ref, bsp_ref,
                  convw_ref, wbr_ref, wgt_ref, bgt_ref, wout_ref, ln1g_ref, ln1b_ref, wr_ref, br_ref,
                  decay_ref, zeta_ref, xi_ref, cdec_ref, bmask_ref, gavg_ref,
                  x1_ref, xs_ref, sel_ref, cnt_ref,
                  state_ref, zc_ref, cntacc_ref):
    b = pl.program_id(0)
    s = pl.program_id(1)
    ts = x_ref.shape[0]

    @pl.when(s == 0)
    def _():
        state_ref[...] = jnp.zeros_like(state_ref)
        zc_ref[...] = jnp.zeros_like(zc_ref)

    @pl.when(jnp.logical_and(b == 0, s == 0))
    def _():
        cntacc_ref[...] = jnp.zeros_like(cntacc_ref)

    x = x_ref[...]
    ada = ada_ref[...]
    sh1, sc1, gt1, sh2, sc2 = ada[0:1], ada[1:2], ada[2:3], ada[3:4], ada[4:5]
    h = (x * (1.0 + sc1) + sh1).astype(MXU_DTYPE)

    def proj(lo, hi):
        return jnp.dot(h, w_in_ref[:, lo:hi], preferred_element_type=F32)

    lane_qk = lax.broadcasted_iota(jnp.int32, (1, RET_QK_WIDTH), 1)
    lane_v = lax.broadcasted_iota(jnp.int32, (1, RET_V_WIDTH), 1)
    qk_masks = [lane_qk // RET_QK_DIM == hd for hd in range(RET_HEADS)]
    v_masks = [lane_v // RET_V_DIM == hd for hd in range(RET_HEADS)]

    qk = proj(_O_QK, _O_VG)
    cos = cos_ref[...]
    sin = sin_ref[...]
    first_half = (lane_qk % RET_QK_DIM) < (RET_QK_DIM // 2)

    def rotary(t):
        swapped = jnp.where(first_half, pltpu.roll(t, RET_QK_WIDTH - RET_QK_DIM // 2, 1),
                            pltpu.roll(t, RET_QK_DIM // 2, 1))
        return t * cos + swapped * sin

    q = rotary(qk[:, :RET_QK_WIDTH])
    k = rotary(qk[:, RET_QK_WIDTH:]) * (RET_QK_DIM ** -0.5)
    vg = proj(_O_VG, _O_GMLP)
    v = vg[:, :RET_V_WIDTH]
    g = vg[:, RET_V_WIDTH:]

    decay = decay_ref[...]
    zeta = zeta_ref[...]
    xi = xi_ref[...]
    cdec = cdec_ref[...]
    bmask = bmask_ref[...]
    gavg = gavg_ref[...]

    o_chunks = []
    state = state_ref[...]
    for c in range(ts // CHUNK):
        rows = slice(c * CHUNK, (c + 1) * CHUNK)
        q_c, k_c, v_c = q[rows], k[rows], v[rows]
        v_m = v_c.astype(MXU_DTYPE)
        q_all = jnp.concatenate([jnp.where(m, q_c, 0.0) for m in qk_masks], axis=0)
        scores = _dot_nt(q_all, k_c) * decay
        o_all = _dot(scores, v_m)
        o = _dot(q_c, state) * xi
        for hd in range(RET_HEADS):
            o = o + jnp.where(v_masks[hd], o_all[hd * CHUNK:(hd + 1) * CHUNK], 0.0)
        kv = _dot((k_c * zeta).T, v_m) * bmask
        state = state * cdec + kv
        o_chunks.append(o)
    state_ref[...] = state
    o = jnp.concatenate(o_chunks, axis=0) if len(o_chunks) > 1 else o_chunks[0]
    mu = _split_dot(o, gavg)
    d = o - mu
    var = _split_dot(d * d, gavg)
    r_br = jax.nn.silu(g) * (d * lax.rsqrt(var + LN_EPS))

    guv = proj(_O_GMLP, _O_CONV)
    u = jax.nn.gelu(guv[:, :GMLP_WIDTH])
    vv = _layernorm_rows(jax.nn.gelu(guv[:, GMLP_WIDTH:]), lng_ref[...], lnb_ref[...]).astype(MXU_DTYPE)
    rr = lax.broadcasted_iota(jnp.int32, (GMLP_GROUPS * CHUNK, CHUNK), 0) % CHUNK
    cc_ = lax.broadcasted_iota(jnp.int32, (GMLP_GROUPS * CHUNK, CHUNK), 1)
    w_sp = jnp.where(cc_ <= rr, wsp_ref[...], 0.0).astype(MXU_DTYPE)
    bsp = bsp_ref[...]
    z_chunks = []
    for c in range(ts // CHUNK):
        z_all = jnp.dot(w_sp, vv[c * CHUNK:(c + 1) * CHUNK], preferred_element_type=F32)
        z = bsp
        for gi in range(GMLP_GROUPS):
            z = z + jnp.where(v_masks[gi], z_all[gi * CHUNK:(gi + 1) * CHUNK], 0.0)
        z_chunks.append(z)
    z = jnp.concatenate(z_chunks, axis=0) if len(z_chunks) > 1 else z_chunks[0]
    s_br = u * z

    cbcx = proj(_O_CONV, _O_CODE)
    gate_b = cbcx[:, :CONV_WIDTH]
    zc = cbcx[:, CONV_WIDTH:2 * CONV_WIDTH] * cbcx[:, 2 * CONV_WIDTH:]
    carry = zc_ref[...]
    prev1 = carry[7:8]
    prev2 = carry[6:7]
    row = lax.broadcasted_iota(jnp.int32, (ts, 1), 0)
    z1 = jnp.where(row == 0, prev1, pltpu.roll(zc, 1, 0))
    z2 = jnp.where(row == 0, prev2, jnp.where(row == 1, prev1, pltpu.roll(zc, 2, 0)))
    zc_ref[...] = zc[ts - 8:ts]
    cw = convw_ref[...]
    k_br = gate_b * (cw[0:1] * z2 + cw[1:2] * z1 + cw[2:3] * zc)

    code = proj(_O_CODE, _IN_WIDTH)
    merged = None
    for i, br in enumerate((r_br, s_br, k_br)):
        y = jnp.dot(br.astype(MXU_DTYPE), wbr_ref[i], preferred_element_type=F32)
        gl = jnp.dot(code[:, i * GATE_RANK:(i + 1) * GATE_RANK].astype(MXU_DTYPE), wgt_ref[i],
                     preferred_element_type=F32) + bgt_ref[i:i + 1]
        t = jax.nn.sigmoid(gl) * y
        merged = t if merged is None else merged + t
    mix = jnp.dot(merged.astype(MXU_DTYPE), wout_ref[...], preferred_element_type=F32)
    x1 = _layernorm_rows(DEEPNORM_ALPHA * x + (1.0 + gt1) * mix, ln1g_ref[...], ln1b_ref[...])
    x1_ref[...] = x1
    h2 = (x1 * (1.0 + sc2) + sh2).astype(MXU_DTYPE)

    logits = jnp.dot(h2, wr_ref[...], preferred_element_type=F32) + br_ref[...]
    lane = lax.broadcasted_iota(jnp.int32, (1, LANES), 1).astype(F32)
    top_v, top_i = [], []
    work = logits
    for _ in range(TOP_K):
        m = jnp.max(work, axis=-1, keepdims=True)
        i_sel = jnp.min(jnp.where(work == m, lane, float(LANES)), axis=-1, keepdims=True)
        top_v.append(m)
        top_i.append(i_sel)
        work = jnp.where(lane == i_sel, -jnp.inf, work)
    exps = [jnp.exp(tv - top_v[0]) for tv in top_v]
    denom = exps[0] + exps[1] + exps[2] + exps[3]
    onehots = [lane == ti for ti in top_i]
    member = jnp.zeros((ts, LANES), F32)
    for oh in onehots:
        member = member + jnp.where(oh, 1.0, 0.0)
    tr = lax.broadcasted_iota(jnp.int32, (ts, ts), 0)
    tc = lax.broadcasted_iota(jnp.int32, (ts, ts), 1)
    before = jnp.where(tc < tr, 1.0, 0.0).astype(MXU_DTYPE)
    cnt = cntacc_ref[...]
    local_rank = jnp.dot(before, member.astype(MXU_DTYPE), preferred_element_type=F32)
    sel = jnp.zeros((ts, LANES), F32)
    for kk in range(TOP_K):
        lrank_k = jnp.sum(jnp.where(onehots[kk], local_rank, 0.0), axis=-1, keepdims=True)
        grank_k = jnp.sum(jnp.where(onehots[kk], local_rank + cnt[0:1], 0.0), axis=-1, keepdims=True)
        sel = sel + jnp.where(lane == float(kk), top_i[kk], 0.0)
        sel = sel + jnp.where(lane == float(TOP_K + kk), exps[kk] / denom, 0.0)
        sel = sel + jnp.where(lane == float(2 * TOP_K + kk), grank_k, 0.0)
        sel = sel + jnp.where(lane == float(3 * TOP_K + kk), lrank_k, 0.0)
    sel_ref[...] = sel
    cnt = cnt + jnp.sum(member, axis=0, keepdims=True)
    cntacc_ref[...] = cnt
    cnt_ref[...] = cnt

    rank_t = jnp.where(member > 0.0, local_rank, -1.0).T
    slot_row = lax.broadcasted_iota(jnp.int32, (SLAB_CAP, ts), 0).astype(F32)
    n_exp = xs_ref.shape[0] // SLAB_CAP
    pick = jnp.concatenate(
        [jnp.where(slot_row == rank_t[e:e + 1], 1.0, 0.0).astype(MXU_DTYPE) for e in range(n_exp)], axis=0)
    xs_ref[...] = jnp.dot(pick, h2, preferred_element_type=F32).astype(xs_ref.dtype)


def _const_spec(shape):
    return pl.BlockSpec(shape, lambda b, s: (0,) * len(shape))


def _mixer_call(x, ada_l, cos_t, sin_t, lw, tables):
    bsz, seq, d = x.shape
    ts = min(SEQ_TILE, seq)
    n_tok = bsz * seq
    row3 = lambda b, s: (b, s, 0)
    in_specs = [
        pl.BlockSpec((None, ts, d), row3),
        pl.BlockSpec((None, 6, d), lambda b, s: (b, 0, 0)),
        pl.BlockSpec((None, ts, RET_QK_WIDTH), row3),
        pl.BlockSpec((None, ts, RET_QK_WIDTH), row3),
    ] + [_const_spec(a.shape) for a in lw] + [_const_spec(a.shape) for a in tables]
    tok_row = lambda b, s: (b * (seq // ts) + s, 0)
    slab_rows = N_EXPERTS * SLAB_CAP
    out_specs = [
        pl.BlockSpec((ts, d), tok_row),
        pl.BlockSpec((slab_rows, d), tok_row),
        pl.BlockSpec((ts, LANES), tok_row),
        pl.BlockSpec((8, LANES), lambda b, s: (0, 0)),
    ]
    out_shape = [
        jax.ShapeDtypeStruct((n_tok, d), F32),
        jax.ShapeDtypeStruct((n_tok // ts * slab_rows, d), MXU_DTYPE),
        jax.ShapeDtypeStruct((n_tok, LANES), F32),
        jax.ShapeDtypeStruct((8, LANES), F32),
    ]
    return pl.pallas_call(
        _mixer_kernel,
        grid=(bsz, seq // ts),
        in_specs=in_specs,
        out_specs=out_specs,
        out_shape=out_shape,
        scratch_shapes=[
            pltpu.VMEM((RET_QK_WIDTH, RET_V_WIDTH), F32),
            pltpu.VMEM((8, CONV_WIDTH), F32),
            pltpu.VMEM((8, LANES), F32),
        ],
        compiler_params=pltpu.CompilerParams(
            dimension_semantics=("arbitrary", "arbitrary"), vmem_limit_bytes=VMEM_LIMIT),
        name="mixer",
    )(x, ada_l, cos_t, sin_t, *lw, *tables)


ROW_GROUP = 8


def _dispatch_kernel(pe_ref, ps_ref, dest_ref, h2_ref, xs_hbm, zbuf, sem, zsem):
    i = pl.program_id(0)
    ts = h2_ref.shape[0]
    blk = zbuf.shape[0]
    n_exp = pe_ref.shape[0]

    @pl.when(i == 0)
    def _():
        zbuf[...] = jnp.zeros_like(zbuf)

        def tail_copy(e):
            start = pl.multiple_of(jnp.maximum(pe_ref[e] - blk, 0), blk)
            return pltpu.make_async_copy(zbuf, xs_hbm.at[pl.ds(start, blk), :], zsem)

        def zstart(e, carry):
            @pl.when(pe_ref[e] > ps_ref[e])
            def _():
                tail_copy(e).start()
            return carry

        def zwait(e, carry):
            @pl.when(pe_ref[e] > ps_ref[e])
            def _():
                tail_copy(e).wait()
            return carry

        lax.fori_loop(0, n_exp, zstart, 0)
        lax.fori_loop(0, n_exp, zwait, 0)

    def row_copy(g, u, kk):
        r = pl.multiple_of(g * ROW_GROUP, ROW_GROUP) + u
        dst = dest_ref[0, 0, (g * ROW_GROUP + u) * TOP_K + kk]
        return pltpu.make_async_copy(h2_ref.at[pl.ds(r, 1), :], xs_hbm.at[pl.ds(dst, 1), :], sem)

    def issue(g, carry):
        for u in range(ROW_GROUP):
            for kk in range(TOP_K):
                row_copy(g, u, kk).start()
        return carry

    def drain(g, carry):
        for u in range(ROW_GROUP):
            for kk in range(TOP_K):
                row_copy(g, u, kk).wait()
        return carry

    lax.fori_loop(0, ts // ROW_GROUP, issue, 0)
    lax.fori_loop(0, ts // ROW_GROUP, drain, 0)


def _dispatch_call(pad_end, pad_start, dest3, h2, n_rows, blk):
    n_tok, d = h2.shape
    n_steps, _, per = dest3.shape
    ts = per // TOP_K
    grid_spec = pltpu.PrefetchScalarGridSpec(
        num_scalar_prefetch=2,
        grid=(n_steps,),
        in_specs=[
            pl.BlockSpec((1, 1, per), lambda i, pe, ps: (i, 0, 0), memory_space=pltpu.SMEM),
            pl.BlockSpec((ts, d), lambda i, pe, ps: (i, 0)),
        ],
        out_specs=pl.BlockSpec(memory_space=pl.ANY),
        scratch_shapes=[pltpu.VMEM((blk, d), F32), pltpu.SemaphoreType.DMA(()),
                        pltpu.SemaphoreType.DMA(())],
    )
    return pl.pallas_call(
        _dispatch_kernel,
        grid_spec=grid_spec,
        out_shape=jax.ShapeDtypeStruct((n_rows, d), F32),
        compiler_params=pltpu.CompilerParams(
            dimension_semantics=("arbitrary",), vmem_limit_bytes=VMEM_LIMIT),
        name="dispatch",
    )(pad_end, pad_start, dest3, h2)


def _expert_mlp(xs, wgu_ref, bgu_ref, wdn_ref, bdn_ref):
    gu = jnp.dot(xs, wgu_ref[...], preferred_element_type=F32) + bgu_ref[...]
    gate = jnp.minimum(gu[:, :D_FF], SWIGLU_LIMIT)
    up = jnp.clip(gu[:, D_FF:], -SWIGLU_LIMIT, SWIGLU_LIMIT)
    act = (up + 1.0) * (gate * jax.nn.sigmoid(SWIGLU_ALPHA * gate))
    return jnp.dot(act.astype(MXU_DTYPE), wdn_ref[...], preferred_element_type=F32) + bdn_ref[...]


def _expert_kernel(be_ref, nu_ref, xs_ref, wgu_ref, bgu_ref, wdn_ref, bdn_ref, ys_ref):
    i = pl.program_id(0)
    n_used = nu_ref[0]

    @pl.when(i < n_used)
    def _():
        ys_ref[...] = _expert_mlp(xs_ref[...].astype(MXU_DTYPE), wgu_ref, bgu_ref, wdn_ref, bdn_ref)

    @pl.when(i >= n_used)
    def _():
        ys_ref[...] = jnp.zeros_like(ys_ref)


def _expert_call(block_e, n_used, xs, w_gu, b_gu, w_down, b_down, blk):
    n_rows, d = xs.shape
    n_blocks = n_rows // blk
    two_f = w_gu.shape[2]
    last = lambda nu: jnp.maximum(nu[0] - 1, 0)
    grid_spec = pltpu.PrefetchScalarGridSpec(
        num_scalar_prefetch=2,
        grid=(n_blocks,),
        in_specs=[
            pl.BlockSpec((blk, d), lambda i, be, nu: (jnp.minimum(i, last(nu)), 0)),
            pl.BlockSpec((None, d, two_f), lambda i, be, nu: (be[i], 0, 0)),
            pl.BlockSpec((None, 1, two_f), lambda i, be, nu: (be[i], 0, 0)),
            pl.BlockSpec((None, two_f // 2, d), lambda i, be, nu: (be[i], 0, 0)),
            pl.BlockSpec((None, 1, d), lambda i, be, nu: (be[i], 0, 0)),
        ],
        out_specs=pl.BlockSpec((blk, d), lambda i, be, nu: (i, 0)),
    )
    n_exp = w_gu.shape[0]
    return pl.pallas_call(
        _expert_kernel,
        grid_spec=grid_spec,
        out_shape=jax.ShapeDtypeStruct((n_rows, d), F32),
        compiler_params=pltpu.CompilerParams(
            dimension_semantics=("arbitrary",), vmem_limit_bytes=VMEM_LIMIT),
        name="experts",
    )(block_e, n_used, xs, w_gu, b_gu.reshape(n_exp, 1, two_f), w_down, b_down.reshape(n_exp, 1, d))


def _combine_kernel(dest_cur_ref, dest_nxt_ref, ys_hbm, x1_ref, sel_ref, ada_ref, lng_ref, lnb_ref,
                    o_ref, gbuf, sem):
    i = pl.program_id(0)
    n_steps = pl.num_programs(0)
    tt = x1_ref.shape[0]
    slot = i % 2

    def row_copy(dest_ref, g, u, kk, dst_slot):
        r = pl.multiple_of(g * ROW_GROUP, ROW_GROUP) + u
        src = dest_ref[0, 0, (g * ROW_GROUP + u) * TOP_K + kk]
        return pltpu.make_async_copy(ys_hbm.at[pl.ds(src, 1), :], gbuf.at[dst_slot, kk, pl.ds(r, 1), :],
                                     sem.at[dst_slot])

    def issue(dest_ref, dst_slot):
        def body(g, carry):
            for u in range(ROW_GROUP):
                for kk in range(TOP_K):
                    row_copy(dest_ref, g, u, kk, dst_slot).start()
            return carry
        lax.fori_loop(0, tt // ROW_GROUP, body, 0)

    @pl.when(i == 0)
    def _():
        issue(dest_cur_ref, 0)

    @pl.when(i + 1 < n_steps)
    def _():
        issue(dest_nxt_ref, 1 - slot)

    def drain(g, carry):
        for u in range(ROW_GROUP):
            for kk in range(TOP_K):
                row_copy(dest_cur_ref, g, u, kk, slot).wait()
        return carry
    lax.fori_loop(0, tt // ROW_GROUP, drain, 0)

    sel = sel_ref[...]
    ffn = None
    for kk in range(TOP_K):
        t = sel[:, TOP_K + kk:TOP_K + kk + 1] * gbuf[slot, kk]
        ffn = t if ffn is None else ffn + t
    ada = ada_ref[...]
    gt2 = ada[5:6]
    o_ref[...] = _layernorm_rows(DEEPNORM_ALPHA * x1_ref[...] + (1.0 + gt2) * ffn, lng_ref[...], lnb_ref[...])


def _combine_call(dest3, ys, x1, sel, ada_l, ln_g, ln_b, seq):
    n_tok, d = x1.shape
    n_steps, _, per = dest3.shape
    tt = per // TOP_K
    tiles_per_seq = seq // tt
    return pl.pallas_call(
        _combine_kernel,
        grid=(n_steps,),
        in_specs=[
            pl.BlockSpec((1, 1, per), lambda i: (i, 0, 0), memory_space=pltpu.SMEM),
            pl.BlockSpec((1, 1, per), lambda i: (jnp.minimum(i + 1, n_steps - 1), 0, 0),
                         memory_space=pltpu.SMEM),
            pl.BlockSpec(memory_space=pl.ANY),
            pl.BlockSpec((tt, d), lambda i: (i, 0)),
            pl.BlockSpec((tt, LANES), lambda i: (i, 0)),
            pl.BlockSpec((None, 6, d), lambda i: (i // tiles_per_seq, 0, 0)),
            pl.BlockSpec((1, d), lambda i: (0, 0)),
            pl.BlockSpec((1, d), lambda i: (0, 0)),
        ],
        out_specs=pl.BlockSpec((tt, d), lambda i: (i, 0)),
        out_shape=jax.ShapeDtypeStruct((n_tok, d), F32),
        scratch_shapes=[pltpu.VMEM((2, TOP_K, tt, d), F32), pltpu.SemaphoreType.DMA((2,))],
        compiler_params=pltpu.CompilerParams(
            dimension_semantics=("arbitrary",), vmem_limit_bytes=VMEM_LIMIT),
        name="combine",
    )(dest3, dest3, ys, x1, sel, ada_l, ln_g.reshape(1, d), ln_b.reshape(1, d))


def _slab_expert_kernel(xs_ref, wgu_ref, bgu_ref, wdn_ref, bdn_ref, ys_ref):
    nt, cap, d = xs_ref.shape
    ys = _expert_mlp(xs_ref[...].reshape(nt * cap, d), wgu_ref, bgu_ref, wdn_ref, bdn_ref)
    ys_ref[...] = ys.reshape(nt, cap, d).astype(ys_ref.dtype)


def _slab_expert_call(xs4, w_gu, b_gu, w_down, b_down):
    n_tiles, n_exp, cap, d = xs4.shape
    nt = min(SLAB_TILES, n_tiles)
    two_f = w_gu.shape[2]
    slab = pl.BlockSpec((nt, None, cap, d), lambda e, g: (g, e, 0, 0))
    return pl.pallas_call(
        _slab_expert_kernel,
        grid=(n_exp, n_tiles // nt),
        in_specs=[
            slab,
            pl.BlockSpec((None, d, two_f), lambda e, g: (e, 0, 0)),
            pl.BlockSpec((None, 1, two_f), lambda e, g: (e, 0, 0)),
            pl.BlockSpec((None, two_f // 2, d), lambda e, g: (e, 0, 0)),
            pl.BlockSpec((None, 1, d), lambda e, g: (e, 0, 0)),
        ],
        out_specs=slab,
        out_shape=jax.ShapeDtypeStruct(xs4.shape, MXU_DTYPE),
        compiler_params=pltpu.CompilerParams(
            dimension_semantics=("arbitrary", "arbitrary"), vmem_limit_bytes=VMEM_LIMIT),
        name="slab_experts",
    )(xs4, w_gu, b_gu.reshape(n_exp, 1, two_f), w_down, b_down.reshape(n_exp, 1, d))


def _slab_combine_kernel(ys_ref, x1_ref, sel_ref, ada_ref, lng_ref, lnb_ref, o_ref):
    n_exp, cap, d = ys_ref.shape
    sel = sel_ref[...]
    col = lax.broadcasted_iota(jnp.int32, (1, n_exp * cap), 1).astype(F32)
    weights = None
    for kk in range(TOP_K):
        target = sel[:, kk:kk + 1] * float(cap) + sel[:, 3 * TOP_K + kk:3 * TOP_K + kk + 1]
        term = jnp.where(col == target, sel[:, TOP_K + kk:TOP_K + kk + 1], 0.0)
        weights = term if weights is None else weights + term
    ffn = _split_dot(weights, ys_ref[...].reshape(n_exp * cap, d))
    gt2 = ada_ref[...][5:6]
    o_ref[...] = _layernorm_rows(DEEPNORM_ALPHA * x1_ref[...] + (1.0 + gt2) * ffn, lng_ref[...], lnb_ref[...])


def _slab_combine_call(ys4, x1, sel, ada_l, ln_g, ln_b, seq):
    n_tiles, n_exp, cap, d = ys4.shape
    n_tok = x1.shape[0]
    ts = n_tok // n_tiles
    tiles_per_seq = seq // ts
    return pl.pallas_call(
        _slab_combine_kernel,
        grid=(n_tiles,),
        in_specs=[
            pl.BlockSpec((None, n_exp, cap, d), lambda i: (i, 0, 0, 0)),
            pl.BlockSpec((ts, d), lambda i: (i, 0)),
            pl.BlockSpec((ts, LANES), lambda i: (i, 0)),
            pl.BlockSpec((None, 6, d), lambda i: (i // tiles_per_seq, 0, 0)),
            pl.BlockSpec((1, d), lambda i: (0, 0)),
            pl.BlockSpec((1, d), lambda i: (0, 0)),
        ],
        out_specs=pl.BlockSpec((ts, d), lambda i: (i, 0)),
        out_shape=jax.ShapeDtypeStruct((n_tok, d), F32),
        compiler_params=pltpu.CompilerParams(
            dimension_semantics=("arbitrary",), vmem_limit_bytes=VMEM_LIMIT),
        name="slab_combine",
    )(ys4, x1, sel, ada_l, ln_g.reshape(1, d), ln_b.reshape(1, d))


def _retention_tables():
    log_gamma = jnp.log(1.0 - 2.0 ** (-5.0 - jnp.arange(RET_HEADS, dtype=F32)))
    idx = jnp.arange(CHUNK, dtype=F32)
    diff = idx[:, None] - idx[None, :]
    decay = jnp.where(diff[None] >= 0,
                      jnp.exp(log_gamma[:, None, None] * jnp.maximum(diff, 0.0)[None]), 0.0)
    decay = decay.reshape(RET_HEADS * CHUNK, CHUNK)
    zeta = jnp.exp(log_gamma[:, None] * (CHUNK - 1.0 - idx)[None])
    zeta_t = jnp.repeat(zeta.T, RET_QK_DIM, axis=1)
    xi = jnp.exp(log_gamma[:, None] * (idx + 1.0)[None]).T
    xi_t = jnp.repeat(xi, RET_V_DIM, axis=1)
    cdec = jnp.repeat(jnp.exp(log_gamma * CHUNK), RET_V_DIM)[None, :]
    row_head = jnp.arange(RET_QK_WIDTH) // RET_QK_DIM
    col_head = jnp.arange(RET_V_WIDTH) // RET_V_DIM
    same_head = row_head[:, None] == col_head[None, :]
    bmask = same_head.astype(F32)
    gavg = ((col_head[:, None] == col_head[None, :]).astype(F32) / RET_V_DIM).astype(MXU_DTYPE)
    return decay, zeta_t, xi_t, cdec, bmask, gavg


def _rotary_tables(positions):
    inv_freq = ROPE_BASE ** (-jnp.arange(0, RET_QK_DIM, 2, dtype=F32) / RET_QK_DIM)
    ang = positions.astype(F32)[..., None] * inv_freq
    cos = jnp.cos(ang)
    sin = jnp.sin(ang)
    cos_t = jnp.tile(jnp.concatenate([cos, cos], axis=-1), (1, 1, RET_HEADS))
    sin_t = jnp.tile(jnp.concatenate([-sin, sin], axis=-1), (1, 1, RET_HEADS))
    return cos_t, sin_t


def _routing_tables(sel, counts, n_tok, blk):
    n_asg = n_tok * TOP_K
    n_blocks = n_asg // blk + N_EXPERTS
    sizes = counts.astype(jnp.int32)
    pad_sizes = (sizes + blk - 1) // blk * blk
    pad_end = jnp.cumsum(pad_sizes)
    pad_start = pad_end - pad_sizes
    idx = sel[:, 0:TOP_K].astype(jnp.int32)
    rank = sel[:, 2 * TOP_K:3 * TOP_K].astype(jnp.int32)
    onehot = idx[:, :, None] == jnp.arange(N_EXPERTS, dtype=jnp.int32)
    dest = jnp.sum(jnp.where(onehot, pad_start, 0), axis=-1) + rank
    n_used = (pad_end[-1] // blk).astype(jnp.int32)
    block_start = jnp.arange(n_blocks, dtype=jnp.int32) * blk
    block_start = jnp.minimum(block_start, (n_used - 1) * blk)
    block_e = jnp.sum((pad_end[None, :] <= block_start[:, None]).astype(jnp.int32), axis=-1)
    block_e = jnp.minimum(block_e, N_EXPERTS - 1)
    return dest, pad_end.astype(jnp.int32), pad_start.astype(jnp.int32), block_e, n_used.reshape(1)


def kernel(x, c, positions, w_ada, b_ada, w_in, gmlp_ln_g, gmlp_ln_b, w_spatial, b_spatial, conv_w,
           w_branch, w_gate_up, b_gate, w_out, ln1_g, ln1_b, w_router, b_router, w_gu, b_gu, w_down,
           b_down, ln2_g, ln2_b):
    bsz, seq, d = x.shape
    n_tok = bsz * seq
    depth = w_ada.shape[0]
    blk = min(MOE_BLOCK, n_tok * TOP_K // N_EXPERTS)
    tt = min(COMBINE_TILE, seq)
    ts = min(SEQ_TILE, seq)

    ada = _ada_call(c, w_ada, b_ada)
    ada = jnp.transpose(ada, (0, 2, 1, 3))
    cos_t, sin_t = _rotary_tables(positions)
    tables = _retention_tables()

    mx = MXU_DTYPE
    w_in_m = w_in.astype(mx)
    w_branch_m = w_branch.astype(mx)
    w_gate_m = w_gate_up.astype(mx)
    w_out_m = w_out.astype(mx)
    w_gu_m = w_gu.astype(mx)
    w_down_m = w_down.astype(mx)
    n_exp = w_router.shape[-1]
    w_router_m = jnp.pad(w_router, ((0, 0), (0, 0), (0, LANES - n_exp))).astype(mx)
    b_router_p = jnp.pad(b_router, ((0, 0), (0, LANES - n_exp)), constant_values=NEG_BIG)

    for l in range(depth):
        bsp_t = jnp.repeat(b_spatial[l].T, GMLP_WIDTH // GMLP_GROUPS, axis=1)
        lw = (
            w_in_m[l],
            gmlp_ln_g[l].reshape(1, -1), gmlp_ln_b[l].reshape(1, -1),
            w_spatial[l].reshape(GMLP_GROUPS * CHUNK, CHUNK), bsp_t,
            jnp.pad(conv_w[l], ((0, 8 - conv_w.shape[1]), (0, 0))),
            w_branch_m[l], w_gate_m[l], b_gate[l], w_out_m[l],
            ln1_g[l].reshape(1, -1), ln1_b[l].reshape(1, -1),
            w_router_m[l], b_router_p[l].reshape(1, -1),
        )
        x1, xs_tiles, sel, cnt = _mixer_call(x, ada[l], cos_t, sin_t, lw, tables)
        moe_w = (w_gu_m[l], b_gu[l], w_down_m[l], b_down[l])

        def tile_local_moe(x1, xs_tiles, sel, cnt, ada_l, ln_g, ln_b, moe_w=moe_w):
            xs4 = xs_tiles.reshape(n_tok // ts, n_exp, SLAB_CAP, d)
            ys4 = _slab_expert_call(xs4, *moe_w)
            return _slab_combine_call(ys4, x1, sel, ada_l, ln_g, ln_b, seq)

        def global_moe(x1, xs_tiles, sel, cnt, ada_l, ln_g, ln_b, moe_w=moe_w):
            sh2 = jnp.repeat(ada_l[:, 3], seq, axis=0)
            sc2 = jnp.repeat(ada_l[:, 4], seq, axis=0)
            h2 = (x1 * (1.0 + sc2) + sh2).astype(mx).astype(F32)
            dest, pad_end, pad_start, block_e, n_used = _routing_tables(sel, cnt[0, :n_exp], n_tok, blk)
            n_rows = (n_tok * TOP_K // blk + n_exp) * blk
            xs = _dispatch_call(pad_end, pad_start, dest.reshape(n_tok // ts, 1, ts * TOP_K), h2, n_rows, blk)
            ys = _expert_call(block_e, n_used, xs, *moe_w, blk)
            return _combine_call(dest.reshape(n_tok // tt, 1, tt * TOP_K), ys, x1, sel, ada_l, ln_g, ln_b, seq)

        fits = jnp.max(sel[:, 3 * TOP_K:4 * TOP_K]) < SLAB_CAP
        x = lax.cond(fits, tile_local_moe, global_moe, x1, xs_tiles, sel, cnt, ada[l], ln2_g[l], ln2_b[l])
        x = x.reshape(bsz, seq, d)
    return x
```

```python
import functools

import jax
import jax.numpy as jnp
from jax import lax
from jax.experimental import pallas as pl
from jax.experimental.pallas import tpu as pltpu

F32 = jnp.float32
MXU_DTYPE = jnp.bfloat16

DEPTH = 4
RET_HEADS = 4
RET_QK_DIM = 32
RET_V_DIM = 64
RET_QK_WIDTH = RET_HEADS * RET_QK_DIM
RET_V_WIDTH = RET_HEADS * RET_V_DIM
CHUNK = 128
ROPE_BASE = 10000.0
GMLP_GROUPS = 4
GMLP_WIDTH = 256
CONV_WIDTH = 256
GATE_RANK = 128
N_BRANCH = 3
N_EXPERTS = 32
TOP_K = 4
D_FF = 256
SWIGLU_LIMIT = 7.0
SWIGLU_ALPHA = 1.702
DEEPNORM_ALPHA = (2.0 * DEPTH) ** 0.25
LN_EPS = 1e-5

_O_QK = 0
_O_VG = 2 * RET_QK_WIDTH
_O_GMLP = _O_VG + 2 * RET_V_WIDTH
_O_CONV = _O_GMLP + 2 * GMLP_WIDTH
_O_CODE = _O_CONV + 3 * CONV_WIDTH
_IN_WIDTH = _O_CODE + N_BRANCH * GATE_RANK

LANES = 128
NEG_BIG = -1e30

SEQ_TILE = 256
MOE_BLOCK = 512
RUN_ALIGN = 8
VMEM_LIMIT = 56 * 1024 * 1024


def _dot(a, b):
    return jnp.dot(a.astype(MXU_DTYPE), b.astype(MXU_DTYPE), preferred_element_type=F32)


def _dot_nt(a, b):
    return lax.dot_general(a.astype(MXU_DTYPE), b.astype(MXU_DTYPE),
                           (((1,), (1,)), ((), ())), preferred_element_type=F32)


def _split_dot(x, w):
    hi = x.astype(MXU_DTYPE)
    lo = (x - hi.astype(F32)).astype(MXU_DTYPE)
    return (jnp.dot(hi, w, preferred_element_type=F32)
            + jnp.dot(lo, w, preferred_element_type=F32))


def _layernorm_rows(x, g, b):
    mu = jnp.mean(x, axis=-1, keepdims=True)
    d = x - mu
    var = jnp.mean(d * d, axis=-1, keepdims=True)
    return d * lax.rsqrt(var + LN_EPS) * g + b


def _ada_kernel(c_ref, w_ref, b_ref, o_ref):
    c_act = jax.nn.silu(c_ref[...])
    o_ref[...] = jnp.dot(c_act, w_ref[...], preferred_element_type=F32,
                         precision=lax.Precision.HIGHEST) + b_ref[...]


def _ada_call(c, w_ada, b_ada):
    depth, d, six_d = w_ada.shape
    bsz = c.shape[0]
    n_col = six_d // d
    return pl.pallas_call(
        _ada_kernel,
        grid=(depth, n_col),
        in_specs=[
            pl.BlockSpec((bsz, d), lambda l, j: (0, 0)),
            pl.BlockSpec((None, d, d), lambda l, j: (l, 0, j)),
            pl.BlockSpec((None, 1, d), lambda l, j: (l, 0, j)),
        ],
        out_specs=pl.BlockSpec((None, None, bsz, d), lambda l, j: (l, j, 0, 0)),
        out_shape=jax.ShapeDtypeStruct((depth, n_col, bsz, d), F32),
        compiler_params=pltpu.CompilerParams(
            dimension_semantics=("arbitrary", "arbitrary"), vmem_limit_bytes=VMEM_LIMIT),
        name="ada_ln",
    )(c, w_ada, b_ada.reshape(depth, 1, six_d))


def _mixer_kernel(x_ref, ada_ref, cos_ref, sin_ref, w_in_ref, lng_ref, lnb_ref, wsp_ref, bsp_ref,
                  convw_ref, wbr_ref, wgt_ref, bgt_ref, wout_ref, ln1g_ref, ln1b_ref, wr_ref, br_ref,
                  decay_ref, zeta_ref, xi_ref, cdec_ref, bmask_ref, gavg_ref,
                  x1_ref, h2_ref, sel_ref, cnt_ref, before_ref,
                  state_ref, zc_ref, cntacc_ref):
    b = pl.program_id(0)
    s = pl.program_id(1)
    ts = x_ref.shape[0]

    @pl.when(s == 0)
    def _():
        state_ref[...] = jnp.zeros_like(state_ref)
        zc_ref[...] = jnp.zeros_like(zc_ref)

    @pl.when(jnp.logical_and(b == 0, s == 0))
    def _():
        cntacc_ref[...] = jnp.zeros_like(cntacc_ref)

    x = x_ref[...]
    ada = ada_ref[...]
    sh1, sc1, gt1, sh2, sc2 = ada[0:1], ada[1:2], ada[2:3], ada[3:4], ada[4:5]
    h = (x * (1.0 + sc1) + sh1).astype(MXU_DTYPE)

    def proj(lo, hi):
        return jnp.dot(h, w_in_ref[:, lo:hi], preferred_element_type=F32)

    lane_qk = lax.broadcasted_iota(jnp.int32, (1, RET_QK_WIDTH), 1)
    lane_v = lax.broadcasted_iota(jnp.int32, (1, RET_V_WIDTH), 1)
    qk_masks = [lane_qk // RET_QK_DIM == hd for hd in range(RET_HEADS)]
    v_masks = [lane_v // RET_V_DIM == hd for hd in range(RET_HEADS)]

    qk = proj(_O_QK, _O_VG)
    cos = cos_ref[...]
    sin = sin_ref[...]
    first_half = (lane_qk % RET_QK_DIM) < (RET_QK_DIM // 2)

    def rotary(t):
        swapped = jnp.where(first_half, pltpu.roll(t, RET_QK_WIDTH - RET_QK_DIM // 2, 1),
                            pltpu.roll(t, RET_QK_DIM // 2, 1))
        return t * cos + swapped * sin

    q = rotary(qk[:, :RET_QK_WIDTH])
    k = rotary(qk[:, RET_QK_WIDTH:]) * (RET_QK_DIM ** -0.5)
    vg = proj(_O_VG, _O_GMLP)
    v = vg[:, :RET_V_WIDTH]
    g = vg[:, RET_V_WIDTH:]

    decay = decay_ref[...]
    zeta = zeta_ref[...]
    xi = xi_ref[...]
    cdec = cdec_ref[...]
    bmask = bmask_ref[...]
    gavg = gavg_ref[...]

    o_chunks = []
    state = state_ref[...]
    for c in range(ts // CHUNK):
        rows = slice(c * CHUNK, (c + 1) * CHUNK)
        q_c, k_c, v_c = q[rows], k[rows], v[rows]
        v_m = v_c.astype(MXU_DTYPE)
        q_all = jnp.concatenate([jnp.where(m, q_c, 0.0) for m in qk_masks], axis=0)
        scores = _dot_nt(q_all, k_c) * decay
        o_all = _dot(scores, v_m)
        o = _dot(q_c, state) * xi
        for hd in range(RET_HEADS):
            o = o + jnp.where(v_masks[hd], o_all[hd * CHUNK:(hd + 1) * CHUNK], 0.0)
        kv = _dot((k_c * zeta).T, v_m) * bmask
        state = state * cdec + kv
        o_chunks.append(o)
    state_ref[...] = state
    o = jnp.concatenate(o_chunks, axis=0) if len(o_chunks) > 1 else o_chunks[0]
    mu = _split_dot(o, gavg)
    d = o - mu
    var = _split_dot(d * d, gavg)
    r_br = jax.nn.silu(g) * (d * lax.rsqrt(var + LN_EPS))

    guv = proj(_O_GMLP, _O_CONV)
    u = jax.nn.gelu(guv[:, :GMLP_WIDTH])
    vv = _layernorm_rows(jax.nn.gelu(guv[:, GMLP_WIDTH:]), lng_ref[...], lnb_ref[...]).astype(MXU_DTYPE)
    rr = lax.broadcasted_iota(jnp.int32, (GMLP_GROUPS * CHUNK, CHUNK), 0) % CHUNK
    cc_ = lax.broadcasted_iota(jnp.int32, (GMLP_GROUPS * CHUNK, CHUNK), 1)
    w_sp = jnp.where(cc_ <= rr, wsp_ref[...], 0.0).astype(MXU_DTYPE)
    bsp = bsp_ref[...]
    z_chunks = []
    for c in range(ts // CHUNK):
        z_all = jnp.dot(w_sp, vv[c * CHUNK:(c + 1) * CHUNK], preferred_element_type=F32)
        z = bsp
        for gi in range(GMLP_GROUPS):
            z = z + jnp.where(v_masks[gi], z_all[gi * CHUNK:(gi + 1) * CHUNK], 0.0)
        z_chunks.append(z)
    z = jnp.concatenate(z_chunks, axis=0) if len(z_chunks) > 1 else z_chunks[0]
    s_br = u * z

    cbcx = proj(_O_CONV, _O_CODE)
    gate_b = cbcx[:, :CONV_WIDTH]
    zc = cbcx[:, CONV_WIDTH:2 * CONV_WIDTH] * cbcx[:, 2 * CONV_WIDTH:]
    carry = zc_ref[...]
    prev1 = carry[7:8]
    prev2 = carry[6:7]
    row = lax.broadcasted_iota(jnp.int32, (ts, 1), 0)
    z1 = jnp.where(row == 0, prev1, pltpu.roll(zc, 1, 0))
    z2 = jnp.where(row == 0, prev2, jnp.where(row == 1, prev1, pltpu.roll(zc, 2, 0)))
    zc_ref[...] = zc[ts - 8:ts]
    cw = convw_ref[...]
    k_br = gate_b * (cw[0:1] * z2 + cw[1:2] * z1 + cw[2:3] * zc)

    code = proj(_O_CODE, _IN_WIDTH)
    merged = None
    for i, br in enumerate((r_br, s_br, k_br)):
        y = jnp.dot(br.astype(MXU_DTYPE), wbr_ref[i], preferred_element_type=F32)
        gl = jnp.dot(code[:, i * GATE_RANK:(i + 1) * GATE_RANK].astype(MXU_DTYPE), wgt_ref[i],
                     preferred_element_type=F32) + bgt_ref[i:i + 1]
        t = jax.nn.sigmoid(gl) * y
        merged = t if merged is None else merged + t
    mix = jnp.dot(merged.astype(MXU_DTYPE), wout_ref[...], preferred_element_type=F32)
    x1 = _layernorm_rows(DEEPNORM_ALPHA * x + (1.0 + gt1) * mix, ln1g_ref[...], ln1b_ref[...])
    x1_ref[...] = x1
    h2 = x1 * (1.0 + sc2) + sh2
    h2_ref[...] = h2

    logits = jnp.dot(h2.astype(MXU_DTYPE), wr_ref[...], preferred_element_type=F32) + br_ref[...]
    lane = lax.broadcasted_iota(jnp.int32, (1, LANES), 1).astype(F32)
    top_v, top_i = [], []
    work = logits
    for _ in range(TOP_K):
        m = jnp.max(work, axis=-1, keepdims=True)
        i_sel = jnp.min(jnp.where(work == m, lane, float(LANES)), axis=-1, keepdims=True)
        top_v.append(m)
        top_i.append(i_sel)
        work = jnp.where(lane == i_sel, -jnp.inf, work)
    exps = [jnp.exp(tv - top_v[0]) for tv in top_v]
    denom = exps[0] + exps[1] + exps[2] + exps[3]
    onehots = [lane == ti for ti in top_i]
    member = jnp.zeros((ts, LANES), F32)
    for oh in onehots:
        member = member + jnp.where(oh, 1.0, 0.0)
    tr = lax.broadcasted_iota(jnp.int32, (ts, ts), 0)
    tc = lax.broadcasted_iota(jnp.int32, (ts, ts), 1)
    before = jnp.where(tc < tr, 1.0, 0.0).astype(MXU_DTYPE)
    cnt = cntacc_ref[...]
    local_rank = jnp.dot(before, member.astype(MXU_DTYPE), preferred_element_type=F32)
    sel = jnp.zeros((ts, LANES), F32)
    for kk in range(TOP_K):
        lrank_k = jnp.sum(jnp.where(onehots[kk], local_rank, 0.0), axis=-1, keepdims=True)
        sel = sel + jnp.where(lane == float(kk), top_i[kk], 0.0)
        sel = sel + jnp.where(lane == float(TOP_K + kk), exps[kk] / denom, 0.0)
        sel = sel + jnp.where(lane == float(2 * TOP_K + kk), lrank_k, 0.0)
    sel_ref[...] = sel
    before_ref[...] = cnt
    tile_cnt = jnp.sum(member, axis=0, keepdims=True)
    cnt = cnt + jnp.floor((tile_cnt + (RUN_ALIGN - 1.0)) * (1.0 / RUN_ALIGN)) * RUN_ALIGN
    cntacc_ref[...] = cnt
    cnt_ref[...] = cnt


def _const_spec(shape):
    return pl.BlockSpec(shape, lambda b, s: (0,) * len(shape))


def _mixer_call(x, ada_l, cos_t, sin_t, lw, tables):
    bsz, seq, d = x.shape
    ts = min(SEQ_TILE, seq)
    n_tok = bsz * seq
    row3 = lambda b, s: (b, s, 0)
    in_specs = [
        pl.BlockSpec((None, ts, d), row3),
        pl.BlockSpec((None, 6, d), lambda b, s: (b, 0, 0)),
        pl.BlockSpec((None, ts, RET_QK_WIDTH), row3),
        pl.BlockSpec((None, ts, RET_QK_WIDTH), row3),
    ] + [_const_spec(a.shape) for a in lw] + [_const_spec(a.shape) for a in tables]
    tok_row = lambda b, s: (b * (seq // ts) + s, 0)
    out_specs = [
        pl.BlockSpec((ts, d), tok_row),
        pl.BlockSpec((ts, d), tok_row),
        pl.BlockSpec((ts, LANES), tok_row),
        pl.BlockSpec((8, LANES), lambda b, s: (0, 0)),
        pl.BlockSpec((None, 8, LANES), lambda b, s: (b * (seq // ts) + s, 0, 0)),
    ]
    out_shape = [
        jax.ShapeDtypeStruct((n_tok, d), F32),
        jax.ShapeDtypeStruct((n_tok, d), F32),
        jax.ShapeDtypeStruct((n_tok, LANES), F32),
        jax.ShapeDtypeStruct((8, LANES), F32),
        jax.ShapeDtypeStruct((n_tok // ts, 8, LANES), F32),
    ]
    return pl.pallas_call(
        _mixer_kernel,
        grid=(bsz, seq // ts),
        in_specs=in_specs,
        out_specs=out_specs,
        out_shape=out_shape,
        scratch_shapes=[
            pltpu.VMEM((RET_QK_WIDTH, RET_V_WIDTH), F32),
            pltpu.VMEM((8, CONV_WIDTH), F32),
            pltpu.VMEM((8, LANES), F32),
        ],
        compiler_params=pltpu.CompilerParams(
            dimension_semantics=("arbitrary", "arbitrary"), vmem_limit_bytes=VMEM_LIMIT),
        name="mixer",
    )(x, ada_l, cos_t, sin_t, *lw, *tables)


def _run_pieces(max_rows):
    units = max_rows // RUN_ALIGN
    pieces = []
    bit = 1
    while bit <= units:
        pieces.append(bit)
        bit *= 2
    return tuple(reversed(pieces))


def _for_each_run_piece(tab_ref, pieces, make_copy, fn):
    for e in range(N_EXPERTS):
        start = tab_ref[0, 0, e]
        units = tab_ref[0, 0, N_EXPERTS + e]
        off = tab_ref[0, 0, 2 * N_EXPERTS + e]
        for bit in pieces:
            done = jnp.bitwise_and(units, -2 * bit) * RUN_ALIGN

            @pl.when(jnp.bitwise_and(units, bit) != 0)
            def _(start=start, off=off, done=done, rows=bit * RUN_ALIGN):
                fn(make_copy(pl.multiple_of(start + done, RUN_ALIGN), pl.multiple_of(off + done, RUN_ALIGN), rows))


def _run_positions(sel, off_row):
    lane = lax.broadcasted_iota(jnp.int32, (1, LANES), 1).astype(F32)
    out = []
    for kk in range(TOP_K):
        base = jnp.sum(jnp.where(lane == sel[:, kk:kk + 1], off_row, 0.0), axis=-1, keepdims=True)
        out.append(base + sel[:, 2 * TOP_K + kk:2 * TOP_K + kk + 1])
    return out


def _dispatch_kernel(pe_ref, ps_ref, tab_cur_ref, tab_prev_ref, off_ref, sel_ref, h2_ref, xs_hbm,
                     zbuf, sbuf, sem, zsem):
    i = pl.program_id(0)
    n_steps = pl.num_programs(0)
    ts = h2_ref.shape[0]
    blk = zbuf.shape[0]
    n_exp = pe_ref.shape[0]
    run_rows = sbuf.shape[1]
    slot = i % 2
    pieces = _run_pieces(ts)

    @pl.when(i == 0)
    def _():
        zbuf[...] = jnp.zeros_like(zbuf)

        def tail_copy(e):
            start = pl.multiple_of(jnp.maximum(pe_ref[e] - blk, 0), blk)
            return pltpu.make_async_copy(zbuf, xs_hbm.at[pl.ds(start, blk), :], zsem)

        def zstart(e, carry):
            @pl.when(pe_ref[e] > ps_ref[e])
            def _():
                tail_copy(e).start()
            return carry

        def zwait(e, carry):
            @pl.when(pe_ref[e] > ps_ref[e])
            def _():
                tail_copy(e).wait()
            return carry

        lax.fori_loop(0, n_exp, zstart, 0)
        lax.fori_loop(0, n_exp, zwait, 0)

    targets = _run_positions(sel_ref[...], off_ref[...][0:1])
    lane = lax.broadcasted_iota(jnp.int32, (1, LANES), 1).astype(F32)
    packed = jnp.zeros((ts, LANES), F32)
    for kk in range(TOP_K):
        packed = packed + jnp.where(lane == float(kk), targets[kk], 0.0)
    targets_t = packed.T
    buf_row = lax.broadcasted_iota(jnp.int32, (run_rows, ts), 0).astype(F32)
    pick = jnp.zeros((run_rows, ts), F32)
    for kk in range(TOP_K):
        pick = jnp.where(buf_row == targets_t[kk:kk + 1], 1.0, pick)
    sbuf[slot] = jnp.dot(pick.astype(MXU_DTYPE), h2_ref[...].astype(MXU_DTYPE), preferred_element_type=F32)

    def push(dst_slot):
        def make(hbm_row, buf_row_, rows):
            return pltpu.make_async_copy(sbuf.at[dst_slot, pl.ds(buf_row_, rows), :],
                                         xs_hbm.at[pl.ds(hbm_row, rows), :], sem.at[dst_slot])
        return make

    _for_each_run_piece(tab_cur_ref, pieces, push(slot), lambda c: c.start())

    @pl.when(i > 0)
    def _():
        _for_each_run_piece(tab_prev_ref, pieces, push(1 - slot), lambda c: c.wait())

    @pl.when(i == n_steps - 1)
    def _():
        _for_each_run_piece(tab_cur_ref, pieces, push(slot), lambda c: c.wait())


def _dispatch_call(pad_end, pad_start, tab3, off3, sel, h2, n_rows, blk):
    n_tok, d = h2.shape
    n_steps = tab3.shape[0]
    ts = n_tok // n_steps
    run_rows = ts * TOP_K + N_EXPERTS * RUN_ALIGN
    grid_spec = pltpu.PrefetchScalarGridSpec(
        num_scalar_prefetch=2,
        grid=(n_steps,),
        in_specs=[
            pl.BlockSpec((1, 1, LANES), lambda i, pe, ps: (i, 0, 0), memory_space=pltpu.SMEM),
            pl.BlockSpec((1, 1, LANES), lambda i, pe, ps: (jnp.maximum(i - 1, 0), 0, 0),
                         memory_space=pltpu.SMEM),
            pl.BlockSpec((None, 8, LANES), lambda i, pe, ps: (i, 0, 0)),
            pl.BlockSpec((ts, LANES), lambda i, pe, ps: (i, 0)),
            pl.BlockSpec((ts, d), lambda i, pe, ps: (i, 0)),
        ],
        out_specs=pl.BlockSpec(memory_space=pl.ANY),
        scratch_shapes=[pltpu.VMEM((blk, d), F32), pltpu.VMEM((2, run_rows, d), F32),
                        pltpu.SemaphoreType.DMA((2,)), pltpu.SemaphoreType.DMA(())],
    )
    return pl.pallas_call(
        _dispatch_kernel,
        grid_spec=grid_spec,
        out_shape=jax.ShapeDtypeStruct((n_rows, d), F32),
        compiler_params=pltpu.CompilerParams(
            dimension_semantics=("arbitrary",), vmem_limit_bytes=VMEM_LIMIT),
        name="dispatch",
    )(pad_end, pad_start, tab3, tab3, off3, sel, h2)


def _expert_mlp(xs, wgu_ref, bgu_ref, wdn_ref, bdn_ref):
    gu = jnp.dot(xs, wgu_ref[...], preferred_element_type=F32) + bgu_ref[...]
    gate = jnp.minimum(gu[:, :D_FF], SWIGLU_LIMIT)
    up = jnp.clip(gu[:, D_FF:], -SWIGLU_LIMIT, SWIGLU_LIMIT)
    act = (up + 1.0) * (gate * jax.nn.sigmoid(SWIGLU_ALPHA * gate))
    return jnp.dot(act.astype(MXU_DTYPE), wdn_ref[...], preferred_element_type=F32) + bdn_ref[...]


def _expert_kernel(be_ref, nu_ref, xs_ref, wgu_ref, bgu_ref, wdn_ref, bdn_ref, ys_ref):
    i = pl.program_id(0)
    n_used = nu_ref[0]

    @pl.when(i < n_used)
    def _():
        ys_ref[...] = _expert_mlp(xs_ref[...].astype(MXU_DTYPE), wgu_ref, bgu_ref, wdn_ref, bdn_ref)

    @pl.when(i >= n_used)
    def _():
        ys_ref[...] = jnp.zeros_like(ys_ref)


def _expert_call(block_e, n_used, xs, w_gu, b_gu, w_down, b_down, blk):
    n_rows, d = xs.shape
    n_blocks = n_rows // blk
    two_f = w_gu.shape[2]
    last = lambda nu: jnp.maximum(nu[0] - 1, 0)
    grid_spec = pltpu.PrefetchScalarGridSpec(
        num_scalar_prefetch=2,
        grid=(n_blocks,),
        in_specs=[
            pl.BlockSpec((blk, d), lambda i, be, nu: (jnp.minimum(i, last(nu)), 0)),
            pl.BlockSpec((None, d, two_f), lambda i, be, nu: (be[i], 0, 0)),
            pl.BlockSpec((None, 1, two_f), lambda i, be, nu: (be[i], 0, 0)),
            pl.BlockSpec((None, two_f // 2, d), lambda i, be, nu: (be[i], 0, 0)),
            pl.BlockSpec((None, 1, d), lambda i, be, nu: (be[i], 0, 0)),
        ],
        out_specs=pl.BlockSpec((blk, d), lambda i, be, nu: (i, 0)),
    )
    n_exp = w_gu.shape[0]
    return pl.pallas_call(
        _expert_kernel,
        grid_spec=grid_spec,
        out_shape=jax.ShapeDtypeStruct((n_rows, d), F32),
        compiler_params=pltpu.CompilerParams(
            dimension_semantics=("arbitrary",), vmem_limit_bytes=VMEM_LIMIT),
        name="experts",
    )(block_e, n_used, xs, w_gu, b_gu.reshape(n_exp, 1, two_f), w_down, b_down.reshape(n_exp, 1, d))


def _combine_kernel(tab_cur_ref, tab_nxt_ref, ys_hbm, off_ref, x1_ref, sel_ref, ada_ref, lng_ref, lnb_ref,
                    o_ref, rbuf, sem):
    i = pl.program_id(0)
    n_steps = pl.num_programs(0)
    ts = x1_ref.shape[0]
    slot = i % 2
    pieces = _run_pieces(ts)

    def fetch(dst_slot):
        def make(hbm_row, buf_row, rows):
            return pltpu.make_async_copy(ys_hbm.at[pl.ds(hbm_row, rows), :],
                                         rbuf.at[dst_slot, pl.ds(buf_row, rows), :], sem.at[dst_slot])
        return make

    @pl.when(i == 0)
    def _():
        rbuf[...] = jnp.zeros_like(rbuf)
        _for_each_run_piece(tab_cur_ref, pieces, fetch(0), lambda c: c.start())

    @pl.when(i + 1 < n_steps)
    def _():
        _for_each_run_piece(tab_nxt_ref, pieces, fetch(1 - slot), lambda c: c.start())

    _for_each_run_piece(tab_cur_ref, pieces, fetch(slot), lambda c: c.wait())

    sel = sel_ref[...]
    targets = _run_positions(sel, off_ref[...][0:1])
    col = lax.broadcasted_iota(jnp.int32, (1, rbuf.shape[1]), 1).astype(F32)
    weights = None
    for kk in range(TOP_K):
        term = jnp.where(col == targets[kk], sel[:, TOP_K + kk:TOP_K + kk + 1], 0.0)
        weights = term if weights is None else weights + term
    ffn = _split_dot(weights, rbuf[slot].astype(MXU_DTYPE))
    gt2 = ada_ref[...][5:6]
    o_ref[...] = _layernorm_rows(DEEPNORM_ALPHA * x1_ref[...] + (1.0 + gt2) * ffn, lng_ref[...], lnb_ref[...])


def _combine_call(tab3, off3, ys, x1, sel, ada_l, ln_g, ln_b, seq):
    n_tok, d = x1.shape
    n_steps = tab3.shape[0]
    ts = n_tok // n_steps
    tiles_per_seq = seq // ts
    run_rows = ts * TOP_K + N_EXPERTS * RUN_ALIGN
    return pl.pallas_call(
        _combine_kernel,
        grid=(n_steps,),
        in_specs=[
            pl.BlockSpec((1, 1, LANES), lambda i: (i, 0, 0), memory_space=pltpu.SMEM),
            pl.BlockSpec((1, 1, LANES), lambda i: (jnp.minimum(i + 1, n_steps - 1), 0, 0),
                         memory_space=pltpu.SMEM),
            pl.BlockSpec(memory_space=pl.ANY),
            pl.BlockSpec((None, 8, LANES), lambda i: (i, 0, 0)),
            pl.BlockSpec((ts, d), lambda i: (i, 0)),
            pl.BlockSpec((ts, LANES), lambda i: (i, 0)),
            pl.BlockSpec((None, 6, d), lambda i: (i // tiles_per_seq, 0, 0)),
            pl.BlockSpec((1, d), lambda i: (0, 0)),
            pl.BlockSpec((1, d), lambda i: (0, 0)),
        ],
        out_specs=pl.BlockSpec((ts, d), lambda i: (i, 0)),
        out_shape=jax.ShapeDtypeStruct((n_tok, d), F32),
        scratch_shapes=[pltpu.VMEM((2, run_rows, d), F32), pltpu.SemaphoreType.DMA((2,))],
        compiler_params=pltpu.CompilerParams(
            dimension_semantics=("arbitrary",), vmem_limit_bytes=VMEM_LIMIT),
        name="combine",
    )(tab3, tab3, ys, off3, x1, sel, ada_l, ln_g.reshape(1, d), ln_b.reshape(1, d))


def _retention_tables():
    log_gamma = jnp.log(1.0 - 2.0 ** (-5.0 - jnp.arange(RET_HEADS, dtype=F32)))
    idx = jnp.arange(CHUNK, dtype=F32)
    diff = idx[:, None] - idx[None, :]
    decay = jnp.where(diff[None] >= 0,
                      jnp.exp(log_gamma[:, None, None] * jnp.maximum(diff, 0.0)[None]), 0.0)
    decay = decay.reshape(RET_HEADS * CHUNK, CHUNK)
    zeta = jnp.exp(log_gamma[:, None] * (CHUNK - 1.0 - idx)[None])
    zeta_t = jnp.repeat(zeta.T, RET_QK_DIM, axis=1)
    xi = jnp.exp(log_gamma[:, None] * (idx + 1.0)[None]).T
    xi_t = jnp.repeat(xi, RET_V_DIM, axis=1)
    cdec = jnp.repeat(jnp.exp(log_gamma * CHUNK), RET_V_DIM)[None, :]
    row_head = jnp.arange(RET_QK_WIDTH) // RET_QK_DIM
    col_head = jnp.arange(RET_V_WIDTH) // RET_V_DIM
    same_head = row_head[:, None] == col_head[None, :]
    bmask = same_head.astype(F32)
    gavg = ((col_head[:, None] == col_head[None, :]).astype(F32) / RET_V_DIM).astype(MXU_DTYPE)
    return decay, zeta_t, xi_t, cdec, bmask, gavg


def _rotary_tables(positions):
    inv_freq = ROPE_BASE ** (-jnp.arange(0, RET_QK_DIM, 2, dtype=F32) / RET_QK_DIM)
    ang = positions.astype(F32)[..., None] * inv_freq
    cos = jnp.cos(ang)
    sin = jnp.sin(ang)
    cos_t = jnp.tile(jnp.concatenate([cos, cos], axis=-1), (1, 1, RET_HEADS))
    sin_t = jnp.tile(jnp.concatenate([-sin, sin], axis=-1), (1, 1, RET_HEADS))
    return cos_t, sin_t


def _moe_rows(n_tok, n_tiles, blk):
    worst = n_tok * TOP_K + n_tiles * N_EXPERTS * (RUN_ALIGN - 1)
    return (-(-worst // blk) + N_EXPERTS) * blk


def _routing_tables(counts, before, n_rows, blk):
    n_blocks = n_rows // blk
    sizes = counts.astype(jnp.int32)
    pad_sizes = (sizes + blk - 1) // blk * blk
    pad_end = jnp.cumsum(pad_sizes)
    pad_start = pad_end - pad_sizes
    n_used = (pad_end[-1] // blk).astype(jnp.int32)
    block_start = jnp.arange(n_blocks, dtype=jnp.int32) * blk
    block_start = jnp.minimum(block_start, (n_used - 1) * blk)
    block_e = jnp.sum((pad_end[None, :] <= block_start[:, None]).astype(jnp.int32), axis=-1)
    block_e = jnp.minimum(block_e, N_EXPERTS - 1)

    before = before.astype(jnp.int32)
    after = jnp.concatenate([before[1:], sizes[None, :]], axis=0)
    run_start = pad_start[None, :] + before
    run_units = (after - before) // RUN_ALIGN
    run_off = (jnp.cumsum(run_units, axis=-1) - run_units) * RUN_ALIGN
    n_tiles = before.shape[0]
    tab = jnp.concatenate([run_start, run_units, run_off,
                           jnp.zeros((n_tiles, LANES - 3 * N_EXPERTS), jnp.int32)], axis=-1)
    off3 = jnp.zeros((n_tiles, 8, LANES), F32).at[:, 0, :N_EXPERTS].set(run_off.astype(F32))
    return (pad_end.astype(jnp.int32), pad_start.astype(jnp.int32), block_e, n_used.reshape(1),
            tab.astype(jnp.int32).reshape(n_tiles, 1, LANES), off3)


def kernel(x, c, positions, w_ada, b_ada, w_in, gmlp_ln_g, gmlp_ln_b, w_spatial, b_spatial, conv_w,
           w_branch, w_gate_up, b_gate, w_out, ln1_g, ln1_b, w_router, b_router, w_gu, b_gu, w_down,
           b_down, ln2_g, ln2_b):
    bsz, seq, d = x.shape
    n_tok = bsz * seq
    depth = w_ada.shape[0]
    blk = min(MOE_BLOCK, n_tok * TOP_K // N_EXPERTS)
    ts = min(SEQ_TILE, seq)

    ada = _ada_call(c, w_ada, b_ada)
    ada = jnp.transpose(ada, (0, 2, 1, 3))
    cos_t, sin_t = _rotary_tables(positions)
    tables = _retention_tables()

    mx = MXU_DTYPE
    w_in_m = w_in.astype(mx)
    w_branch_m = w_branch.astype(mx)
    w_gate_m = w_gate_up.astype(mx)
    w_out_m = w_out.astype(mx)
    w_gu_m = w_gu.astype(mx)
    w_down_m = w_down.astype(mx)
    n_exp = w_router.shape[-1]
    w_router_m = jnp.pad(w_router, ((0, 0), (0, 0), (0, LANES - n_exp))).astype(mx)
    b_router_p = jnp.pad(b_router, ((0, 0), (0, LANES - n_exp)), constant_values=NEG_BIG)

    for l in range(depth):
        bsp_t = jnp.repeat(b_spatial[l].T, GMLP_WIDTH // GMLP_GROUPS, axis=1)
        lw = (
            w_in_m[l],
            gmlp_ln_g[l].reshape(1, -1), gmlp_ln_b[l].reshape(1, -1),
            w_spatial[l].reshape(GMLP_GROUPS * CHUNK, CHUNK), bsp_t,
            jnp.pad(conv_w[l], ((0, 8 - conv_w.shape[1]), (0, 0))),
            w_branch_m[l], w_gate_m[l], b_gate[l], w_out_m[l],
            ln1_g[l].reshape(1, -1), ln1_b[l].reshape(1, -1),
            w_router_m[l], b_router_p[l].reshape(1, -1),
        )
        x1, h2, sel, cnt, before = _mixer_call(x, ada[l], cos_t, sin_t, lw, tables)
        n_rows = _moe_rows(n_tok, n_tok // ts, blk)
        pad_end, pad_start, block_e, n_used, tab3, off3 = _routing_tables(
            cnt[0, :n_exp], before[:, 0, :n_exp], n_rows, blk)
        xs = _dispatch_call(pad_end, pad_start, tab3, off3, sel, h2, n_rows, blk)
        ys = _expert_call(block_e, n_used, xs, w_gu_m[l], b_gu[l], w_down_m[l], b_down[l], blk)
        x = _combine_call(tab3, off3, ys, x1, sel, ada[l], ln2_g[l], ln2_b[l], seq).reshape(bsz, seq, d)
    return x
```

```python
import functools

import jax
import jax.numpy as jnp
from jax import lax
from jax.experimental import pallas as pl
from jax.experimental.pallas import tpu as pltpu

F32 = jnp.float32
MXU_DTYPE = jnp.bfloat16

DEPTH = 4
RET_HEADS = 4
RET_QK_DIM = 32
RET_V_DIM = 64
RET_QK_WIDTH = RET_HEADS * RET_QK_DIM
RET_V_WIDTH = RET_HEADS * RET_V_DIM
CHUNK = 128
ROPE_BASE = 10000.0
GMLP_GROUPS = 4
GMLP_WIDTH = 256
CONV_WIDTH = 256
GATE_RANK = 128
N_BRANCH = 3
N_EXPERTS = 32
TOP_K = 4
D_FF = 256
SWIGLU_LIMIT = 7.0
SWIGLU_ALPHA = 1.702
DEEPNORM_ALPHA = (2.0 * DEPTH) ** 0.25
LN_EPS = 1e-5

_O_QK = 0
_O_VG = 2 * RET_QK_WIDTH
_O_GMLP = _O_VG + 2 * RET_V_WIDTH
_O_CONV = _O_GMLP + 2 * GMLP_WIDTH
_O_CODE = _O_CONV + 3 * CONV_WIDTH
_IN_WIDTH = _O_CODE + N_BRANCH * GATE_RANK

LANES = 128
NEG_BIG = -1e30

SEQ_TILE = 256
MOE_BLOCK = 512
RUN_ALIGN = 16
VMEM_LIMIT = 56 * 1024 * 1024


def _dot(a, b):
    return jnp.dot(a.astype(MXU_DTYPE), b.astype(MXU_DTYPE), preferred_element_type=F32)


def _dot_nt(a, b):
    return lax.dot_general(a.astype(MXU_DTYPE), b.astype(MXU_DTYPE),
                           (((1,), (1,)), ((), ())), preferred_element_type=F32)


def _split_dot(x, w):
    hi = x.astype(MXU_DTYPE)
    lo = (x - hi.astype(F32)).astype(MXU_DTYPE)
    return (jnp.dot(hi, w, preferred_element_type=F32)
            + jnp.dot(lo, w, preferred_element_type=F32))


def _layernorm_rows(x, g, b):
    mu = jnp.mean(x, axis=-1, keepdims=True)
    d = x - mu
    var = jnp.mean(d * d, axis=-1, keepdims=True)
    return d * lax.rsqrt(var + LN_EPS) * g + b


def _ada_kernel(c_ref, w_ref, b_ref, o_ref):
    c_act = jax.nn.silu(c_ref[...])
    o_ref[...] = jnp.dot(c_act, w_ref[...], preferred_element_type=F32,
                         precision=lax.Precision.HIGHEST) + b_ref[...]


def _ada_call(c, w_ada, b_ada):
    depth, d, six_d = w_ada.shape
    bsz = c.shape[0]
    n_col = six_d // d
    return pl.pallas_call(
        _ada_kernel,
        grid=(depth, n_col),
        in_specs=[
            pl.BlockSpec((bsz, d), lambda l, j: (0, 0)),
            pl.BlockSpec((None, d, d), lambda l, j: (l, 0, j)),
            pl.BlockSpec((None, 1, d), lambda l, j: (l, 0, j)),
        ],
        out_specs=pl.BlockSpec((None, None, bsz, d), lambda l, j: (l, j, 0, 0)),
        out_shape=jax.ShapeDtypeStruct((depth, n_col, bsz, d), F32),
        compiler_params=pltpu.CompilerParams(
            dimension_semantics=("arbitrary", "arbitrary"), vmem_limit_bytes=VMEM_LIMIT),
        name="ada_ln",
    )(c, w_ada, b_ada.reshape(depth, 1, six_d))


def _mixer_kernel(x_ref, ada_ref, cos_ref, sin_ref, w_in_ref, lng_ref, lnb_ref, wsp_ref, bsp_ref,
                  convw_ref, wbr_ref, wgt_ref, bgt_ref, wout_ref, ln1g_ref, ln1b_ref, wr_ref, br_ref,
                  decay_ref, zeta_ref, xi_ref, cdec_ref, bmask_ref, gavg_ref,
                  x1_ref, h2_ref, sel_ref, cnt_ref, before_ref,
                  state_ref, zc_ref, cntacc_ref):
    b = pl.program_id(0)
    s = pl.program_id(1)
    ts = x_ref.shape[0]

    @pl.when(s == 0)
    def _():
        state_ref[...] = jnp.zeros_like(state_ref)
        zc_ref[...] = jnp.zeros_like(zc_ref)

    @pl.when(jnp.logical_and(b == 0, s == 0))
    def _():
        cntacc_ref[...] = jnp.zeros_like(cntacc_ref)

    x = x_ref[...]
    ada = ada_ref[...]
    sh1, sc1, gt1, sh2, sc2 = ada[0:1], ada[1:2], ada[2:3], ada[3:4], ada[4:5]
    h = (x * (1.0 + sc1) + sh1).astype(MXU_DTYPE)

    def proj(lo, hi):
        return jnp.dot(h, w_in_ref[:, lo:hi], preferred_element_type=F32)

    lane_qk = lax.broadcasted_iota(jnp.int32, (1, RET_QK_WIDTH), 1)
    lane_v = lax.broadcasted_iota(jnp.int32, (1, RET_V_WIDTH), 1)
    qk_masks = [lane_qk // RET_QK_DIM == hd for hd in range(RET_HEADS)]
    v_masks = [lane_v // RET_V_DIM == hd for hd in range(RET_HEADS)]

    qk = proj(_O_QK, _O_VG)
    cos = cos_ref[...]
    sin = sin_ref[...]
    first_half = (lane_qk % RET_QK_DIM) < (RET_QK_DIM // 2)

    def rotary(t):
        swapped = jnp.where(first_half, pltpu.roll(t, RET_QK_WIDTH - RET_QK_DIM // 2, 1),
                            pltpu.roll(t, RET_QK_DIM // 2, 1))
        return t * cos + swapped * sin

    q = rotary(qk[:, :RET_QK_WIDTH])
    k = rotary(qk[:, RET_QK_WIDTH:]) * (RET_QK_DIM ** -0.5)
    vg = proj(_O_VG, _O_GMLP)
    v = vg[:, :RET_V_WIDTH]
    g = vg[:, RET_V_WIDTH:]

    decay = decay_ref[...]
    zeta = zeta_ref[...]
    xi = xi_ref[...]
    cdec = cdec_ref[...]
    bmask = bmask_ref[...]
    gavg = gavg_ref[...]

    o_chunks = []
    state = state_ref[...]
    for c in range(ts // CHUNK):
        rows = slice(c * CHUNK, (c + 1) * CHUNK)
        q_c, k_c, v_c = q[rows], k[rows], v[rows]
        v_m = v_c.astype(MXU_DTYPE)
        q_all = jnp.concatenate([jnp.where(m, q_c, 0.0) for m in qk_masks], axis=0)
        scores = _dot_nt(q_all, k_c) * decay
        o_all = _dot(scores, v_m)
        o = _dot(q_c, state) * xi
        for hd in range(RET_HEADS):
            o = o + jnp.where(v_masks[hd], o_all[hd * CHUNK:(hd + 1) * CHUNK], 0.0)
        kv = _dot((k_c * zeta).T, v_m) * bmask
        state = state * cdec + kv
        o_chunks.append(o)
    state_ref[...] = state
    o = jnp.concatenate(o_chunks, axis=0) if len(o_chunks) > 1 else o_chunks[0]
    mu = _split_dot(o, gavg)
    d = o - mu
    var = _split_dot(d * d, gavg)
    r_br = jax.nn.silu(g) * (d * lax.rsqrt(var + LN_EPS))

    guv = proj(_O_GMLP, _O_CONV)
    u = jax.nn.gelu(guv[:, :GMLP_WIDTH])
    vv = _layernorm_rows(jax.nn.gelu(guv[:, GMLP_WIDTH:]), lng_ref[...], lnb_ref[...]).astype(MXU_DTYPE)
    rr = lax.broadcasted_iota(jnp.int32, (GMLP_GROUPS * CHUNK, CHUNK), 0) % CHUNK
    cc_ = lax.broadcasted_iota(jnp.int32, (GMLP_GROUPS * CHUNK, CHUNK), 1)
    w_sp = jnp.where(cc_ <= rr, wsp_ref[...], 0.0).astype(MXU_DTYPE)
    bsp = bsp_ref[...]
    z_chunks = []
    for c in range(ts // CHUNK):
        z_all = jnp.dot(w_sp, vv[c * CHUNK:(c + 1) * CHUNK], preferred_element_type=F32)
        z = bsp
        for gi in range(GMLP_GROUPS):
            z = z + jnp.where(v_masks[gi], z_all[gi * CHUNK:(gi + 1) * CHUNK], 0.0)
        z_chunks.append(z)
    z = jnp.concatenate(z_chunks, axis=0) if len(z_chunks) > 1 else z_chunks[0]
    s_br = u * z

    cbcx = proj(_O_CONV, _O_CODE)
    gate_b = cbcx[:, :CONV_WIDTH]
    zc = cbcx[:, CONV_WIDTH:2 * CONV_WIDTH] * cbcx[:, 2 * CONV_WIDTH:]
    carry = zc_ref[...]
    prev1 = carry[7:8]
    prev2 = carry[6:7]
    row = lax.broadcasted_iota(jnp.int32, (ts, 1), 0)
    z1 = jnp.where(row == 0, prev1, pltpu.roll(zc, 1, 0))
    z2 = jnp.where(row == 0, prev2, jnp.where(row == 1, prev1, pltpu.roll(zc, 2, 0)))
    zc_ref[...] = zc[ts - 8:ts]
    cw = convw_ref[...]
    k_br = gate_b * (cw[0:1] * z2 + cw[1:2] * z1 + cw[2:3] * zc)

    code = proj(_O_CODE, _IN_WIDTH)
    merged = None
    for i, br in enumerate((r_br, s_br, k_br)):
        y = jnp.dot(br.astype(MXU_DTYPE), wbr_ref[i], preferred_element_type=F32)
        gl = jnp.dot(code[:, i * GATE_RANK:(i + 1) * GATE_RANK].astype(MXU_DTYPE), wgt_ref[i],
                     preferred_element_type=F32) + bgt_ref[i:i + 1]
        t = jax.nn.sigmoid(gl) * y
        merged = t if merged is None else merged + t
    mix = jnp.dot(merged.astype(MXU_DTYPE), wout_ref[...], preferred_element_type=F32)
    x1 = _layernorm_rows(DEEPNORM_ALPHA * x + (1.0 + gt1) * mix, ln1g_ref[...], ln1b_ref[...])
    x1_ref[...] = x1
    h2 = (x1 * (1.0 + sc2) + sh2).astype(MXU_DTYPE)
    h2_ref[...] = h2

    logits = jnp.dot(h2, wr_ref[...], preferred_element_type=F32) + br_ref[...]
    lane = lax.broadcasted_iota(jnp.int32, (1, LANES), 1).astype(F32)
    top_v, top_i = [], []
    work = logits
    for _ in range(TOP_K):
        m = jnp.max(work, axis=-1, keepdims=True)
        i_sel = jnp.min(jnp.where(work == m, lane, float(LANES)), axis=-1, keepdims=True)
        top_v.append(m)
        top_i.append(i_sel)
        work = jnp.where(lane == i_sel, -jnp.inf, work)
    exps = [jnp.exp(tv - top_v[0]) for tv in top_v]
    denom = exps[0] + exps[1] + exps[2] + exps[3]
    onehots = [lane == ti for ti in top_i]
    member = jnp.zeros((ts, LANES), F32)
    for oh in onehots:
        member = member + jnp.where(oh, 1.0, 0.0)
    tr = lax.broadcasted_iota(jnp.int32, (ts, ts), 0)
    tc = lax.broadcasted_iota(jnp.int32, (ts, ts), 1)
    before = jnp.where(tc < tr, 1.0, 0.0).astype(MXU_DTYPE)
    cnt = cntacc_ref[...]
    local_rank = jnp.dot(before, member.astype(MXU_DTYPE), preferred_element_type=F32)
    sel = jnp.zeros((ts, LANES), F32)
    for kk in range(TOP_K):
        lrank_k = jnp.sum(jnp.where(onehots[kk], local_rank, 0.0), axis=-1, keepdims=True)
        sel = sel + jnp.where(lane == float(kk), top_i[kk], 0.0)
        sel = sel + jnp.where(lane == float(TOP_K + kk), exps[kk] / denom, 0.0)
        sel = sel + jnp.where(lane == float(2 * TOP_K + kk), lrank_k, 0.0)
    sel_ref[...] = sel
    before_ref[...] = cnt
    tile_cnt = jnp.sum(member, axis=0, keepdims=True)
    cnt = cnt + jnp.floor((tile_cnt + (RUN_ALIGN - 1.0)) * (1.0 / RUN_ALIGN)) * RUN_ALIGN
    cntacc_ref[...] = cnt
    cnt_ref[...] = cnt


def _const_spec(shape):
    return pl.BlockSpec(shape, lambda b, s: (0,) * len(shape))


def _mixer_call(x, ada_l, cos_t, sin_t, lw, tables):
    bsz, seq, d = x.shape
    ts = min(SEQ_TILE, seq)
    n_tok = bsz * seq
    row3 = lambda b, s: (b, s, 0)
    in_specs = [
        pl.BlockSpec((None, ts, d), row3),
        pl.BlockSpec((None, 6, d), lambda b, s: (b, 0, 0)),
        pl.BlockSpec((None, ts, RET_QK_WIDTH), row3),
        pl.BlockSpec((None, ts, RET_QK_WIDTH), row3),
    ] + [_const_spec(a.shape) for a in lw] + [_const_spec(a.shape) for a in tables]
    tok_row = lambda b, s: (b * (seq // ts) + s, 0)
    out_specs = [
        pl.BlockSpec((ts, d), tok_row),
        pl.BlockSpec((ts, d), tok_row),
        pl.BlockSpec((ts, LANES), tok_row),
        pl.BlockSpec((8, LANES), lambda b, s: (0, 0)),
        pl.BlockSpec((None, 8, LANES), lambda b, s: (b * (seq // ts) + s, 0, 0)),
    ]
    out_shape = [
        jax.ShapeDtypeStruct((n_tok, d), F32),
        jax.ShapeDtypeStruct((n_tok, d), MXU_DTYPE),
        jax.ShapeDtypeStruct((n_tok, LANES), F32),
        jax.ShapeDtypeStruct((8, LANES), F32),
        jax.ShapeDtypeStruct((n_tok // ts, 8, LANES), F32),
    ]
    return pl.pallas_call(
        _mixer_kernel,
        grid=(bsz, seq // ts),
        in_specs=in_specs,
        out_specs=out_specs,
        out_shape=out_shape,
        scratch_shapes=[
            pltpu.VMEM((RET_QK_WIDTH, RET_V_WIDTH), F32),
            pltpu.VMEM((8, CONV_WIDTH), F32),
            pltpu.VMEM((8, LANES), F32),
        ],
        compiler_params=pltpu.CompilerParams(
            dimension_semantics=("arbitrary", "arbitrary"), vmem_limit_bytes=VMEM_LIMIT),
        name="mixer",
    )(x, ada_l, cos_t, sin_t, *lw, *tables)


def _run_pieces(max_rows):
    units = max_rows // RUN_ALIGN
    pieces = []
    bit = 1
    while bit <= units:
        pieces.append(bit)
        bit *= 2
    return tuple(reversed(pieces))


def _for_each_run_piece(tab_ref, pieces, make_copy, fn):
    for e in range(N_EXPERTS):
        start = tab_ref[0, 0, e]
        units = tab_ref[0, 0, N_EXPERTS + e]
        off = tab_ref[0, 0, 2 * N_EXPERTS + e]
        for bit in pieces:
            done = jnp.bitwise_and(units, -2 * bit) * RUN_ALIGN

            @pl.when(jnp.bitwise_and(units, bit) != 0)
            def _(start=start, off=off, done=done, rows=bit * RUN_ALIGN):
                fn(make_copy(pl.multiple_of(start + done, RUN_ALIGN), pl.multiple_of(off + done, RUN_ALIGN), rows))


def _run_positions(sel, off_row):
    lane = lax.broadcasted_iota(jnp.int32, (1, LANES), 1).astype(F32)
    out = []
    for kk in range(TOP_K):
        base = jnp.sum(jnp.where(lane == sel[:, kk:kk + 1], off_row, 0.0), axis=-1, keepdims=True)
        out.append(base + sel[:, 2 * TOP_K + kk:2 * TOP_K + kk + 1])
    return out


def _dispatch_kernel(pe_ref, ps_ref, tab_cur_ref, tab_prev_ref, off_ref, sel_ref, h2_ref, xs_hbm,
                     zbuf, sbuf, sem, zsem):
    i = pl.program_id(0)
    n_steps = pl.num_programs(0)
    ts = h2_ref.shape[0]
    blk = zbuf.shape[0]
    n_exp = pe_ref.shape[0]
    run_rows = sbuf.shape[1]
    slot = i % 2
    pieces = _run_pieces(ts)

    @pl.when(i == 0)
    def _():
        zbuf[...] = jnp.zeros_like(zbuf)

        def tail_copy(e):
            start = pl.multiple_of(jnp.maximum(pe_ref[e] - blk, 0), blk)
            return pltpu.make_async_copy(zbuf, xs_hbm.at[pl.ds(start, blk), :], zsem)

        def zstart(e, carry):
            @pl.when(pe_ref[e] > ps_ref[e])
            def _():
                tail_copy(e).start()
            return carry

        def zwait(e, carry):
            @pl.when(pe_ref[e] > ps_ref[e])
            def _():
                tail_copy(e).wait()
            return carry

        lax.fori_loop(0, n_exp, zstart, 0)
        lax.fori_loop(0, n_exp, zwait, 0)

    targets = _run_positions(sel_ref[...], off_ref[...][0:1])
    lane = lax.broadcasted_iota(jnp.int32, (1, LANES), 1).astype(F32)
    packed = jnp.zeros((ts, LANES), F32)
    for kk in range(TOP_K):
        packed = packed + jnp.where(lane == float(kk), targets[kk], 0.0)
    targets_t = packed.T
    buf_row = lax.broadcasted_iota(jnp.int32, (run_rows, ts), 0).astype(F32)
    pick = jnp.zeros((run_rows, ts), F32)
    for kk in range(TOP_K):
        pick = jnp.where(buf_row == targets_t[kk:kk + 1], 1.0, pick)
    sbuf[slot] = jnp.dot(pick.astype(MXU_DTYPE), h2_ref[...].astype(MXU_DTYPE),
                         preferred_element_type=F32).astype(sbuf.dtype)

    def push(dst_slot):
        def make(hbm_row, buf_row_, rows):
            return pltpu.make_async_copy(sbuf.at[dst_slot, pl.ds(buf_row_, rows), :],
                                         xs_hbm.at[pl.ds(hbm_row, rows), :], sem.at[dst_slot])
        return make

    _for_each_run_piece(tab_cur_ref, pieces, push(slot), lambda c: c.start())

    @pl.when(i > 0)
    def _():
        _for_each_run_piece(tab_prev_ref, pieces, push(1 - slot), lambda c: c.wait())

    @pl.when(i == n_steps - 1)
    def _():
        _for_each_run_piece(tab_cur_ref, pieces, push(slot), lambda c: c.wait())


def _dispatch_call(pad_end, pad_start, tab3, off3, sel, h2, n_rows, blk):
    n_tok, d = h2.shape
    n_steps = tab3.shape[0]
    ts = n_tok // n_steps
    run_rows = ts * TOP_K + N_EXPERTS * RUN_ALIGN
    grid_spec = pltpu.PrefetchScalarGridSpec(
        num_scalar_prefetch=2,
        grid=(n_steps,),
        in_specs=[
            pl.BlockSpec((1, 1, LANES), lambda i, pe, ps: (i, 0, 0), memory_space=pltpu.SMEM),
            pl.BlockSpec((1, 1, LANES), lambda i, pe, ps: (jnp.maximum(i - 1, 0), 0, 0),
                         memory_space=pltpu.SMEM),
            pl.BlockSpec((None, 8, LANES), lambda i, pe, ps: (i, 0, 0)),
            pl.BlockSpec((ts, LANES), lambda i, pe, ps: (i, 0)),
            pl.BlockSpec((ts, d), lambda i, pe, ps: (i, 0)),
        ],
        out_specs=pl.BlockSpec(memory_space=pl.ANY),
        scratch_shapes=[pltpu.VMEM((blk, d), MXU_DTYPE), pltpu.VMEM((2, run_rows, d), MXU_DTYPE),
                        pltpu.SemaphoreType.DMA((2,)), pltpu.SemaphoreType.DMA(())],
    )
    return pl.pallas_call(
        _dispatch_kernel,
        grid_spec=grid_spec,
        out_shape=jax.ShapeDtypeStruct((n_rows, d), MXU_DTYPE),
        compiler_params=pltpu.CompilerParams(
            dimension_semantics=("arbitrary",), vmem_limit_bytes=VMEM_LIMIT),
        name="dispatch",
    )(pad_end, pad_start, tab3, tab3, off3, sel, h2)


def _expert_mlp(xs, wgu_ref, bgu_ref, wdn_ref, bdn_ref):
    gu = jnp.dot(xs, wgu_ref[...], preferred_element_type=F32) + bgu_ref[...]
    gate = jnp.minimum(gu[:, :D_FF], SWIGLU_LIMIT)
    up = jnp.clip(gu[:, D_FF:], -SWIGLU_LIMIT, SWIGLU_LIMIT)
    act = (up + 1.0) * (gate * jax.nn.sigmoid(SWIGLU_ALPHA * gate))
    return jnp.dot(act.astype(MXU_DTYPE), wdn_ref[...], preferred_element_type=F32) + bdn_ref[...]


def _expert_kernel(be_ref, nu_ref, xs_ref, wgu_ref, bgu_ref, wdn_ref, bdn_ref, ys_ref):
    i = pl.program_id(0)
    n_used = nu_ref[0]

    @pl.when(i < n_used)
    def _():
        ys_ref[...] = _expert_mlp(xs_ref[...], wgu_ref, bgu_ref, wdn_ref, bdn_ref).astype(ys_ref.dtype)

    @pl.when(i >= n_used)
    def _():
        ys_ref[...] = jnp.zeros_like(ys_ref)


def _expert_call(block_e, n_used, xs, w_gu, b_gu, w_down, b_down, blk):
    n_rows, d = xs.shape
    n_blocks = n_rows // blk
    two_f = w_gu.shape[2]
    last = lambda nu: jnp.maximum(nu[0] - 1, 0)
    grid_spec = pltpu.PrefetchScalarGridSpec(
        num_scalar_prefetch=2,
        grid=(n_blocks,),
        in_specs=[
            pl.BlockSpec((blk, d), lambda i, be, nu: (jnp.minimum(i, last(nu)), 0)),
            pl.BlockSpec((None, d, two_f), lambda i, be, nu: (be[i], 0, 0)),
            pl.BlockSpec((None, 1, two_f), lambda i, be, nu: (be[i], 0, 0)),
            pl.BlockSpec((None, two_f // 2, d), lambda i, be, nu: (be[i], 0, 0)),
            pl.BlockSpec((None, 1, d), lambda i, be, nu: (be[i], 0, 0)),
        ],
        out_specs=pl.BlockSpec((blk, d), lambda i, be, nu: (i, 0)),
    )
    n_exp = w_gu.shape[0]
    return pl.pallas_call(
        _expert_kernel,
        grid_spec=grid_spec,
        out_shape=jax.ShapeDtypeStruct((n_rows, d), MXU_DTYPE),
        compiler_params=pltpu.CompilerParams(
            dimension_semantics=("arbitrary",), vmem_limit_bytes=VMEM_LIMIT),
        name="experts",
    )(block_e, n_used, xs, w_gu, b_gu.reshape(n_exp, 1, two_f), w_down, b_down.reshape(n_exp, 1, d))


def _combine_kernel(tab_cur_ref, tab_nxt_ref, ys_hbm, off_ref, x1_ref, sel_ref, ada_ref, lng_ref, lnb_ref,
                    o_ref, rbuf, sem):
    i = pl.program_id(0)
    n_steps = pl.num_programs(0)
    ts = x1_ref.shape[0]
    slot = i % 2
    pieces = _run_pieces(ts)

    def fetch(dst_slot):
        def make(hbm_row, buf_row, rows):
            return pltpu.make_async_copy(ys_hbm.at[pl.ds(hbm_row, rows), :],
                                         rbuf.at[dst_slot, pl.ds(buf_row, rows), :], sem.at[dst_slot])
        return make

    @pl.when(i == 0)
    def _():
        rbuf[...] = jnp.zeros_like(rbuf)
        _for_each_run_piece(tab_cur_ref, pieces, fetch(0), lambda c: c.start())

    @pl.when(i + 1 < n_steps)
    def _():
        _for_each_run_piece(tab_nxt_ref, pieces, fetch(1 - slot), lambda c: c.start())

    _for_each_run_piece(tab_cur_ref, pieces, fetch(slot), lambda c: c.wait())

    sel = sel_ref[...]
    targets = _run_positions(sel, off_ref[...][0:1])
    col = lax.broadcasted_iota(jnp.int32, (1, rbuf.shape[1]), 1).astype(F32)
    weights = None
    for kk in range(TOP_K):
        term = jnp.where(col == targets[kk], sel[:, TOP_K + kk:TOP_K + kk + 1], 0.0)
        weights = term if weights is None else weights + term
    ffn = _split_dot(weights, rbuf[slot])
    gt2 = ada_ref[...][5:6]
    o_ref[...] = _layernorm_rows(DEEPNORM_ALPHA * x1_ref[...] + (1.0 + gt2) * ffn, lng_ref[...], lnb_ref[...])


def _combine_call(tab3, off3, ys, x1, sel, ada_l, ln_g, ln_b, seq):
    n_tok, d = x1.shape
    n_steps = tab3.shape[0]
    ts = n_tok // n_steps
    tiles_per_seq = seq // ts
    run_rows = ts * TOP_K + N_EXPERTS * RUN_ALIGN
    return pl.pallas_call(
        _combine_kernel,
        grid=(n_steps,),
        in_specs=[
            pl.BlockSpec((1, 1, LANES), lambda i: (i, 0, 0), memory_space=pltpu.SMEM),
            pl.BlockSpec((1, 1, LANES), lambda i: (jnp.minimum(i + 1, n_steps - 1), 0, 0),
                         memory_space=pltpu.SMEM),
            pl.BlockSpec(memory_space=pl.ANY),
            pl.BlockSpec((None, 8, LANES), lambda i: (i, 0, 0)),
            pl.BlockSpec((ts, d), lambda i: (i, 0)),
            pl.BlockSpec((ts, LANES), lambda i: (i, 0)),
            pl.BlockSpec((None, 6, d), lambda i: (i // tiles_per_seq, 0, 0)),
            pl.BlockSpec((1, d), lambda i: (0, 0)),
            pl.BlockSpec((1, d), lambda i: (0, 0)),
        ],
        out_specs=pl.BlockSpec((ts, d), lambda i: (i, 0)),
        out_shape=jax.ShapeDtypeStruct((n_tok, d), F32),
        scratch_shapes=[pltpu.VMEM((2, run_rows, d), MXU_DTYPE), pltpu.SemaphoreType.DMA((2,))],
        compiler_params=pltpu.CompilerParams(
            dimension_semantics=("arbitrary",), vmem_limit_bytes=VMEM_LIMIT),
        name="combine",
    )(tab3, tab3, ys, off3, x1, sel, ada_l, ln_g.reshape(1, d), ln_b.reshape(1, d))


def _retention_tables():
    log_gamma = jnp.log(1.0 - 2.0 ** (-5.0 - jnp.arange(RET_HEADS, dtype=F32)))
    idx = jnp.arange(CHUNK, dtype=F32)
    diff = idx[:, None] - idx[None, :]
    decay = jnp.where(diff[None] >= 0,
                      jnp.exp(log_gamma[:, None, None] * jnp.maximum(diff, 0.0)[None]), 0.0)
    decay = decay.reshape(RET_HEADS * CHUNK, CHUNK)
    zeta = jnp.exp(log_gamma[:, None] * (CHUNK - 1.0 - idx)[None])
    zeta_t = jnp.repeat(zeta.T, RET_QK_DIM, axis=1)
    xi = jnp.exp(log_gamma[:, None] * (idx + 1.0)[None]).T
    xi_t = jnp.repeat(xi, RET_V_DIM, axis=1)
    cdec = jnp.repeat(jnp.exp(log_gamma * CHUNK), RET_V_DIM)[None, :]
    row_head = jnp.arange(RET_QK_WIDTH) // RET_QK_DIM
    col_head = jnp.arange(RET_V_WIDTH) // RET_V_DIM
    same_head = row_head[:, None] == col_head[None, :]
    bmask = same_head.astype(F32)
    gavg = ((col_head[:, None] == col_head[None, :]).astype(F32) / RET_V_DIM).astype(MXU_DTYPE)
    return decay, zeta_t, xi_t, cdec, bmask, gavg


def _rotary_tables(positions):
    inv_freq = ROPE_BASE ** (-jnp.arange(0, RET_QK_DIM, 2, dtype=F32) / RET_QK_DIM)
    ang = positions.astype(F32)[..., None] * inv_freq
    cos = jnp.cos(ang)
    sin = jnp.sin(ang)
    cos_t = jnp.tile(jnp.concatenate([cos, cos], axis=-1), (1, 1, RET_HEADS))
    sin_t = jnp.tile(jnp.concatenate([-sin, sin], axis=-1), (1, 1, RET_HEADS))
    return cos_t, sin_t


def _moe_rows(n_tok, n_tiles, blk):
    worst = n_tok * TOP_K + n_tiles * N_EXPERTS * (RUN_ALIGN - 1)
    return (-(-worst // blk) + N_EXPERTS) * blk


def _routing_tables(counts, before, n_rows, blk):
    n_blocks = n_rows // blk
    sizes = counts.astype(jnp.int32)
    pad_sizes = (sizes + blk - 1) // blk * blk
    pad_end = jnp.cumsum(pad_sizes)
    pad_start = pad_end - pad_sizes
    n_used = (pad_end[-1] // blk).astype(jnp.int32)
    block_start = jnp.arange(n_blocks, dtype=jnp.int32) * blk
    block_start = jnp.minimum(block_start, (n_used - 1) * blk)
    block_e = jnp.sum((pad_end[None, :] <= block_start[:, None]).astype(jnp.int32), axis=-1)
    block_e = jnp.minimum(block_e, N_EXPERTS - 1)

    before = before.astype(jnp.int32)
    after = jnp.concatenate([before[1:], sizes[None, :]], axis=0)
    run_start = pad_start[None, :] + before
    run_units = (after - before) // RUN_ALIGN
    run_off = (jnp.cumsum(run_units, axis=-1) - run_units) * RUN_ALIGN
    n_tiles = before.shape[0]
    tab = jnp.concatenate([run_start, run_units, run_off,
                           jnp.zeros((n_tiles, LANES - 3 * N_EXPERTS), jnp.int32)], axis=-1)
    off3 = jnp.zeros((n_tiles, 8, LANES), F32).at[:, 0, :N_EXPERTS].set(run_off.astype(F32))
    return (pad_end.astype(jnp.int32), pad_start.astype(jnp.int32), block_e, n_used.reshape(1),
            tab.astype(jnp.int32).reshape(n_tiles, 1, LANES), off3)


def kernel(x, c, positions, w_ada, b_ada, w_in, gmlp_ln_g, gmlp_ln_b, w_spatial, b_spatial, conv_w,
           w_branch, w_gate_up, b_gate, w_out, ln1_g, ln1_b, w_router, b_router, w_gu, b_gu, w_down,
           b_down, ln2_g, ln2_b):
    bsz, seq, d = x.shape
    n_tok = bsz * seq
    depth = w_ada.shape[0]
    blk = min(MOE_BLOCK, n_tok * TOP_K // N_EXPERTS)
    ts = min(SEQ_TILE, seq)

    ada = _ada_call(c, w_ada, b_ada)
    ada = jnp.transpose(ada, (0, 2, 1, 3))
    cos_t, sin_t = _rotary_tables(positions)
    tables = _retention_tables()

    mx = MXU_DTYPE
    w_in_m = w_in.astype(mx)
    w_branch_m = w_branch.astype(mx)
    w_gate_m = w_gate_up.astype(mx)
    w_out_m = w_out.astype(mx)
    w_gu_m = w_gu.astype(mx)
    w_down_m = w_down.astype(mx)
    n_exp = w_router.shape[-1]
    w_router_m = jnp.pad(w_router, ((0, 0), (0, 0), (0, LANES - n_exp))).astype(mx)
    b_router_p = jnp.pad(b_router, ((0, 0), (0, LANES - n_exp)), constant_values=NEG_BIG)

    for l in range(depth):
        bsp_t = jnp.repeat(b_spatial[l].T, GMLP_WIDTH // GMLP_GROUPS, axis=1)
        lw = (
            w_in_m[l],
            gmlp_ln_g[l].reshape(1, -1), gmlp_ln_b[l].reshape(1, -1),
            w_spatial[l].reshape(GMLP_GROUPS * CHUNK, CHUNK), bsp_t,
            jnp.pad(conv_w[l], ((0, 8 - conv_w.shape[1]), (0, 0))),
            w_branch_m[l], w_gate_m[l], b_gate[l], w_out_m[l],
            ln1_g[l].reshape(1, -1), ln1_b[l].reshape(1, -1),
            w_router_m[l], b_router_p[l].reshape(1, -1),
        )
        x1, h2, sel, cnt, before = _mixer_call(x, ada[l], cos_t, sin_t, lw, tables)
        n_rows = _moe_rows(n_tok, n_tok // ts, blk)
        pad_end, pad_start, block_e, n_used, tab3, off3 = _routing_tables(
            cnt[0, :n_exp], before[:, 0, :n_exp], n_rows, blk)
        xs = _dispatch_call(pad_end, pad_start, tab3, off3, sel, h2, n_rows, blk)
        ys = _expert_call(block_e, n_used, xs, w_gu_m[l], b_gu[l], w_down_m[l], b_down[l], blk)
        x = _combine_call(tab3, off3, ys, x1, sel, ada[l], ln2_g[l], ln2_b[l], seq).reshape(bsz, seq, d)
    return x
```

```python
import functools

import jax
import jax.numpy as jnp
from jax import lax
from jax.experimental import pallas as pl
from jax.experimental.pallas import tpu as pltpu

F32 = jnp.float32
MXU_DTYPE = jnp.bfloat16

DEPTH = 4
RET_HEADS = 4
RET_QK_DIM = 32
RET_V_DIM = 64
RET_QK_WIDTH = RET_HEADS * RET_QK_DIM
RET_V_WIDTH = RET_HEADS * RET_V_DIM
CHUNK = 128
ROPE_BASE = 10000.0
GMLP_GROUPS = 4
GMLP_WIDTH = 256
CONV_WIDTH = 256
GATE_RANK = 128
N_BRANCH = 3
N_EXPERTS = 32
TOP_K = 4
D_FF = 256
SWIGLU_LIMIT = 7.0
SWIGLU_ALPHA = 1.702
DEEPNORM_ALPHA = (2.0 * DEPTH) ** 0.25
LN_EPS = 1e-5

_O_QK = 0
_O_VG = 2 * RET_QK_WIDTH
_O_GMLP = _O_VG + 2 * RET_V_WIDTH
_O_CONV = _O_GMLP + 2 * GMLP_WIDTH
_O_CODE = _O_CONV + 3 * CONV_WIDTH
_IN_WIDTH = _O_CODE + N_BRANCH * GATE_RANK

LANES = 128
NEG_BIG = -1e30

SEQ_TILE = 256
MOE_BLOCK = 512
RUN_ALIGN = 16
VMEM_LIMIT = 56 * 1024 * 1024


def _dot(a, b):
    return jnp.dot(a.astype(MXU_DTYPE), b.astype(MXU_DTYPE), preferred_element_type=F32)


def _dot_nt(a, b):
    return lax.dot_general(a.astype(MXU_DTYPE), b.astype(MXU_DTYPE),
                           (((1,), (1,)), ((), ())), preferred_element_type=F32)


def _split_dot(x, w):
    hi = x.astype(MXU_DTYPE)
    lo = (x - hi.astype(F32)).astype(MXU_DTYPE)
    return (jnp.dot(hi, w, preferred_element_type=F32)
            + jnp.dot(lo, w, preferred_element_type=F32))


def _layernorm_rows(x, g, b):
    mu = jnp.mean(x, axis=-1, keepdims=True)
    d = x - mu
    var = jnp.mean(d * d, axis=-1, keepdims=True)
    return d * lax.rsqrt(var + LN_EPS) * g + b


def _ada_kernel(c_ref, w_ref, b_ref, o_ref):
    c_act = jax.nn.silu(c_ref[...])
    o_ref[...] = jnp.dot(c_act, w_ref[...], preferred_element_type=F32,
                         precision=lax.Precision.HIGHEST) + b_ref[...]


def _ada_call(c, w_ada, b_ada):
    depth, d, six_d = w_ada.shape
    bsz = c.shape[0]
    n_col = six_d // d
    return pl.pallas_call(
        _ada_kernel,
        grid=(depth, n_col),
        in_specs=[
            pl.BlockSpec((bsz, d), lambda l, j: (0, 0)),
            pl.BlockSpec((None, d, d), lambda l, j: (l, 0, j)),
            pl.BlockSpec((None, 1, d), lambda l, j: (l, 0, j)),
        ],
        out_specs=pl.BlockSpec((None, None, bsz, d), lambda l, j: (l, j, 0, 0)),
        out_shape=jax.ShapeDtypeStruct((depth, n_col, bsz, d), F32),
        compiler_params=pltpu.CompilerParams(
            dimension_semantics=("arbitrary", "arbitrary"), vmem_limit_bytes=VMEM_LIMIT),
        name="ada_ln",
    )(c, w_ada, b_ada.reshape(depth, 1, six_d))


def _mixer_kernel(x_ref, ada_ref, cos_ref, sin_ref, w_in_ref, lng_ref, lnb_ref, wsp_ref, bsp_ref,
                  convw_ref, wbr_ref, wgt_ref, bgt_ref, wout_ref, ln1g_ref, ln1b_ref, wr_ref, br_ref,
                  decay_ref, zeta_ref, xi_ref, cdec_ref, bmask_ref, gavg_ref,
                  x1_ref, h2_ref, sel_ref, cnt_ref, before_ref,
                  state_ref, zc_ref, cntacc_ref):
    b = pl.program_id(0)
    s = pl.program_id(1)
    ts = x_ref.shape[0]

    @pl.when(s == 0)
    def _():
        state_ref[...] = jnp.zeros_like(state_ref)
        zc_ref[...] = jnp.zeros_like(zc_ref)

    @pl.when(jnp.logical_and(b == 0, s == 0))
    def _():
        cntacc_ref[...] = jnp.zeros_like(cntacc_ref)

    x = x_ref[...]
    ada = ada_ref[...]
    sh1, sc1, gt1, sh2, sc2 = ada[0:1], ada[1:2], ada[2:3], ada[3:4], ada[4:5]
    h = (x * (1.0 + sc1) + sh1).astype(MXU_DTYPE)

    def proj(lo, hi):
        return jnp.dot(h, w_in_ref[:, lo:hi], preferred_element_type=F32)

    lane_qk = lax.broadcasted_iota(jnp.int32, (1, RET_QK_WIDTH), 1)
    lane_v = lax.broadcasted_iota(jnp.int32, (1, RET_V_WIDTH), 1)
    qk_masks = [lane_qk // RET_QK_DIM == hd for hd in range(RET_HEADS)]
    v_masks = [lane_v // RET_V_DIM == hd for hd in range(RET_HEADS)]

    qk = proj(_O_QK, _O_VG)
    cos = cos_ref[...]
    sin = sin_ref[...]
    first_half = (lane_qk % RET_QK_DIM) < (RET_QK_DIM // 2)

    def rotary(t):
        swapped = jnp.where(first_half, pltpu.roll(t, RET_QK_WIDTH - RET_QK_DIM // 2, 1),
                            pltpu.roll(t, RET_QK_DIM // 2, 1))
        return t * cos + swapped * sin

    q = rotary(qk[:, :RET_QK_WIDTH])
    k = rotary(qk[:, RET_QK_WIDTH:]) * (RET_QK_DIM ** -0.5)
    vg = proj(_O_VG, _O_GMLP)
    v = vg[:, :RET_V_WIDTH]
    g = vg[:, RET_V_WIDTH:]

    decay = decay_ref[...]
    zeta = zeta_ref[...]
    xi = xi_ref[...]
    cdec = cdec_ref[...]
    bmask = bmask_ref[...]
    gavg = gavg_ref[...]

    o_chunks = []
    state = state_ref[...]
    for c in range(ts // CHUNK):
        rows = slice(c * CHUNK, (c + 1) * CHUNK)
        q_c, k_c, v_c = q[rows], k[rows], v[rows]
        v_m = v_c.astype(MXU_DTYPE)
        q_all = jnp.concatenate([jnp.where(m, q_c, 0.0) for m in qk_masks], axis=0)
        scores = _dot_nt(q_all, k_c) * decay
        o_all = _dot(scores, v_m)
        o = _dot(q_c, state) * xi
        for hd in range(RET_HEADS):
            o = o + jnp.where(v_masks[hd], o_all[hd * CHUNK:(hd + 1) * CHUNK], 0.0)
        kv = _dot((k_c * zeta).T, v_m) * bmask
        state = state * cdec + kv
        o_chunks.append(o)
    state_ref[...] = state
    o = jnp.concatenate(o_chunks, axis=0) if len(o_chunks) > 1 else o_chunks[0]
    mu = _split_dot(o, gavg)
    d = o - mu
    var = _split_dot(d * d, gavg)
    r_br = jax.nn.silu(g) * (d * lax.rsqrt(var + LN_EPS))

    guv = proj(_O_GMLP, _O_CONV)
    u = jax.nn.gelu(guv[:, :GMLP_WIDTH])
    vv = _layernorm_rows(jax.nn.gelu(guv[:, GMLP_WIDTH:]), lng_ref[...], lnb_ref[...]).astype(MXU_DTYPE)
    rr = lax.broadcasted_iota(jnp.int32, (GMLP_GROUPS * CHUNK, CHUNK), 0) % CHUNK
    cc_ = lax.broadcasted_iota(jnp.int32, (GMLP_GROUPS * CHUNK, CHUNK), 1)
    w_sp = jnp.where(cc_ <= rr, wsp_ref[...], 0.0).astype(MXU_DTYPE)
    bsp = bsp_ref[...]
    z_chunks = []
    for c in range(ts // CHUNK):
        z_all = jnp.dot(w_sp, vv[c * CHUNK:(c + 1) * CHUNK], preferred_element_type=F32)
        z = bsp
        for gi in range(GMLP_GROUPS):
            z = z + jnp.where(v_masks[gi], z_all[gi * CHUNK:(gi + 1) * CHUNK], 0.0)
        z_chunks.append(z)
    z = jnp.concatenate(z_chunks, axis=0) if len(z_chunks) > 1 else z_chunks[0]
    s_br = u * z

    cbcx = proj(_O_CONV, _O_CODE)
    gate_b = cbcx[:, :CONV_WIDTH]
    zc = cbcx[:, CONV_WIDTH:2 * CONV_WIDTH] * cbcx[:, 2 * CONV_WIDTH:]
    carry = zc_ref[...]
    prev1 = carry[7:8]
    prev2 = carry[6:7]
    row = lax.broadcasted_iota(jnp.int32, (ts, 1), 0)
    z1 = jnp.where(row == 0, prev1, pltpu.roll(zc, 1, 0))
    z2 = jnp.where(row == 0, prev2, jnp.where(row == 1, prev1, pltpu.roll(zc, 2, 0)))
    zc_ref[...] = zc[ts - 8:ts]
    cw = convw_ref[...]
    k_br = gate_b * (cw[0:1] * z2 + cw[1:2] * z1 + cw[2:3] * zc)

    code = proj(_O_CODE, _IN_WIDTH)
    merged = None
    for i, br in enumerate((r_br, s_br, k_br)):
        y = jnp.dot(br.astype(MXU_DTYPE), wbr_ref[i], preferred_element_type=F32)
        gl = jnp.dot(code[:, i * GATE_RANK:(i + 1) * GATE_RANK].astype(MXU_DTYPE), wgt_ref[i],
                     preferred_element_type=F32) + bgt_ref[i:i + 1]
        t = jax.nn.sigmoid(gl) * y
        merged = t if merged is None else merged + t
    mix = jnp.dot(merged.astype(MXU_DTYPE), wout_ref[...], preferred_element_type=F32)
    x1 = _layernorm_rows(DEEPNORM_ALPHA * x + (1.0 + gt1) * mix, ln1g_ref[...], ln1b_ref[...])
    x1_ref[...] = x1
    h2 = (x1 * (1.0 + sc2) + sh2).astype(MXU_DTYPE)
    h2_ref[...] = h2

    logits = jnp.dot(h2, wr_ref[...], preferred_element_type=F32) + br_ref[...]
    lane = lax.broadcasted_iota(jnp.int32, (1, LANES), 1).astype(F32)
    top_v, top_i = [], []
    work = logits
    for _ in range(TOP_K):
        m = jnp.max(work, axis=-1, keepdims=True)
        i_sel = jnp.min(jnp.where(work == m, lane, float(LANES)), axis=-1, keepdims=True)
        top_v.append(m)
        top_i.append(i_sel)
        work = jnp.where(lane == i_sel, -jnp.inf, work)
    exps = [jnp.exp(tv - top_v[0]) for tv in top_v]
    denom = exps[0] + exps[1] + exps[2] + exps[3]
    onehots = [lane == ti for ti in top_i]
    member = jnp.zeros((ts, LANES), F32)
    for oh in onehots:
        member = member + jnp.where(oh, 1.0, 0.0)
    tr = lax.broadcasted_iota(jnp.int32, (ts, ts), 0)
    tc = lax.broadcasted_iota(jnp.int32, (ts, ts), 1)
    before = jnp.where(tc < tr, 1.0, 0.0).astype(MXU_DTYPE)
    cnt = cntacc_ref[...]
    local_rank = jnp.dot(before, member.astype(MXU_DTYPE), preferred_element_type=F32)
    sel = jnp.zeros((ts, LANES), F32)
    for kk in range(TOP_K):
        lrank_k = jnp.sum(jnp.where(onehots[kk], local_rank, 0.0), axis=-1, keepdims=True)
        sel = sel + jnp.where(lane == float(kk), top_i[kk], 0.0)
        sel = sel + jnp.where(lane == float(TOP_K + kk), exps[kk] / denom, 0.0)
        sel = sel + jnp.where(lane == float(2 * TOP_K + kk), lrank_k, 0.0)
    sel_ref[...] = sel
    before_ref[...] = cnt
    tile_cnt = jnp.sum(member, axis=0, keepdims=True)
    cnt = cnt + jnp.floor((tile_cnt + (RUN_ALIGN - 1.0)) * (1.0 / RUN_ALIGN)) * RUN_ALIGN
    cntacc_ref[...] = cnt
    cnt_ref[...] = cnt


def _const_spec(shape):
    return pl.BlockSpec(shape, lambda b, s: (0,) * len(shape))


def _mixer_call(x, ada_l, cos_t, sin_t, lw, tables):
    bsz, seq, d = x.shape
    ts = min(SEQ_TILE, seq)
    n_tok = bsz * seq
    row3 = lambda b, s: (b, s, 0)
    in_specs = [
        pl.BlockSpec((None, ts, d), row3),
        pl.BlockSpec((None, 6, d), lambda b, s: (b, 0, 0)),
        pl.BlockSpec((None, ts, RET_QK_WIDTH), row3),
        pl.BlockSpec((None, ts, RET_QK_WIDTH), row3),
    ] + [_const_spec(a.shape) for a in lw] + [_const_spec(a.shape) for a in tables]
    tok_row = lambda b, s: (b * (seq // ts) + s, 0)
    out_specs = [
        pl.BlockSpec((ts, d), tok_row),
        pl.BlockSpec((ts, d), tok_row),
        pl.BlockSpec((ts, LANES), tok_row),
        pl.BlockSpec((8, LANES), lambda b, s: (0, 0)),
        pl.BlockSpec((None, 8, LANES), lambda b, s: (b * (seq // ts) + s, 0, 0)),
    ]
    out_shape = [
        jax.ShapeDtypeStruct((n_tok, d), F32),
        jax.ShapeDtypeStruct((n_tok, d), MXU_DTYPE),
        jax.ShapeDtypeStruct((n_tok, LANES), F32),
        jax.ShapeDtypeStruct((8, LANES), F32),
        jax.ShapeDtypeStruct((n_tok // ts, 8, LANES), F32),
    ]
    return pl.pallas_call(
        _mixer_kernel,
        grid=(bsz, seq // ts),
        in_specs=in_specs,
        out_specs=out_specs,
        out_shape=out_shape,
        scratch_shapes=[
            pltpu.VMEM((RET_QK_WIDTH, RET_V_WIDTH), F32),
            pltpu.VMEM((8, CONV_WIDTH), F32),
            pltpu.VMEM((8, LANES), F32),
        ],
        compiler_params=pltpu.CompilerParams(
            dimension_semantics=("arbitrary", "arbitrary"), vmem_limit_bytes=VMEM_LIMIT),
        name="mixer",
    )(x, ada_l, cos_t, sin_t, *lw, *tables)


def _run_pieces(max_rows):
    units = max_rows // RUN_ALIGN
    pieces = []
    bit = 1
    while bit <= units:
        pieces.append(bit)
        bit *= 2
    return tuple(reversed(pieces))


def _piece_table_width(n_classes):
    return -(-(n_classes * 2 * N_EXPERTS + n_classes) // LANES) * LANES


def _for_each_run_piece(tab_ref, pieces, make_copy, fn):
    for c, bit in enumerate(pieces):
        count = tab_ref[0, 0, len(pieces) * 2 * N_EXPERTS + c]

        def body(j, carry, c=c, bit=bit):
            hbm_row = tab_ref[0, 0, c * 2 * N_EXPERTS + j]
            buf_row = tab_ref[0, 0, (c * 2 + 1) * N_EXPERTS + j]
            fn(make_copy(pl.multiple_of(hbm_row, RUN_ALIGN), pl.multiple_of(buf_row, RUN_ALIGN),
                         bit * RUN_ALIGN))
            return carry

        lax.fori_loop(0, count, body, 0)


def _run_positions(sel, off_row):
    lane = lax.broadcasted_iota(jnp.int32, (1, LANES), 1).astype(F32)
    out = []
    for kk in range(TOP_K):
        base = jnp.sum(jnp.where(lane == sel[:, kk:kk + 1], off_row, 0.0), axis=-1, keepdims=True)
        out.append(base + sel[:, 2 * TOP_K + kk:2 * TOP_K + kk + 1])
    return out


def _dispatch_kernel(pe_ref, ps_ref, tab_cur_ref, tab_prev_ref, off_ref, sel_ref, h2_ref, xs_hbm,
                     zbuf, sbuf, sem, zsem):
    i = pl.program_id(0)
    n_steps = pl.num_programs(0)
    ts = h2_ref.shape[0]
    blk = zbuf.shape[0]
    n_exp = pe_ref.shape[0]
    run_rows = sbuf.shape[1]
    slot = i % 2
    pieces = _run_pieces(ts)

    @pl.when(i == 0)
    def _():
        zbuf[...] = jnp.zeros_like(zbuf)

        def tail_copy(e):
            start = pl.multiple_of(jnp.maximum(pe_ref[e] - blk, 0), blk)
            return pltpu.make_async_copy(zbuf, xs_hbm.at[pl.ds(start, blk), :], zsem)

        def zstart(e, carry):
            @pl.when(pe_ref[e] > ps_ref[e])
            def _():
                tail_copy(e).start()
            return carry

        def zwait(e, carry):
            @pl.when(pe_ref[e] > ps_ref[e])
            def _():
                tail_copy(e).wait()
            return carry

        lax.fori_loop(0, n_exp, zstart, 0)
        lax.fori_loop(0, n_exp, zwait, 0)

    targets = _run_positions(sel_ref[...], off_ref[...][0:1])
    lane = lax.broadcasted_iota(jnp.int32, (1, LANES), 1).astype(F32)
    packed = jnp.zeros((ts, LANES), F32)
    for kk in range(TOP_K):
        packed = packed + jnp.where(lane == float(kk), targets[kk], 0.0)
    targets_t = packed.T
    buf_row = lax.broadcasted_iota(jnp.int32, (run_rows, ts), 0).astype(F32)
    pick = jnp.zeros((run_rows, ts), F32)
    for kk in range(TOP_K):
        pick = jnp.where(buf_row == targets_t[kk:kk + 1], 1.0, pick)
    sbuf[slot] = jnp.dot(pick.astype(MXU_DTYPE), h2_ref[...].astype(MXU_DTYPE),
                         preferred_element_type=F32).astype(sbuf.dtype)

    def push(dst_slot):
        def make(hbm_row, buf_row_, rows):
            return pltpu.make_async_copy(sbuf.at[dst_slot, pl.ds(buf_row_, rows), :],
                                         xs_hbm.at[pl.ds(hbm_row, rows), :], sem.at[dst_slot])
        return make

    _for_each_run_piece(tab_cur_ref, pieces, push(slot), lambda c: c.start())

    @pl.when(i > 0)
    def _():
        _for_each_run_piece(tab_prev_ref, pieces, push(1 - slot), lambda c: c.wait())

    @pl.when(i == n_steps - 1)
    def _():
        _for_each_run_piece(tab_cur_ref, pieces, push(slot), lambda c: c.wait())


def _dispatch_call(pad_end, pad_start, tab3, off3, sel, h2, n_rows, blk):
    n_tok, d = h2.shape
    n_steps = tab3.shape[0]
    ts = n_tok // n_steps
    run_rows = ts * TOP_K + N_EXPERTS * RUN_ALIGN
    grid_spec = pltpu.PrefetchScalarGridSpec(
        num_scalar_prefetch=2,
        grid=(n_steps,),
        in_specs=[
            pl.BlockSpec((1, 1, tab3.shape[2]), lambda i, pe, ps: (i, 0, 0), memory_space=pltpu.SMEM),
            pl.BlockSpec((1, 1, tab3.shape[2]), lambda i, pe, ps: (jnp.maximum(i - 1, 0), 0, 0),
                         memory_space=pltpu.SMEM),
            pl.BlockSpec((None, 8, LANES), lambda i, pe, ps: (i, 0, 0)),
            pl.BlockSpec((ts, LANES), lambda i, pe, ps: (i, 0)),
            pl.BlockSpec((ts, d), lambda i, pe, ps: (i, 0)),
        ],
        out_specs=pl.BlockSpec(memory_space=pl.ANY),
        scratch_shapes=[pltpu.VMEM((blk, d), MXU_DTYPE), pltpu.VMEM((2, run_rows, d), MXU_DTYPE),
                        pltpu.SemaphoreType.DMA((2,)), pltpu.SemaphoreType.DMA(())],
    )
    return pl.pallas_call(
        _dispatch_kernel,
        grid_spec=grid_spec,
        out_shape=jax.ShapeDtypeStruct((n_rows, d), MXU_DTYPE),
        compiler_params=pltpu.CompilerParams(
            dimension_semantics=("arbitrary",), vmem_limit_bytes=VMEM_LIMIT),
        name="dispatch",
    )(pad_end, pad_start, tab3, tab3, off3, sel, h2)


def _expert_mlp(xs, wgu_ref, bgu_ref, wdn_ref, bdn_ref):
    gu = jnp.dot(xs, wgu_ref[...], preferred_element_type=F32) + bgu_ref[...]
    gate = jnp.minimum(gu[:, :D_FF], SWIGLU_LIMIT)
    up = jnp.clip(gu[:, D_FF:], -SWIGLU_LIMIT, SWIGLU_LIMIT)
    act = (up + 1.0) * (gate * jax.nn.sigmoid(SWIGLU_ALPHA * gate))
    return jnp.dot(act.astype(MXU_DTYPE), wdn_ref[...], preferred_element_type=F32) + bdn_ref[...]


def _expert_kernel(be_ref, nu_ref, xs_ref, wgu_ref, bgu_ref, wdn_ref, bdn_ref, ys_ref):
    i = pl.program_id(0)
    n_used = nu_ref[0]

    @pl.when(i < n_used)
    def _():
        half = xs_ref.shape[0] // 2
        for part in range(2):
            rows = slice(part * half, (part + 1) * half)
            ys_ref[rows, :] = _expert_mlp(xs_ref[rows, :], wgu_ref, bgu_ref, wdn_ref, bdn_ref).astype(ys_ref.dtype)

    @pl.when(i >= n_used)
    def _():
        ys_ref[...] = jnp.zeros_like(ys_ref)


def _expert_call(block_e, n_used, xs, w_gu, b_gu, w_down, b_down, blk):
    n_rows, d = xs.shape
    n_blocks = n_rows // blk
    two_f = w_gu.shape[2]
    last = lambda nu: jnp.maximum(nu[0] - 1, 0)
    grid_spec = pltpu.PrefetchScalarGridSpec(
        num_scalar_prefetch=2,
        grid=(n_blocks,),
        in_specs=[
            pl.BlockSpec((blk, d), lambda i, be, nu: (jnp.minimum(i, last(nu)), 0)),
            pl.BlockSpec((None, d, two_f), lambda i, be, nu: (be[i], 0, 0)),
            pl.BlockSpec((None, 1, two_f), lambda i, be, nu: (be[i], 0, 0)),
            pl.BlockSpec((None, two_f // 2, d), lambda i, be, nu: (be[i], 0, 0)),
            pl.BlockSpec((None, 1, d), lambda i, be, nu: (be[i], 0, 0)),
        ],
        out_specs=pl.BlockSpec((blk, d), lambda i, be, nu: (i, 0)),
    )
    n_exp = w_gu.shape[0]
    return pl.pallas_call(
        _expert_kernel,
        grid_spec=grid_spec,
        out_shape=jax.ShapeDtypeStruct((n_rows, d), MXU_DTYPE),
        compiler_params=pltpu.CompilerParams(
            dimension_semantics=("arbitrary",), vmem_limit_bytes=VMEM_LIMIT),
        name="experts",
    )(block_e, n_used, xs, w_gu, b_gu.reshape(n_exp, 1, two_f), w_down, b_down.reshape(n_exp, 1, d))


def _combine_kernel(tab_cur_ref, tab_nxt_ref, ys_hbm, off_ref, x1_ref, sel_ref, ada_ref, lng_ref, lnb_ref,
                    o_ref, rbuf, sem):
    i = pl.program_id(0)
    n_steps = pl.num_programs(0)
    ts = x1_ref.shape[0]
    slot = i % 2
    pieces = _run_pieces(ts)

    def fetch(dst_slot):
        def make(hbm_row, buf_row, rows):
            return pltpu.make_async_copy(ys_hbm.at[pl.ds(hbm_row, rows), :],
                                         rbuf.at[dst_slot, pl.ds(buf_row, rows), :], sem.at[dst_slot])
        return make

    @pl.when(i == 0)
    def _():
        rbuf[...] = jnp.zeros_like(rbuf)
        _for_each_run_piece(tab_cur_ref, pieces, fetch(0), lambda c: c.start())

    @pl.when(i + 1 < n_steps)
    def _():
        _for_each_run_piece(tab_nxt_ref, pieces, fetch(1 - slot), lambda c: c.start())

    _for_each_run_piece(tab_cur_ref, pieces, fetch(slot), lambda c: c.wait())

    sel = sel_ref[...]
    targets = _run_positions(sel, off_ref[...][0:1])
    col = lax.broadcasted_iota(jnp.int32, (1, rbuf.shape[1]), 1).astype(F32)
    weights = None
    for kk in range(TOP_K):
        term = jnp.where(col == targets[kk], sel[:, TOP_K + kk:TOP_K + kk + 1], 0.0)
        weights = term if weights is None else weights + term
    ffn = _split_dot(weights, rbuf[slot])
    gt2 = ada_ref[...][5:6]
    o_ref[...] = _layernorm_rows(DEEPNORM_ALPHA * x1_ref[...] + (1.0 + gt2) * ffn, lng_ref[...], lnb_ref[...])


def _combine_call(tab3, off3, ys, x1, sel, ada_l, ln_g, ln_b, seq):
    n_tok, d = x1.shape
    n_steps = tab3.shape[0]
    ts = n_tok // n_steps
    tiles_per_seq = seq // ts
    run_rows = ts * TOP_K + N_EXPERTS * RUN_ALIGN
    return pl.pallas_call(
        _combine_kernel,
        grid=(n_steps,),
        in_specs=[
            pl.BlockSpec((1, 1, tab3.shape[2]), lambda i: (i, 0, 0), memory_space=pltpu.SMEM),
            pl.BlockSpec((1, 1, tab3.shape[2]), lambda i: (jnp.minimum(i + 1, n_steps - 1), 0, 0),
                         memory_space=pltpu.SMEM),
            pl.BlockSpec(memory_space=pl.ANY),
            pl.BlockSpec((None, 8, LANES), lambda i: (i, 0, 0)),
            pl.BlockSpec((ts, d), lambda i: (i, 0)),
            pl.BlockSpec((ts, LANES), lambda i: (i, 0)),
            pl.BlockSpec((None, 6, d), lambda i: (i // tiles_per_seq, 0, 0)),
            pl.BlockSpec((1, d), lambda i: (0, 0)),
            pl.BlockSpec((1, d), lambda i: (0, 0)),
        ],
        out_specs=pl.BlockSpec((ts, d), lambda i: (i, 0)),
        out_shape=jax.ShapeDtypeStruct((n_tok, d), F32),
        scratch_shapes=[pltpu.VMEM((2, run_rows, d), MXU_DTYPE), pltpu.SemaphoreType.DMA((2,))],
        compiler_params=pltpu.CompilerParams(
            dimension_semantics=("arbitrary",), vmem_limit_bytes=VMEM_LIMIT),
        name="combine",
    )(tab3, tab3, ys, off3, x1, sel, ada_l, ln_g.reshape(1, d), ln_b.reshape(1, d))


def _retention_tables():
    log_gamma = jnp.log(1.0 - 2.0 ** (-5.0 - jnp.arange(RET_HEADS, dtype=F32)))
    idx = jnp.arange(CHUNK, dtype=F32)
    diff = idx[:, None] - idx[None, :]
    decay = jnp.where(diff[None] >= 0,
                      jnp.exp(log_gamma[:, None, None] * jnp.maximum(diff, 0.0)[None]), 0.0)
    decay = decay.reshape(RET_HEADS * CHUNK, CHUNK)
    zeta = jnp.exp(log_gamma[:, None] * (CHUNK - 1.0 - idx)[None])
    zeta_t = jnp.repeat(zeta.T, RET_QK_DIM, axis=1)
    xi = jnp.exp(log_gamma[:, None] * (idx + 1.0)[None]).T
    xi_t = jnp.repeat(xi, RET_V_DIM, axis=1)
    cdec = jnp.repeat(jnp.exp(log_gamma * CHUNK), RET_V_DIM)[None, :]
    row_head = jnp.arange(RET_QK_WIDTH) // RET_QK_DIM
    col_head = jnp.arange(RET_V_WIDTH) // RET_V_DIM
    same_head = row_head[:, None] == col_head[None, :]
    bmask = same_head.astype(F32)
    gavg = ((col_head[:, None] == col_head[None, :]).astype(F32) / RET_V_DIM).astype(MXU_DTYPE)
    return decay, zeta_t, xi_t, cdec, bmask, gavg


def _rotary_tables(positions):
    inv_freq = ROPE_BASE ** (-jnp.arange(0, RET_QK_DIM, 2, dtype=F32) / RET_QK_DIM)
    ang = positions.astype(F32)[..., None] * inv_freq
    cos = jnp.cos(ang)
    sin = jnp.sin(ang)
    cos_t = jnp.tile(jnp.concatenate([cos, cos], axis=-1), (1, 1, RET_HEADS))
    sin_t = jnp.tile(jnp.concatenate([-sin, sin], axis=-1), (1, 1, RET_HEADS))
    return cos_t, sin_t


def _moe_rows(n_tok, n_tiles, blk):
    worst = n_tok * TOP_K + n_tiles * N_EXPERTS * (RUN_ALIGN - 1)
    return (-(-worst // blk) + N_EXPERTS) * blk


def _routing_tables(counts, before, n_rows, blk, pieces):
    n_blocks = n_rows // blk
    sizes = counts.astype(jnp.int32)
    pad_sizes = (sizes + blk - 1) // blk * blk
    pad_end = jnp.cumsum(pad_sizes)
    pad_start = pad_end - pad_sizes
    n_used = (pad_end[-1] // blk).astype(jnp.int32)
    block_start = jnp.arange(n_blocks, dtype=jnp.int32) * blk
    block_start = jnp.minimum(block_start, (n_used - 1) * blk)
    block_e = jnp.sum((pad_end[None, :] <= block_start[:, None]).astype(jnp.int32), axis=-1)
    block_e = jnp.minimum(block_e, N_EXPERTS - 1)

    before = before.astype(jnp.int32)
    after = jnp.concatenate([before[1:], sizes[None, :]], axis=0)
    run_start = pad_start[None, :] + before
    run_units = (after - before) // RUN_ALIGN
    run_off = (jnp.cumsum(run_units, axis=-1) - run_units) * RUN_ALIGN
    n_tiles = before.shape[0]
    slot_id = jnp.arange(N_EXPERTS, dtype=jnp.int32)
    cols, counts_c = [], []
    for bit in pieces:
        has = jnp.bitwise_and(run_units, bit) != 0
        done = jnp.bitwise_and(run_units, -2 * bit) * RUN_ALIGN
        pos = jnp.cumsum(has.astype(jnp.int32), axis=-1) - 1
        place = jnp.logical_and(has[:, None, :], pos[:, None, :] == slot_id[None, :, None])
        cols.append(jnp.sum(jnp.where(place, (run_start + done)[:, None, :], 0), axis=-1))
        cols.append(jnp.sum(jnp.where(place, (run_off + done)[:, None, :], 0), axis=-1))
        counts_c.append(jnp.sum(has.astype(jnp.int32), axis=-1, keepdims=True))
    width = _piece_table_width(len(pieces))
    used = len(pieces) * (2 * N_EXPERTS + 1)
    tab = jnp.concatenate(cols + counts_c + [jnp.zeros((n_tiles, width - used), jnp.int32)], axis=-1)
    off3 = jnp.zeros((n_tiles, 8, LANES), F32).at[:, 0, :N_EXPERTS].set(run_off.astype(F32))
    return (pad_end.astype(jnp.int32), pad_start.astype(jnp.int32), block_e, n_used.reshape(1),
            tab.astype(jnp.int32).reshape(n_tiles, 1, width), off3)


def kernel(x, c, positions, w_ada, b_ada, w_in, gmlp_ln_g, gmlp_ln_b, w_spatial, b_spatial, conv_w,
           w_branch, w_gate_up, b_gate, w_out, ln1_g, ln1_b, w_router, b_router, w_gu, b_gu, w_down,
           b_down, ln2_g, ln2_b):
    bsz, seq, d = x.shape
    n_tok = bsz * seq
    depth = w_ada.shape[0]
    blk = min(MOE_BLOCK, n_tok * TOP_K // N_EXPERTS)
    ts = min(SEQ_TILE, seq)

    ada = _ada_call(c, w_ada, b_ada)
    ada = jnp.transpose(ada, (0, 2, 1, 3))
    cos_t, sin_t = _rotary_tables(positions)
    tables = _retention_tables()

    mx = MXU_DTYPE
    w_in_m = w_in.astype(mx)
    w_branch_m = w_branch.astype(mx)
    w_gate_m = w_gate_up.astype(mx)
    w_out_m = w_out.astype(mx)
    w_gu_m = w_gu.astype(mx)
    w_down_m = w_down.astype(mx)
    n_exp = w_router.shape[-1]
    w_router_m = jnp.pad(w_router, ((0, 0), (0, 0), (0, LANES - n_exp))).astype(mx)
    b_router_p = jnp.pad(b_router, ((0, 0), (0, LANES - n_exp)), constant_values=NEG_BIG)

    for l in range(depth):
        bsp_t = jnp.repeat(b_spatial[l].T, GMLP_WIDTH // GMLP_GROUPS, axis=1)
        lw = (
            w_in_m[l],
            gmlp_ln_g[l].reshape(1, -1), gmlp_ln_b[l].reshape(1, -1),
            w_spatial[l].reshape(GMLP_GROUPS * CHUNK, CHUNK), bsp_t,
            jnp.pad(conv_w[l], ((0, 8 - conv_w.shape[1]), (0, 0))),
            w_branch_m[l], w_gate_m[l], b_gate[l], w_out_m[l],
            ln1_g[l].reshape(1, -1), ln1_b[l].reshape(1, -1),
            w_router_m[l], b_router_p[l].reshape(1, -1),
        )
        x1, h2, sel, cnt, before = _mixer_call(x, ada[l], cos_t, sin_t, lw, tables)
        n_rows = _moe_rows(n_tok, n_tok // ts, blk)
        pad_end, pad_start, block_e, n_used, tab3, off3 = _routing_tables(
            cnt[0, :n_exp], before[:, 0, :n_exp], n_rows, blk, _run_pieces(ts))
        xs = _dispatch_call(pad_end, pad_start, tab3, off3, sel, h2, n_rows, blk)
        ys = _expert_call(block_e, n_used, xs, w_gu_m[l], b_gu[l], w_down_m[l], b_down[l], blk)
        x = _combine_call(tab3, off3, ys, x1, sel, ada[l], ln2_g[l], ln2_b[l], seq).reshape(bsz, seq, d)
    return x
```

```python
import functools

import jax
import jax.numpy as jnp
from jax import lax
from jax.experimental import pallas as pl
from jax.experimental.pallas import tpu as pltpu

F32 = jnp.float32
MXU_DTYPE = jnp.bfloat16

DEPTH = 4
RET_HEADS = 4
RET_QK_DIM = 32
RET_V_DIM = 64
RET_QK_WIDTH = RET_HEADS * RET_QK_DIM
RET_V_WIDTH = RET_HEADS * RET_V_DIM
CHUNK = 128
ROPE_BASE = 10000.0
GMLP_GROUPS = 4
GMLP_WIDTH = 256
CONV_WIDTH = 256
GATE_RANK = 128
N_BRANCH = 3
N_EXPERTS = 32
TOP_K = 4
D_FF = 256
SWIGLU_LIMIT = 7.0
SWIGLU_ALPHA = 1.702
DEEPNORM_ALPHA = (2.0 * DEPTH) ** 0.25
LN_EPS = 1e-5

_O_QK = 0
_O_VG = 2 * RET_QK_WIDTH
_O_GMLP = _O_VG + 2 * RET_V_WIDTH
_O_CONV = _O_GMLP + 2 * GMLP_WIDTH
_O_CODE = _O_CONV + 3 * CONV_WIDTH
_IN_WIDTH = _O_CODE + N_BRANCH * GATE_RANK

LANES = 128
NEG_BIG = -1e30

SEQ_TILE = 256
MOE_BLOCK = 1024
EXPERT_PART = 256
RUN_ALIGN = 16
VMEM_LIMIT = 56 * 1024 * 1024


def _dot(a, b):
    return jnp.dot(a.astype(MXU_DTYPE), b.astype(MXU_DTYPE), preferred_element_type=F32)


def _dot_nt(a, b):
    return lax.dot_general(a.astype(MXU_DTYPE), b.astype(MXU_DTYPE),
                           (((1,), (1,)), ((), ())), preferred_element_type=F32)


def _split_dot(x, w):
    hi = x.astype(MXU_DTYPE)
    lo = (x - hi.astype(F32)).astype(MXU_DTYPE)
    return (jnp.dot(hi, w, preferred_element_type=F32)
            + jnp.dot(lo, w, preferred_element_type=F32))


def _layernorm_rows(x, g, b):
    mu = jnp.mean(x, axis=-1, keepdims=True)
    d = x - mu
    var = jnp.mean(d * d, axis=-1, keepdims=True)
    return d * lax.rsqrt(var + LN_EPS) * g + b


def _ada_kernel(c_ref, w_ref, b_ref, o_ref):
    c_act = jax.nn.silu(c_ref[...])
    o_ref[...] = jnp.dot(c_act, w_ref[...], preferred_element_type=F32,
                         precision=lax.Precision.HIGHEST) + b_ref[...]


def _ada_call(c, w_ada, b_ada):
    depth, d, six_d = w_ada.shape
    bsz = c.shape[0]
    n_col = six_d // d
    return pl.pallas_call(
        _ada_kernel,
        grid=(depth, n_col),
        in_specs=[
            pl.BlockSpec((bsz, d), lambda l, j: (0, 0)),
            pl.BlockSpec((None, d, d), lambda l, j: (l, 0, j)),
            pl.BlockSpec((None, 1, d), lambda l, j: (l, 0, j)),
        ],
        out_specs=pl.BlockSpec((None, None, bsz, d), lambda l, j: (l, j, 0, 0)),
        out_shape=jax.ShapeDtypeStruct((depth, n_col, bsz, d), F32),
        compiler_params=pltpu.CompilerParams(
            dimension_semantics=("arbitrary", "arbitrary"), vmem_limit_bytes=VMEM_LIMIT),
        name="ada_ln",
    )(c, w_ada, b_ada.reshape(depth, 1, six_d))


def _mixer_kernel(x_ref, ada_ref, cos_ref, sin_ref, w_in_ref, lng_ref, lnb_ref, wsp_ref, bsp_ref,
                  convw_ref, wbr_ref, wgt_ref, bgt_ref, wout_ref, ln1g_ref, ln1b_ref, wr_ref, br_ref,
                  decay_ref, zeta_ref, xi_ref, cdec_ref, bmask_ref, gavg_ref,
                  x1_ref, h2_ref, sel_ref, cnt_ref, before_ref,
                  state_ref, zc_ref, cntacc_ref):
    b = pl.program_id(0)
    s = pl.program_id(1)
    ts = x_ref.shape[0]

    @pl.when(s == 0)
    def _():
        state_ref[...] = jnp.zeros_like(state_ref)
        zc_ref[...] = jnp.zeros_like(zc_ref)

    @pl.when(jnp.logical_and(b == 0, s == 0))
    def _():
        cntacc_ref[...] = jnp.zeros_like(cntacc_ref)

    x = x_ref[...]
    ada = ada_ref[...]
    sh1, sc1, gt1, sh2, sc2 = ada[0:1], ada[1:2], ada[2:3], ada[3:4], ada[4:5]
    h = (x * (1.0 + sc1) + sh1).astype(MXU_DTYPE)

    def proj(lo, hi):
        return jnp.dot(h, w_in_ref[:, lo:hi], preferred_element_type=F32)

    lane_qk = lax.broadcasted_iota(jnp.int32, (1, RET_QK_WIDTH), 1)
    lane_v = lax.broadcasted_iota(jnp.int32, (1, RET_V_WIDTH), 1)
    qk_masks = [lane_qk // RET_QK_DIM == hd for hd in range(RET_HEADS)]
    v_masks = [lane_v // RET_V_DIM == hd for hd in range(RET_HEADS)]

    qk = proj(_O_QK, _O_VG)
    cos = cos_ref[...]
    sin = sin_ref[...]
    first_half = (lane_qk % RET_QK_DIM) < (RET_QK_DIM // 2)

    def rotary(t):
        swapped = jnp.where(first_half, pltpu.roll(t, RET_QK_WIDTH - RET_QK_DIM // 2, 1),
                            pltpu.roll(t, RET_QK_DIM // 2, 1))
        return t * cos + swapped * sin

    q = rotary(qk[:, :RET_QK_WIDTH])
    k = rotary(qk[:, RET_QK_WIDTH:]) * (RET_QK_DIM ** -0.5)
    vg = proj(_O_VG, _O_GMLP)
    v = vg[:, :RET_V_WIDTH]
    g = vg[:, RET_V_WIDTH:]

    decay = decay_ref[...]
    zeta = zeta_ref[...]
    xi = xi_ref[...]
    cdec = cdec_ref[...]
    bmask = bmask_ref[...]
    gavg = gavg_ref[...]

    o_chunks = []
    state = state_ref[...]
    for c in range(ts // CHUNK):
        rows = slice(c * CHUNK, (c + 1) * CHUNK)
        q_c, k_c, v_c = q[rows], k[rows], v[rows]
        v_m = v_c.astype(MXU_DTYPE)
        q_all = jnp.concatenate([jnp.where(m, q_c, 0.0) for m in qk_masks], axis=0)
        scores = _dot_nt(q_all, k_c) * decay
        o_all = _dot(scores, v_m)
        o = _dot(q_c, state) * xi
        for hd in range(RET_HEADS):
            o = o + jnp.where(v_masks[hd], o_all[hd * CHUNK:(hd + 1) * CHUNK], 0.0)
        kv = _dot((k_c * zeta).T, v_m) * bmask
        state = state * cdec + kv
        o_chunks.append(o)
    state_ref[...] = state
    o = jnp.concatenate(o_chunks, axis=0) if len(o_chunks) > 1 else o_chunks[0]
    mu = _split_dot(o, gavg)
    d = o - mu
    var = _split_dot(d * d, gavg)
    r_br = jax.nn.silu(g) * (d * lax.rsqrt(var + LN_EPS))

    guv = proj(_O_GMLP, _O_CONV)
    u = jax.nn.gelu(guv[:, :GMLP_WIDTH])
    vv = _layernorm_rows(jax.nn.gelu(guv[:, GMLP_WIDTH:]), lng_ref[...], lnb_ref[...]).astype(MXU_DTYPE)
    rr = lax.broadcasted_iota(jnp.int32, (GMLP_GROUPS * CHUNK, CHUNK), 0) % CHUNK
    cc_ = lax.broadcasted_iota(jnp.int32, (GMLP_GROUPS * CHUNK, CHUNK), 1)
    w_sp = jnp.where(cc_ <= rr, wsp_ref[...], 0.0).astype(MXU_DTYPE)
    bsp = bsp_ref[...]
    z_chunks = []
    for c in range(ts // CHUNK):
        z_all = jnp.dot(w_sp, vv[c * CHUNK:(c + 1) * CHUNK], preferred_element_type=F32)
        z = bsp
        for gi in range(GMLP_GROUPS):
            z = z + jnp.where(v_masks[gi], z_all[gi * CHUNK:(gi + 1) * CHUNK], 0.0)
        z_chunks.append(z)
    z = jnp.concatenate(z_chunks, axis=0) if len(z_chunks) > 1 else z_chunks[0]
    s_br = u * z

    cbcx = proj(_O_CONV, _O_CODE)
    gate_b = cbcx[:, :CONV_WIDTH]
    zc = cbcx[:, CONV_WIDTH:2 * CONV_WIDTH] * cbcx[:, 2 * CONV_WIDTH:]
    carry = zc_ref[...]
    prev1 = carry[7:8]
    prev2 = carry[6:7]
    row = lax.broadcasted_iota(jnp.int32, (ts, 1), 0)
    z1 = jnp.where(row == 0, prev1, pltpu.roll(zc, 1, 0))
    z2 = jnp.where(row == 0, prev2, jnp.where(row == 1, prev1, pltpu.roll(zc, 2, 0)))
    zc_ref[...] = zc[ts - 8:ts]
    cw = convw_ref[...]
    k_br = gate_b * (cw[0:1] * z2 + cw[1:2] * z1 + cw[2:3] * zc)

    code = proj(_O_CODE, _IN_WIDTH)
    merged = None
    for i, br in enumerate((r_br, s_br, k_br)):
        y = jnp.dot(br.astype(MXU_DTYPE), wbr_ref[i], preferred_element_type=F32)
        gl = jnp.dot(code[:, i * GATE_RANK:(i + 1) * GATE_RANK].astype(MXU_DTYPE), wgt_ref[i],
                     preferred_element_type=F32) + bgt_ref[i:i + 1]
        t = jax.nn.sigmoid(gl) * y
        merged = t if merged is None else merged + t
    mix = jnp.dot(merged.astype(MXU_DTYPE), wout_ref[...], preferred_element_type=F32)
    x1 = _layernorm_rows(DEEPNORM_ALPHA * x + (1.0 + gt1) * mix, ln1g_ref[...], ln1b_ref[...])
    x1_ref[...] = x1
    h2 = (x1 * (1.0 + sc2) + sh2).astype(MXU_DTYPE)
    h2_ref[...] = h2

    logits = jnp.dot(h2, wr_ref[...], preferred_element_type=F32) + br_ref[...]
    lane = lax.broadcasted_iota(jnp.int32, (1, LANES), 1).astype(F32)
    top_v, top_i = [], []
    work = logits
    for _ in range(TOP_K):
        m = jnp.max(work, axis=-1, keepdims=True)
        i_sel = jnp.argmax(work, axis=-1, keepdims=True).astype(F32)
        top_v.append(m)
        top_i.append(i_sel)
        work = jnp.where(lane == i_sel, -jnp.inf, work)
    exps = [jnp.exp(tv - top_v[0]) for tv in top_v]
    denom = exps[0] + exps[1] + exps[2] + exps[3]
    onehots = [lane == ti for ti in top_i]
    member = jnp.zeros((ts, LANES), F32)
    for oh in onehots:
        member = member + jnp.where(oh, 1.0, 0.0)
    tr = lax.broadcasted_iota(jnp.int32, (ts, ts), 0)
    tc = lax.broadcasted_iota(jnp.int32, (ts, ts), 1)
    before = jnp.where(tc < tr, 1.0, 0.0).astype(MXU_DTYPE)
    cnt = cntacc_ref[...]
    local_rank = jnp.dot(before, member.astype(MXU_DTYPE), preferred_element_type=F32)
    sel = jnp.zeros((ts, LANES), F32)
    for kk in range(TOP_K):
        lrank_k = jnp.sum(jnp.where(onehots[kk], local_rank, 0.0), axis=-1, keepdims=True)
        sel = sel + jnp.where(lane == float(kk), top_i[kk], 0.0)
        sel = sel + jnp.where(lane == float(TOP_K + kk), exps[kk] / denom, 0.0)
        sel = sel + jnp.where(lane == float(2 * TOP_K + kk), lrank_k, 0.0)
    sel_ref[...] = sel
    before_ref[...] = cnt
    tile_cnt = jnp.sum(member, axis=0, keepdims=True)
    cnt = cnt + jnp.floor((tile_cnt + (RUN_ALIGN - 1.0)) * (1.0 / RUN_ALIGN)) * RUN_ALIGN
    cntacc_ref[...] = cnt
    cnt_ref[...] = cnt


def _const_spec(shape):
    return pl.BlockSpec(shape, lambda b, s: (0,) * len(shape))


def _mixer_call(x, ada_l, cos_t, sin_t, lw, tables):
    bsz, seq, d = x.shape
    ts = min(SEQ_TILE, seq)
    n_tok = bsz * seq
    row3 = lambda b, s: (b, s, 0)
    in_specs = [
        pl.BlockSpec((None, ts, d), row3),
        pl.BlockSpec((None, 6, d), lambda b, s: (b, 0, 0)),
        pl.BlockSpec((None, ts, RET_QK_WIDTH), row3),
        pl.BlockSpec((None, ts, RET_QK_WIDTH), row3),
    ] + [_const_spec(a.shape) for a in lw] + [_const_spec(a.shape) for a in tables]
    tok_row = lambda b, s: (b * (seq // ts) + s, 0)
    out_specs = [
        pl.BlockSpec((ts, d), tok_row),
        pl.BlockSpec((ts, d), tok_row),
        pl.BlockSpec((ts, LANES), tok_row),
        pl.BlockSpec((8, LANES), lambda b, s: (0, 0)),
        pl.BlockSpec((None, 8, LANES), lambda b, s: (b * (seq // ts) + s, 0, 0)),
    ]
    out_shape = [
        jax.ShapeDtypeStruct((n_tok, d), F32),
        jax.ShapeDtypeStruct((n_tok, d), MXU_DTYPE),
        jax.ShapeDtypeStruct((n_tok, LANES), F32),
        jax.ShapeDtypeStruct((8, LANES), F32),
        jax.ShapeDtypeStruct((n_tok // ts, 8, LANES), F32),
    ]
    return pl.pallas_call(
        _mixer_kernel,
        grid=(bsz, seq // ts),
        in_specs=in_specs,
        out_specs=out_specs,
        out_shape=out_shape,
        scratch_shapes=[
            pltpu.VMEM((RET_QK_WIDTH, RET_V_WIDTH), F32),
            pltpu.VMEM((8, CONV_WIDTH), F32),
            pltpu.VMEM((8, LANES), F32),
        ],
        compiler_params=pltpu.CompilerParams(
            dimension_semantics=("arbitrary", "arbitrary"), vmem_limit_bytes=VMEM_LIMIT),
        name="mixer",
    )(x, ada_l, cos_t, sin_t, *lw, *tables)


def _run_pieces(max_rows):
    units = max_rows // RUN_ALIGN
    pieces = []
    bit = 1
    while bit <= units:
        pieces.append(bit)
        bit *= 2
    return tuple(reversed(pieces))


def _piece_table_width(n_classes):
    return -(-(n_classes * 2 * N_EXPERTS + n_classes) // LANES) * LANES


def _for_each_run_piece(tab_ref, pieces, make_copy, fn):
    for c, bit in enumerate(pieces):
        count = tab_ref[0, 0, len(pieces) * 2 * N_EXPERTS + c]

        def body(j, carry, c=c, bit=bit):
            hbm_row = tab_ref[0, 0, c * 2 * N_EXPERTS + j]
            buf_row = tab_ref[0, 0, (c * 2 + 1) * N_EXPERTS + j]
            fn(make_copy(pl.multiple_of(hbm_row, RUN_ALIGN), pl.multiple_of(buf_row, RUN_ALIGN),
                         bit * RUN_ALIGN))
            return carry

        lax.fori_loop(0, count, body, 0)


SEL_IDX, SEL_WEIGHT, SEL_RANK, SEL_ROW = 0, TOP_K, 2 * TOP_K, 3 * TOP_K


def _dispatch_kernel(pe_ref, ps_ref, tab_cur_ref, tab_prev_ref, sel_ref, h2_ref, xs_hbm,
                     zbuf, sbuf, sem, zsem):
    i = pl.program_id(0)
    n_steps = pl.num_programs(0)
    ts = h2_ref.shape[0]
    blk = zbuf.shape[0]
    n_exp = pe_ref.shape[0]
    run_rows = sbuf.shape[1]
    slot = i % 2
    pieces = _run_pieces(ts)

    @pl.when(i == 0)
    def _():
        zbuf[...] = jnp.zeros_like(zbuf)

        def tail_copy(e):
            start = pl.multiple_of(jnp.maximum(pe_ref[e] - blk, 0), blk)
            return pltpu.make_async_copy(zbuf, xs_hbm.at[pl.ds(start, blk), :], zsem)

        def zstart(e, carry):
            @pl.when(pe_ref[e] > ps_ref[e])
            def _():
                tail_copy(e).start()
            return carry

        def zwait(e, carry):
            @pl.when(pe_ref[e] > ps_ref[e])
            def _():
                tail_copy(e).wait()
            return carry

        lax.fori_loop(0, n_exp, zstart, 0)
        lax.fori_loop(0, n_exp, zwait, 0)

    sel_t = sel_ref[...].T
    buf_row = lax.broadcasted_iota(jnp.int32, (run_rows, ts), 0).astype(F32)
    pick = jnp.zeros((run_rows, ts), F32)
    for kk in range(TOP_K):
        pick = jnp.where(buf_row == sel_t[SEL_ROW + kk:SEL_ROW + kk + 1], 1.0, pick)
    sbuf[slot] = jnp.dot(pick.astype(MXU_DTYPE), h2_ref[...].astype(MXU_DTYPE),
                         preferred_element_type=F32).astype(sbuf.dtype)

    def push(dst_slot):
        def make(hbm_row, buf_row_, rows):
            return pltpu.make_async_copy(sbuf.at[dst_slot, pl.ds(buf_row_, rows), :],
                                         xs_hbm.at[pl.ds(hbm_row, rows), :], sem.at[dst_slot])
        return make

    _for_each_run_piece(tab_cur_ref, pieces, push(slot), lambda c: c.start())

    @pl.when(i > 0)
    def _():
        _for_each_run_piece(tab_prev_ref, pieces, push(1 - slot), lambda c: c.wait())

    @pl.when(i == n_steps - 1)
    def _():
        _for_each_run_piece(tab_cur_ref, pieces, push(slot), lambda c: c.wait())


def _dispatch_call(pad_end, pad_start, tab3, sel, h2, n_rows, blk):
    n_tok, d = h2.shape
    n_steps = tab3.shape[0]
    ts = n_tok // n_steps
    run_rows = ts * TOP_K + N_EXPERTS * RUN_ALIGN
    grid_spec = pltpu.PrefetchScalarGridSpec(
        num_scalar_prefetch=2,
        grid=(n_steps,),
        in_specs=[
            pl.BlockSpec((1, 1, tab3.shape[2]), lambda i, pe, ps: (i, 0, 0), memory_space=pltpu.SMEM),
            pl.BlockSpec((1, 1, tab3.shape[2]), lambda i, pe, ps: (jnp.maximum(i - 1, 0), 0, 0),
                         memory_space=pltpu.SMEM),
            pl.BlockSpec((ts, LANES), lambda i, pe, ps: (i, 0)),
            pl.BlockSpec((ts, d), lambda i, pe, ps: (i, 0)),
        ],
        out_specs=pl.BlockSpec(memory_space=pl.ANY),
        scratch_shapes=[pltpu.VMEM((blk, d), MXU_DTYPE), pltpu.VMEM((2, run_rows, d), MXU_DTYPE),
                        pltpu.SemaphoreType.DMA((2,)), pltpu.SemaphoreType.DMA(())],
    )
    return pl.pallas_call(
        _dispatch_kernel,
        grid_spec=grid_spec,
        out_shape=jax.ShapeDtypeStruct((n_rows, d), MXU_DTYPE),
        compiler_params=pltpu.CompilerParams(
            dimension_semantics=("arbitrary",), vmem_limit_bytes=VMEM_LIMIT),
        name="dispatch",
    )(pad_end, pad_start, tab3, tab3, sel, h2)


def _expert_mlp(xs, wgu_ref, bgu_ref, wdn_ref, bdn_ref):
    gu = jnp.dot(xs, wgu_ref[...], preferred_element_type=F32) + bgu_ref[...]
    gate = jnp.minimum(gu[:, :D_FF], SWIGLU_LIMIT)
    up = jnp.clip(gu[:, D_FF:], -SWIGLU_LIMIT, SWIGLU_LIMIT)
    act = (up + 1.0) * (gate * jax.nn.sigmoid(SWIGLU_ALPHA * gate))
    return jnp.dot(act.astype(MXU_DTYPE), wdn_ref[...], preferred_element_type=F32) + bdn_ref[...]


def _expert_kernel(be_ref, nu_ref, xs_ref, wgu_ref, bgu_ref, wdn_ref, bdn_ref, ys_ref):
    i = pl.program_id(0)
    n_used = nu_ref[0]

    @pl.when(i < n_used)
    def _():
        part = min(EXPERT_PART, xs_ref.shape[0])
        for p in range(xs_ref.shape[0] // part):
            rows = slice(p * part, (p + 1) * part)
            ys_ref[rows, :] = _expert_mlp(xs_ref[rows, :], wgu_ref, bgu_ref, wdn_ref, bdn_ref).astype(ys_ref.dtype)

    @pl.when(i >= n_used)
    def _():
        ys_ref[...] = jnp.zeros_like(ys_ref)


def _expert_call(block_e, n_used, xs, w_gu, b_gu, w_down, b_down, blk):
    n_rows, d = xs.shape
    n_blocks = n_rows // blk
    two_f = w_gu.shape[2]
    last = lambda nu: jnp.maximum(nu[0] - 1, 0)
    grid_spec = pltpu.PrefetchScalarGridSpec(
        num_scalar_prefetch=2,
        grid=(n_blocks,),
        in_specs=[
            pl.BlockSpec((blk, d), lambda i, be, nu: (jnp.minimum(i, last(nu)), 0)),
            pl.BlockSpec((None, d, two_f), lambda i, be, nu: (be[i], 0, 0)),
            pl.BlockSpec((None, 1, two_f), lambda i, be, nu: (be[i], 0, 0)),
            pl.BlockSpec((None, two_f // 2, d), lambda i, be, nu: (be[i], 0, 0)),
            pl.BlockSpec((None, 1, d), lambda i, be, nu: (be[i], 0, 0)),
        ],
        out_specs=pl.BlockSpec((blk, d), lambda i, be, nu: (i, 0)),
    )
    n_exp = w_gu.shape[0]
    return pl.pallas_call(
        _expert_kernel,
        grid_spec=grid_spec,
        out_shape=jax.ShapeDtypeStruct((n_rows, d), MXU_DTYPE),
        compiler_params=pltpu.CompilerParams(
            dimension_semantics=("arbitrary",), vmem_limit_bytes=VMEM_LIMIT),
        name="experts",
    )(block_e, n_used, xs, w_gu, b_gu.reshape(n_exp, 1, two_f), w_down, b_down.reshape(n_exp, 1, d))


def _combine_kernel(tab_cur_ref, tab_nxt_ref, ys_hbm, x1_ref, sel_ref, ada_ref, lng_ref, lnb_ref,
                    o_ref, rbuf, sem):
    i = pl.program_id(0)
    n_steps = pl.num_programs(0)
    ts = x1_ref.shape[0]
    slot = i % 2
    pieces = _run_pieces(ts)

    def fetch(dst_slot):
        def make(hbm_row, buf_row, rows):
            return pltpu.make_async_copy(ys_hbm.at[pl.ds(hbm_row, rows), :],
                                         rbuf.at[dst_slot, pl.ds(buf_row, rows), :], sem.at[dst_slot])
        return make

    @pl.when(i == 0)
    def _():
        rbuf[...] = jnp.zeros_like(rbuf)
        _for_each_run_piece(tab_cur_ref, pieces, fetch(0), lambda c: c.start())

    @pl.when(i + 1 < n_steps)
    def _():
        _for_each_run_piece(tab_nxt_ref, pieces, fetch(1 - slot), lambda c: c.start())

    _for_each_run_piece(tab_cur_ref, pieces, fetch(slot), lambda c: c.wait())

    sel = sel_ref[...]
    col = lax.broadcasted_iota(jnp.int32, (1, rbuf.shape[1]), 1).astype(F32)
    weights = jnp.zeros((ts, rbuf.shape[1]), F32)
    for kk in range(TOP_K):
        weights = jnp.where(col == sel[:, SEL_ROW + kk:SEL_ROW + kk + 1],
                            sel[:, SEL_WEIGHT + kk:SEL_WEIGHT + kk + 1], weights)
    hi = weights.astype(MXU_DTYPE)
    lo = (weights - hi.astype(F32)).astype(MXU_DTYPE)
    both = jnp.dot(jnp.concatenate([hi, lo], axis=0), rbuf[slot], preferred_element_type=F32)
    ffn = both[:ts] + both[ts:]
    gt2 = ada_ref[...][5:6]
    o_ref[...] = _layernorm_rows(DEEPNORM_ALPHA * x1_ref[...] + (1.0 + gt2) * ffn, lng_ref[...], lnb_ref[...])


def _combine_call(tab3, ys, x1, sel, ada_l, ln_g, ln_b, seq):
    n_tok, d = x1.shape
    n_steps = tab3.shape[0]
    ts = n_tok // n_steps
    tiles_per_seq = seq // ts
    run_rows = ts * TOP_K + N_EXPERTS * RUN_ALIGN
    return pl.pallas_call(
        _combine_kernel,
        grid=(n_steps,),
        in_specs=[
            pl.BlockSpec((1, 1, tab3.shape[2]), lambda i: (i, 0, 0), memory_space=pltpu.SMEM),
            pl.BlockSpec((1, 1, tab3.shape[2]), lambda i: (jnp.minimum(i + 1, n_steps - 1), 0, 0),
                         memory_space=pltpu.SMEM),
            pl.BlockSpec(memory_space=pl.ANY),
            pl.BlockSpec((ts, d), lambda i: (i, 0)),
            pl.BlockSpec((ts, LANES), lambda i: (i, 0)),
            pl.BlockSpec((None, 6, d), lambda i: (i // tiles_per_seq, 0, 0)),
            pl.BlockSpec((1, d), lambda i: (0, 0)),
            pl.BlockSpec((1, d), lambda i: (0, 0)),
        ],
        out_specs=pl.BlockSpec((ts, d), lambda i: (i, 0)),
        out_shape=jax.ShapeDtypeStruct((n_tok, d), F32),
        scratch_shapes=[pltpu.VMEM((2, run_rows, d), MXU_DTYPE), pltpu.SemaphoreType.DMA((2,))],
        compiler_params=pltpu.CompilerParams(
            dimension_semantics=("arbitrary",), vmem_limit_bytes=VMEM_LIMIT),
        name="combine",
    )(tab3, tab3, ys, x1, sel, ada_l, ln_g.reshape(1, d), ln_b.reshape(1, d))


def _retention_tables():
    log_gamma = jnp.log(1.0 - 2.0 ** (-5.0 - jnp.arange(RET_HEADS, dtype=F32)))
    idx = jnp.arange(CHUNK, dtype=F32)
    diff = idx[:, None] - idx[None, :]
    decay = jnp.where(diff[None] >= 0,
                      jnp.exp(log_gamma[:, None, None] * jnp.maximum(diff, 0.0)[None]), 0.0)
    decay = decay.reshape(RET_HEADS * CHUNK, CHUNK)
    zeta = jnp.exp(log_gamma[:, None] * (CHUNK - 1.0 - idx)[None])
    zeta_t = jnp.repeat(zeta.T, RET_QK_DIM, axis=1)
    xi = jnp.exp(log_gamma[:, None] * (idx + 1.0)[None]).T
    xi_t = jnp.repeat(xi, RET_V_DIM, axis=1)
    cdec = jnp.repeat(jnp.exp(log_gamma * CHUNK), RET_V_DIM)[None, :]
    row_head = jnp.arange(RET_QK_WIDTH) // RET_QK_DIM
    col_head = jnp.arange(RET_V_WIDTH) // RET_V_DIM
    same_head = row_head[:, None] == col_head[None, :]
    bmask = same_head.astype(F32)
    gavg = ((col_head[:, None] == col_head[None, :]).astype(F32) / RET_V_DIM).astype(MXU_DTYPE)
    return decay, zeta_t, xi_t, cdec, bmask, gavg


def _rotary_tables(positions):
    inv_freq = ROPE_BASE ** (-jnp.arange(0, RET_QK_DIM, 2, dtype=F32) / RET_QK_DIM)
    ang = positions.astype(F32)[..., None] * inv_freq
    cos = jnp.cos(ang)
    sin = jnp.sin(ang)
    cos_t = jnp.tile(jnp.concatenate([cos, cos], axis=-1), (1, 1, RET_HEADS))
    sin_t = jnp.tile(jnp.concatenate([-sin, sin], axis=-1), (1, 1, RET_HEADS))
    return cos_t, sin_t


def _moe_rows(n_tok, n_tiles, blk):
    worst = n_tok * TOP_K + n_tiles * N_EXPERTS * (RUN_ALIGN - 1)
    return (-(-worst // blk) + N_EXPERTS) * blk


def _routing_tables(sel, counts, before, n_rows, blk, pieces):
    n_blocks = n_rows // blk
    sizes = counts.astype(jnp.int32)
    pad_sizes = (sizes + blk - 1) // blk * blk
    pad_end = jnp.cumsum(pad_sizes)
    pad_start = pad_end - pad_sizes
    n_used = (pad_end[-1] // blk).astype(jnp.int32)
    block_start = jnp.arange(n_blocks, dtype=jnp.int32) * blk
    block_start = jnp.minimum(block_start, (n_used - 1) * blk)
    block_e = jnp.sum((pad_end[None, :] <= block_start[:, None]).astype(jnp.int32), axis=-1)
    block_e = jnp.minimum(block_e, N_EXPERTS - 1)

    before = before.astype(jnp.int32)
    after = jnp.concatenate([before[1:], sizes[None, :]], axis=0)
    run_start = pad_start[None, :] + before
    run_units = (after - before) // RUN_ALIGN
    run_off = (jnp.cumsum(run_units, axis=-1) - run_units) * RUN_ALIGN
    n_tiles = before.shape[0]
    slot_id = jnp.arange(N_EXPERTS, dtype=jnp.int32)
    cols, counts_c = [], []
    for bit in pieces:
        has = jnp.bitwise_and(run_units, bit) != 0
        done = jnp.bitwise_and(run_units, -2 * bit) * RUN_ALIGN
        pos = jnp.cumsum(has.astype(jnp.int32), axis=-1) - 1
        place = jnp.logical_and(has[:, None, :], pos[:, None, :] == slot_id[None, :, None])
        cols.append(jnp.sum(jnp.where(place, (run_start + done)[:, None, :], 0), axis=-1))
        cols.append(jnp.sum(jnp.where(place, (run_off + done)[:, None, :], 0), axis=-1))
        counts_c.append(jnp.sum(has.astype(jnp.int32), axis=-1, keepdims=True))
    width = _piece_table_width(len(pieces))
    used = len(pieces) * (2 * N_EXPERTS + 1)
    tab = jnp.concatenate(cols + counts_c + [jnp.zeros((n_tiles, width - used), jnp.int32)], axis=-1)

    n_tok = sel.shape[0]
    idx = sel[:, SEL_IDX:SEL_IDX + TOP_K].astype(jnp.int32)
    off_tok = jnp.repeat(run_off, n_tok // n_tiles, axis=0)
    base = jnp.sum(jnp.where(idx[:, :, None] == slot_id, off_tok[:, None, :], 0), axis=-1)
    rows = base.astype(F32) + sel[:, SEL_RANK:SEL_RANK + TOP_K]
    sel = jnp.concatenate([sel[:, :SEL_ROW], rows, sel[:, SEL_ROW + TOP_K:]], axis=1)
    return (pad_end.astype(jnp.int32), pad_start.astype(jnp.int32), block_e, n_used.reshape(1),
            tab.astype(jnp.int32).reshape(n_tiles, 1, width), sel)


def kernel(x, c, positions, w_ada, b_ada, w_in, gmlp_ln_g, gmlp_ln_b, w_spatial, b_spatial, conv_w,
           w_branch, w_gate_up, b_gate, w_out, ln1_g, ln1_b, w_router, b_router, w_gu, b_gu, w_down,
           b_down, ln2_g, ln2_b):
    bsz, seq, d = x.shape
    n_tok = bsz * seq
    depth = w_ada.shape[0]
    blk = min(MOE_BLOCK, n_tok * TOP_K // N_EXPERTS)
    ts = min(SEQ_TILE, seq)

    ada = _ada_call(c, w_ada, b_ada)
    ada = jnp.transpose(ada, (0, 2, 1, 3))
    cos_t, sin_t = _rotary_tables(positions)
    tables = _retention_tables()

    mx = MXU_DTYPE
    w_in_m = w_in.astype(mx)
    w_branch_m = w_branch.astype(mx)
    w_gate_m = w_gate_up.astype(mx)
    w_out_m = w_out.astype(mx)
    w_gu_m = w_gu.astype(mx)
    w_down_m = w_down.astype(mx)
    n_exp = w_router.shape[-1]
    w_router_m = jnp.pad(w_router, ((0, 0), (0, 0), (0, LANES - n_exp))).astype(mx)
    b_router_p = jnp.pad(b_router, ((0, 0), (0, LANES - n_exp)), constant_values=NEG_BIG)

    for l in range(depth):
        bsp_t = jnp.repeat(b_spatial[l].T, GMLP_WIDTH // GMLP_GROUPS, axis=1)
        lw = (
            w_in_m[l],
            gmlp_ln_g[l].reshape(1, -1), gmlp_ln_b[l].reshape(1, -1),
            w_spatial[l].reshape(GMLP_GROUPS * CHUNK, CHUNK), bsp_t,
            jnp.pad(conv_w[l], ((0, 8 - conv_w.shape[1]), (0, 0))),
            w_branch_m[l], w_gate_m[l], b_gate[l], w_out_m[l],
            ln1_g[l].reshape(1, -1), ln1_b[l].reshape(1, -1),
            w_router_m[l], b_router_p[l].reshape(1, -1),
        )
        x1, h2, sel, cnt, before = _mixer_call(x, ada[l], cos_t, sin_t, lw, tables)
        n_rows = _moe_rows(n_tok, n_tok // ts, blk)
        pad_end, pad_start, block_e, n_used, tab3, sel = _routing_tables(
            sel, cnt[0, :n_exp], before[:, 0, :n_exp], n_rows, blk, _run_pieces(ts))
        xs = _dispatch_call(pad_end, pad_start, tab3, sel, h2, n_rows, blk)
        ys = _expert_call(block_e, n_used, xs, w_gu_m[l], b_gu[l], w_down_m[l], b_down[l], blk)
        x = _combine_call(tab3, ys, x1, sel, ada[l], ln2_g[l], ln2_b[l], seq).reshape(bsz, seq, d)
    return x
```

```python
import functools

import jax
import jax.numpy as jnp
from jax import lax
from jax.experimental import pallas as pl
from jax.experimental.pallas import tpu as pltpu

F32 = jnp.float32
MXU_DTYPE = jnp.bfloat16

DEPTH = 4
RET_HEADS = 4
RET_QK_DIM = 32
RET_V_DIM = 64
RET_QK_WIDTH = RET_HEADS * RET_QK_DIM
RET_V_WIDTH = RET_HEADS * RET_V_DIM
CHUNK = 128
ROPE_BASE = 10000.0
GMLP_GROUPS = 4
GMLP_WIDTH = 256
CONV_WIDTH = 256
GATE_RANK = 128
N_BRANCH = 3
N_EXPERTS = 32
TOP_K = 4
D_FF = 256
SWIGLU_LIMIT = 7.0
SWIGLU_ALPHA = 1.702
DEEPNORM_ALPHA = (2.0 * DEPTH) ** 0.25
LN_EPS = 1e-5

_O_QK = 0
_O_VG = 2 * RET_QK_WIDTH
_O_GMLP = _O_VG + 2 * RET_V_WIDTH
_O_CONV = _O_GMLP + 2 * GMLP_WIDTH
_O_CODE = _O_CONV + 3 * CONV_WIDTH
_IN_WIDTH = _O_CODE + N_BRANCH * GATE_RANK

LANES = 128
NEG_BIG = -1e30

SEQ_TILE = 256
MIXER_STEP = 512
MOE_BLOCK = 1024
EXPERT_PART = 256
RUN_ALIGN = 16
VMEM_LIMIT = 56 * 1024 * 1024


def _dot(a, b):
    return jnp.dot(a.astype(MXU_DTYPE), b.astype(MXU_DTYPE), preferred_element_type=F32)


def _dot_nt(a, b):
    return lax.dot_general(a.astype(MXU_DTYPE), b.astype(MXU_DTYPE),
                           (((1,), (1,)), ((), ())), preferred_element_type=F32)


def _split_dot(x, w):
    hi = x.astype(MXU_DTYPE)
    lo = (x - hi.astype(F32)).astype(MXU_DTYPE)
    return (jnp.dot(hi, w, preferred_element_type=F32)
            + jnp.dot(lo, w, preferred_element_type=F32))


def _layernorm_rows(x, g, b):
    mu = jnp.mean(x, axis=-1, keepdims=True)
    d = x - mu
    var = jnp.mean(d * d, axis=-1, keepdims=True)
    return d * lax.rsqrt(var + LN_EPS) * g + b


def _ada_kernel(c_ref, w_ref, b_ref, o_ref):
    c_act = jax.nn.silu(c_ref[...])
    o_ref[...] = jnp.dot(c_act, w_ref[...], preferred_element_type=F32,
                         precision=lax.Precision.HIGHEST) + b_ref[...]


def _ada_call(c, w_ada, b_ada):
    depth, d, six_d = w_ada.shape
    bsz = c.shape[0]
    n_col = six_d // d
    return pl.pallas_call(
        _ada_kernel,
        grid=(depth, n_col),
        in_specs=[
            pl.BlockSpec((bsz, d), lambda l, j: (0, 0)),
            pl.BlockSpec((None, d, d), lambda l, j: (l, 0, j)),
            pl.BlockSpec((None, 1, d), lambda l, j: (l, 0, j)),
        ],
        out_specs=pl.BlockSpec((None, None, bsz, d), lambda l, j: (l, j, 0, 0)),
        out_shape=jax.ShapeDtypeStruct((depth, n_col, bsz, d), F32),
        compiler_params=pltpu.CompilerParams(
            dimension_semantics=("arbitrary", "arbitrary"), vmem_limit_bytes=VMEM_LIMIT),
        name="ada_ln",
    )(c, w_ada, b_ada.reshape(depth, 1, six_d))


def _mixer_kernel(x_ref, ada_ref, cos_ref, sin_ref, *rest):
    params = rest[:-8]
    x1_ref, h2_ref, sel_ref, cnt_ref, before_ref, state_ref, zc_ref, cntacc_ref = rest[-8:]
    b = pl.program_id(0)
    s = pl.program_id(1)

    @pl.when(s == 0)
    def _():
        state_ref[...] = jnp.zeros_like(state_ref)
        zc_ref[...] = jnp.zeros_like(zc_ref)

    @pl.when(jnp.logical_and(b == 0, s == 0))
    def _():
        cntacc_ref[...] = jnp.zeros_like(cntacc_ref)

    n_sub = before_ref.shape[0]
    ts = x_ref.shape[0] // n_sub
    for u in range(n_sub):
        rows = pl.ds(u * ts, ts)
        _mixer_tile(x_ref.at[rows], ada_ref, cos_ref.at[rows], sin_ref.at[rows], *params,
                    x1_ref.at[rows], h2_ref.at[rows], sel_ref.at[rows], cnt_ref, before_ref.at[u],
                    state_ref, zc_ref, cntacc_ref)


def _mixer_tile(x_ref, ada_ref, cos_ref, sin_ref, w_in_ref, lng_ref, lnb_ref, wsp_ref, bsp_ref,
                convw_ref, wbr_ref, wgt_ref, bgt_ref, wout_ref, ln1g_ref, ln1b_ref, wr_ref, br_ref,
                decay_ref, zeta_ref, xi_ref, cdec_ref, bmask_ref, gavg_ref,
                x1_ref, h2_ref, sel_ref, cnt_ref, before_ref,
                state_ref, zc_ref, cntacc_ref):
    ts = x_ref.shape[0]
    x = x_ref[...]
    ada = ada_ref[...]
    sh1, sc1, gt1, sh2, sc2 = ada[0:1], ada[1:2], ada[2:3], ada[3:4], ada[4:5]
    h = (x * (1.0 + sc1) + sh1).astype(MXU_DTYPE)

    def proj(lo, hi):
        return jnp.dot(h, w_in_ref[:, lo:hi], preferred_element_type=F32)

    lane_qk = lax.broadcasted_iota(jnp.int32, (1, RET_QK_WIDTH), 1)
    lane_v = lax.broadcasted_iota(jnp.int32, (1, RET_V_WIDTH), 1)
    qk_masks = [lane_qk // RET_QK_DIM == hd for hd in range(RET_HEADS)]
    v_masks = [lane_v // RET_V_DIM == hd for hd in range(RET_HEADS)]

    qk = proj(_O_QK, _O_VG)
    cos = cos_ref[...]
    sin = sin_ref[...]
    first_half = (lane_qk % RET_QK_DIM) < (RET_QK_DIM // 2)

    def rotary(t):
        swapped = jnp.where(first_half, pltpu.roll(t, RET_QK_WIDTH - RET_QK_DIM // 2, 1),
                            pltpu.roll(t, RET_QK_DIM // 2, 1))
        return t * cos + swapped * sin

    q = rotary(qk[:, :RET_QK_WIDTH])
    k = rotary(qk[:, RET_QK_WIDTH:]) * (RET_QK_DIM ** -0.5)
    vg = proj(_O_VG, _O_GMLP)
    v = vg[:, :RET_V_WIDTH]
    g = vg[:, RET_V_WIDTH:]

    decay = decay_ref[...]
    zeta = zeta_ref[...]
    xi = xi_ref[...]
    cdec = cdec_ref[...]
    bmask = bmask_ref[...]
    gavg = gavg_ref[...]

    o_chunks = []
    state = state_ref[...]
    for c in range(ts // CHUNK):
        rows = slice(c * CHUNK, (c + 1) * CHUNK)
        q_c, k_c, v_c = q[rows], k[rows], v[rows]
        v_m = v_c.astype(MXU_DTYPE)
        q_all = jnp.concatenate([jnp.where(m, q_c, 0.0) for m in qk_masks], axis=0)
        scores = _dot_nt(q_all, k_c) * decay
        o_all = _dot(scores, v_m)
        o = _dot(q_c, state) * xi
        for hd in range(RET_HEADS):
            o = o + jnp.where(v_masks[hd], o_all[hd * CHUNK:(hd + 1) * CHUNK], 0.0)
        kv = _dot((k_c * zeta).T, v_m) * bmask
        state = state * cdec + kv
        o_chunks.append(o)
    state_ref[...] = state
    o = jnp.concatenate(o_chunks, axis=0) if len(o_chunks) > 1 else o_chunks[0]
    mu = _split_dot(o, gavg)
    d = o - mu
    var = _split_dot(d * d, gavg)
    r_br = jax.nn.silu(g) * (d * lax.rsqrt(var + LN_EPS))

    guv = proj(_O_GMLP, _O_CONV)
    u = jax.nn.gelu(guv[:, :GMLP_WIDTH])
    vv = _layernorm_rows(jax.nn.gelu(guv[:, GMLP_WIDTH:]), lng_ref[...], lnb_ref[...]).astype(MXU_DTYPE)
    rr = lax.broadcasted_iota(jnp.int32, (GMLP_GROUPS * CHUNK, CHUNK), 0) % CHUNK
    cc_ = lax.broadcasted_iota(jnp.int32, (GMLP_GROUPS * CHUNK, CHUNK), 1)
    w_sp = jnp.where(cc_ <= rr, wsp_ref[...], 0.0).astype(MXU_DTYPE)
    bsp = bsp_ref[...]
    z_chunks = []
    for c in range(ts // CHUNK):
        z_all = jnp.dot(w_sp, vv[c * CHUNK:(c + 1) * CHUNK], preferred_element_type=F32)
        z = bsp
        for gi in range(GMLP_GROUPS):
            z = z + jnp.where(v_masks[gi], z_all[gi * CHUNK:(gi + 1) * CHUNK], 0.0)
        z_chunks.append(z)
    z = jnp.concatenate(z_chunks, axis=0) if len(z_chunks) > 1 else z_chunks[0]
    s_br = u * z

    cbcx = proj(_O_CONV, _O_CODE)
    gate_b = cbcx[:, :CONV_WIDTH]
    zc = cbcx[:, CONV_WIDTH:2 * CONV_WIDTH] * cbcx[:, 2 * CONV_WIDTH:]
    carry = zc_ref[...]
    prev1 = carry[7:8]
    prev2 = carry[6:7]
    row = lax.broadcasted_iota(jnp.int32, (ts, 1), 0)
    z1 = jnp.where(row == 0, prev1, pltpu.roll(zc, 1, 0))
    z2 = jnp.where(row == 0, prev2, jnp.where(row == 1, prev1, pltpu.roll(zc, 2, 0)))
    zc_ref[...] = zc[ts - 8:ts]
    cw = convw_ref[...]
    k_br = gate_b * (cw[0:1] * z2 + cw[1:2] * z1 + cw[2:3] * zc)

    code = proj(_O_CODE, _IN_WIDTH)
    merged = None
    for i, br in enumerate((r_br, s_br, k_br)):
        y = jnp.dot(br.astype(MXU_DTYPE), wbr_ref[i], preferred_element_type=F32)
        gl = jnp.dot(code[:, i * GATE_RANK:(i + 1) * GATE_RANK].astype(MXU_DTYPE), wgt_ref[i],
                     preferred_element_type=F32) + bgt_ref[i:i + 1]
        t = jax.nn.sigmoid(gl) * y
        merged = t if merged is None else merged + t
    mix = jnp.dot(merged.astype(MXU_DTYPE), wout_ref[...], preferred_element_type=F32)
    x1 = _layernorm_rows(DEEPNORM_ALPHA * x + (1.0 + gt1) * mix, ln1g_ref[...], ln1b_ref[...])
    x1_ref[...] = x1
    h2 = (x1 * (1.0 + sc2) + sh2).astype(MXU_DTYPE)
    h2_ref[...] = h2

    logits = jnp.dot(h2, wr_ref[...], preferred_element_type=F32) + br_ref[...]
    lane = lax.broadcasted_iota(jnp.int32, (1, LANES), 1).astype(F32)
    top_v, top_i = [], []
    work = logits
    for _ in range(TOP_K):
        m = jnp.max(work, axis=-1, keepdims=True)
        i_sel = jnp.argmax(work, axis=-1, keepdims=True).astype(F32)
        top_v.append(m)
        top_i.append(i_sel)
        work = jnp.where(lane == i_sel, -jnp.inf, work)
    exps = [jnp.exp(tv - top_v[0]) for tv in top_v]
    denom = exps[0] + exps[1] + exps[2] + exps[3]
    onehots = [lane == ti for ti in top_i]
    member = jnp.zeros((ts, LANES), F32)
    for oh in onehots:
        member = member + jnp.where(oh, 1.0, 0.0)
    tr = lax.broadcasted_iota(jnp.int32, (ts, ts), 0)
    tc = lax.broadcasted_iota(jnp.int32, (ts, ts), 1)
    before = jnp.where(tc < tr, 1.0, 0.0).astype(MXU_DTYPE)
    cnt = cntacc_ref[...]
    local_rank = jnp.dot(before, member.astype(MXU_DTYPE), preferred_element_type=F32)
    sel = jnp.zeros((ts, LANES), F32)
    for kk in range(TOP_K):
        lrank_k = jnp.sum(jnp.where(onehots[kk], local_rank, 0.0), axis=-1, keepdims=True)
        sel = sel + jnp.where(lane == float(kk), top_i[kk], 0.0)
        sel = sel + jnp.where(lane == float(TOP_K + kk), exps[kk] / denom, 0.0)
        sel = sel + jnp.where(lane == float(2 * TOP_K + kk), lrank_k, 0.0)
    sel_ref[...] = sel
    before_ref[...] = cnt
    tile_cnt = jnp.sum(member, axis=0, keepdims=True)
    cnt = cnt + jnp.floor((tile_cnt + (RUN_ALIGN - 1.0)) * (1.0 / RUN_ALIGN)) * RUN_ALIGN
    cntacc_ref[...] = cnt
    cnt_ref[...] = cnt


def _const_spec(shape):
    return pl.BlockSpec(shape, lambda b, s: (0,) * len(shape))


def _mixer_call(x, ada_l, cos_t, sin_t, lw, tables):
    bsz, seq, d = x.shape
    ts = min(SEQ_TILE, seq)
    step = min(MIXER_STEP, seq)
    n_tok = bsz * seq
    row3 = lambda b, s: (b, s, 0)
    in_specs = [
        pl.BlockSpec((None, step, d), row3),
        pl.BlockSpec((None, 6, d), lambda b, s: (b, 0, 0)),
        pl.BlockSpec((None, step, RET_QK_WIDTH), row3),
        pl.BlockSpec((None, step, RET_QK_WIDTH), row3),
    ] + [_const_spec(a.shape) for a in lw] + [_const_spec(a.shape) for a in tables]
    tok_row = lambda b, s: (b * (seq // step) + s, 0)
    out_specs = [
        pl.BlockSpec((step, d), tok_row),
        pl.BlockSpec((step, d), tok_row),
        pl.BlockSpec((step, LANES), tok_row),
        pl.BlockSpec((8, LANES), lambda b, s: (0, 0)),
        pl.BlockSpec((step // ts, 8, LANES), lambda b, s: (b * (seq // step) + s, 0, 0)),
    ]
    out_shape = [
        jax.ShapeDtypeStruct((n_tok, d), F32),
        jax.ShapeDtypeStruct((n_tok, d), MXU_DTYPE),
        jax.ShapeDtypeStruct((n_tok, LANES), F32),
        jax.ShapeDtypeStruct((8, LANES), F32),
        jax.ShapeDtypeStruct((n_tok // ts, 8, LANES), F32),
    ]
    return pl.pallas_call(
        _mixer_kernel,
        grid=(bsz, seq // step),
        in_specs=in_specs,
        out_specs=out_specs,
        out_shape=out_shape,
        scratch_shapes=[
            pltpu.VMEM((RET_QK_WIDTH, RET_V_WIDTH), F32),
            pltpu.VMEM((8, CONV_WIDTH), F32),
            pltpu.VMEM((8, LANES), F32),
        ],
        compiler_params=pltpu.CompilerParams(
            dimension_semantics=("arbitrary", "arbitrary"), vmem_limit_bytes=VMEM_LIMIT),
        name="mixer",
    )(x, ada_l, cos_t, sin_t, *lw, *tables)


def _run_pieces(max_rows):
    units = max_rows // RUN_ALIGN
    pieces = []
    bit = 1
    while bit <= units:
        pieces.append(bit)
        bit *= 2
    return tuple(reversed(pieces))


def _piece_table_width(n_classes):
    return -(-(n_classes * 2 * N_EXPERTS + n_classes) // LANES) * LANES


def _for_each_run_piece(tab_ref, pieces, make_copy, fn):
    for c, bit in enumerate(pieces):
        count = tab_ref[0, 0, len(pieces) * 2 * N_EXPERTS + c]

        def body(j, carry, c=c, bit=bit):
            hbm_row = tab_ref[0, 0, c * 2 * N_EXPERTS + j]
            buf_row = tab_ref[0, 0, (c * 2 + 1) * N_EXPERTS + j]
            fn(make_copy(pl.multiple_of(hbm_row, RUN_ALIGN), pl.multiple_of(buf_row, RUN_ALIGN),
                         bit * RUN_ALIGN))
            return carry

        lax.fori_loop(0, count, body, 0)


SEL_IDX, SEL_WEIGHT, SEL_RANK, SEL_ROW = 0, TOP_K, 2 * TOP_K, 3 * TOP_K


def _dispatch_kernel(pe_ref, ps_ref, tab_cur_ref, tab_prev_ref, sel_ref, h2_ref, xs_hbm,
                     zbuf, sbuf, sem, zsem):
    i = pl.program_id(0)
    n_steps = pl.num_programs(0)
    ts = h2_ref.shape[0]
    blk = zbuf.shape[0]
    n_exp = pe_ref.shape[0]
    run_rows = sbuf.shape[1]
    slot = i % 2
    pieces = _run_pieces(ts)

    @pl.when(i == 0)
    def _():
        zbuf[...] = jnp.zeros_like(zbuf)

        def tail_copy(e):
            start = pl.multiple_of(jnp.maximum(pe_ref[e] - blk, 0), blk)
            return pltpu.make_async_copy(zbuf, xs_hbm.at[pl.ds(start, blk), :], zsem)

        def zstart(e, carry):
            @pl.when(pe_ref[e] > ps_ref[e])
            def _():
                tail_copy(e).start()
            return carry

        def zwait(e, carry):
            @pl.when(pe_ref[e] > ps_ref[e])
            def _():
                tail_copy(e).wait()
            return carry

        lax.fori_loop(0, n_exp, zstart, 0)
        lax.fori_loop(0, n_exp, zwait, 0)

    sel_t = sel_ref[...].T
    buf_row = lax.broadcasted_iota(jnp.int32, (run_rows, ts), 0).astype(F32)
    pick = jnp.zeros((run_rows, ts), F32)
    for kk in range(TOP_K):
        pick = jnp.where(buf_row == sel_t[SEL_ROW + kk:SEL_ROW + kk + 1], 1.0, pick)
    sbuf[slot] = jnp.dot(pick.astype(MXU_DTYPE), h2_ref[...].astype(MXU_DTYPE),
                         preferred_element_type=F32).astype(sbuf.dtype)

    def push(dst_slot):
        def make(hbm_row, buf_row_, rows):
            return pltpu.make_async_copy(sbuf.at[dst_slot, pl.ds(buf_row_, rows), :],
                                         xs_hbm.at[pl.ds(hbm_row, rows), :], sem.at[dst_slot])
        return make

    _for_each_run_piece(tab_cur_ref, pieces, push(slot), lambda c: c.start())

    @pl.when(i > 0)
    def _():
        _for_each_run_piece(tab_prev_ref, pieces, push(1 - slot), lambda c: c.wait())

    @pl.when(i == n_steps - 1)
    def _():
        _for_each_run_piece(tab_cur_ref, pieces, push(slot), lambda c: c.wait())


def _dispatch_call(pad_end, pad_start, tab3, sel, h2, n_rows, blk):
    n_tok, d = h2.shape
    n_steps = tab3.shape[0]
    ts = n_tok // n_steps
    run_rows = ts * TOP_K + N_EXPERTS * RUN_ALIGN
    grid_spec = pltpu.PrefetchScalarGridSpec(
        num_scalar_prefetch=2,
        grid=(n_steps,),
        in_specs=[
            pl.BlockSpec((1, 1, tab3.shape[2]), lambda i, pe, ps: (i, 0, 0), memory_space=pltpu.SMEM),
            pl.BlockSpec((1, 1, tab3.shape[2]), lambda i, pe, ps: (jnp.maximum(i - 1, 0), 0, 0),
                         memory_space=pltpu.SMEM),
            pl.BlockSpec((ts, LANES), lambda i, pe, ps: (i, 0)),
            pl.BlockSpec((ts, d), lambda i, pe, ps: (i, 0)),
        ],
        out_specs=pl.BlockSpec(memory_space=pl.ANY),
        scratch_shapes=[pltpu.VMEM((blk, d), MXU_DTYPE), pltpu.VMEM((2, run_rows, d), MXU_DTYPE),
                        pltpu.SemaphoreType.DMA((2,)), pltpu.SemaphoreType.DMA(())],
    )
    return pl.pallas_call(
        _dispatch_kernel,
        grid_spec=grid_spec,
        out_shape=jax.ShapeDtypeStruct((n_rows, d), MXU_DTYPE),
        compiler_params=pltpu.CompilerParams(
            dimension_semantics=("arbitrary",), vmem_limit_bytes=VMEM_LIMIT),
        name="dispatch",
    )(pad_end, pad_start, tab3, tab3, sel, h2)


def _expert_mlp(xs, wgu_ref, bgu_ref, wdn_ref, bdn_ref):
    gu = jnp.dot(xs, wgu_ref[...], preferred_element_type=F32) + bgu_ref[...]
    gate = jnp.minimum(gu[:, :D_FF], SWIGLU_LIMIT)
    up = jnp.clip(gu[:, D_FF:], -SWIGLU_LIMIT, SWIGLU_LIMIT)
    act = (up + 1.0) * (gate * jax.nn.sigmoid(SWIGLU_ALPHA * gate))
    return jnp.dot(act.astype(MXU_DTYPE), wdn_ref[...], preferred_element_type=F32) + bdn_ref[...]


def _expert_kernel(be_ref, nu_ref, xs_ref, wgu_ref, bgu_ref, wdn_ref, bdn_ref, ys_ref):
    i = pl.program_id(0)
    n_used = nu_ref[0]

    @pl.when(i < n_used)
    def _():
        part = min(EXPERT_PART, xs_ref.shape[0])
        for p in range(xs_ref.shape[0] // part):
            rows = slice(p * part, (p + 1) * part)
            ys_ref[rows, :] = _expert_mlp(xs_ref[rows, :], wgu_ref, bgu_ref, wdn_ref, bdn_ref).astype(ys_ref.dtype)

    @pl.when(i >= n_used)
    def _():
        ys_ref[...] = jnp.zeros_like(ys_ref)


def _expert_call(block_e, n_used, xs, w_gu, b_gu, w_down, b_down, blk):
    n_rows, d = xs.shape
    n_blocks = n_rows // blk
    two_f = w_gu.shape[2]
    last = lambda nu: jnp.maximum(nu[0] - 1, 0)
    grid_spec = pltpu.PrefetchScalarGridSpec(
        num_scalar_prefetch=2,
        grid=(n_blocks,),
        in_specs=[
            pl.BlockSpec((blk, d), lambda i, be, nu: (jnp.minimum(i, last(nu)), 0)),
            pl.BlockSpec((None, d, two_f), lambda i, be, nu: (be[i], 0, 0)),
            pl.BlockSpec((None, 1, two_f), lambda i, be, nu: (be[i], 0, 0)),
            pl.BlockSpec((None, two_f // 2, d), lambda i, be, nu: (be[i], 0, 0)),
            pl.BlockSpec((None, 1, d), lambda i, be, nu: (be[i], 0, 0)),
        ],
        out_specs=pl.BlockSpec((blk, d), lambda i, be, nu: (i, 0)),
    )
    n_exp = w_gu.shape[0]
    return pl.pallas_call(
        _expert_kernel,
        grid_spec=grid_spec,
        out_shape=jax.ShapeDtypeStruct((n_rows, d), MXU_DTYPE),
        compiler_params=pltpu.CompilerParams(
            dimension_semantics=("arbitrary",), vmem_limit_bytes=VMEM_LIMIT),
        name="experts",
    )(block_e, n_used, xs, w_gu, b_gu.reshape(n_exp, 1, two_f), w_down, b_down.reshape(n_exp, 1, d))


def _combine_kernel(tab_cur_ref, tab_nxt_ref, ys_hbm, x1_ref, sel_ref, ada_ref, lng_ref, lnb_ref,
                    o_ref, rbuf, sem):
    i = pl.program_id(0)
    n_steps = pl.num_programs(0)
    ts = x1_ref.shape[0]
    slot = i % 2
    pieces = _run_pieces(ts)

    def fetch(dst_slot):
        def make(hbm_row, buf_row, rows):
            return pltpu.make_async_copy(ys_hbm.at[pl.ds(hbm_row, rows), :],
                                         rbuf.at[dst_slot, pl.ds(buf_row, rows), :], sem.at[dst_slot])
        return make

    @pl.when(i == 0)
    def _():
        rbuf[...] = jnp.zeros_like(rbuf)
        _for_each_run_piece(tab_cur_ref, pieces, fetch(0), lambda c: c.start())

    @pl.when(i + 1 < n_steps)
    def _():
        _for_each_run_piece(tab_nxt_ref, pieces, fetch(1 - slot), lambda c: c.start())

    _for_each_run_piece(tab_cur_ref, pieces, fetch(slot), lambda c: c.wait())

    sel = sel_ref[...]
    col = lax.broadcasted_iota(jnp.int32, (1, rbuf.shape[1]), 1).astype(F32)
    weights = jnp.zeros((ts, rbuf.shape[1]), F32)
    for kk in range(TOP_K):
        weights = jnp.where(col == sel[:, SEL_ROW + kk:SEL_ROW + kk + 1],
                            sel[:, SEL_WEIGHT + kk:SEL_WEIGHT + kk + 1], weights)
    hi = weights.astype(MXU_DTYPE)
    lo = (weights - hi.astype(F32)).astype(MXU_DTYPE)
    both = jnp.dot(jnp.concatenate([hi, lo], axis=0), rbuf[slot], preferred_element_type=F32)
    ffn = both[:ts] + both[ts:]
    gt2 = ada_ref[...][5:6]
    o_ref[...] = _layernorm_rows(DEEPNORM_ALPHA * x1_ref[...] + (1.0 + gt2) * ffn, lng_ref[...], lnb_ref[...])


def _combine_call(tab3, ys, x1, sel, ada_l, ln_g, ln_b, seq):
    n_tok, d = x1.shape
    n_steps = tab3.shape[0]
    ts = n_tok // n_steps
    tiles_per_seq = seq // ts
    run_rows = ts * TOP_K + N_EXPERTS * RUN_ALIGN
    return pl.pallas_call(
        _combine_kernel,
        grid=(n_steps,),
        in_specs=[
            pl.BlockSpec((1, 1, tab3.shape[2]), lambda i: (i, 0, 0), memory_space=pltpu.SMEM),
            pl.BlockSpec((1, 1, tab3.shape[2]), lambda i: (jnp.minimum(i + 1, n_steps - 1), 0, 0),
                         memory_space=pltpu.SMEM),
            pl.BlockSpec(memory_space=pl.ANY),
            pl.BlockSpec((ts, d), lambda i: (i, 0)),
            pl.BlockSpec((ts, LANES), lambda i: (i, 0)),
            pl.BlockSpec((None, 6, d), lambda i: (i // tiles_per_seq, 0, 0)),
            pl.BlockSpec((1, d), lambda i: (0, 0)),
            pl.BlockSpec((1, d), lambda i: (0, 0)),
        ],
        out_specs=pl.BlockSpec((ts, d), lambda i: (i, 0)),
        out_shape=jax.ShapeDtypeStruct((n_tok, d), F32),
        scratch_shapes=[pltpu.VMEM((2, run_rows, d), MXU_DTYPE), pltpu.SemaphoreType.DMA((2,))],
        compiler_params=pltpu.CompilerParams(
            dimension_semantics=("arbitrary",), vmem_limit_bytes=VMEM_LIMIT),
        name="combine",
    )(tab3, tab3, ys, x1, sel, ada_l, ln_g.reshape(1, d), ln_b.reshape(1, d))


def _retention_tables():
    log_gamma = jnp.log(1.0 - 2.0 ** (-5.0 - jnp.arange(RET_HEADS, dtype=F32)))
    idx = jnp.arange(CHUNK, dtype=F32)
    diff = idx[:, None] - idx[None, :]
    decay = jnp.where(diff[None] >= 0,
                      jnp.exp(log_gamma[:, None, None] * jnp.maximum(diff, 0.0)[None]), 0.0)
    decay = decay.reshape(RET_HEADS * CHUNK, CHUNK)
    zeta = jnp.exp(log_gamma[:, None] * (CHUNK - 1.0 - idx)[None])
    zeta_t = jnp.repeat(zeta.T, RET_QK_DIM, axis=1)
    xi = jnp.exp(log_gamma[:, None] * (idx + 1.0)[None]).T
    xi_t = jnp.repeat(xi, RET_V_DIM, axis=1)
    cdec = jnp.repeat(jnp.exp(log_gamma * CHUNK), RET_V_DIM)[None, :]
    row_head = jnp.arange(RET_QK_WIDTH) // RET_QK_DIM
    col_head = jnp.arange(RET_V_WIDTH) // RET_V_DIM
    same_head = row_head[:, None] == col_head[None, :]
    bmask = same_head.astype(F32)
    gavg = ((col_head[:, None] == col_head[None, :]).astype(F32) / RET_V_DIM).astype(MXU_DTYPE)
    return decay, zeta_t, xi_t, cdec, bmask, gavg


def _rotary_tables(positions):
    inv_freq = ROPE_BASE ** (-jnp.arange(0, RET_QK_DIM, 2, dtype=F32) / RET_QK_DIM)
    ang = positions.astype(F32)[..., None] * inv_freq
    cos = jnp.cos(ang)
    sin = jnp.sin(ang)
    cos_t = jnp.tile(jnp.concatenate([cos, cos], axis=-1), (1, 1, RET_HEADS))
    sin_t = jnp.tile(jnp.concatenate([-sin, sin], axis=-1), (1, 1, RET_HEADS))
    return cos_t, sin_t


def _moe_rows(n_tok, n_tiles, blk):
    worst = n_tok * TOP_K + n_tiles * N_EXPERTS * (RUN_ALIGN - 1)
    return (-(-worst // blk) + N_EXPERTS) * blk


def _routing_tables(sel, counts, before, n_rows, blk, pieces):
    n_blocks = n_rows // blk
    sizes = counts.astype(jnp.int32)
    pad_sizes = (sizes + blk - 1) // blk * blk
    pad_end = jnp.cumsum(pad_sizes)
    pad_start = pad_end - pad_sizes
    n_used = (pad_end[-1] // blk).astype(jnp.int32)
    block_start = jnp.arange(n_blocks, dtype=jnp.int32) * blk
    block_start = jnp.minimum(block_start, (n_used - 1) * blk)
    block_e = jnp.sum((pad_end[None, :] <= block_start[:, None]).astype(jnp.int32), axis=-1)
    block_e = jnp.minimum(block_e, N_EXPERTS - 1)

    before = before.astype(jnp.int32)
    after = jnp.concatenate([before[1:], sizes[None, :]], axis=0)
    run_start = pad_start[None, :] + before
    run_units = (after - before) // RUN_ALIGN
    run_off = (jnp.cumsum(run_units, axis=-1) - run_units) * RUN_ALIGN
    n_tiles = before.shape[0]
    slot_id = jnp.arange(N_EXPERTS, dtype=jnp.int32)
    cols, counts_c = [], []
    for bit in pieces:
        has = jnp.bitwise_and(run_units, bit) != 0
        done = jnp.bitwise_and(run_units, -2 * bit) * RUN_ALIGN
        pos = jnp.cumsum(has.astype(jnp.int32), axis=-1) - 1
        place = jnp.logical_and(has[:, None, :], pos[:, None, :] == slot_id[None, :, None])
        cols.append(jnp.sum(jnp.where(place, (run_start + done)[:, None, :], 0), axis=-1))
        cols.append(jnp.sum(jnp.where(place, (run_off + done)[:, None, :], 0), axis=-1))
        counts_c.append(jnp.sum(has.astype(jnp.int32), axis=-1, keepdims=True))
    width = _piece_table_width(len(pieces))
    used = len(pieces) * (2 * N_EXPERTS + 1)
    tab = jnp.concatenate(cols + counts_c + [jnp.zeros((n_tiles, width - used), jnp.int32)], axis=-1)

    n_tok = sel.shape[0]
    idx = sel[:, SEL_IDX:SEL_IDX + TOP_K].astype(jnp.int32)
    off_tok = jnp.repeat(run_off, n_tok // n_tiles, axis=0)
    base = jnp.sum(jnp.where(idx[:, :, None] == slot_id, off_tok[:, None, :], 0), axis=-1)
    rows = base.astype(F32) + sel[:, SEL_RANK:SEL_RANK + TOP_K]
    sel = jnp.concatenate([sel[:, :SEL_ROW], rows, sel[:, SEL_ROW + TOP_K:]], axis=1)
    return (pad_end.astype(jnp.int32), pad_start.astype(jnp.int32), block_e, n_used.reshape(1),
            tab.astype(jnp.int32).reshape(n_tiles, 1, width), sel)


def kernel(x, c, positions, w_ada, b_ada, w_in, gmlp_ln_g, gmlp_ln_b, w_spatial, b_spatial, conv_w,
           w_branch, w_gate_up, b_gate, w_out, ln1_g, ln1_b, w_router, b_router, w_gu, b_gu, w_down,
           b_down, ln2_g, ln2_b):
    bsz, seq, d = x.shape
    n_tok = bsz * seq
    depth = w_ada.shape[0]
    blk = min(MOE_BLOCK, n_tok * TOP_K // N_EXPERTS)
    ts = min(SEQ_TILE, seq)

    ada = _ada_call(c, w_ada, b_ada)
    ada = jnp.transpose(ada, (0, 2, 1, 3))
    cos_t, sin_t = _rotary_tables(positions)
    tables = _retention_tables()

    mx = MXU_DTYPE
    w_in_m = w_in.astype(mx)
    w_branch_m = w_branch.astype(mx)
    w_gate_m = w_gate_up.astype(mx)
    w_out_m = w_out.astype(mx)
    w_gu_m = w_gu.astype(mx)
    w_down_m = w_down.astype(mx)
    n_exp = w_router.shape[-1]
    w_router_m = jnp.pad(w_router, ((0, 0), (0, 0), (0, LANES - n_exp))).astype(mx)
    b_router_p = jnp.pad(b_router, ((0, 0), (0, LANES - n_exp)), constant_values=NEG_BIG)

    for l in range(depth):
        bsp_t = jnp.repeat(b_spatial[l].T, GMLP_WIDTH // GMLP_GROUPS, axis=1)
        lw = (
            w_in_m[l],
            gmlp_ln_g[l].reshape(1, -1), gmlp_ln_b[l].reshape(1, -1),
            w_spatial[l].reshape(GMLP_GROUPS * CHUNK, CHUNK), bsp_t,
            jnp.pad(conv_w[l], ((0, 8 - conv_w.shape[1]), (0, 0))),
            w_branch_m[l], w_gate_m[l], b_gate[l], w_out_m[l],
            ln1_g[l].reshape(1, -1), ln1_b[l].reshape(1, -1),
            w_router_m[l], b_router_p[l].reshape(1, -1),
        )
        x1, h2, sel, cnt, before = _mixer_call(x, ada[l], cos_t, sin_t, lw, tables)
        n_rows = _moe_rows(n_tok, n_tok // ts, blk)
        pad_end, pad_start, block_e, n_used, tab3, sel = _routing_tables(
            sel, cnt[0, :n_exp], before[:, 0, :n_exp], n_rows, blk, _run_pieces(ts))
        xs = _dispatch_call(pad_end, pad_start, tab3, sel, h2, n_rows, blk)
        ys = _expert_call(block_e, n_used, xs, w_gu_m[l], b_gu[l], w_down_m[l], b_down[l], blk)
        x = _combine_call(tab3, ys, x1, sel, ada[l], ln2_g[l], ln2_b[l], seq).reshape(bsz, seq, d)
    return x
```

```python
import functools

import jax
import jax.numpy as jnp
from jax import lax
from jax.experimental import pallas as pl
from jax.experimental.pallas import tpu as pltpu

F32 = jnp.float32
MXU_DTYPE = jnp.bfloat16

DEPTH = 4
RET_HEADS = 4
RET_QK_DIM = 32
RET_V_DIM = 64
RET_QK_WIDTH = RET_HEADS * RET_QK_DIM
RET_V_WIDTH = RET_HEADS * RET_V_DIM
CHUNK = 128
ROPE_BASE = 10000.0
GMLP_GROUPS = 4
GMLP_WIDTH = 256
CONV_WIDTH = 256
GATE_RANK = 128
N_BRANCH = 3
N_EXPERTS = 32
TOP_K = 4
D_FF = 256
SWIGLU_LIMIT = 7.0
SWIGLU_ALPHA = 1.702
DEEPNORM_ALPHA = (2.0 * DEPTH) ** 0.25
LN_EPS = 1e-5

_O_QK = 0
_O_VG = 2 * RET_QK_WIDTH
_O_GMLP = _O_VG + 2 * RET_V_WIDTH
_O_CONV = _O_GMLP + 2 * GMLP_WIDTH
_O_CODE = _O_CONV + 3 * CONV_WIDTH
_IN_WIDTH = _O_CODE + N_BRANCH * GATE_RANK

LANES = 128
NEG_BIG = -1e30

SEQ_TILE = 256
MIXER_STEP = 512
MOE_BLOCK = 1024
EXPERT_PART = 256
RUN_ALIGN = 16
VMEM_LIMIT = 56 * 1024 * 1024


def _dot(a, b):
    return jnp.dot(a.astype(MXU_DTYPE), b.astype(MXU_DTYPE), preferred_element_type=F32)


def _dot_nt(a, b):
    return lax.dot_general(a.astype(MXU_DTYPE), b.astype(MXU_DTYPE),
                           (((1,), (1,)), ((), ())), preferred_element_type=F32)


def _split_dot(x, w):
    hi = x.astype(MXU_DTYPE)
    lo = (x - hi.astype(F32)).astype(MXU_DTYPE)
    return (jnp.dot(hi, w, preferred_element_type=F32)
            + jnp.dot(lo, w, preferred_element_type=F32))


def _layernorm_rows(x, g, b):
    mu = jnp.mean(x, axis=-1, keepdims=True)
    d = x - mu
    var = jnp.mean(d * d, axis=-1, keepdims=True)
    return d * lax.rsqrt(var + LN_EPS) * g + b


def _ada_kernel(c_ref, w_ref, b_ref, o_ref):
    c_act = jax.nn.silu(c_ref[...])
    o_ref[...] = jnp.dot(c_act, w_ref[...], preferred_element_type=F32,
                         precision=lax.Precision.HIGHEST) + b_ref[...]


def _ada_call(c, w_ada, b_ada):
    depth, d, six_d = w_ada.shape
    bsz = c.shape[0]
    n_col = six_d // d
    return pl.pallas_call(
        _ada_kernel,
        grid=(depth, n_col),
        in_specs=[
            pl.BlockSpec((bsz, d), lambda l, j: (0, 0)),
            pl.BlockSpec((None, d, d), lambda l, j: (l, 0, j)),
            pl.BlockSpec((None, 1, d), lambda l, j: (l, 0, j)),
        ],
        out_specs=pl.BlockSpec((None, None, bsz, d), lambda l, j: (l, j, 0, 0)),
        out_shape=jax.ShapeDtypeStruct((depth, n_col, bsz, d), F32),
        compiler_params=pltpu.CompilerParams(
            dimension_semantics=("arbitrary", "arbitrary"), vmem_limit_bytes=VMEM_LIMIT),
        name="ada_ln",
    )(c, w_ada, b_ada.reshape(depth, 1, six_d))


def _mixer_kernel(x_ref, ada_ref, cos_ref, sin_ref, *rest):
    params = rest[:-8]
    x1_ref, h2_ref, sel_ref, cnt_ref, before_ref, state_ref, zc_ref, cntacc_ref = rest[-8:]
    b = pl.program_id(0)
    s = pl.program_id(1)

    @pl.when(s == 0)
    def _():
        state_ref[...] = jnp.zeros_like(state_ref)
        zc_ref[...] = jnp.zeros_like(zc_ref)

    @pl.when(jnp.logical_and(b == 0, s == 0))
    def _():
        cntacc_ref[...] = jnp.zeros_like(cntacc_ref)

    n_sub = before_ref.shape[0]
    ts = x_ref.shape[0] // n_sub
    for u in range(n_sub):
        rows = pl.ds(u * ts, ts)
        _mixer_tile(x_ref.at[rows], ada_ref, cos_ref.at[rows], sin_ref.at[rows], *params,
                    x1_ref.at[rows], h2_ref.at[rows], sel_ref.at[rows], cnt_ref, before_ref.at[u],
                    state_ref, zc_ref, cntacc_ref)


def _mixer_tile(x_ref, ada_ref, cos_ref, sin_ref, w_in_ref, lng_ref, lnb_ref, wsp_ref, bsp_ref,
                convw_ref, wbr_ref, wgt_ref, bgt_ref, wout_ref, ln1g_ref, ln1b_ref, wr_ref, br_ref,
                decay_ref, zeta_ref, xi_ref, cdec_ref, bmask_ref, gavg_ref,
                x1_ref, h2_ref, sel_ref, cnt_ref, before_ref,
                state_ref, zc_ref, cntacc_ref):
    ts = x_ref.shape[0]
    x = x_ref[...]
    ada = ada_ref[...]
    sh1, sc1, gt1, sh2, sc2 = ada[0:1], ada[1:2], ada[2:3], ada[3:4], ada[4:5]
    h = (x * (1.0 + sc1) + sh1).astype(MXU_DTYPE)

    def proj(lo, hi):
        return jnp.dot(h, w_in_ref[:, lo:hi], preferred_element_type=F32)

    lane_qk = lax.broadcasted_iota(jnp.int32, (1, RET_QK_WIDTH), 1)
    lane_v = lax.broadcasted_iota(jnp.int32, (1, RET_V_WIDTH), 1)
    qk_masks = [lane_qk // RET_QK_DIM == hd for hd in range(RET_HEADS)]
    v_masks = [lane_v // RET_V_DIM == hd for hd in range(RET_HEADS)]

    qk = proj(_O_QK, _O_VG)
    cos = cos_ref[...]
    sin = sin_ref[...]
    first_half = (lane_qk % RET_QK_DIM) < (RET_QK_DIM // 2)

    def rotary(t):
        swapped = jnp.where(first_half, pltpu.roll(t, RET_QK_WIDTH - RET_QK_DIM // 2, 1),
                            pltpu.roll(t, RET_QK_DIM // 2, 1))
        return t * cos + swapped * sin

    q = rotary(qk[:, :RET_QK_WIDTH])
    k = rotary(qk[:, RET_QK_WIDTH:]) * (RET_QK_DIM ** -0.5)
    vg = proj(_O_VG, _O_GMLP)
    v = vg[:, :RET_V_WIDTH]
    g = vg[:, RET_V_WIDTH:]

    decay = decay_ref[...]
    zeta = zeta_ref[...]
    xi = xi_ref[...]
    cdec = cdec_ref[...]
    bmask = bmask_ref[...]
    gavg = gavg_ref[...]

    o_chunks = []
    state = state_ref[...]
    for c in range(ts // CHUNK):
        rows = slice(c * CHUNK, (c + 1) * CHUNK)
        q_c, k_c, v_c = q[rows], k[rows], v[rows]
        v_m = v_c.astype(MXU_DTYPE)
        q_all = jnp.concatenate([jnp.where(m, q_c, 0.0) for m in qk_masks], axis=0)
        scores = _dot_nt(q_all, k_c) * decay
        o_all = _dot(scores, v_m)
        o = _dot(q_c, state) * xi
        for hd in range(RET_HEADS):
            o = o + jnp.where(v_masks[hd], o_all[hd * CHUNK:(hd + 1) * CHUNK], 0.0)
        kv = _dot((k_c * zeta).T, v_m) * bmask
        state = state * cdec + kv
        o_chunks.append(o)
    state_ref[...] = state
    o = jnp.concatenate(o_chunks, axis=0) if len(o_chunks) > 1 else o_chunks[0]
    mu = _split_dot(o, gavg)
    d = o - mu
    var = _split_dot(d * d, gavg)
    r_br = jax.nn.silu(g) * (d * lax.rsqrt(var + LN_EPS))

    guv = proj(_O_GMLP, _O_CONV)
    u = jax.nn.gelu(guv[:, :GMLP_WIDTH])
    vv = _layernorm_rows(jax.nn.gelu(guv[:, GMLP_WIDTH:]), lng_ref[...], lnb_ref[...]).astype(MXU_DTYPE)
    rr = lax.broadcasted_iota(jnp.int32, (GMLP_GROUPS * CHUNK, CHUNK), 0) % CHUNK
    cc_ = lax.broadcasted_iota(jnp.int32, (GMLP_GROUPS * CHUNK, CHUNK), 1)
    w_sp = jnp.where(cc_ <= rr, wsp_ref[...], 0.0).astype(MXU_DTYPE)
    bsp = bsp_ref[...]
    z_chunks = []
    for c in range(ts // CHUNK):
        z_all = jnp.dot(w_sp, vv[c * CHUNK:(c + 1) * CHUNK], preferred_element_type=F32)
        z = bsp
        for gi in range(GMLP_GROUPS):
            z = z + jnp.where(v_masks[gi], z_all[gi * CHUNK:(gi + 1) * CHUNK], 0.0)
        z_chunks.append(z)
    z = jnp.concatenate(z_chunks, axis=0) if len(z_chunks) > 1 else z_chunks[0]
    s_br = u * z

    cbcx = proj(_O_CONV, _O_CODE)
    gate_b = cbcx[:, :CONV_WIDTH]
    zc = cbcx[:, CONV_WIDTH:2 * CONV_WIDTH] * cbcx[:, 2 * CONV_WIDTH:]
    carry = zc_ref[...]
    prev1 = carry[7:8]
    prev2 = carry[6:7]
    row = lax.broadcasted_iota(jnp.int32, (ts, 1), 0)
    z1 = jnp.where(row == 0, prev1, pltpu.roll(zc, 1, 0))
    z2 = jnp.where(row == 0, prev2, jnp.where(row == 1, prev1, pltpu.roll(zc, 2, 0)))
    zc_ref[...] = zc[ts - 8:ts]
    cw = convw_ref[...]
    k_br = gate_b * (cw[0:1] * z2 + cw[1:2] * z1 + cw[2:3] * zc)

    code = proj(_O_CODE, _IN_WIDTH)
    merged = None
    for i, br in enumerate((r_br, s_br, k_br)):
        y = jnp.dot(br.astype(MXU_DTYPE), wbr_ref[i], preferred_element_type=F32)
        gl = jnp.dot(code[:, i * GATE_RANK:(i + 1) * GATE_RANK].astype(MXU_DTYPE), wgt_ref[i],
                     preferred_element_type=F32) + bgt_ref[i:i + 1]
        t = jax.nn.sigmoid(gl) * y
        merged = t if merged is None else merged + t
    mix = jnp.dot(merged.astype(MXU_DTYPE), wout_ref[...], preferred_element_type=F32)
    x1 = _layernorm_rows(DEEPNORM_ALPHA * x + (1.0 + gt1) * mix, ln1g_ref[...], ln1b_ref[...])
    x1_ref[...] = x1
    h2 = (x1 * (1.0 + sc2) + sh2).astype(MXU_DTYPE)
    h2_ref[...] = h2

    logits = jnp.dot(h2, wr_ref[...], preferred_element_type=F32) + br_ref[...]
    lane = lax.broadcasted_iota(jnp.int32, (1, LANES), 1).astype(F32)
    top_v, top_i = [], []
    work = logits
    for _ in range(TOP_K):
        m = jnp.max(work, axis=-1, keepdims=True)
        i_sel = jnp.argmax(work, axis=-1, keepdims=True).astype(F32)
        top_v.append(m)
        top_i.append(i_sel)
        work = jnp.where(lane == i_sel, -jnp.inf, work)
    exps = [jnp.exp(tv - top_v[0]) for tv in top_v]
    denom = exps[0] + exps[1] + exps[2] + exps[3]
    onehots = [lane == ti for ti in top_i]
    member = jnp.zeros((ts, LANES), F32)
    for oh in onehots:
        member = member + jnp.where(oh, 1.0, 0.0)
    tr = lax.broadcasted_iota(jnp.int32, (ts, ts), 0)
    tc = lax.broadcasted_iota(jnp.int32, (ts, ts), 1)
    before = jnp.where(tc < tr, 1.0, 0.0).astype(MXU_DTYPE)
    cnt = cntacc_ref[...]
    local_rank = jnp.dot(before, member.astype(MXU_DTYPE), preferred_element_type=F32)
    sel = jnp.zeros((ts, LANES), F32)
    for kk in range(TOP_K):
        lrank_k = jnp.sum(jnp.where(onehots[kk], local_rank, 0.0), axis=-1, keepdims=True)
        sel = sel + jnp.where(lane == float(kk), top_i[kk], 0.0)
        sel = sel + jnp.where(lane == float(TOP_K + kk), exps[kk] / denom, 0.0)
        sel = sel + jnp.where(lane == float(2 * TOP_K + kk), lrank_k, 0.0)
    sel_ref[...] = sel
    before_ref[...] = cnt
    tile_cnt = jnp.sum(member, axis=0, keepdims=True)
    cnt = cnt + jnp.floor((tile_cnt + (RUN_ALIGN - 1.0)) * (1.0 / RUN_ALIGN)) * RUN_ALIGN
    cntacc_ref[...] = cnt
    cnt_ref[...] = cnt


def _const_spec(shape):
    return pl.BlockSpec(shape, lambda b, s: (0,) * len(shape))


def _param_spec(a, layer):
    if layer is None:
        return _const_spec(a.shape)
    return pl.BlockSpec((None,) + a.shape[1:], lambda b, s: (layer,) + (0,) * (a.ndim - 1))


def _mixer_call(x, ada_l, cos_t, sin_t, lw, tables):
    bsz, seq, d = x.shape
    ts = min(SEQ_TILE, seq)
    step = min(MIXER_STEP, seq)
    n_tok = bsz * seq
    row3 = lambda b, s: (b, s, 0)
    in_specs = [
        pl.BlockSpec((None, step, d), row3),
        pl.BlockSpec((None, 6, d), lambda b, s: (b, 0, 0)),
        pl.BlockSpec((None, step, RET_QK_WIDTH), row3),
        pl.BlockSpec((None, step, RET_QK_WIDTH), row3),
    ] + [_param_spec(a, lay) for a, lay in lw] + [_const_spec(a.shape) for a in tables]
    tok_row = lambda b, s: (b * (seq // step) + s, 0)
    out_specs = [
        pl.BlockSpec((step, d), tok_row),
        pl.BlockSpec((step, d), tok_row),
        pl.BlockSpec((step, LANES), tok_row),
        pl.BlockSpec((8, LANES), lambda b, s: (0, 0)),
        pl.BlockSpec((step // ts, 8, LANES), lambda b, s: (b * (seq // step) + s, 0, 0)),
    ]
    out_shape = [
        jax.ShapeDtypeStruct((n_tok, d), F32),
        jax.ShapeDtypeStruct((n_tok, d), MXU_DTYPE),
        jax.ShapeDtypeStruct((n_tok, LANES), F32),
        jax.ShapeDtypeStruct((8, LANES), F32),
        jax.ShapeDtypeStruct((n_tok // ts, 8, LANES), F32),
    ]
    return pl.pallas_call(
        _mixer_kernel,
        grid=(bsz, seq // step),
        in_specs=in_specs,
        out_specs=out_specs,
        out_shape=out_shape,
        scratch_shapes=[
            pltpu.VMEM((RET_QK_WIDTH, RET_V_WIDTH), F32),
            pltpu.VMEM((8, CONV_WIDTH), F32),
            pltpu.VMEM((8, LANES), F32),
        ],
        compiler_params=pltpu.CompilerParams(
            dimension_semantics=("arbitrary", "arbitrary"), vmem_limit_bytes=VMEM_LIMIT),
        name="mixer",
    )(x, ada_l, cos_t, sin_t, *[a for a, _ in lw], *tables)


def _run_pieces(max_rows):
    units = max_rows // RUN_ALIGN
    pieces = []
    bit = 1
    while bit <= units:
        pieces.append(bit)
        bit *= 2
    return tuple(reversed(pieces))


def _piece_table_width(n_classes):
    return -(-(n_classes * 2 * N_EXPERTS + n_classes) // LANES) * LANES


def _for_each_run_piece(tab_ref, pieces, make_copy, fn):
    for c, bit in enumerate(pieces):
        count = tab_ref[0, 0, len(pieces) * 2 * N_EXPERTS + c]

        def body(j, carry, c=c, bit=bit):
            hbm_row = tab_ref[0, 0, c * 2 * N_EXPERTS + j]
            buf_row = tab_ref[0, 0, (c * 2 + 1) * N_EXPERTS + j]
            fn(make_copy(pl.multiple_of(hbm_row, RUN_ALIGN), pl.multiple_of(buf_row, RUN_ALIGN),
                         bit * RUN_ALIGN))
            return carry

        lax.fori_loop(0, count, body, 0)


SEL_IDX, SEL_WEIGHT, SEL_RANK = 0, TOP_K, 2 * TOP_K
ROWS_PAD = 8


def _dispatch_kernel(pe_ref, ps_ref, tab_cur_ref, tab_prev_ref, rows_ref, h2_ref, xs_hbm,
                     zbuf, sbuf, sem, zsem):
    i = pl.program_id(0)
    n_steps = pl.num_programs(0)
    ts = h2_ref.shape[0]
    blk = zbuf.shape[0]
    n_exp = pe_ref.shape[0]
    run_rows = sbuf.shape[1]
    slot = i % 2
    pieces = _run_pieces(ts)

    @pl.when(i == 0)
    def _():
        zbuf[...] = jnp.zeros_like(zbuf)

        def tail_copy(e):
            start = pl.multiple_of(jnp.maximum(pe_ref[e] - blk, 0), blk)
            return pltpu.make_async_copy(zbuf, xs_hbm.at[pl.ds(start, blk), :], zsem)

        def zstart(e, carry):
            @pl.when(pe_ref[e] > ps_ref[e])
            def _():
                tail_copy(e).start()
            return carry

        def zwait(e, carry):
            @pl.when(pe_ref[e] > ps_ref[e])
            def _():
                tail_copy(e).wait()
            return carry

        lax.fori_loop(0, n_exp, zstart, 0)
        lax.fori_loop(0, n_exp, zwait, 0)

    rows_t = rows_ref[...]
    buf_row = lax.broadcasted_iota(jnp.int32, (run_rows, ts), 0).astype(F32)
    pick = jnp.zeros((run_rows, ts), F32)
    for kk in range(TOP_K):
        pick = jnp.where(buf_row == rows_t[kk:kk + 1], 1.0, pick)
    sbuf[slot] = jnp.dot(pick.astype(MXU_DTYPE), h2_ref[...].astype(MXU_DTYPE),
                         preferred_element_type=F32).astype(sbuf.dtype)

    def push(dst_slot):
        def make(hbm_row, buf_row_, rows):
            return pltpu.make_async_copy(sbuf.at[dst_slot, pl.ds(buf_row_, rows), :],
                                         xs_hbm.at[pl.ds(hbm_row, rows), :], sem.at[dst_slot])
        return make

    _for_each_run_piece(tab_cur_ref, pieces, push(slot), lambda c: c.start())

    @pl.when(i > 0)
    def _():
        _for_each_run_piece(tab_prev_ref, pieces, push(1 - slot), lambda c: c.wait())

    @pl.when(i == n_steps - 1)
    def _():
        _for_each_run_piece(tab_cur_ref, pieces, push(slot), lambda c: c.wait())


def _dispatch_call(pad_end, pad_start, tab3, rows8, h2, n_rows, blk):
    n_tok, d = h2.shape
    n_steps = tab3.shape[0]
    ts = n_tok // n_steps
    run_rows = ts * TOP_K + N_EXPERTS * RUN_ALIGN
    grid_spec = pltpu.PrefetchScalarGridSpec(
        num_scalar_prefetch=2,
        grid=(n_steps,),
        in_specs=[
            pl.BlockSpec((1, 1, tab3.shape[2]), lambda i, pe, ps: (i, 0, 0), memory_space=pltpu.SMEM),
            pl.BlockSpec((1, 1, tab3.shape[2]), lambda i, pe, ps: (jnp.maximum(i - 1, 0), 0, 0),
                         memory_space=pltpu.SMEM),
            pl.BlockSpec((ROWS_PAD, ts), lambda i, pe, ps: (0, i)),
            pl.BlockSpec((ts, d), lambda i, pe, ps: (i, 0)),
        ],
        out_specs=pl.BlockSpec(memory_space=pl.ANY),
        scratch_shapes=[pltpu.VMEM((blk, d), MXU_DTYPE), pltpu.VMEM((2, run_rows, d), MXU_DTYPE),
                        pltpu.SemaphoreType.DMA((2,)), pltpu.SemaphoreType.DMA(())],
    )
    return pl.pallas_call(
        _dispatch_kernel,
        grid_spec=grid_spec,
        out_shape=jax.ShapeDtypeStruct((n_rows, d), MXU_DTYPE),
        compiler_params=pltpu.CompilerParams(
            dimension_semantics=("arbitrary",), vmem_limit_bytes=VMEM_LIMIT),
        name="dispatch",
    )(pad_end, pad_start, tab3, tab3, rows8, h2)


def _expert_up(xs, wgu_ref, bgu_ref):
    return jnp.dot(xs, wgu_ref[...], preferred_element_type=F32) + bgu_ref[...]


def _expert_down(gu, wdn_ref, bdn_ref):
    gate = jnp.minimum(gu[:, :D_FF], SWIGLU_LIMIT)
    up = jnp.clip(gu[:, D_FF:], -SWIGLU_LIMIT, SWIGLU_LIMIT)
    act = (up + 1.0) * (gate * jax.nn.sigmoid(SWIGLU_ALPHA * gate))
    return jnp.dot(act.astype(MXU_DTYPE), wdn_ref[...], preferred_element_type=F32) + bdn_ref[...]


def _expert_kernel(be_ref, nu_ref, xs_ref, wgu_ref, bgu_ref, wdn_ref, bdn_ref, ys_ref):
    i = pl.program_id(0)
    n_used = nu_ref[0]

    @pl.when(i < n_used)
    def _():
        part = min(EXPERT_PART, xs_ref.shape[0])
        n_parts = xs_ref.shape[0] // part
        rows = [slice(p * part, (p + 1) * part) for p in range(n_parts)]
        gu = _expert_up(xs_ref[rows[0], :], wgu_ref, bgu_ref)
        for p in range(n_parts):
            gu_next = _expert_up(xs_ref[rows[p + 1], :], wgu_ref, bgu_ref) if p + 1 < n_parts else None
            ys_ref[rows[p], :] = _expert_down(gu, wdn_ref, bdn_ref).astype(ys_ref.dtype)
            gu = gu_next

    @pl.when(i >= n_used)
    def _():
        ys_ref[...] = jnp.zeros_like(ys_ref)


def _expert_call(block_e, n_used, xs, w_gu, b_gu, w_down, b_down, blk, layer):
    n_rows, d = xs.shape
    n_blocks = n_rows // blk
    two_f = w_gu.shape[3]
    last = lambda nu: jnp.maximum(nu[0] - 1, 0)
    of_block = lambda i, be, nu: (layer, be[i], 0, 0)
    grid_spec = pltpu.PrefetchScalarGridSpec(
        num_scalar_prefetch=2,
        grid=(n_blocks,),
        in_specs=[
            pl.BlockSpec((blk, d), lambda i, be, nu: (jnp.minimum(i, last(nu)), 0)),
            pl.BlockSpec((None, None, d, two_f), of_block),
            pl.BlockSpec((None, None, 1, two_f), of_block),
            pl.BlockSpec((None, None, two_f // 2, d), of_block),
            pl.BlockSpec((None, None, 1, d), of_block),
        ],
        out_specs=pl.BlockSpec((blk, d), lambda i, be, nu: (i, 0)),
    )
    return pl.pallas_call(
        _expert_kernel,
        grid_spec=grid_spec,
        out_shape=jax.ShapeDtypeStruct((n_rows, d), MXU_DTYPE),
        compiler_params=pltpu.CompilerParams(
            dimension_semantics=("arbitrary",), vmem_limit_bytes=VMEM_LIMIT),
        name="experts",
    )(block_e, n_used, xs, w_gu, b_gu, w_down, b_down)


def _combine_kernel(tab_cur_ref, tab_nxt_ref, ys_hbm, x1_ref, sel_ref, rows_ref, ada_ref, lng_ref, lnb_ref,
                    o_ref, rbuf, sem):
    i = pl.program_id(0)
    n_steps = pl.num_programs(0)
    ts = x1_ref.shape[0]
    slot = i % 2
    pieces = _run_pieces(ts)

    def fetch(dst_slot):
        def make(hbm_row, buf_row, rows):
            return pltpu.make_async_copy(ys_hbm.at[pl.ds(hbm_row, rows), :],
                                         rbuf.at[dst_slot, pl.ds(buf_row, rows), :], sem.at[dst_slot])
        return make

    @pl.when(i == 0)
    def _():
        rbuf[...] = jnp.zeros_like(rbuf)
        _for_each_run_piece(tab_cur_ref, pieces, fetch(0), lambda c: c.start())

    @pl.when(i + 1 < n_steps)
    def _():
        _for_each_run_piece(tab_nxt_ref, pieces, fetch(1 - slot), lambda c: c.start())

    _for_each_run_piece(tab_cur_ref, pieces, fetch(slot), lambda c: c.wait())

    sel = sel_ref[...]
    rows_t = rows_ref[...]
    rows_c = jnp.concatenate([rows_t, jnp.zeros((LANES - ROWS_PAD, ts), F32)], axis=0).T
    col = lax.broadcasted_iota(jnp.int32, (1, rbuf.shape[1]), 1).astype(F32)
    weights = jnp.zeros((ts, rbuf.shape[1]), F32)
    for kk in range(TOP_K):
        weights = jnp.where(col == rows_c[:, kk:kk + 1], sel[:, SEL_WEIGHT + kk:SEL_WEIGHT + kk + 1], weights)
    hi = weights.astype(MXU_DTYPE)
    lo = (weights - hi.astype(F32)).astype(MXU_DTYPE)
    both = jnp.dot(jnp.concatenate([hi, lo], axis=0), rbuf[slot], preferred_element_type=F32)
    ffn = both[:ts] + both[ts:]
    gt2 = ada_ref[...][5:6]
    o_ref[...] = _layernorm_rows(DEEPNORM_ALPHA * x1_ref[...] + (1.0 + gt2) * ffn, lng_ref[...], lnb_ref[...])


def _combine_call(tab3, ys, x1, sel, rows8, ada_l, ln_g, ln_b, seq):
    n_tok, d = x1.shape
    n_steps = tab3.shape[0]
    ts = n_tok // n_steps
    tiles_per_seq = seq // ts
    run_rows = ts * TOP_K + N_EXPERTS * RUN_ALIGN
    return pl.pallas_call(
        _combine_kernel,
        grid=(n_steps,),
        in_specs=[
            pl.BlockSpec((1, 1, tab3.shape[2]), lambda i: (i, 0, 0), memory_space=pltpu.SMEM),
            pl.BlockSpec((1, 1, tab3.shape[2]), lambda i: (jnp.minimum(i + 1, n_steps - 1), 0, 0),
                         memory_space=pltpu.SMEM),
            pl.BlockSpec(memory_space=pl.ANY),
            pl.BlockSpec((ts, d), lambda i: (i, 0)),
            pl.BlockSpec((ts, LANES), lambda i: (i, 0)),
            pl.BlockSpec((ROWS_PAD, ts), lambda i: (0, i)),
            pl.BlockSpec((None, 6, d), lambda i: (i // tiles_per_seq, 0, 0)),
            pl.BlockSpec((1, d), lambda i: (0, 0)),
            pl.BlockSpec((1, d), lambda i: (0, 0)),
        ],
        out_specs=pl.BlockSpec((ts, d), lambda i: (i, 0)),
        out_shape=jax.ShapeDtypeStruct((n_tok, d), F32),
        scratch_shapes=[pltpu.VMEM((2, run_rows, d), MXU_DTYPE), pltpu.SemaphoreType.DMA((2,))],
        compiler_params=pltpu.CompilerParams(
            dimension_semantics=("arbitrary",), vmem_limit_bytes=VMEM_LIMIT),
        name="combine",
    )(tab3, tab3, ys, x1, sel, rows8, ada_l, ln_g.reshape(1, d), ln_b.reshape(1, d))


def _retention_tables():
    log_gamma = jnp.log(1.0 - 2.0 ** (-5.0 - jnp.arange(RET_HEADS, dtype=F32)))
    idx = jnp.arange(CHUNK, dtype=F32)
    diff = idx[:, None] - idx[None, :]
    decay = jnp.where(diff[None] >= 0,
                      jnp.exp(log_gamma[:, None, None] * jnp.maximum(diff, 0.0)[None]), 0.0)
    decay = decay.reshape(RET_HEADS * CHUNK, CHUNK)
    zeta = jnp.exp(log_gamma[:, None] * (CHUNK - 1.0 - idx)[None])
    zeta_t = jnp.repeat(zeta.T, RET_QK_DIM, axis=1)
    xi = jnp.exp(log_gamma[:, None] * (idx + 1.0)[None]).T
    xi_t = jnp.repeat(xi, RET_V_DIM, axis=1)
    cdec = jnp.repeat(jnp.exp(log_gamma * CHUNK), RET_V_DIM)[None, :]
    row_head = jnp.arange(RET_QK_WIDTH) // RET_QK_DIM
    col_head = jnp.arange(RET_V_WIDTH) // RET_V_DIM
    same_head = row_head[:, None] == col_head[None, :]
    bmask = same_head.astype(F32)
    gavg = ((col_head[:, None] == col_head[None, :]).astype(F32) / RET_V_DIM).astype(MXU_DTYPE)
    return decay, zeta_t, xi_t, cdec, bmask, gavg


def _rotary_tables(positions):
    inv_freq = ROPE_BASE ** (-jnp.arange(0, RET_QK_DIM, 2, dtype=F32) / RET_QK_DIM)
    ang = positions.astype(F32)[..., None] * inv_freq
    cos = jnp.cos(ang)
    sin = jnp.sin(ang)
    cos_t = jnp.tile(jnp.concatenate([cos, cos], axis=-1), (1, 1, RET_HEADS))
    sin_t = jnp.tile(jnp.concatenate([-sin, sin], axis=-1), (1, 1, RET_HEADS))
    return cos_t, sin_t


def _moe_rows(n_tok, n_tiles, blk):
    worst = n_tok * TOP_K + n_tiles * N_EXPERTS * (RUN_ALIGN - 1)
    return (-(-worst // blk) + N_EXPERTS) * blk


def _routing_tables(sel, counts, before, n_rows, blk, pieces):
    n_blocks = n_rows // blk
    sizes = counts.astype(jnp.int32)
    pad_sizes = (sizes + blk - 1) // blk * blk
    pad_end = jnp.cumsum(pad_sizes)
    pad_start = pad_end - pad_sizes
    n_used = (pad_end[-1] // blk).astype(jnp.int32)
    block_start = jnp.arange(n_blocks, dtype=jnp.int32) * blk
    block_start = jnp.minimum(block_start, (n_used - 1) * blk)
    block_e = jnp.sum((pad_end[None, :] <= block_start[:, None]).astype(jnp.int32), axis=-1)
    block_e = jnp.minimum(block_e, N_EXPERTS - 1)

    before = before.astype(jnp.int32)
    after = jnp.concatenate([before[1:], sizes[None, :]], axis=0)
    run_start = pad_start[None, :] + before
    run_units = (after - before) // RUN_ALIGN
    run_off = (jnp.cumsum(run_units, axis=-1) - run_units) * RUN_ALIGN
    n_tiles = before.shape[0]
    slot_id = jnp.arange(N_EXPERTS, dtype=jnp.int32)
    cols, counts_c = [], []
    for bit in pieces:
        has = jnp.bitwise_and(run_units, bit) != 0
        done = jnp.bitwise_and(run_units, -2 * bit) * RUN_ALIGN
        pos = jnp.cumsum(has.astype(jnp.int32), axis=-1) - 1
        place = jnp.logical_and(has[:, None, :], pos[:, None, :] == slot_id[None, :, None])
        cols.append(jnp.sum(jnp.where(place, (run_start + done)[:, None, :], 0), axis=-1))
        cols.append(jnp.sum(jnp.where(place, (run_off + done)[:, None, :], 0), axis=-1))
        counts_c.append(jnp.sum(has.astype(jnp.int32), axis=-1, keepdims=True))
    width = _piece_table_width(len(pieces))
    used = len(pieces) * (2 * N_EXPERTS + 1)
    tab = jnp.concatenate(cols + counts_c + [jnp.zeros((n_tiles, width - used), jnp.int32)], axis=-1)

    n_tok = sel.shape[0]
    idx = sel[:, SEL_IDX:SEL_IDX + TOP_K].astype(jnp.int32)
    off_tok = jnp.repeat(run_off, n_tok // n_tiles, axis=0)
    base = jnp.sum(jnp.where(idx[:, :, None] == slot_id, off_tok[:, None, :], 0), axis=-1)
    rows = base.astype(F32) + sel[:, SEL_RANK:SEL_RANK + TOP_K]
    rows8 = jnp.pad(rows.T, ((0, ROWS_PAD - TOP_K), (0, 0)))
    return (pad_end.astype(jnp.int32), pad_start.astype(jnp.int32), block_e, n_used.reshape(1),
            tab.astype(jnp.int32).reshape(n_tiles, 1, width), rows8)


def kernel(x, c, positions, w_ada, b_ada, w_in, gmlp_ln_g, gmlp_ln_b, w_spatial, b_spatial, conv_w,
           w_branch, w_gate_up, b_gate, w_out, ln1_g, ln1_b, w_router, b_router, w_gu, b_gu, w_down,
           b_down, ln2_g, ln2_b):
    bsz, seq, d = x.shape
    n_tok = bsz * seq
    depth = w_ada.shape[0]
    blk = min(MOE_BLOCK, n_tok * TOP_K // N_EXPERTS)
    ts = min(SEQ_TILE, seq)

    ada = _ada_call(c, w_ada, b_ada)
    ada = jnp.transpose(ada, (0, 2, 1, 3))
    cos_t, sin_t = _rotary_tables(positions)
    tables = _retention_tables()

    mx = MXU_DTYPE
    w_in_m = w_in.astype(mx)
    w_branch_m = w_branch.astype(mx)
    w_gate_m = w_gate_up.astype(mx)
    w_out_m = w_out.astype(mx)
    w_gu_m = w_gu.astype(mx)
    w_down_m = w_down.astype(mx)
    b_gu4 = b_gu[:, :, None, :]
    b_down4 = b_down[:, :, None, :]
    n_exp = w_router.shape[-1]
    w_router_m = jnp.pad(w_router, ((0, 0), (0, 0), (0, LANES - n_exp))).astype(mx)
    b_router_p = jnp.pad(b_router, ((0, 0), (0, LANES - n_exp)), constant_values=NEG_BIG)

    for l in range(depth):
        bsp_t = jnp.repeat(b_spatial[l].T, GMLP_WIDTH // GMLP_GROUPS, axis=1)
        lw = (
            (w_in_m, l),
            (gmlp_ln_g[l].reshape(1, -1), None), (gmlp_ln_b[l].reshape(1, -1), None),
            (w_spatial[l].reshape(GMLP_GROUPS * CHUNK, CHUNK), None), (bsp_t, None),
            (jnp.pad(conv_w[l], ((0, 8 - conv_w.shape[1]), (0, 0))), None),
            (w_branch_m, l), (w_gate_m, l), (b_gate, l), (w_out_m, l),
            (ln1_g[l].reshape(1, -1), None), (ln1_b[l].reshape(1, -1), None),
            (w_router_m, l), (b_router_p[l].reshape(1, -1), None),
        )
        x1, h2, sel, cnt, before = _mixer_call(x, ada[l], cos_t, sin_t, lw, tables)
        n_rows = _moe_rows(n_tok, n_tok // ts, blk)
        pad_end, pad_start, block_e, n_used, tab3, rows8 = _routing_tables(
            sel, cnt[0, :n_exp], before[:, 0, :n_exp], n_rows, blk, _run_pieces(ts))
        xs = _dispatch_call(pad_end, pad_start, tab3, rows8, h2, n_rows, blk)
        ys = _expert_call(block_e, n_used, xs, w_gu_m, b_gu4, w_down_m, b_down4, blk, l)
        x = _combine_call(tab3, ys, x1, sel, rows8, ada[l], ln2_g[l], ln2_b[l], seq).reshape(bsz, seq, d)
    return x
```

```python
import functools

import jax
import jax.numpy as jnp
from jax import lax
from jax.experimental import pallas as pl
from jax.experimental.pallas import tpu as pltpu

F32 = jnp.float32
MXU_DTYPE = jnp.bfloat16

DEPTH = 4
RET_HEADS = 4
RET_QK_DIM = 32
RET_V_DIM = 64
RET_QK_WIDTH = RET_HEADS * RET_QK_DIM
RET_V_WIDTH = RET_HEADS * RET_V_DIM
CHUNK = 128
ROPE_BASE = 10000.0
GMLP_GROUPS = 4
GMLP_WIDTH = 256
CONV_WIDTH = 256
GATE_RANK = 128
N_BRANCH = 3
N_EXPERTS = 32
TOP_K = 4
D_FF = 256
SWIGLU_LIMIT = 7.0
SWIGLU_ALPHA = 1.702
DEEPNORM_ALPHA = (2.0 * DEPTH) ** 0.25
LN_EPS = 1e-5

_O_QK = 0
_O_VG = 2 * RET_QK_WIDTH
_O_GMLP = _O_VG + 2 * RET_V_WIDTH
_O_CONV = _O_GMLP + 2 * GMLP_WIDTH
_O_CODE = _O_CONV + 3 * CONV_WIDTH
_IN_WIDTH = _O_CODE + N_BRANCH * GATE_RANK

LANES = 128
NEG_BIG = -1e30

SEQ_TILE = 256
MIXER_STEP = 512
MOE_BLOCK = 1024
EXPERT_PART = 256
COMBINE_PART = 128
RUN_ALIGN = 16
VMEM_LIMIT = 56 * 1024 * 1024


def _dot(a, b):
    return jnp.dot(a.astype(MXU_DTYPE), b.astype(MXU_DTYPE), preferred_element_type=F32)


def _dot_nt(a, b):
    return lax.dot_general(a.astype(MXU_DTYPE), b.astype(MXU_DTYPE),
                           (((1,), (1,)), ((), ())), preferred_element_type=F32)


def _split_dot(x, w):
    hi = x.astype(MXU_DTYPE)
    lo = (x - hi.astype(F32)).astype(MXU_DTYPE)
    return (jnp.dot(hi, w, preferred_element_type=F32)
            + jnp.dot(lo, w, preferred_element_type=F32))


def _layernorm_rows(x, g, b):
    mu = jnp.mean(x, axis=-1, keepdims=True)
    d = x - mu
    var = jnp.mean(d * d, axis=-1, keepdims=True)
    return d * lax.rsqrt(var + LN_EPS) * g + b


def _ada_kernel(c_ref, w_ref, b_ref, o_ref):
    c_act = jax.nn.silu(c_ref[...])
    o_ref[...] = jnp.dot(c_act, w_ref[...], preferred_element_type=F32,
                         precision=lax.Precision.HIGHEST) + b_ref[...]


def _ada_call(c, w_ada, b_ada):
    depth, d, six_d = w_ada.shape
    bsz = c.shape[0]
    n_col = six_d // d
    return pl.pallas_call(
        _ada_kernel,
        grid=(depth, n_col),
        in_specs=[
            pl.BlockSpec((bsz, d), lambda l, j: (0, 0)),
            pl.BlockSpec((None, d, d), lambda l, j: (l, 0, j)),
            pl.BlockSpec((None, 1, d), lambda l, j: (l, 0, j)),
        ],
        out_specs=pl.BlockSpec((None, None, bsz, d), lambda l, j: (l, j, 0, 0)),
        out_shape=jax.ShapeDtypeStruct((depth, n_col, bsz, d), F32),
        compiler_params=pltpu.CompilerParams(
            dimension_semantics=("arbitrary", "arbitrary"), vmem_limit_bytes=VMEM_LIMIT),
        name="ada_ln",
    )(c, w_ada, b_ada.reshape(depth, 1, six_d))


def _mixer_kernel(x_ref, ada_ref, cos_ref, sin_ref, *rest):
    params = rest[:-8]
    x1_ref, h2_ref, sel_ref, cnt_ref, before_ref, state_ref, zc_ref, cntacc_ref = rest[-8:]
    b = pl.program_id(0)
    s = pl.program_id(1)

    @pl.when(s == 0)
    def _():
        state_ref[...] = jnp.zeros_like(state_ref)
        zc_ref[...] = jnp.zeros_like(zc_ref)

    @pl.when(jnp.logical_and(b == 0, s == 0))
    def _():
        cntacc_ref[...] = jnp.zeros_like(cntacc_ref)

    n_sub = before_ref.shape[0]
    ts = x_ref.shape[0] // n_sub
    for u in range(n_sub):
        rows = pl.ds(u * ts, ts)
        _mixer_tile(x_ref.at[rows], ada_ref, cos_ref.at[rows], sin_ref.at[rows], *params,
                    x1_ref.at[rows], h2_ref.at[rows], sel_ref.at[rows], cnt_ref, before_ref.at[u],
                    state_ref, zc_ref, cntacc_ref)


def _mixer_tile(x_ref, ada_ref, cos_ref, sin_ref, w_in_ref, lng_ref, lnb_ref, wsp_ref, bsp_ref,
                convw_ref, wbr_ref, wgt_ref, bgt_ref, wout_ref, ln1g_ref, ln1b_ref, wr_ref, br_ref,
                decay_ref, zeta_ref, xi_ref, cdec_ref, bmask_ref, gavg_ref,
                x1_ref, h2_ref, sel_ref, cnt_ref, before_ref,
                state_ref, zc_ref, cntacc_ref):
    ts = x_ref.shape[0]
    x = x_ref[...]
    ada = ada_ref[...]
    sh1, sc1, gt1, sh2, sc2 = ada[0:1], ada[1:2], ada[2:3], ada[3:4], ada[4:5]
    h = (x * (1.0 + sc1) + sh1).astype(MXU_DTYPE)

    def proj(lo, hi):
        return jnp.dot(h, w_in_ref[:, lo:hi], preferred_element_type=F32)

    lane_qk = lax.broadcasted_iota(jnp.int32, (1, RET_QK_WIDTH), 1)
    lane_v = lax.broadcasted_iota(jnp.int32, (1, RET_V_WIDTH), 1)
    qk_masks = [lane_qk // RET_QK_DIM == hd for hd in range(RET_HEADS)]
    v_masks = [lane_v // RET_V_DIM == hd for hd in range(RET_HEADS)]

    qk = proj(_O_QK, _O_VG)
    cos = cos_ref[...]
    sin = sin_ref[...]
    first_half = (lane_qk % RET_QK_DIM) < (RET_QK_DIM // 2)

    def rotary(t):
        swapped = jnp.where(first_half, pltpu.roll(t, RET_QK_WIDTH - RET_QK_DIM // 2, 1),
                            pltpu.roll(t, RET_QK_DIM // 2, 1))
        return t * cos + swapped * sin

    q = rotary(qk[:, :RET_QK_WIDTH])
    k = rotary(qk[:, RET_QK_WIDTH:]) * (RET_QK_DIM ** -0.5)
    vg = proj(_O_VG, _O_GMLP)
    v = vg[:, :RET_V_WIDTH]
    g = vg[:, RET_V_WIDTH:]

    decay = decay_ref[...]
    zeta = zeta_ref[...]
    xi = xi_ref[...]
    cdec = cdec_ref[...]
    bmask = bmask_ref[...]
    gavg = gavg_ref[...]

    o_chunks = []
    state = state_ref[...]
    for c in range(ts // CHUNK):
        rows = slice(c * CHUNK, (c + 1) * CHUNK)
        q_c, k_c, v_c = q[rows], k[rows], v[rows]
        v_m = v_c.astype(MXU_DTYPE)
        q_all = jnp.concatenate([jnp.where(m, q_c, 0.0) for m in qk_masks], axis=0)
        scores = _dot_nt(q_all, k_c) * decay
        o_all = _dot(scores, v_m)
        o = _dot(q_c, state) * xi
        for hd in range(RET_HEADS):
            o = o + jnp.where(v_masks[hd], o_all[hd * CHUNK:(hd + 1) * CHUNK], 0.0)
        kv = _dot((k_c * zeta).T, v_m) * bmask
        state = state * cdec + kv
        o_chunks.append(o)
    state_ref[...] = state
    o = jnp.concatenate(o_chunks, axis=0) if len(o_chunks) > 1 else o_chunks[0]
    mu = _split_dot(o, gavg)
    d = o - mu
    var = _split_dot(d * d, gavg)
    r_br = jax.nn.silu(g) * (d * lax.rsqrt(var + LN_EPS))

    guv = proj(_O_GMLP, _O_CONV)
    u = jax.nn.gelu(guv[:, :GMLP_WIDTH])
    vv = _layernorm_rows(jax.nn.gelu(guv[:, GMLP_WIDTH:]), lng_ref[...], lnb_ref[...]).astype(MXU_DTYPE)
    rr = lax.broadcasted_iota(jnp.int32, (GMLP_GROUPS * CHUNK, CHUNK), 0) % CHUNK
    cc_ = lax.broadcasted_iota(jnp.int32, (GMLP_GROUPS * CHUNK, CHUNK), 1)
    w_sp = jnp.where(cc_ <= rr, wsp_ref[...], 0.0).astype(MXU_DTYPE)
    bsp = bsp_ref[...]
    z_chunks = []
    for c in range(ts // CHUNK):
        z_all = jnp.dot(w_sp, vv[c * CHUNK:(c + 1) * CHUNK], preferred_element_type=F32)
        z = bsp
        for gi in range(GMLP_GROUPS):
            z = z + jnp.where(v_masks[gi], z_all[gi * CHUNK:(gi + 1) * CHUNK], 0.0)
        z_chunks.append(z)
    z = jnp.concatenate(z_chunks, axis=0) if len(z_chunks) > 1 else z_chunks[0]
    s_br = u * z

    cbcx = proj(_O_CONV, _O_CODE)
    gate_b = cbcx[:, :CONV_WIDTH]
    zc = cbcx[:, CONV_WIDTH:2 * CONV_WIDTH] * cbcx[:, 2 * CONV_WIDTH:]
    carry = zc_ref[...]
    prev1 = carry[7:8]
    prev2 = carry[6:7]
    row = lax.broadcasted_iota(jnp.int32, (ts, 1), 0)
    z1 = jnp.where(row == 0, prev1, pltpu.roll(zc, 1, 0))
    z2 = jnp.where(row == 0, prev2, jnp.where(row == 1, prev1, pltpu.roll(zc, 2, 0)))
    zc_ref[...] = zc[ts - 8:ts]
    cw = convw_ref[...]
    k_br = gate_b * (cw[0:1] * z2 + cw[1:2] * z1 + cw[2:3] * zc)

    code = proj(_O_CODE, _IN_WIDTH)
    merged = None
    for i, br in enumerate((r_br, s_br, k_br)):
        y = jnp.dot(br.astype(MXU_DTYPE), wbr_ref[i], preferred_element_type=F32)
        gl = jnp.dot(code[:, i * GATE_RANK:(i + 1) * GATE_RANK].astype(MXU_DTYPE), wgt_ref[i],
                     preferred_element_type=F32) + bgt_ref[i:i + 1]
        t = jax.nn.sigmoid(gl) * y
        merged = t if merged is None else merged + t
    mix = jnp.dot(merged.astype(MXU_DTYPE), wout_ref[...], preferred_element_type=F32)
    x1 = _layernorm_rows(DEEPNORM_ALPHA * x + (1.0 + gt1) * mix, ln1g_ref[...], ln1b_ref[...])
    x1_ref[...] = x1
    h2 = (x1 * (1.0 + sc2) + sh2).astype(MXU_DTYPE)
    h2_ref[...] = h2

    logits = jnp.dot(h2, wr_ref[...], preferred_element_type=F32) + br_ref[...]
    lane = lax.broadcasted_iota(jnp.int32, (1, LANES), 1).astype(F32)
    top_v, top_i = [], []
    work = logits
    for _ in range(TOP_K):
        m = jnp.max(work, axis=-1, keepdims=True)
        i_sel = jnp.argmax(work, axis=-1, keepdims=True).astype(F32)
        top_v.append(m)
        top_i.append(i_sel)
        work = jnp.where(lane == i_sel, -jnp.inf, work)
    exps = [jnp.exp(tv - top_v[0]) for tv in top_v]
    denom = exps[0] + exps[1] + exps[2] + exps[3]
    onehots = [lane == ti for ti in top_i]
    member = jnp.zeros((ts, LANES), F32)
    for oh in onehots:
        member = member + jnp.where(oh, 1.0, 0.0)
    tr = lax.broadcasted_iota(jnp.int32, (ts, ts), 0)
    tc = lax.broadcasted_iota(jnp.int32, (ts, ts), 1)
    before = jnp.where(tc < tr, 1.0, 0.0).astype(MXU_DTYPE)
    cnt = cntacc_ref[...]
    local_rank = jnp.dot(before, member.astype(MXU_DTYPE), preferred_element_type=F32)
    sel = jnp.zeros((ts, LANES), F32)
    for kk in range(TOP_K):
        lrank_k = jnp.sum(jnp.where(onehots[kk], local_rank, 0.0), axis=-1, keepdims=True)
        sel = sel + jnp.where(lane == float(kk), top_i[kk], 0.0)
        sel = sel + jnp.where(lane == float(TOP_K + kk), exps[kk] / denom, 0.0)
        sel = sel + jnp.where(lane == float(2 * TOP_K + kk), lrank_k, 0.0)
    sel_ref[...] = sel
    before_ref[...] = cnt
    tile_cnt = jnp.sum(member, axis=0, keepdims=True)
    cnt = cnt + jnp.floor((tile_cnt + (RUN_ALIGN - 1.0)) * (1.0 / RUN_ALIGN)) * RUN_ALIGN
    cntacc_ref[...] = cnt
    cnt_ref[...] = cnt


def _const_spec(shape):
    return pl.BlockSpec(shape, lambda b, s: (0,) * len(shape))


def _param_spec(a, layer):
    if layer is None:
        return _const_spec(a.shape)
    return pl.BlockSpec((None,) + a.shape[1:], lambda b, s: (layer,) + (0,) * (a.ndim - 1))


def _mixer_call(x, ada_l, cos_t, sin_t, lw, tables):
    bsz, seq, d = x.shape
    ts = min(SEQ_TILE, seq)
    step = min(MIXER_STEP, seq)
    n_tok = bsz * seq
    row3 = lambda b, s: (b, s, 0)
    in_specs = [
        pl.BlockSpec((None, step, d), row3),
        pl.BlockSpec((None, 6, d), lambda b, s: (b, 0, 0)),
        pl.BlockSpec((None, step, RET_QK_WIDTH), row3),
        pl.BlockSpec((None, step, RET_QK_WIDTH), row3),
    ] + [_param_spec(a, lay) for a, lay in lw] + [_const_spec(a.shape) for a in tables]
    tok_row = lambda b, s: (b * (seq // step) + s, 0)
    out_specs = [
        pl.BlockSpec((step, d), tok_row),
        pl.BlockSpec((step, d), tok_row),
        pl.BlockSpec((step, LANES), tok_row),
        pl.BlockSpec((8, LANES), lambda b, s: (0, 0)),
        pl.BlockSpec((step // ts, 8, LANES), lambda b, s: (b * (seq // step) + s, 0, 0)),
    ]
    out_shape = [
        jax.ShapeDtypeStruct((n_tok, d), F32),
        jax.ShapeDtypeStruct((n_tok, d), MXU_DTYPE),
        jax.ShapeDtypeStruct((n_tok, LANES), F32),
        jax.ShapeDtypeStruct((8, LANES), F32),
        jax.ShapeDtypeStruct((n_tok // ts, 8, LANES), F32),
    ]
    return pl.pallas_call(
        _mixer_kernel,
        grid=(bsz, seq // step),
        in_specs=in_specs,
        out_specs=out_specs,
        out_shape=out_shape,
        scratch_shapes=[
            pltpu.VMEM((RET_QK_WIDTH, RET_V_WIDTH), F32),
            pltpu.VMEM((8, CONV_WIDTH), F32),
            pltpu.VMEM((8, LANES), F32),
        ],
        compiler_params=pltpu.CompilerParams(
            dimension_semantics=("arbitrary", "arbitrary"), vmem_limit_bytes=VMEM_LIMIT),
        name="mixer",
    )(x, ada_l, cos_t, sin_t, *[a for a, _ in lw], *tables)


def _run_pieces(max_rows):
    units = max_rows // RUN_ALIGN
    pieces = []
    bit = 1
    while bit <= units:
        pieces.append(bit)
        bit *= 2
    return tuple(reversed(pieces))


def _piece_table_width(n_classes):
    return -(-(n_classes * 2 * N_EXPERTS + n_classes) // LANES) * LANES


def _for_each_run_piece(tab_ref, pieces, make_copy, fn):
    for c, bit in enumerate(pieces):
        count = tab_ref[0, 0, len(pieces) * 2 * N_EXPERTS + c]

        def body(j, carry, c=c, bit=bit):
            hbm_row = tab_ref[0, 0, c * 2 * N_EXPERTS + j]
            buf_row = tab_ref[0, 0, (c * 2 + 1) * N_EXPERTS + j]
            fn(make_copy(pl.multiple_of(hbm_row, RUN_ALIGN), pl.multiple_of(buf_row, RUN_ALIGN),
                         bit * RUN_ALIGN))
            return carry

        lax.fori_loop(0, count, body, 0)


SEL_IDX, SEL_WEIGHT, SEL_RANK = 0, TOP_K, 2 * TOP_K
ROWS_PAD = 8


def _dispatch_kernel(pe_ref, ps_ref, tab_cur_ref, tab_prev_ref, rows_ref, h2_ref, xs_hbm,
                     zbuf, sbuf, sem, zsem):
    i = pl.program_id(0)
    n_steps = pl.num_programs(0)
    ts = h2_ref.shape[0]
    blk = zbuf.shape[0]
    n_exp = pe_ref.shape[0]
    run_rows = sbuf.shape[1]
    slot = i % 2
    pieces = _run_pieces(ts)

    @pl.when(i == 0)
    def _():
        zbuf[...] = jnp.zeros_like(zbuf)

        def tail_copy(e):
            start = pl.multiple_of(jnp.maximum(pe_ref[e] - blk, 0), blk)
            return pltpu.make_async_copy(zbuf, xs_hbm.at[pl.ds(start, blk), :], zsem)

        def zstart(e, carry):
            @pl.when(pe_ref[e] > ps_ref[e])
            def _():
                tail_copy(e).start()
            return carry

        def zwait(e, carry):
            @pl.when(pe_ref[e] > ps_ref[e])
            def _():
                tail_copy(e).wait()
            return carry

        lax.fori_loop(0, n_exp, zstart, 0)
        lax.fori_loop(0, n_exp, zwait, 0)

    rows_t = rows_ref[...]
    buf_row = lax.broadcasted_iota(jnp.int32, (run_rows, ts), 0).astype(F32)
    pick = jnp.zeros((run_rows, ts), F32)
    for kk in range(TOP_K):
        pick = jnp.where(buf_row == rows_t[kk:kk + 1], 1.0, pick)
    sbuf[slot] = jnp.dot(pick.astype(MXU_DTYPE), h2_ref[...].astype(MXU_DTYPE),
                         preferred_element_type=F32).astype(sbuf.dtype)

    def push(dst_slot):
        def make(hbm_row, buf_row_, rows):
            return pltpu.make_async_copy(sbuf.at[dst_slot, pl.ds(buf_row_, rows), :],
                                         xs_hbm.at[pl.ds(hbm_row, rows), :], sem.at[dst_slot])
        return make

    _for_each_run_piece(tab_cur_ref, pieces, push(slot), lambda c: c.start())

    @pl.when(i > 0)
    def _():
        _for_each_run_piece(tab_prev_ref, pieces, push(1 - slot), lambda c: c.wait())

    @pl.when(i == n_steps - 1)
    def _():
        _for_each_run_piece(tab_cur_ref, pieces, push(slot), lambda c: c.wait())


def _dispatch_call(pad_end, pad_start, tab3, rows8, h2, n_rows, blk):
    n_tok, d = h2.shape
    n_steps = tab3.shape[0]
    ts = n_tok // n_steps
    run_rows = ts * TOP_K + N_EXPERTS * RUN_ALIGN
    grid_spec = pltpu.PrefetchScalarGridSpec(
        num_scalar_prefetch=2,
        grid=(n_steps,),
        in_specs=[
            pl.BlockSpec((1, 1, tab3.shape[2]), lambda i, pe, ps: (i, 0, 0), memory_space=pltpu.SMEM),
            pl.BlockSpec((1, 1, tab3.shape[2]), lambda i, pe, ps: (jnp.maximum(i - 1, 0), 0, 0),
                         memory_space=pltpu.SMEM),
            pl.BlockSpec((ROWS_PAD, ts), lambda i, pe, ps: (0, i)),
            pl.BlockSpec((ts, d), lambda i, pe, ps: (i, 0)),
        ],
        out_specs=pl.BlockSpec(memory_space=pl.ANY),
        scratch_shapes=[pltpu.VMEM((blk, d), MXU_DTYPE), pltpu.VMEM((2, run_rows, d), MXU_DTYPE),
                        pltpu.SemaphoreType.DMA((2,)), pltpu.SemaphoreType.DMA(())],
    )
    return pl.pallas_call(
        _dispatch_kernel,
        grid_spec=grid_spec,
        out_shape=jax.ShapeDtypeStruct((n_rows, d), MXU_DTYPE),
        compiler_params=pltpu.CompilerParams(
            dimension_semantics=("arbitrary",), vmem_limit_bytes=VMEM_LIMIT),
        name="dispatch",
    )(pad_end, pad_start, tab3, tab3, rows8, h2)


def _expert_up(xs, wgu, bgu_ref):
    return jnp.dot(xs, wgu, preferred_element_type=F32) + bgu_ref[...]


def _expert_down(gu, wdn, bdn_ref):
    gate = jnp.minimum(gu[:, :D_FF], SWIGLU_LIMIT)
    up = jnp.clip(gu[:, D_FF:], -SWIGLU_LIMIT, SWIGLU_LIMIT)
    act = (up + 1.0) * (gate * jax.nn.sigmoid(SWIGLU_ALPHA * gate))
    return jnp.dot(act.astype(MXU_DTYPE), wdn, preferred_element_type=F32) + bdn_ref[...]


def _expert_kernel(be_ref, nu_ref, xs_ref, wgu_ref, bgu_ref, wdn_ref, bdn_ref, ys_ref):
    i = pl.program_id(0)
    n_used = nu_ref[0]

    @pl.when(i < n_used)
    def _():
        part = min(EXPERT_PART, xs_ref.shape[0])
        n_parts = xs_ref.shape[0] // part
        rows = [slice(p * part, (p + 1) * part) for p in range(n_parts)]
        wgu = wgu_ref[...].astype(MXU_DTYPE)
        wdn = wdn_ref[...].astype(MXU_DTYPE)
        gu = _expert_up(xs_ref[rows[0], :], wgu, bgu_ref)
        for p in range(n_parts):
            gu_next = _expert_up(xs_ref[rows[p + 1], :], wgu, bgu_ref) if p + 1 < n_parts else None
            ys_ref[rows[p], :] = _expert_down(gu, wdn, bdn_ref).astype(ys_ref.dtype)
            gu = gu_next

    @pl.when(i >= n_used)
    def _():
        ys_ref[...] = jnp.zeros_like(ys_ref)


def _expert_call(block_e, n_used, xs, w_gu, b_gu, w_down, b_down, blk, layer):
    n_rows, d = xs.shape
    n_blocks = n_rows // blk
    two_f = w_gu.shape[3]
    last = lambda nu: jnp.maximum(nu[0] - 1, 0)
    of_block = lambda i, be, nu: (layer, be[i], 0, 0)
    grid_spec = pltpu.PrefetchScalarGridSpec(
        num_scalar_prefetch=2,
        grid=(n_blocks,),
        in_specs=[
            pl.BlockSpec((blk, d), lambda i, be, nu: (jnp.minimum(i, last(nu)), 0)),
            pl.BlockSpec((None, None, d, two_f), of_block),
            pl.BlockSpec((None, None, 1, two_f), of_block),
            pl.BlockSpec((None, None, two_f // 2, d), of_block),
            pl.BlockSpec((None, None, 1, d), of_block),
        ],
        out_specs=pl.BlockSpec((blk, d), lambda i, be, nu: (i, 0)),
    )
    return pl.pallas_call(
        _expert_kernel,
        grid_spec=grid_spec,
        out_shape=jax.ShapeDtypeStruct((n_rows, d), MXU_DTYPE),
        compiler_params=pltpu.CompilerParams(
            dimension_semantics=("arbitrary",), vmem_limit_bytes=VMEM_LIMIT),
        name="experts",
    )(block_e, n_used, xs, w_gu, b_gu, w_down, b_down)


def _combine_kernel(tab_cur_ref, tab_nxt_ref, ys_hbm, x1_ref, sel_ref, rows_ref, ada_ref, lng_ref, lnb_ref,
                    o_ref, rbuf, sem):
    i = pl.program_id(0)
    n_steps = pl.num_programs(0)
    ts = x1_ref.shape[0]
    slot = i % 2
    pieces = _run_pieces(ts)

    def fetch(dst_slot):
        def make(hbm_row, buf_row, rows):
            return pltpu.make_async_copy(ys_hbm.at[pl.ds(hbm_row, rows), :],
                                         rbuf.at[dst_slot, pl.ds(buf_row, rows), :], sem.at[dst_slot])
        return make

    @pl.when(i == 0)
    def _():
        rbuf[...] = jnp.zeros_like(rbuf)
        _for_each_run_piece(tab_cur_ref, pieces, fetch(0), lambda c: c.start())

    @pl.when(i + 1 < n_steps)
    def _():
        _for_each_run_piece(tab_nxt_ref, pieces, fetch(1 - slot), lambda c: c.start())

    _for_each_run_piece(tab_cur_ref, pieces, fetch(slot), lambda c: c.wait())

    sel = sel_ref[...]
    rows_t = rows_ref[...]
    rows_c = jnp.concatenate([rows_t, jnp.zeros((LANES - ROWS_PAD, ts), F32)], axis=0).T
    col = lax.broadcasted_iota(jnp.int32, (1, rbuf.shape[1]), 1).astype(F32)
    gt2 = ada_ref[...][5:6]
    part = min(COMBINE_PART, ts)

    def weight_pieces(tok):
        weights = jnp.zeros((part, rbuf.shape[1]), F32)
        for kk in range(TOP_K):
            weights = jnp.where(col == rows_c[tok, kk:kk + 1],
                                sel[tok, SEL_WEIGHT + kk:SEL_WEIGHT + kk + 1], weights)
        hi = weights.astype(MXU_DTYPE)
        lo = (weights - hi.astype(F32)).astype(MXU_DTYPE)
        return jnp.concatenate([hi, lo], axis=0)

    toks = [slice(p * part, (p + 1) * part) for p in range(ts // part)]
    pieces_w = weight_pieces(toks[0])
    for p, tok in enumerate(toks):
        nxt = weight_pieces(toks[p + 1]) if p + 1 < len(toks) else None
        both = jnp.dot(pieces_w, rbuf[slot], preferred_element_type=F32)
        ffn = both[:part] + both[part:]
        o_ref[tok, :] = _layernorm_rows(DEEPNORM_ALPHA * x1_ref[tok, :] + (1.0 + gt2) * ffn,
                                        lng_ref[...], lnb_ref[...])
        pieces_w = nxt


def _combine_call(tab3, ys, x1, sel, rows8, ada_l, ln_g, ln_b, seq):
    n_tok, d = x1.shape
    n_steps = tab3.shape[0]
    ts = n_tok // n_steps
    tiles_per_seq = seq // ts
    run_rows = ts * TOP_K + N_EXPERTS * RUN_ALIGN
    return pl.pallas_call(
        _combine_kernel,
        grid=(n_steps,),
        in_specs=[
            pl.BlockSpec((1, 1, tab3.shape[2]), lambda i: (i, 0, 0), memory_space=pltpu.SMEM),
            pl.BlockSpec((1, 1, tab3.shape[2]), lambda i: (jnp.minimum(i + 1, n_steps - 1), 0, 0),
                         memory_space=pltpu.SMEM),
            pl.BlockSpec(memory_space=pl.ANY),
            pl.BlockSpec((ts, d), lambda i: (i, 0)),
            pl.BlockSpec((ts, LANES), lambda i: (i, 0)),
            pl.BlockSpec((ROWS_PAD, ts), lambda i: (0, i)),
            pl.BlockSpec((None, 6, d), lambda i: (i // tiles_per_seq, 0, 0)),
            pl.BlockSpec((1, d), lambda i: (0, 0)),
            pl.BlockSpec((1, d), lambda i: (0, 0)),
        ],
        out_specs=pl.BlockSpec((ts, d), lambda i: (i, 0)),
        out_shape=jax.ShapeDtypeStruct((n_tok, d), F32),
        scratch_shapes=[pltpu.VMEM((2, run_rows, d), MXU_DTYPE), pltpu.SemaphoreType.DMA((2,))],
        compiler_params=pltpu.CompilerParams(
            dimension_semantics=("arbitrary",), vmem_limit_bytes=VMEM_LIMIT),
        name="combine",
    )(tab3, tab3, ys, x1, sel, rows8, ada_l, ln_g.reshape(1, d), ln_b.reshape(1, d))


def _retention_tables():
    log_gamma = jnp.log(1.0 - 2.0 ** (-5.0 - jnp.arange(RET_HEADS, dtype=F32)))
    idx = jnp.arange(CHUNK, dtype=F32)
    diff = idx[:, None] - idx[None, :]
    decay = jnp.where(diff[None] >= 0,
                      jnp.exp(log_gamma[:, None, None] * jnp.maximum(diff, 0.0)[None]), 0.0)
    decay = decay.reshape(RET_HEADS * CHUNK, CHUNK)
    zeta = jnp.exp(log_gamma[:, None] * (CHUNK - 1.0 - idx)[None])
    zeta_t = jnp.repeat(zeta.T, RET_QK_DIM, axis=1)
    xi = jnp.exp(log_gamma[:, None] * (idx + 1.0)[None]).T
    xi_t = jnp.repeat(xi, RET_V_DIM, axis=1)
    cdec = jnp.repeat(jnp.exp(log_gamma * CHUNK), RET_V_DIM)[None, :]
    row_head = jnp.arange(RET_QK_WIDTH) // RET_QK_DIM
    col_head = jnp.arange(RET_V_WIDTH) // RET_V_DIM
    same_head = row_head[:, None] == col_head[None, :]
    bmask = same_head.astype(F32)
    gavg = ((col_head[:, None] == col_head[None, :]).astype(F32) / RET_V_DIM).astype(MXU_DTYPE)
    return decay, zeta_t, xi_t, cdec, bmask, gavg


def _rotary_tables(positions):
    inv_freq = ROPE_BASE ** (-jnp.arange(0, RET_QK_DIM, 2, dtype=F32) / RET_QK_DIM)
    ang = positions.astype(F32)[..., None] * inv_freq
    cos = jnp.cos(ang)
    sin = jnp.sin(ang)
    cos_t = jnp.tile(jnp.concatenate([cos, cos], axis=-1), (1, 1, RET_HEADS))
    sin_t = jnp.tile(jnp.concatenate([-sin, sin], axis=-1), (1, 1, RET_HEADS))
    return cos_t, sin_t


def _moe_rows(n_tok, n_tiles, blk):
    worst = n_tok * TOP_K + n_tiles * N_EXPERTS * (RUN_ALIGN - 1)
    return (-(-worst // blk) + N_EXPERTS) * blk


def _routing_tables(sel, counts, before, n_rows, blk, pieces):
    n_blocks = n_rows // blk
    sizes = counts.astype(jnp.int32)
    pad_sizes = (sizes + blk - 1) // blk * blk
    pad_end = jnp.cumsum(pad_sizes)
    pad_start = pad_end - pad_sizes
    n_used = (pad_end[-1] // blk).astype(jnp.int32)
    block_start = jnp.arange(n_blocks, dtype=jnp.int32) * blk
    block_start = jnp.minimum(block_start, (n_used - 1) * blk)
    block_e = jnp.sum((pad_end[None, :] <= block_start[:, None]).astype(jnp.int32), axis=-1)
    block_e = jnp.minimum(block_e, N_EXPERTS - 1)

    before = before.astype(jnp.int32)
    after = jnp.concatenate([before[1:], sizes[None, :]], axis=0)
    run_start = pad_start[None, :] + before
    run_units = (after - before) // RUN_ALIGN
    run_off = (jnp.cumsum(run_units, axis=-1) - run_units) * RUN_ALIGN
    n_tiles = before.shape[0]
    slot_id = jnp.arange(N_EXPERTS, dtype=jnp.int32)
    cols, counts_c = [], []
    for bit in pieces:
        has = jnp.bitwise_and(run_units, bit) != 0
        done = jnp.bitwise_and(run_units, -2 * bit) * RUN_ALIGN
        pos = jnp.cumsum(has.astype(jnp.int32), axis=-1) - 1
        place = jnp.logical_and(has[:, None, :], pos[:, None, :] == slot_id[None, :, None])
        cols.append(jnp.sum(jnp.where(place, (run_start + done)[:, None, :], 0), axis=-1))
        cols.append(jnp.sum(jnp.where(place, (run_off + done)[:, None, :], 0), axis=-1))
        counts_c.append(jnp.sum(has.astype(jnp.int32), axis=-1, keepdims=True))
    width = _piece_table_width(len(pieces))
    used = len(pieces) * (2 * N_EXPERTS + 1)
    tab = jnp.concatenate(cols + counts_c + [jnp.zeros((n_tiles, width - used), jnp.int32)], axis=-1)

    n_tok = sel.shape[0]
    idx = sel[:, SEL_IDX:SEL_IDX + TOP_K].astype(jnp.int32)
    off_tok = jnp.repeat(run_off, n_tok // n_tiles, axis=0)
    base = jnp.sum(jnp.where(idx[:, :, None] == slot_id, off_tok[:, None, :], 0), axis=-1)
    rows = base.astype(F32) + sel[:, SEL_RANK:SEL_RANK + TOP_K]
    rows8 = jnp.pad(rows.T, ((0, ROWS_PAD - TOP_K), (0, 0)))
    return (pad_end.astype(jnp.int32), pad_start.astype(jnp.int32), block_e, n_used.reshape(1),
            tab.astype(jnp.int32).reshape(n_tiles, 1, width), rows8)


def kernel(x, c, positions, w_ada, b_ada, w_in, gmlp_ln_g, gmlp_ln_b, w_spatial, b_spatial, conv_w,
           w_branch, w_gate_up, b_gate, w_out, ln1_g, ln1_b, w_router, b_router, w_gu, b_gu, w_down,
           b_down, ln2_g, ln2_b):
    bsz, seq, d = x.shape
    n_tok = bsz * seq
    depth = w_ada.shape[0]
    blk = min(MOE_BLOCK, n_tok * TOP_K // N_EXPERTS)
    ts = min(SEQ_TILE, seq)

    ada = _ada_call(c, w_ada, b_ada)
    ada = jnp.transpose(ada, (0, 2, 1, 3))
    cos_t, sin_t = _rotary_tables(positions)
    tables = _retention_tables()

    mx = MXU_DTYPE
    w_in_m = w_in.astype(mx)
    w_branch_m = w_branch.astype(mx)
    w_gate_m = w_gate_up.astype(mx)
    w_out_m = w_out.astype(mx)
    b_gu4 = b_gu[:, :, None, :]
    b_down4 = b_down[:, :, None, :]
    n_exp = w_router.shape[-1]
    w_router_m = jnp.pad(w_router, ((0, 0), (0, 0), (0, LANES - n_exp))).astype(mx)
    b_router_p = jnp.pad(b_router, ((0, 0), (0, LANES - n_exp)), constant_values=NEG_BIG)

    for l in range(depth):
        bsp_t = jnp.repeat(b_spatial[l].T, GMLP_WIDTH // GMLP_GROUPS, axis=1)
        lw = (
            (w_in_m, l),
            (gmlp_ln_g[l].reshape(1, -1), None), (gmlp_ln_b[l].reshape(1, -1), None),
            (w_spatial[l].reshape(GMLP_GROUPS * CHUNK, CHUNK), None), (bsp_t, None),
            (jnp.pad(conv_w[l], ((0, 8 - conv_w.shape[1]), (0, 0))), None),
            (w_branch_m, l), (w_gate_m, l), (b_gate, l), (w_out_m, l),
            (ln1_g[l].reshape(1, -1), None), (ln1_b[l].reshape(1, -1), None),
            (w_router_m, l), (b_router_p[l].reshape(1, -1), None),
        )
        x1, h2, sel, cnt, before = _mixer_call(x, ada[l], cos_t, sin_t, lw, tables)
        n_rows = _moe_rows(n_tok, n_tok // ts, blk)
        pad_end, pad_start, block_e, n_used, tab3, rows8 = _routing_tables(
            sel, cnt[0, :n_exp], before[:, 0, :n_exp], n_rows, blk, _run_pieces(ts))
        xs = _dispatch_call(pad_end, pad_start, tab3, rows8, h2, n_rows, blk)
        ys = _expert_call(block_e, n_used, xs, w_gu, b_gu4, w_down, b_down4, blk, l)
        x = _combine_call(tab3, ys, x1, sel, rows8, ada[l], ln2_g[l], ln2_b[l], seq).reshape(bsz, seq, d)
    return x
```

```python
import functools

import jax
import jax.numpy as jnp
from jax import lax
from jax.experimental import pallas as pl
from jax.experimental.pallas import tpu as pltpu

F32 = jnp.float32
MXU_DTYPE = jnp.bfloat16

DEPTH = 4
RET_HEADS = 4
RET_QK_DIM = 32
RET_V_DIM = 64
RET_QK_WIDTH = RET_HEADS * RET_QK_DIM
RET_V_WIDTH = RET_HEADS * RET_V_DIM
CHUNK = 128
ROPE_BASE = 10000.0
GMLP_GROUPS = 4
GMLP_WIDTH = 256
CONV_WIDTH = 256
GATE_RANK = 128
N_BRANCH = 3
N_EXPERTS = 32
TOP_K = 4
D_FF = 256
SWIGLU_LIMIT = 7.0
SWIGLU_ALPHA = 1.702
DEEPNORM_ALPHA = (2.0 * DEPTH) ** 0.25
LN_EPS = 1e-5

_O_QK = 0
_O_VG = 2 * RET_QK_WIDTH
_O_GMLP = _O_VG + 2 * RET_V_WIDTH
_O_CONV = _O_GMLP + 2 * GMLP_WIDTH
_O_CODE = _O_CONV + 3 * CONV_WIDTH
_IN_WIDTH = _O_CODE + N_BRANCH * GATE_RANK

LANES = 128
NEG_BIG = -1e30

SEQ_TILE = 256
MIXER_STEP = 512
MOE_BLOCK = 1024
EXPERT_PART = 256
COMBINE_PART = 128
RUN_ALIGN = 16
VMEM_LIMIT = 56 * 1024 * 1024


def _dot(a, b):
    return jnp.dot(a.astype(MXU_DTYPE), b.astype(MXU_DTYPE), preferred_element_type=F32)


def _dot_nt(a, b):
    return lax.dot_general(a.astype(MXU_DTYPE), b.astype(MXU_DTYPE),
                           (((1,), (1,)), ((), ())), preferred_element_type=F32)


def _split_dot(x, w):
    hi = x.astype(MXU_DTYPE)
    lo = (x - hi.astype(F32)).astype(MXU_DTYPE)
    return (jnp.dot(hi, w, preferred_element_type=F32)
            + jnp.dot(lo, w, preferred_element_type=F32))


def _layernorm_rows(x, g, b):
    mu = jnp.mean(x, axis=-1, keepdims=True)
    d = x - mu
    var = jnp.mean(d * d, axis=-1, keepdims=True)
    return d * lax.rsqrt(var + LN_EPS) * g + b


def _ada_kernel(c_ref, w_ref, b_ref, o_ref):
    c_act = jax.nn.silu(c_ref[...])
    o_ref[...] = jnp.dot(c_act, w_ref[...], preferred_element_type=F32,
                         precision=lax.Precision.HIGHEST) + b_ref[...]


def _ada_call(c, w_ada, b_ada):
    depth, d, six_d = w_ada.shape
    bsz = c.shape[0]
    n_col = six_d // d
    return pl.pallas_call(
        _ada_kernel,
        grid=(depth, n_col),
        in_specs=[
            pl.BlockSpec((bsz, d), lambda l, j: (0, 0)),
            pl.BlockSpec((None, d, d), lambda l, j: (l, 0, j)),
            pl.BlockSpec((None, 1, d), lambda l, j: (l, 0, j)),
        ],
        out_specs=pl.BlockSpec((None, None, bsz, d), lambda l, j: (l, j, 0, 0)),
        out_shape=jax.ShapeDtypeStruct((depth, n_col, bsz, d), F32),
        compiler_params=pltpu.CompilerParams(
            dimension_semantics=("arbitrary", "arbitrary"), vmem_limit_bytes=VMEM_LIMIT),
        name="ada_ln",
    )(c, w_ada, b_ada.reshape(depth, 1, six_d))


_MIXER_PHASES = 3


def _mixer_kernel(x_ref, ada_ref, cos_ref, sin_ref, *rest):
    params = rest[:-8]
    x1_ref, h2_ref, sel_ref, cnt_ref, before_ref, state_ref, zc_ref, cntacc_ref = rest[-8:]
    b = pl.program_id(0)
    s = pl.program_id(1)

    @pl.when(s == 0)
    def _():
        state_ref[...] = jnp.zeros_like(state_ref)
        zc_ref[...] = jnp.zeros_like(zc_ref)

    @pl.when(jnp.logical_and(b == 0, s == 0))
    def _():
        cntacc_ref[...] = jnp.zeros_like(cntacc_ref)

    n_sub = before_ref.shape[0]
    ts = x_ref.shape[0] // n_sub
    tiles = []
    for u in range(n_sub):
        rows = pl.ds(u * ts, ts)
        tiles.append(_mixer_tile(x_ref.at[rows], ada_ref, cos_ref.at[rows], sin_ref.at[rows], *params,
                                 x1_ref.at[rows], h2_ref.at[rows], sel_ref.at[rows], cnt_ref,
                                 before_ref.at[u], state_ref, zc_ref, cntacc_ref))
    for _ in range(_MIXER_PHASES):
        for tile in tiles:
            next(tile, None)


def _mixer_tile(x_ref, ada_ref, cos_ref, sin_ref, w_in_ref, lng_ref, lnb_ref, wsp_ref, bsp_ref,
                convw_ref, wbr_ref, wgt_ref, bgt_ref, wout_ref, ln1g_ref, ln1b_ref, wr_ref, br_ref,
                decay_ref, zeta_ref, xi_ref, cdec_ref, bmask_ref, gavg_ref,
                x1_ref, h2_ref, sel_ref, cnt_ref, before_ref,
                state_ref, zc_ref, cntacc_ref):
    ts = x_ref.shape[0]
    x = x_ref[...]
    ada = ada_ref[...]
    sh1, sc1, gt1, sh2, sc2 = ada[0:1], ada[1:2], ada[2:3], ada[3:4], ada[4:5]
    h = (x * (1.0 + sc1) + sh1).astype(MXU_DTYPE)

    def proj(lo, hi):
        return jnp.dot(h, w_in_ref[:, lo:hi], preferred_element_type=F32)

    lane_qk = lax.broadcasted_iota(jnp.int32, (1, RET_QK_WIDTH), 1)
    lane_v = lax.broadcasted_iota(jnp.int32, (1, RET_V_WIDTH), 1)
    qk_masks = [lane_qk // RET_QK_DIM == hd for hd in range(RET_HEADS)]
    v_masks = [lane_v // RET_V_DIM == hd for hd in range(RET_HEADS)]

    qk = proj(_O_QK, _O_VG)
    cos = cos_ref[...]
    sin = sin_ref[...]
    first_half = (lane_qk % RET_QK_DIM) < (RET_QK_DIM // 2)

    def rotary(t):
        swapped = jnp.where(first_half, pltpu.roll(t, RET_QK_WIDTH - RET_QK_DIM // 2, 1),
                            pltpu.roll(t, RET_QK_DIM // 2, 1))
        return t * cos + swapped * sin

    q = rotary(qk[:, :RET_QK_WIDTH])
    k = rotary(qk[:, RET_QK_WIDTH:]) * (RET_QK_DIM ** -0.5)
    vg = proj(_O_VG, _O_GMLP)
    v = vg[:, :RET_V_WIDTH]
    g = vg[:, RET_V_WIDTH:]

    decay = decay_ref[...]
    zeta = zeta_ref[...]
    xi = xi_ref[...]
    cdec = cdec_ref[...]
    bmask = bmask_ref[...]
    gavg = gavg_ref[...]

    o_chunks = []
    state = state_ref[...]
    for c in range(ts // CHUNK):
        rows = slice(c * CHUNK, (c + 1) * CHUNK)
        q_c, k_c, v_c = q[rows], k[rows], v[rows]
        v_m = v_c.astype(MXU_DTYPE)
        q_all = jnp.concatenate([jnp.where(m, q_c, 0.0) for m in qk_masks], axis=0)
        scores = _dot_nt(q_all, k_c) * decay
        o_all = _dot(scores, v_m)
        o = _dot(q_c, state) * xi
        for hd in range(RET_HEADS):
            o = o + jnp.where(v_masks[hd], o_all[hd * CHUNK:(hd + 1) * CHUNK], 0.0)
        kv = _dot((k_c * zeta).T, v_m) * bmask
        state = state * cdec + kv
        o_chunks.append(o)
    state_ref[...] = state
    o = jnp.concatenate(o_chunks, axis=0) if len(o_chunks) > 1 else o_chunks[0]
    mu = _split_dot(o, gavg)
    d = o - mu
    var = _split_dot(d * d, gavg)
    r_br = jax.nn.silu(g) * (d * lax.rsqrt(var + LN_EPS))

    guv = proj(_O_GMLP, _O_CONV)
    u = jax.nn.gelu(guv[:, :GMLP_WIDTH])
    vv = _layernorm_rows(jax.nn.gelu(guv[:, GMLP_WIDTH:]), lng_ref[...], lnb_ref[...]).astype(MXU_DTYPE)
    rr = lax.broadcasted_iota(jnp.int32, (GMLP_GROUPS * CHUNK, CHUNK), 0) % CHUNK
    cc_ = lax.broadcasted_iota(jnp.int32, (GMLP_GROUPS * CHUNK, CHUNK), 1)
    w_sp = jnp.where(cc_ <= rr, wsp_ref[...], 0.0).astype(MXU_DTYPE)
    bsp = bsp_ref[...]
    z_chunks = []
    for c in range(ts // CHUNK):
        z_all = jnp.dot(w_sp, vv[c * CHUNK:(c + 1) * CHUNK], preferred_element_type=F32)
        z = bsp
        for gi in range(GMLP_GROUPS):
            z = z + jnp.where(v_masks[gi], z_all[gi * CHUNK:(gi + 1) * CHUNK], 0.0)
        z_chunks.append(z)
    z = jnp.concatenate(z_chunks, axis=0) if len(z_chunks) > 1 else z_chunks[0]
    s_br = u * z

    cbcx = proj(_O_CONV, _O_CODE)
    gate_b = cbcx[:, :CONV_WIDTH]
    zc = cbcx[:, CONV_WIDTH:2 * CONV_WIDTH] * cbcx[:, 2 * CONV_WIDTH:]
    carry = zc_ref[...]
    prev1 = carry[7:8]
    prev2 = carry[6:7]
    row = lax.broadcasted_iota(jnp.int32, (ts, 1), 0)
    z1 = jnp.where(row == 0, prev1, pltpu.roll(zc, 1, 0))
    z2 = jnp.where(row == 0, prev2, jnp.where(row == 1, prev1, pltpu.roll(zc, 2, 0)))
    zc_ref[...] = zc[ts - 8:ts]
    cw = convw_ref[...]
    k_br = gate_b * (cw[0:1] * z2 + cw[1:2] * z1 + cw[2:3] * zc)

    code = proj(_O_CODE, _IN_WIDTH)
    yield
    merged = None
    for i, br in enumerate((r_br, s_br, k_br)):
        y = jnp.dot(br.astype(MXU_DTYPE), wbr_ref[i], preferred_element_type=F32)
        gl = jnp.dot(code[:, i * GATE_RANK:(i + 1) * GATE_RANK].astype(MXU_DTYPE), wgt_ref[i],
                     preferred_element_type=F32) + bgt_ref[i:i + 1]
        t = jax.nn.sigmoid(gl) * y
        merged = t if merged is None else merged + t
    mix = jnp.dot(merged.astype(MXU_DTYPE), wout_ref[...], preferred_element_type=F32)
    x1 = _layernorm_rows(DEEPNORM_ALPHA * x + (1.0 + gt1) * mix, ln1g_ref[...], ln1b_ref[...])
    x1_ref[...] = x1
    h2 = (x1 * (1.0 + sc2) + sh2).astype(MXU_DTYPE)
    h2_ref[...] = h2

    logits = jnp.dot(h2, wr_ref[...], preferred_element_type=F32) + br_ref[...]
    yield
    lane = lax.broadcasted_iota(jnp.int32, (1, LANES), 1).astype(F32)
    top_v, top_i = [], []
    work = logits
    for _ in range(TOP_K):
        m = jnp.max(work, axis=-1, keepdims=True)
        i_sel = jnp.argmax(work, axis=-1, keepdims=True).astype(F32)
        top_v.append(m)
        top_i.append(i_sel)
        work = jnp.where(lane == i_sel, -jnp.inf, work)
    exps = [jnp.exp(tv - top_v[0]) for tv in top_v]
    denom = exps[0] + exps[1] + exps[2] + exps[3]
    onehots = [lane == ti for ti in top_i]
    member = jnp.zeros((ts, LANES), F32)
    for oh in onehots:
        member = member + jnp.where(oh, 1.0, 0.0)
    tr = lax.broadcasted_iota(jnp.int32, (ts, ts), 0)
    tc = lax.broadcasted_iota(jnp.int32, (ts, ts), 1)
    before = jnp.where(tc < tr, 1.0, 0.0).astype(MXU_DTYPE)
    cnt = cntacc_ref[...]
    local_rank = jnp.dot(before, member.astype(MXU_DTYPE), preferred_element_type=F32)
    sel = jnp.zeros((ts, LANES), F32)
    for kk in range(TOP_K):
        lrank_k = jnp.sum(jnp.where(onehots[kk], local_rank, 0.0), axis=-1, keepdims=True)
        sel = sel + jnp.where(lane == float(kk), top_i[kk], 0.0)
        sel = sel + jnp.where(lane == float(TOP_K + kk), exps[kk] / denom, 0.0)
        sel = sel + jnp.where(lane == float(2 * TOP_K + kk), lrank_k, 0.0)
    sel_ref[...] = sel
    before_ref[...] = cnt
    tile_cnt = jnp.sum(member, axis=0, keepdims=True)
    cnt = cnt + jnp.floor((tile_cnt + (RUN_ALIGN - 1.0)) * (1.0 / RUN_ALIGN)) * RUN_ALIGN
    cntacc_ref[...] = cnt
    cnt_ref[...] = cnt


def _const_spec(shape):
    return pl.BlockSpec(shape, lambda b, s: (0,) * len(shape))


def _param_spec(a, layer):
    if layer is None:
        return _const_spec(a.shape)
    return pl.BlockSpec((None,) + a.shape[1:], lambda b, s: (layer,) + (0,) * (a.ndim - 1))


def _mixer_call(x, ada_l, cos_t, sin_t, lw, tables):
    bsz, seq, d = x.shape
    ts = min(SEQ_TILE, seq)
    step = min(MIXER_STEP, seq)
    n_tok = bsz * seq
    row3 = lambda b, s: (b, s, 0)
    in_specs = [
        pl.BlockSpec((None, step, d), row3),
        pl.BlockSpec((None, 6, d), lambda b, s: (b, 0, 0)),
        pl.BlockSpec((None, step, RET_QK_WIDTH), row3),
        pl.BlockSpec((None, step, RET_QK_WIDTH), row3),
    ] + [_param_spec(a, lay) for a, lay in lw] + [_const_spec(a.shape) for a in tables]
    tok_row = lambda b, s: (b * (seq // step) + s, 0)
    out_specs = [
        pl.BlockSpec((step, d), tok_row),
        pl.BlockSpec((step, d), tok_row),
        pl.BlockSpec((step, LANES), tok_row),
        pl.BlockSpec((8, LANES), lambda b, s: (0, 0)),
        pl.BlockSpec((step // ts, 8, LANES), lambda b, s: (b * (seq // step) + s, 0, 0)),
    ]
    out_shape = [
        jax.ShapeDtypeStruct((n_tok, d), F32),
        jax.ShapeDtypeStruct((n_tok, d), MXU_DTYPE),
        jax.ShapeDtypeStruct((n_tok, LANES), F32),
        jax.ShapeDtypeStruct((8, LANES), F32),
        jax.ShapeDtypeStruct((n_tok // ts, 8, LANES), F32),
    ]
    return pl.pallas_call(
        _mixer_kernel,
        grid=(bsz, seq // step),
        in_specs=in_specs,
        out_specs=out_specs,
        out_shape=out_shape,
        scratch_shapes=[
            pltpu.VMEM((RET_QK_WIDTH, RET_V_WIDTH), F32),
            pltpu.VMEM((8, CONV_WIDTH), F32),
            pltpu.VMEM((8, LANES), F32),
        ],
        compiler_params=pltpu.CompilerParams(
            dimension_semantics=("arbitrary", "arbitrary"), vmem_limit_bytes=VMEM_LIMIT),
        name="mixer",
    )(x, ada_l, cos_t, sin_t, *[a for a, _ in lw], *tables)


def _run_pieces(max_rows):
    units = max_rows // RUN_ALIGN
    pieces = []
    bit = 1
    while bit <= units:
        pieces.append(bit)
        bit *= 2
    return tuple(reversed(pieces))


def _piece_table_width(n_classes):
    return -(-(n_classes * 2 * N_EXPERTS + n_classes) // LANES) * LANES


def _for_each_run_piece(tab_ref, pieces, make_copy, fn):
    for c, bit in enumerate(pieces):
        count = tab_ref[0, 0, len(pieces) * 2 * N_EXPERTS + c]

        def body(j, carry, c=c, bit=bit):
            hbm_row = tab_ref[0, 0, c * 2 * N_EXPERTS + j]
            buf_row = tab_ref[0, 0, (c * 2 + 1) * N_EXPERTS + j]
            fn(make_copy(pl.multiple_of(hbm_row, RUN_ALIGN), pl.multiple_of(buf_row, RUN_ALIGN),
                         bit * RUN_ALIGN))
            return carry

        lax.fori_loop(0, count, body, 0)


SEL_IDX, SEL_WEIGHT, SEL_RANK = 0, TOP_K, 2 * TOP_K
ROWS_PAD = 8


def _dispatch_kernel(pe_ref, ps_ref, tab_cur_ref, tab_prev_ref, rows_ref, h2_ref, xs_hbm,
                     zbuf, sbuf, sem, zsem):
    i = pl.program_id(0)
    n_steps = pl.num_programs(0)
    ts = h2_ref.shape[0]
    blk = zbuf.shape[0]
    n_exp = pe_ref.shape[0]
    run_rows = sbuf.shape[1]
    slot = i % 2
    pieces = _run_pieces(ts)

    @pl.when(i == 0)
    def _():
        zbuf[...] = jnp.zeros_like(zbuf)

        def tail_copy(e):
            start = pl.multiple_of(jnp.maximum(pe_ref[e] - blk, 0), blk)
            return pltpu.make_async_copy(zbuf, xs_hbm.at[pl.ds(start, blk), :], zsem)

        def zstart(e, carry):
            @pl.when(pe_ref[e] > ps_ref[e])
            def _():
                tail_copy(e).start()
            return carry

        def zwait(e, carry):
            @pl.when(pe_ref[e] > ps_ref[e])
            def _():
                tail_copy(e).wait()
            return carry

        lax.fori_loop(0, n_exp, zstart, 0)
        lax.fori_loop(0, n_exp, zwait, 0)

    rows_t = rows_ref[...]
    buf_row = lax.broadcasted_iota(jnp.int32, (run_rows, ts), 0).astype(F32)
    pick = jnp.zeros((run_rows, ts), F32)
    for kk in range(TOP_K):
        pick = jnp.where(buf_row == rows_t[kk:kk + 1], 1.0, pick)
    sbuf[slot] = jnp.dot(pick.astype(MXU_DTYPE), h2_ref[...].astype(MXU_DTYPE),
                         preferred_element_type=F32).astype(sbuf.dtype)

    def push(dst_slot):
        def make(hbm_row, buf_row_, rows):
            return pltpu.make_async_copy(sbuf.at[dst_slot, pl.ds(buf_row_, rows), :],
                                         xs_hbm.at[pl.ds(hbm_row, rows), :], sem.at[dst_slot])
        return make

    _for_each_run_piece(tab_cur_ref, pieces, push(slot), lambda c: c.start())

    @pl.when(i > 0)
    def _():
        _for_each_run_piece(tab_prev_ref, pieces, push(1 - slot), lambda c: c.wait())

    @pl.when(i == n_steps - 1)
    def _():
        _for_each_run_piece(tab_cur_ref, pieces, push(slot), lambda c: c.wait())


def _dispatch_call(pad_end, pad_start, tab3, rows8, h2, n_rows, blk):
    n_tok, d = h2.shape
    n_steps = tab3.shape[0]
    ts = n_tok // n_steps
    run_rows = ts * TOP_K + N_EXPERTS * RUN_ALIGN
    grid_spec = pltpu.PrefetchScalarGridSpec(
        num_scalar_prefetch=2,
        grid=(n_steps,),
        in_specs=[
            pl.BlockSpec((1, 1, tab3.shape[2]), lambda i, pe, ps: (i, 0, 0), memory_space=pltpu.SMEM),
            pl.BlockSpec((1, 1, tab3.shape[2]), lambda i, pe, ps: (jnp.maximum(i - 1, 0), 0, 0),
                         memory_space=pltpu.SMEM),
            pl.BlockSpec((ROWS_PAD, ts), lambda i, pe, ps: (0, i)),
            pl.BlockSpec((ts, d), lambda i, pe, ps: (i, 0)),
        ],
        out_specs=pl.BlockSpec(memory_space=pl.ANY),
        scratch_shapes=[pltpu.VMEM((blk, d), MXU_DTYPE), pltpu.VMEM((2, run_rows, d), MXU_DTYPE),
                        pltpu.SemaphoreType.DMA((2,)), pltpu.SemaphoreType.DMA(())],
    )
    return pl.pallas_call(
        _dispatch_kernel,
        grid_spec=grid_spec,
        out_shape=jax.ShapeDtypeStruct((n_rows, d), MXU_DTYPE),
        compiler_params=pltpu.CompilerParams(
            dimension_semantics=("arbitrary",), vmem_limit_bytes=VMEM_LIMIT),
        name="dispatch",
    )(pad_end, pad_start, tab3, tab3, rows8, h2)


def _expert_up(xs, wgu, bgu_ref):
    return jnp.dot(xs, wgu, preferred_element_type=F32) + bgu_ref[...]


def _expert_down(gu, wdn, bdn_ref):
    gate = jnp.minimum(gu[:, :D_FF], SWIGLU_LIMIT)
    up = jnp.clip(gu[:, D_FF:], -SWIGLU_LIMIT, SWIGLU_LIMIT)
    act = (up + 1.0) * (gate * jax.nn.sigmoid(SWIGLU_ALPHA * gate))
    return jnp.dot(act.astype(MXU_DTYPE), wdn, preferred_element_type=F32) + bdn_ref[...]


def _expert_kernel(be_ref, nu_ref, xs_ref, wgu_ref, bgu_ref, wdn_ref, bdn_ref, ys_ref):
    i = pl.program_id(0)
    n_used = nu_ref[0]

    @pl.when(i < n_used)
    def _():
        part = min(EXPERT_PART, xs_ref.shape[0])
        n_parts = xs_ref.shape[0] // part
        rows = [slice(p * part, (p + 1) * part) for p in range(n_parts)]
        wgu = wgu_ref[...].astype(MXU_DTYPE)
        wdn = wdn_ref[...].astype(MXU_DTYPE)
        gu = _expert_up(xs_ref[rows[0], :], wgu, bgu_ref)
        for p in range(n_parts):
            gu_next = _expert_up(xs_ref[rows[p + 1], :], wgu, bgu_ref) if p + 1 < n_parts else None
            ys_ref[rows[p], :] = _expert_down(gu, wdn, bdn_ref).astype(ys_ref.dtype)
            gu = gu_next

    @pl.when(i >= n_used)
    def _():
        ys_ref[...] = jnp.zeros_like(ys_ref)


def _expert_call(block_e, n_used, xs, w_gu, b_gu, w_down, b_down, blk, layer):
    n_rows, d = xs.shape
    n_blocks = n_rows // blk
    two_f = w_gu.shape[3]
    last = lambda nu: jnp.maximum(nu[0] - 1, 0)
    of_block = lambda i, be, nu: (layer, be[i], 0, 0)
    grid_spec = pltpu.PrefetchScalarGridSpec(
        num_scalar_prefetch=2,
        grid=(n_blocks,),
        in_specs=[
            pl.BlockSpec((blk, d), lambda i, be, nu: (jnp.minimum(i, last(nu)), 0)),
            pl.BlockSpec((None, None, d, two_f), of_block),
            pl.BlockSpec((None, None, 1, two_f), of_block),
            pl.BlockSpec((None, None, two_f // 2, d), of_block),
            pl.BlockSpec((None, None, 1, d), of_block),
        ],
        out_specs=pl.BlockSpec((blk, d), lambda i, be, nu: (i, 0)),
    )
    return pl.pallas_call(
        _expert_kernel,
        grid_spec=grid_spec,
        out_shape=jax.ShapeDtypeStruct((n_rows, d), MXU_DTYPE),
        compiler_params=pltpu.CompilerParams(
            dimension_semantics=("arbitrary",), vmem_limit_bytes=VMEM_LIMIT),
        name="experts",
    )(block_e, n_used, xs, w_gu, b_gu, w_down, b_down)


def _combine_kernel(tab_cur_ref, tab_nxt_ref, ys_hbm, x1_ref, sel_ref, rows_ref, ada_ref, lng_ref, lnb_ref,
                    o_ref, rbuf, sem):
    i = pl.program_id(0)
    n_steps = pl.num_programs(0)
    ts = x1_ref.shape[0]
    slot = i % 2
    pieces = _run_pieces(ts)

    def fetch(dst_slot):
        def make(hbm_row, buf_row, rows):
            return pltpu.make_async_copy(ys_hbm.at[pl.ds(hbm_row, rows), :],
                                         rbuf.at[dst_slot, pl.ds(buf_row, rows), :], sem.at[dst_slot])
        return make

    @pl.when(i == 0)
    def _():
        rbuf[...] = jnp.zeros_like(rbuf)
        _for_each_run_piece(tab_cur_ref, pieces, fetch(0), lambda c: c.start())

    @pl.when(i + 1 < n_steps)
    def _():
        _for_each_run_piece(tab_nxt_ref, pieces, fetch(1 - slot), lambda c: c.start())

    _for_each_run_piece(tab_cur_ref, pieces, fetch(slot), lambda c: c.wait())

    sel = sel_ref[...]
    rows_t = rows_ref[...]
    rows_c = jnp.concatenate([rows_t, jnp.zeros((LANES - ROWS_PAD, ts), F32)], axis=0).T
    col = lax.broadcasted_iota(jnp.int32, (1, rbuf.shape[1]), 1).astype(F32)
    gt2 = ada_ref[...][5:6]
    part = min(COMBINE_PART, ts)

    def weight_pieces(tok):
        weights = jnp.zeros((part, rbuf.shape[1]), F32)
        for kk in range(TOP_K):
            weights = jnp.where(col == rows_c[tok, kk:kk + 1],
                                sel[tok, SEL_WEIGHT + kk:SEL_WEIGHT + kk + 1], weights)
        hi = weights.astype(MXU_DTYPE)
        lo = (weights - hi.astype(F32)).astype(MXU_DTYPE)
        return jnp.concatenate([hi, lo], axis=0)

    toks = [slice(p * part, (p + 1) * part) for p in range(ts // part)]
    pieces_w = weight_pieces(toks[0])
    for p, tok in enumerate(toks):
        nxt = weight_pieces(toks[p + 1]) if p + 1 < len(toks) else None
        both = jnp.dot(pieces_w, rbuf[slot], preferred_element_type=F32)
        ffn = both[:part] + both[part:]
        o_ref[tok, :] = _layernorm_rows(DEEPNORM_ALPHA * x1_ref[tok, :] + (1.0 + gt2) * ffn,
                                        lng_ref[...], lnb_ref[...])
        pieces_w = nxt


def _combine_call(tab3, ys, x1, sel, rows8, ada_l, ln_g, ln_b, seq):
    n_tok, d = x1.shape
    n_steps = tab3.shape[0]
    ts = n_tok // n_steps
    tiles_per_seq = seq // ts
    run_rows = ts * TOP_K + N_EXPERTS * RUN_ALIGN
    return pl.pallas_call(
        _combine_kernel,
        grid=(n_steps,),
        in_specs=[
            pl.BlockSpec((1, 1, tab3.shape[2]), lambda i: (i, 0, 0), memory_space=pltpu.SMEM),
            pl.BlockSpec((1, 1, tab3.shape[2]), lambda i: (jnp.minimum(i + 1, n_steps - 1), 0, 0),
                         memory_space=pltpu.SMEM),
            pl.BlockSpec(memory_space=pl.ANY),
            pl.BlockSpec((ts, d), lambda i: (i, 0)),
            pl.BlockSpec((ts, LANES), lambda i: (i, 0)),
            pl.BlockSpec((ROWS_PAD, ts), lambda i: (0, i)),
            pl.BlockSpec((None, 6, d), lambda i: (i // tiles_per_seq, 0, 0)),
            pl.BlockSpec((1, d), lambda i: (0, 0)),
            pl.BlockSpec((1, d), lambda i: (0, 0)),
        ],
        out_specs=pl.BlockSpec((ts, d), lambda i: (i, 0)),
        out_shape=jax.ShapeDtypeStruct((n_tok, d), F32),
        scratch_shapes=[pltpu.VMEM((2, run_rows, d), MXU_DTYPE), pltpu.SemaphoreType.DMA((2,))],
        compiler_params=pltpu.CompilerParams(
            dimension_semantics=("arbitrary",), vmem_limit_bytes=VMEM_LIMIT),
        name="combine",
    )(tab3, tab3, ys, x1, sel, rows8, ada_l, ln_g.reshape(1, d), ln_b.reshape(1, d))


def _retention_tables():
    log_gamma = jnp.log(1.0 - 2.0 ** (-5.0 - jnp.arange(RET_HEADS, dtype=F32)))
    idx = jnp.arange(CHUNK, dtype=F32)
    diff = idx[:, None] - idx[None, :]
    decay = jnp.where(diff[None] >= 0,
                      jnp.exp(log_gamma[:, None, None] * jnp.maximum(diff, 0.0)[None]), 0.0)
    decay = decay.reshape(RET_HEADS * CHUNK, CHUNK)
    zeta = jnp.exp(log_gamma[:, None] * (CHUNK - 1.0 - idx)[None])
    zeta_t = jnp.repeat(zeta.T, RET_QK_DIM, axis=1)
    xi = jnp.exp(log_gamma[:, None] * (idx + 1.0)[None]).T
    xi_t = jnp.repeat(xi, RET_V_DIM, axis=1)
    cdec = jnp.repeat(jnp.exp(log_gamma * CHUNK), RET_V_DIM)[None, :]
    row_head = jnp.arange(RET_QK_WIDTH) // RET_QK_DIM
    col_head = jnp.arange(RET_V_WIDTH) // RET_V_DIM
    same_head = row_head[:, None] == col_head[None, :]
    bmask = same_head.astype(F32)
    gavg = ((col_head[:, None] == col_head[None, :]).astype(F32) / RET_V_DIM).astype(MXU_DTYPE)
    return decay, zeta_t, xi_t, cdec, bmask, gavg


def _rotary_tables(positions):
    inv_freq = ROPE_BASE ** (-jnp.arange(0, RET_QK_DIM, 2, dtype=F32) / RET_QK_DIM)
    ang = positions.astype(F32)[..., None] * inv_freq
    cos = jnp.cos(ang)
    sin = jnp.sin(ang)
    cos_t = jnp.tile(jnp.concatenate([cos, cos], axis=-1), (1, 1, RET_HEADS))
    sin_t = jnp.tile(jnp.concatenate([-sin, sin], axis=-1), (1, 1, RET_HEADS))
    return cos_t, sin_t


def _moe_rows(n_tok, n_tiles, blk):
    worst = n_tok * TOP_K + n_tiles * N_EXPERTS * (RUN_ALIGN - 1)
    return (-(-worst // blk) + N_EXPERTS) * blk


def _routing_tables(sel, counts, before, n_rows, blk, pieces):
    n_blocks = n_rows // blk
    sizes = counts.astype(jnp.int32)
    pad_sizes = (sizes + blk - 1) // blk * blk
    pad_end = jnp.cumsum(pad_sizes)
    pad_start = pad_end - pad_sizes
    n_used = (pad_end[-1] // blk).astype(jnp.int32)
    block_start = jnp.arange(n_blocks, dtype=jnp.int32) * blk
    block_start = jnp.minimum(block_start, (n_used - 1) * blk)
    block_e = jnp.sum((pad_end[None, :] <= block_start[:, None]).astype(jnp.int32), axis=-1)
    block_e = jnp.minimum(block_e, N_EXPERTS - 1)

    before = before.astype(jnp.int32)
    after = jnp.concatenate([before[1:], sizes[None, :]], axis=0)
    run_start = pad_start[None, :] + before
    run_units = (after - before) // RUN_ALIGN
    run_off = (jnp.cumsum(run_units, axis=-1) - run_units) * RUN_ALIGN
    n_tiles = before.shape[0]
    slot_id = jnp.arange(N_EXPERTS, dtype=jnp.int32)
    cols, counts_c = [], []
    for bit in pieces:
        has = jnp.bitwise_and(run_units, bit) != 0
        done = jnp.bitwise_and(run_units, -2 * bit) * RUN_ALIGN
        pos = jnp.cumsum(has.astype(jnp.int32), axis=-1) - 1
        place = jnp.logical_and(has[:, None, :], pos[:, None, :] == slot_id[None, :, None])
        cols.append(jnp.sum(jnp.where(place, (run_start + done)[:, None, :], 0), axis=-1))
        cols.append(jnp.sum(jnp.where(place, (run_off + done)[:, None, :], 0), axis=-1))
        counts_c.append(jnp.sum(has.astype(jnp.int32), axis=-1, keepdims=True))
    width = _piece_table_width(len(pieces))
    used = len(pieces) * (2 * N_EXPERTS + 1)
    tab = jnp.concatenate(cols + counts_c + [jnp.zeros((n_tiles, width - used), jnp.int32)], axis=-1)

    n_tok = sel.shape[0]
    idx = sel[:, SEL_IDX:SEL_IDX + TOP_K].astype(jnp.int32)
    off_tok = jnp.repeat(run_off, n_tok // n_tiles, axis=0)
    base = jnp.sum(jnp.where(idx[:, :, None] == slot_id, off_tok[:, None, :], 0), axis=-1)
    rows = base.astype(F32) + sel[:, SEL_RANK:SEL_RANK + TOP_K]
    rows8 = jnp.pad(rows.T, ((0, ROWS_PAD - TOP_K), (0, 0)))
    return (pad_end.astype(jnp.int32), pad_start.astype(jnp.int32), block_e, n_used.reshape(1),
            tab.astype(jnp.int32).reshape(n_tiles, 1, width), rows8)


def kernel(x, c, positions, w_ada, b_ada, w_in, gmlp_ln_g, gmlp_ln_b, w_spatial, b_spatial, conv_w,
           w_branch, w_gate_up, b_gate, w_out, ln1_g, ln1_b, w_router, b_router, w_gu, b_gu, w_down,
           b_down, ln2_g, ln2_b):
    bsz, seq, d = x.shape
    n_tok = bsz * seq
    depth = w_ada.shape[0]
    blk = min(MOE_BLOCK, n_tok * TOP_K // N_EXPERTS)
    ts = min(SEQ_TILE, seq)

    ada = _ada_call(c, w_ada, b_ada)
    ada = jnp.transpose(ada, (0, 2, 1, 3))
    cos_t, sin_t = _rotary_tables(positions)
    tables = _retention_tables()

    mx = MXU_DTYPE
    w_in_m = w_in.astype(mx)
    w_branch_m = w_branch.astype(mx)
    w_gate_m = w_gate_up.astype(mx)
    w_out_m = w_out.astype(mx)
    b_gu4 = b_gu[:, :, None, :]
    b_down4 = b_down[:, :, None, :]
    n_exp = w_router.shape[-1]
    w_router_m = jnp.pad(w_router, ((0, 0), (0, 0), (0, LANES - n_exp))).astype(mx)
    b_router_p = jnp.pad(b_router, ((0, 0), (0, LANES - n_exp)), constant_values=NEG_BIG)

    for l in range(depth):
        bsp_t = jnp.repeat(b_spatial[l].T, GMLP_WIDTH // GMLP_GROUPS, axis=1)
        lw = (
            (w_in_m, l),
            (gmlp_ln_g[l].reshape(1, -1), None), (gmlp_ln_b[l].reshape(1, -1), None),
            (w_spatial[l].reshape(GMLP_GROUPS * CHUNK, CHUNK), None), (bsp_t, None),
            (jnp.pad(conv_w[l], ((0, 8 - conv_w.shape[1]), (0, 0))), None),
            (w_branch_m, l), (w_gate_m, l), (b_gate, l), (w_out_m, l),
            (ln1_g[l].reshape(1, -1), None), (ln1_b[l].reshape(1, -1), None),
            (w_router_m, l), (b_router_p[l].reshape(1, -1), None),
        )
        x1, h2, sel, cnt, before = _mixer_call(x, ada[l], cos_t, sin_t, lw, tables)
        n_rows = _moe_rows(n_tok, n_tok // ts, blk)
        pad_end, pad_start, block_e, n_used, tab3, rows8 = _routing_tables(
            sel, cnt[0, :n_exp], before[:, 0, :n_exp], n_rows, blk, _run_pieces(ts))
        xs = _dispatch_call(pad_end, pad_start, tab3, rows8, h2, n_rows, blk)
        ys = _expert_call(block_e, n_used, xs, w_gu, b_gu4, w_down, b_down4, blk, l)
        x = _combine_call(tab3, ys, x1, sel, rows8, ada[l], ln2_g[l], ln2_b[l], seq).reshape(bsz, seq, d)
    return x
```

```python
import functools

import jax
import jax.numpy as jnp
from jax import lax
from jax.experimental import pallas as pl
from jax.experimental.pallas import tpu as pltpu

F32 = jnp.float32
MXU_DTYPE = jnp.bfloat16

DEPTH = 4
RET_HEADS = 4
RET_QK_DIM = 32
RET_V_DIM = 64
RET_QK_WIDTH = RET_HEADS * RET_QK_DIM
RET_V_WIDTH = RET_HEADS * RET_V_DIM
CHUNK = 128
ROPE_BASE = 10000.0
GMLP_GROUPS = 4
GMLP_WIDTH = 256
CONV_WIDTH = 256
GATE_RANK = 128
N_BRANCH = 3
N_EXPERTS = 32
TOP_K = 4
D_FF = 256
SWIGLU_LIMIT = 7.0
SWIGLU_ALPHA = 1.702
DEEPNORM_ALPHA = (2.0 * DEPTH) ** 0.25
LN_EPS = 1e-5

_O_QK = 0
_O_VG = 2 * RET_QK_WIDTH
_O_GMLP = _O_VG + 2 * RET_V_WIDTH
_O_CONV = _O_GMLP + 2 * GMLP_WIDTH
_O_CODE = _O_CONV + 3 * CONV_WIDTH
_IN_WIDTH = _O_CODE + N_BRANCH * GATE_RANK

LANES = 128
NEG_BIG = -1e30

SEQ_TILE = 256
MIXER_STEP = 512
MOE_BLOCK = 1024
EXPERT_PART = 256
COMBINE_PART = 128
SELT_ROWS = 16
RUN_ALIGN = 16
VMEM_LIMIT = 56 * 1024 * 1024


def _dot(a, b):
    return jnp.dot(a.astype(MXU_DTYPE), b.astype(MXU_DTYPE), preferred_element_type=F32)


def _dot_nt(a, b):
    return lax.dot_general(a.astype(MXU_DTYPE), b.astype(MXU_DTYPE),
                           (((1,), (1,)), ((), ())), preferred_element_type=F32)


def _split_dot(x, w):
    hi = x.astype(MXU_DTYPE)
    lo = (x - hi.astype(F32)).astype(MXU_DTYPE)
    return (jnp.dot(hi, w, preferred_element_type=F32)
            + jnp.dot(lo, w, preferred_element_type=F32))


def _layernorm_rows(x, g, b):
    mu = jnp.mean(x, axis=-1, keepdims=True)
    d = x - mu
    var = jnp.mean(d * d, axis=-1, keepdims=True)
    return d * lax.rsqrt(var + LN_EPS) * g + b


def _ada_kernel(c_ref, w_ref, b_ref, o_ref):
    c_act = jax.nn.silu(c_ref[...])
    o_ref[...] = jnp.dot(c_act, w_ref[...], preferred_element_type=F32,
                         precision=lax.Precision.HIGHEST) + b_ref[...]


def _ada_call(c, w_ada, b_ada):
    depth, d, six_d = w_ada.shape
    bsz = c.shape[0]
    n_col = six_d // d
    return pl.pallas_call(
        _ada_kernel,
        grid=(depth, n_col),
        in_specs=[
            pl.BlockSpec((bsz, d), lambda l, j: (0, 0)),
            pl.BlockSpec((None, d, d), lambda l, j: (l, 0, j)),
            pl.BlockSpec((None, 1, d), lambda l, j: (l, 0, j)),
        ],
        out_specs=pl.BlockSpec((None, None, bsz, d), lambda l, j: (l, j, 0, 0)),
        out_shape=jax.ShapeDtypeStruct((depth, n_col, bsz, d), F32),
        compiler_params=pltpu.CompilerParams(
            dimension_semantics=("arbitrary", "arbitrary"), vmem_limit_bytes=VMEM_LIMIT),
        name="ada_ln",
    )(c, w_ada, b_ada.reshape(depth, 1, six_d))


_MIXER_PHASES = 3


def _mixer_kernel(x_ref, ada_ref, cos_ref, sin_ref, *rest):
    params = rest[:-9]
    x1_ref, h2_ref, sel_ref, selt_ref, cnt_ref, before_ref, state_ref, zc_ref, cntacc_ref = rest[-9:]
    b = pl.program_id(0)
    s = pl.program_id(1)

    @pl.when(s == 0)
    def _():
        state_ref[...] = jnp.zeros_like(state_ref)
        zc_ref[...] = jnp.zeros_like(zc_ref)

    @pl.when(jnp.logical_and(b == 0, s == 0))
    def _():
        cntacc_ref[...] = jnp.zeros_like(cntacc_ref)

    n_sub = before_ref.shape[0]
    ts = x_ref.shape[0] // n_sub
    tiles = []
    for u in range(n_sub):
        rows = pl.ds(u * ts, ts)
        tiles.append(_mixer_tile(x_ref.at[rows], ada_ref, cos_ref.at[rows], sin_ref.at[rows], *params,
                                 x1_ref.at[rows], h2_ref.at[rows], sel_ref.at[rows], selt_ref.at[:, rows],
                                 cnt_ref, before_ref.at[u], state_ref, zc_ref, cntacc_ref))
    for _ in range(_MIXER_PHASES):
        for tile in tiles:
            next(tile, None)


def _mixer_tile(x_ref, ada_ref, cos_ref, sin_ref, w_in_ref, lng_ref, lnb_ref, wsp_ref, bsp_ref,
                convw_ref, wbr_ref, wgt_ref, bgt_ref, wout_ref, ln1g_ref, ln1b_ref, wr_ref, br_ref,
                decay_ref, zeta_ref, xi_ref, cdec_ref, bmask_ref, gavg_ref,
                x1_ref, h2_ref, sel_ref, selt_ref, cnt_ref, before_ref,
                state_ref, zc_ref, cntacc_ref):
    ts = x_ref.shape[0]
    x = x_ref[...]
    ada = ada_ref[...]
    sh1, sc1, gt1, sh2, sc2 = ada[0:1], ada[1:2], ada[2:3], ada[3:4], ada[4:5]
    h = (x * (1.0 + sc1) + sh1).astype(MXU_DTYPE)

    def proj(lo, hi):
        return jnp.dot(h, w_in_ref[:, lo:hi], preferred_element_type=F32)

    lane_qk = lax.broadcasted_iota(jnp.int32, (1, RET_QK_WIDTH), 1)
    lane_v = lax.broadcasted_iota(jnp.int32, (1, RET_V_WIDTH), 1)
    qk_masks = [lane_qk // RET_QK_DIM == hd for hd in range(RET_HEADS)]
    v_masks = [lane_v // RET_V_DIM == hd for hd in range(RET_HEADS)]

    qk = proj(_O_QK, _O_VG)
    cos = cos_ref[...]
    sin = sin_ref[...]
    first_half = (lane_qk % RET_QK_DIM) < (RET_QK_DIM // 2)

    def rotary(t):
        swapped = jnp.where(first_half, pltpu.roll(t, RET_QK_WIDTH - RET_QK_DIM // 2, 1),
                            pltpu.roll(t, RET_QK_DIM // 2, 1))
        return t * cos + swapped * sin

    q = rotary(qk[:, :RET_QK_WIDTH])
    k = rotary(qk[:, RET_QK_WIDTH:]) * (RET_QK_DIM ** -0.5)
    vg = proj(_O_VG, _O_GMLP)
    v = vg[:, :RET_V_WIDTH]
    g = vg[:, RET_V_WIDTH:]

    decay = decay_ref[...]
    zeta = zeta_ref[...]
    xi = xi_ref[...]
    cdec = cdec_ref[...]
    bmask = bmask_ref[...]
    gavg = gavg_ref[...]

    o_chunks = []
    state = state_ref[...]
    for c in range(ts // CHUNK):
        rows = slice(c * CHUNK, (c + 1) * CHUNK)
        q_c, k_c, v_c = q[rows], k[rows], v[rows]
        v_m = v_c.astype(MXU_DTYPE)
        q_all = jnp.concatenate([jnp.where(m, q_c, 0.0) for m in qk_masks], axis=0)
        scores = _dot_nt(q_all, k_c) * decay
        o_all = _dot(scores, v_m)
        o = _dot(q_c, state) * xi
        for hd in range(RET_HEADS):
            o = o + jnp.where(v_masks[hd], o_all[hd * CHUNK:(hd + 1) * CHUNK], 0.0)
        kv = _dot((k_c * zeta).T, v_m) * bmask
        state = state * cdec + kv
        o_chunks.append(o)
    state_ref[...] = state
    o = jnp.concatenate(o_chunks, axis=0) if len(o_chunks) > 1 else o_chunks[0]
    mu = _split_dot(o, gavg)
    d = o - mu
    var = _split_dot(d * d, gavg)
    r_br = jax.nn.silu(g) * (d * lax.rsqrt(var + LN_EPS))

    guv = proj(_O_GMLP, _O_CONV)
    u = jax.nn.gelu(guv[:, :GMLP_WIDTH])
    vv = _layernorm_rows(jax.nn.gelu(guv[:, GMLP_WIDTH:]), lng_ref[...], lnb_ref[...]).astype(MXU_DTYPE)
    rr = lax.broadcasted_iota(jnp.int32, (GMLP_GROUPS * CHUNK, CHUNK), 0) % CHUNK
    cc_ = lax.broadcasted_iota(jnp.int32, (GMLP_GROUPS * CHUNK, CHUNK), 1)
    w_sp = jnp.where(cc_ <= rr, wsp_ref[...], 0.0).astype(MXU_DTYPE)
    bsp = bsp_ref[...]
    z_chunks = []
    for c in range(ts // CHUNK):
        z_all = jnp.dot(w_sp, vv[c * CHUNK:(c + 1) * CHUNK], preferred_element_type=F32)
        z = bsp
        for gi in range(GMLP_GROUPS):
            z = z + jnp.where(v_masks[gi], z_all[gi * CHUNK:(gi + 1) * CHUNK], 0.0)
        z_chunks.append(z)
    z = jnp.concatenate(z_chunks, axis=0) if len(z_chunks) > 1 else z_chunks[0]
    s_br = u * z

    cbcx = proj(_O_CONV, _O_CODE)
    gate_b = cbcx[:, :CONV_WIDTH]
    zc = cbcx[:, CONV_WIDTH:2 * CONV_WIDTH] * cbcx[:, 2 * CONV_WIDTH:]
    carry = zc_ref[...]
    prev1 = carry[7:8]
    prev2 = carry[6:7]
    row = lax.broadcasted_iota(jnp.int32, (ts, 1), 0)
    z1 = jnp.where(row == 0, prev1, pltpu.roll(zc, 1, 0))
    z2 = jnp.where(row == 0, prev2, jnp.where(row == 1, prev1, pltpu.roll(zc, 2, 0)))
    zc_ref[...] = zc[ts - 8:ts]
    cw = convw_ref[...]
    k_br = gate_b * (cw[0:1] * z2 + cw[1:2] * z1 + cw[2:3] * zc)

    code = proj(_O_CODE, _IN_WIDTH)
    yield
    merged = None
    for i, br in enumerate((r_br, s_br, k_br)):
        y = jnp.dot(br.astype(MXU_DTYPE), wbr_ref[i], preferred_element_type=F32)
        gl = jnp.dot(code[:, i * GATE_RANK:(i + 1) * GATE_RANK].astype(MXU_DTYPE), wgt_ref[i],
                     preferred_element_type=F32) + bgt_ref[i:i + 1]
        t = jax.nn.sigmoid(gl) * y
        merged = t if merged is None else merged + t
    mix = jnp.dot(merged.astype(MXU_DTYPE), wout_ref[...], preferred_element_type=F32)
    x1 = _layernorm_rows(DEEPNORM_ALPHA * x + (1.0 + gt1) * mix, ln1g_ref[...], ln1b_ref[...])
    x1_ref[...] = x1
    h2 = (x1 * (1.0 + sc2) + sh2).astype(MXU_DTYPE)
    h2_ref[...] = h2

    logits = jnp.dot(h2, wr_ref[...], preferred_element_type=F32) + br_ref[...]
    yield
    lane = lax.broadcasted_iota(jnp.int32, (1, LANES), 1).astype(F32)
    top_v, top_i = [], []
    work = logits
    for _ in range(TOP_K):
        m = jnp.max(work, axis=-1, keepdims=True)
        i_sel = jnp.argmax(work, axis=-1, keepdims=True).astype(F32)
        top_v.append(m)
        top_i.append(i_sel)
        work = jnp.where(lane == i_sel, -jnp.inf, work)
    exps = [jnp.exp(tv - top_v[0]) for tv in top_v]
    denom = exps[0] + exps[1] + exps[2] + exps[3]
    onehots = [lane == ti for ti in top_i]
    member = jnp.zeros((ts, LANES), F32)
    for oh in onehots:
        member = member + jnp.where(oh, 1.0, 0.0)
    tr = lax.broadcasted_iota(jnp.int32, (ts, ts), 0)
    tc = lax.broadcasted_iota(jnp.int32, (ts, ts), 1)
    before = jnp.where(tc < tr, 1.0, 0.0).astype(MXU_DTYPE)
    cnt = cntacc_ref[...]
    local_rank = jnp.dot(before, member.astype(MXU_DTYPE), preferred_element_type=F32)
    sel = jnp.zeros((ts, LANES), F32)
    for kk in range(TOP_K):
        lrank_k = jnp.sum(jnp.where(onehots[kk], local_rank, 0.0), axis=-1, keepdims=True)
        sel = sel + jnp.where(lane == float(kk), top_i[kk], 0.0)
        sel = sel + jnp.where(lane == float(TOP_K + kk), exps[kk] / denom, 0.0)
        sel = sel + jnp.where(lane == float(2 * TOP_K + kk), lrank_k, 0.0)
    sel_ref[...] = sel
    selt_ref[...] = sel.T[:selt_ref.shape[0]]
    before_ref[...] = cnt
    tile_cnt = jnp.sum(member, axis=0, keepdims=True)
    cnt = cnt + jnp.floor((tile_cnt + (RUN_ALIGN - 1.0)) * (1.0 / RUN_ALIGN)) * RUN_ALIGN
    cntacc_ref[...] = cnt
    cnt_ref[...] = cnt


def _const_spec(shape):
    return pl.BlockSpec(shape, lambda b, s: (0,) * len(shape))


def _param_spec(a, layer):
    if layer is None:
        return _const_spec(a.shape)
    return pl.BlockSpec((None,) + a.shape[1:], lambda b, s: (layer,) + (0,) * (a.ndim - 1))


def _mixer_call(x, ada_l, cos_t, sin_t, lw, tables):
    bsz, seq, d = x.shape
    ts = min(SEQ_TILE, seq)
    step = min(MIXER_STEP, seq)
    n_tok = bsz * seq
    row3 = lambda b, s: (b, s, 0)
    in_specs = [
        pl.BlockSpec((None, step, d), row3),
        pl.BlockSpec((None, 6, d), lambda b, s: (b, 0, 0)),
        pl.BlockSpec((None, step, RET_QK_WIDTH), row3),
        pl.BlockSpec((None, step, RET_QK_WIDTH), row3),
    ] + [_param_spec(a, lay) for a, lay in lw] + [_const_spec(a.shape) for a in tables]
    tok_row = lambda b, s: (b * (seq // step) + s, 0)
    out_specs = [
        pl.BlockSpec((step, d), tok_row),
        pl.BlockSpec((step, d), tok_row),
        pl.BlockSpec((step, LANES), tok_row),
        pl.BlockSpec((SELT_ROWS, step), lambda b, s: (0, b * (seq // step) + s)),
        pl.BlockSpec((8, LANES), lambda b, s: (0, 0)),
        pl.BlockSpec((step // ts, 8, LANES), lambda b, s: (b * (seq // step) + s, 0, 0)),
    ]
    out_shape = [
        jax.ShapeDtypeStruct((n_tok, d), F32),
        jax.ShapeDtypeStruct((n_tok, d), MXU_DTYPE),
        jax.ShapeDtypeStruct((n_tok, LANES), F32),
        jax.ShapeDtypeStruct((SELT_ROWS, n_tok), F32),
        jax.ShapeDtypeStruct((8, LANES), F32),
        jax.ShapeDtypeStruct((n_tok // ts, 8, LANES), F32),
    ]
    return pl.pallas_call(
        _mixer_kernel,
        grid=(bsz, seq // step),
        in_specs=in_specs,
        out_specs=out_specs,
        out_shape=out_shape,
        scratch_shapes=[
            pltpu.VMEM((RET_QK_WIDTH, RET_V_WIDTH), F32),
            pltpu.VMEM((8, CONV_WIDTH), F32),
            pltpu.VMEM((8, LANES), F32),
        ],
        compiler_params=pltpu.CompilerParams(
            dimension_semantics=("arbitrary", "arbitrary"), vmem_limit_bytes=VMEM_LIMIT),
        name="mixer",
    )(x, ada_l, cos_t, sin_t, *[a for a, _ in lw], *tables)


def _run_pieces(max_rows):
    units = max_rows // RUN_ALIGN
    pieces = []
    bit = 1
    while bit <= units:
        pieces.append(bit)
        bit *= 2
    return tuple(reversed(pieces))


def _piece_table_width(n_classes):
    return -(-(n_classes * 2 * N_EXPERTS + n_classes) // LANES) * LANES


def _for_each_run_piece(tab_ref, pieces, make_copy, fn):
    for c, bit in enumerate(pieces):
        count = tab_ref[0, 0, len(pieces) * 2 * N_EXPERTS + c]

        def body(j, carry, c=c, bit=bit):
            hbm_row = tab_ref[0, 0, c * 2 * N_EXPERTS + j]
            buf_row = tab_ref[0, 0, (c * 2 + 1) * N_EXPERTS + j]
            fn(make_copy(pl.multiple_of(hbm_row, RUN_ALIGN), pl.multiple_of(buf_row, RUN_ALIGN),
                         bit * RUN_ALIGN))
            return carry

        lax.fori_loop(0, count, body, 0)


SEL_IDX, SEL_WEIGHT, SEL_RANK = 0, TOP_K, 2 * TOP_K
ROWS_PAD = 8


def _dispatch_kernel(pe_ref, ps_ref, tab_cur_ref, tab_prev_ref, rows_ref, h2_ref, xs_hbm,
                     zbuf, sbuf, sem, zsem):
    i = pl.program_id(0)
    n_steps = pl.num_programs(0)
    ts = h2_ref.shape[0]
    blk = zbuf.shape[0]
    n_exp = pe_ref.shape[0]
    run_rows = sbuf.shape[1]
    slot = i % 2
    pieces = _run_pieces(ts)

    @pl.when(i == 0)
    def _():
        zbuf[...] = jnp.zeros_like(zbuf)

        def tail_copy(e):
            start = pl.multiple_of(jnp.maximum(pe_ref[e] - blk, 0), blk)
            return pltpu.make_async_copy(zbuf, xs_hbm.at[pl.ds(start, blk), :], zsem)

        def zstart(e, carry):
            @pl.when(pe_ref[e] > ps_ref[e])
            def _():
                tail_copy(e).start()
            return carry

        def zwait(e, carry):
            @pl.when(pe_ref[e] > ps_ref[e])
            def _():
                tail_copy(e).wait()
            return carry

        lax.fori_loop(0, n_exp, zstart, 0)
        lax.fori_loop(0, n_exp, zwait, 0)

    rows_t = rows_ref[...]
    h2 = h2_ref[...].astype(MXU_DTYPE)
    n_parts = 2
    part = run_rows // n_parts

    def pick_rows(lo):
        buf_row = lax.broadcasted_iota(jnp.int32, (part, ts), 0).astype(F32) + float(lo)
        pick = jnp.zeros((part, ts), F32)
        for kk in range(TOP_K):
            pick = jnp.where(buf_row == rows_t[kk:kk + 1], 1.0, pick)
        return pick.astype(MXU_DTYPE)

    pick = pick_rows(0)
    for p in range(n_parts):
        nxt = pick_rows((p + 1) * part) if p + 1 < n_parts else None
        sbuf[slot, p * part:(p + 1) * part, :] = jnp.dot(
            pick, h2, preferred_element_type=F32).astype(sbuf.dtype)
        pick = nxt

    def push(dst_slot):
        def make(hbm_row, buf_row_, rows):
            return pltpu.make_async_copy(sbuf.at[dst_slot, pl.ds(buf_row_, rows), :],
                                         xs_hbm.at[pl.ds(hbm_row, rows), :], sem.at[dst_slot])
        return make

    _for_each_run_piece(tab_cur_ref, pieces, push(slot), lambda c: c.start())

    @pl.when(i > 0)
    def _():
        _for_each_run_piece(tab_prev_ref, pieces, push(1 - slot), lambda c: c.wait())

    @pl.when(i == n_steps - 1)
    def _():
        _for_each_run_piece(tab_cur_ref, pieces, push(slot), lambda c: c.wait())


def _dispatch_call(pad_end, pad_start, tab3, rows8, h2, n_rows, blk):
    n_tok, d = h2.shape
    n_steps = tab3.shape[0]
    ts = n_tok // n_steps
    run_rows = ts * TOP_K + N_EXPERTS * RUN_ALIGN
    grid_spec = pltpu.PrefetchScalarGridSpec(
        num_scalar_prefetch=2,
        grid=(n_steps,),
        in_specs=[
            pl.BlockSpec((1, 1, tab3.shape[2]), lambda i, pe, ps: (i, 0, 0), memory_space=pltpu.SMEM),
            pl.BlockSpec((1, 1, tab3.shape[2]), lambda i, pe, ps: (jnp.maximum(i - 1, 0), 0, 0),
                         memory_space=pltpu.SMEM),
            pl.BlockSpec((ROWS_PAD, ts), lambda i, pe, ps: (0, i)),
            pl.BlockSpec((ts, d), lambda i, pe, ps: (i, 0)),
        ],
        out_specs=pl.BlockSpec(memory_space=pl.ANY),
        scratch_shapes=[pltpu.VMEM((blk, d), MXU_DTYPE), pltpu.VMEM((2, run_rows, d), MXU_DTYPE),
                        pltpu.SemaphoreType.DMA((2,)), pltpu.SemaphoreType.DMA(())],
    )
    return pl.pallas_call(
        _dispatch_kernel,
        grid_spec=grid_spec,
        out_shape=jax.ShapeDtypeStruct((n_rows, d), MXU_DTYPE),
        compiler_params=pltpu.CompilerParams(
            dimension_semantics=("arbitrary",), vmem_limit_bytes=VMEM_LIMIT),
        name="dispatch",
    )(pad_end, pad_start, tab3, tab3, rows8, h2)


def _expert_up(xs, wgu, bgu_ref):
    return jnp.dot(xs, wgu, preferred_element_type=F32) + bgu_ref[...]


def _expert_down(gu, wdn, bdn_ref):
    gate = jnp.minimum(gu[:, :D_FF], SWIGLU_LIMIT)
    up = jnp.clip(gu[:, D_FF:], -SWIGLU_LIMIT, SWIGLU_LIMIT)
    act = (up + 1.0) * (gate * jax.nn.sigmoid(SWIGLU_ALPHA * gate))
    return jnp.dot(act.astype(MXU_DTYPE), wdn, preferred_element_type=F32) + bdn_ref[...]


def _expert_kernel(be_ref, nu_ref, xs_ref, wgu_ref, bgu_ref, wdn_ref, bdn_ref, ys_ref):
    i = pl.program_id(0)
    n_used = nu_ref[0]

    @pl.when(i < n_used)
    def _():
        part = min(EXPERT_PART, xs_ref.shape[0])
        n_parts = xs_ref.shape[0] // part
        rows = [slice(p * part, (p + 1) * part) for p in range(n_parts)]
        wgu = wgu_ref[...].astype(MXU_DTYPE)
        wdn = wdn_ref[...].astype(MXU_DTYPE)
        gu = _expert_up(xs_ref[rows[0], :], wgu, bgu_ref)
        for p in range(n_parts):
            gu_next = _expert_up(xs_ref[rows[p + 1], :], wgu, bgu_ref) if p + 1 < n_parts else None
            ys_ref[rows[p], :] = _expert_down(gu, wdn, bdn_ref).astype(ys_ref.dtype)
            gu = gu_next

    @pl.when(i >= n_used)
    def _():
        ys_ref[...] = jnp.zeros_like(ys_ref)


def _expert_call(block_e, n_used, xs, w_gu, b_gu, w_down, b_down, blk, layer):
    n_rows, d = xs.shape
    n_blocks = n_rows // blk
    two_f = w_gu.shape[3]
    last = lambda nu: jnp.maximum(nu[0] - 1, 0)
    of_block = lambda i, be, nu: (layer, be[i], 0, 0)
    grid_spec = pltpu.PrefetchScalarGridSpec(
        num_scalar_prefetch=2,
        grid=(n_blocks,),
        in_specs=[
            pl.BlockSpec((blk, d), lambda i, be, nu: (jnp.minimum(i, last(nu)), 0)),
            pl.BlockSpec((None, None, d, two_f), of_block),
            pl.BlockSpec((None, None, 1, two_f), of_block),
            pl.BlockSpec((None, None, two_f // 2, d), of_block),
            pl.BlockSpec((None, None, 1, d), of_block),
        ],
        out_specs=pl.BlockSpec((blk, d), lambda i, be, nu: (i, 0)),
    )
    return pl.pallas_call(
        _expert_kernel,
        grid_spec=grid_spec,
        out_shape=jax.ShapeDtypeStruct((n_rows, d), MXU_DTYPE),
        compiler_params=pltpu.CompilerParams(
            dimension_semantics=("arbitrary",), vmem_limit_bytes=VMEM_LIMIT),
        name="experts",
    )(block_e, n_used, xs, w_gu, b_gu, w_down, b_down)


def _combine_kernel(tab_cur_ref, tab_nxt_ref, ys_hbm, x1_ref, sel_ref, rows_ref, ada_ref, lng_ref, lnb_ref,
                    o_ref, rbuf, sem):
    i = pl.program_id(0)
    n_steps = pl.num_programs(0)
    ts = x1_ref.shape[0]
    slot = i % 2
    pieces = _run_pieces(ts)

    def fetch(dst_slot):
        def make(hbm_row, buf_row, rows):
            return pltpu.make_async_copy(ys_hbm.at[pl.ds(hbm_row, rows), :],
                                         rbuf.at[dst_slot, pl.ds(buf_row, rows), :], sem.at[dst_slot])
        return make

    @pl.when(i == 0)
    def _():
        rbuf[...] = jnp.zeros_like(rbuf)
        _for_each_run_piece(tab_cur_ref, pieces, fetch(0), lambda c: c.start())

    @pl.when(i + 1 < n_steps)
    def _():
        _for_each_run_piece(tab_nxt_ref, pieces, fetch(1 - slot), lambda c: c.start())

    _for_each_run_piece(tab_cur_ref, pieces, fetch(slot), lambda c: c.wait())

    sel = sel_ref[...]
    rows_t = rows_ref[...]
    rows_c = jnp.concatenate([rows_t, jnp.zeros((LANES - ROWS_PAD, ts), F32)], axis=0).T
    col = lax.broadcasted_iota(jnp.int32, (1, rbuf.shape[1]), 1).astype(F32)
    gt2 = ada_ref[...][5:6]
    part = min(COMBINE_PART, ts)

    def weight_pieces(tok):
        weights = jnp.zeros((part, rbuf.shape[1]), F32)
        for kk in range(TOP_K):
            weights = jnp.where(col == rows_c[tok, kk:kk + 1],
                                sel[tok, SEL_WEIGHT + kk:SEL_WEIGHT + kk + 1], weights)
        hi = weights.astype(MXU_DTYPE)
        lo = (weights - hi.astype(F32)).astype(MXU_DTYPE)
        return jnp.concatenate([hi, lo], axis=0)

    toks = [slice(p * part, (p + 1) * part) for p in range(ts // part)]
    pieces_w = weight_pieces(toks[0])
    for p, tok in enumerate(toks):
        nxt = weight_pieces(toks[p + 1]) if p + 1 < len(toks) else None
        both = jnp.dot(pieces_w, rbuf[slot], preferred_element_type=F32)
        ffn = both[:part] + both[part:]
        o_ref[tok, :] = _layernorm_rows(DEEPNORM_ALPHA * x1_ref[tok, :] + (1.0 + gt2) * ffn,
                                        lng_ref[...], lnb_ref[...])
        pieces_w = nxt


def _combine_call(tab3, ys, x1, sel, rows8, ada_l, ln_g, ln_b, seq):
    n_tok, d = x1.shape
    n_steps = tab3.shape[0]
    ts = n_tok // n_steps
    tiles_per_seq = seq // ts
    run_rows = ts * TOP_K + N_EXPERTS * RUN_ALIGN
    return pl.pallas_call(
        _combine_kernel,
        grid=(n_steps,),
        in_specs=[
            pl.BlockSpec((1, 1, tab3.shape[2]), lambda i: (i, 0, 0), memory_space=pltpu.SMEM),
            pl.BlockSpec((1, 1, tab3.shape[2]), lambda i: (jnp.minimum(i + 1, n_steps - 1), 0, 0),
                         memory_space=pltpu.SMEM),
            pl.BlockSpec(memory_space=pl.ANY),
            pl.BlockSpec((ts, d), lambda i: (i, 0)),
            pl.BlockSpec((ts, LANES), lambda i: (i, 0)),
            pl.BlockSpec((ROWS_PAD, ts), lambda i: (0, i)),
            pl.BlockSpec((None, 6, d), lambda i: (i // tiles_per_seq, 0, 0)),
            pl.BlockSpec((1, d), lambda i: (0, 0)),
            pl.BlockSpec((1, d), lambda i: (0, 0)),
        ],
        out_specs=pl.BlockSpec((ts, d), lambda i: (i, 0)),
        out_shape=jax.ShapeDtypeStruct((n_tok, d), F32),
        scratch_shapes=[pltpu.VMEM((2, run_rows, d), MXU_DTYPE), pltpu.SemaphoreType.DMA((2,))],
        compiler_params=pltpu.CompilerParams(
            dimension_semantics=("arbitrary",), vmem_limit_bytes=VMEM_LIMIT),
        name="combine",
    )(tab3, tab3, ys, x1, sel, rows8, ada_l, ln_g.reshape(1, d), ln_b.reshape(1, d))


def _retention_tables():
    log_gamma = jnp.log(1.0 - 2.0 ** (-5.0 - jnp.arange(RET_HEADS, dtype=F32)))
    idx = jnp.arange(CHUNK, dtype=F32)
    diff = idx[:, None] - idx[None, :]
    decay = jnp.where(diff[None] >= 0,
                      jnp.exp(log_gamma[:, None, None] * jnp.maximum(diff, 0.0)[None]), 0.0)
    decay = decay.reshape(RET_HEADS * CHUNK, CHUNK)
    zeta = jnp.exp(log_gamma[:, None] * (CHUNK - 1.0 - idx)[None])
    zeta_t = jnp.repeat(zeta.T, RET_QK_DIM, axis=1)
    xi = jnp.exp(log_gamma[:, None] * (idx + 1.0)[None]).T
    xi_t = jnp.repeat(xi, RET_V_DIM, axis=1)
    cdec = jnp.repeat(jnp.exp(log_gamma * CHUNK), RET_V_DIM)[None, :]
    row_head = jnp.arange(RET_QK_WIDTH) // RET_QK_DIM
    col_head = jnp.arange(RET_V_WIDTH) // RET_V_DIM
    same_head = row_head[:, None] == col_head[None, :]
    bmask = same_head.astype(F32)
    gavg = ((col_head[:, None] == col_head[None, :]).astype(F32) / RET_V_DIM).astype(MXU_DTYPE)
    return decay, zeta_t, xi_t, cdec, bmask, gavg


def _rotary_tables(positions):
    inv_freq = ROPE_BASE ** (-jnp.arange(0, RET_QK_DIM, 2, dtype=F32) / RET_QK_DIM)
    ang = positions.astype(F32)[..., None] * inv_freq
    cos = jnp.cos(ang)
    sin = jnp.sin(ang)
    cos_t = jnp.tile(jnp.concatenate([cos, cos], axis=-1), (1, 1, RET_HEADS))
    sin_t = jnp.tile(jnp.concatenate([-sin, sin], axis=-1), (1, 1, RET_HEADS))
    return cos_t, sin_t


def _moe_rows(n_tok, n_tiles, blk):
    worst = n_tok * TOP_K + n_tiles * N_EXPERTS * (RUN_ALIGN - 1)
    return (-(-worst // blk) + N_EXPERTS) * blk


def _routing_tables(sel_t, counts, before, n_rows, blk, pieces):
    n_blocks = n_rows // blk
    sizes = counts.astype(jnp.int32)
    pad_sizes = (sizes + blk - 1) // blk * blk
    pad_end = jnp.cumsum(pad_sizes)
    pad_start = pad_end - pad_sizes
    n_used = (pad_end[-1] // blk).astype(jnp.int32)
    block_start = jnp.arange(n_blocks, dtype=jnp.int32) * blk
    block_start = jnp.minimum(block_start, (n_used - 1) * blk)
    block_e = jnp.sum((pad_end[None, :] <= block_start[:, None]).astype(jnp.int32), axis=-1)
    block_e = jnp.minimum(block_e, N_EXPERTS - 1)

    before = before.astype(jnp.int32)
    after = jnp.concatenate([before[1:], sizes[None, :]], axis=0)
    run_start = pad_start[None, :] + before
    run_units = (after - before) // RUN_ALIGN
    run_off = (jnp.cumsum(run_units, axis=-1) - run_units) * RUN_ALIGN
    n_tiles = before.shape[0]
    slot_id = jnp.arange(N_EXPERTS, dtype=jnp.int32)
    cols, counts_c = [], []
    for bit in pieces:
        has = jnp.bitwise_and(run_units, bit) != 0
        done = jnp.bitwise_and(run_units, -2 * bit) * RUN_ALIGN
        pos = jnp.cumsum(has.astype(jnp.int32), axis=-1) - 1
        place = jnp.logical_and(has[:, None, :], pos[:, None, :] == slot_id[None, :, None])
        cols.append(jnp.sum(jnp.where(place, (run_start + done)[:, None, :], 0), axis=-1))
        cols.append(jnp.sum(jnp.where(place, (run_off + done)[:, None, :], 0), axis=-1))
        counts_c.append(jnp.sum(has.astype(jnp.int32), axis=-1, keepdims=True))
    width = _piece_table_width(len(pieces))
    used = len(pieces) * (2 * N_EXPERTS + 1)
    tab = jnp.concatenate(cols + counts_c + [jnp.zeros((n_tiles, width - used), jnp.int32)], axis=-1)

    n_tok = sel_t.shape[1]
    idx = sel_t[SEL_IDX:SEL_IDX + TOP_K].astype(jnp.int32)
    off_tok = jnp.repeat(run_off.T, n_tok // n_tiles, axis=1)
    base = jnp.sum(jnp.where(idx[:, None, :] == slot_id[None, :, None], off_tok[None], 0), axis=1)
    rows = base.astype(F32) + sel_t[SEL_RANK:SEL_RANK + TOP_K]
    rows8 = jnp.pad(rows, ((0, ROWS_PAD - TOP_K), (0, 0)))
    return (pad_end.astype(jnp.int32), pad_start.astype(jnp.int32), block_e, n_used.reshape(1),
            tab.astype(jnp.int32).reshape(n_tiles, 1, width), rows8)


def kernel(x, c, positions, w_ada, b_ada, w_in, gmlp_ln_g, gmlp_ln_b, w_spatial, b_spatial, conv_w,
           w_branch, w_gate_up, b_gate, w_out, ln1_g, ln1_b, w_router, b_router, w_gu, b_gu, w_down,
           b_down, ln2_g, ln2_b):
    bsz, seq, d = x.shape
    n_tok = bsz * seq
    depth = w_ada.shape[0]
    blk = min(MOE_BLOCK, n_tok * TOP_K // N_EXPERTS)
    ts = min(SEQ_TILE, seq)

    ada = _ada_call(c, w_ada, b_ada)
    ada = jnp.transpose(ada, (0, 2, 1, 3))
    cos_t, sin_t = _rotary_tables(positions)
    tables = _retention_tables()

    mx = MXU_DTYPE
    w_in_m = w_in.astype(mx)
    w_branch_m = w_branch.astype(mx)
    w_gate_m = w_gate_up.astype(mx)
    w_out_m = w_out.astype(mx)
    b_gu4 = b_gu[:, :, None, :]
    b_down4 = b_down[:, :, None, :]
    n_exp = w_router.shape[-1]
    w_router_m = jnp.pad(w_router, ((0, 0), (0, 0), (0, LANES - n_exp))).astype(mx)
    b_router_p = jnp.pad(b_router, ((0, 0), (0, LANES - n_exp)), constant_values=NEG_BIG)

    for l in range(depth):
        bsp_t = jnp.repeat(b_spatial[l].T, GMLP_WIDTH // GMLP_GROUPS, axis=1)
        lw = (
            (w_in_m, l),
            (gmlp_ln_g[l].reshape(1, -1), None), (gmlp_ln_b[l].reshape(1, -1), None),
            (w_spatial[l].reshape(GMLP_GROUPS * CHUNK, CHUNK), None), (bsp_t, None),
            (jnp.pad(conv_w[l], ((0, 8 - conv_w.shape[1]), (0, 0))), None),
            (w_branch_m, l), (w_gate_m, l), (b_gate, l), (w_out_m, l),
            (ln1_g[l].reshape(1, -1), None), (ln1_b[l].reshape(1, -1), None),
            (w_router_m, l), (b_router_p[l].reshape(1, -1), None),
        )
        x1, h2, sel, sel_t, cnt, before = _mixer_call(x, ada[l], cos_t, sin_t, lw, tables)
        n_rows = _moe_rows(n_tok, n_tok // ts, blk)
        pad_end, pad_start, block_e, n_used, tab3, rows8 = _routing_tables(
            sel_t, cnt[0, :n_exp], before[:, 0, :n_exp], n_rows, blk, _run_pieces(ts))
        xs = _dispatch_call(pad_end, pad_start, tab3, rows8, h2, n_rows, blk)
        ys = _expert_call(block_e, n_used, xs, w_gu, b_gu4, w_down, b_down4, blk, l)
        x = _combine_call(tab3, ys, x1, sel, rows8, ada[l], ln2_g[l], ln2_b[l], seq).reshape(bsz, seq, d)
    return x
```

```python
import functools

import jax
import jax.numpy as jnp
from jax import lax
from jax.experimental import pallas as pl
from jax.experimental.pallas import tpu as pltpu

F32 = jnp.float32
MXU_DTYPE = jnp.bfloat16

DEPTH = 4
RET_HEADS = 4
RET_QK_DIM = 32
RET_V_DIM = 64
RET_QK_WIDTH = RET_HEADS * RET_QK_DIM
RET_V_WIDTH = RET_HEADS * RET_V_DIM
CHUNK = 128
ROPE_BASE = 10000.0
GMLP_GROUPS = 4
GMLP_WIDTH = 256
CONV_WIDTH = 256
GATE_RANK = 128
N_BRANCH = 3
N_EXPERTS = 32
TOP_K = 4
D_FF = 256
SWIGLU_LIMIT = 7.0
SWIGLU_ALPHA = 1.702
DEEPNORM_ALPHA = (2.0 * DEPTH) ** 0.25
LN_EPS = 1e-5

_O_QK = 0
_O_VG = 2 * RET_QK_WIDTH
_O_GMLP = _O_VG + 2 * RET_V_WIDTH
_O_CONV = _O_GMLP + 2 * GMLP_WIDTH
_O_CODE = _O_CONV + 3 * CONV_WIDTH
_IN_WIDTH = _O_CODE + N_BRANCH * GATE_RANK

LANES = 128
NEG_BIG = -1e30

SEQ_TILE = 256
MIXER_STEP = 512
MOE_BLOCK = 2048
EXPERT_PART = 256
COMBINE_PART = 128
SELT_ROWS = 16
RUN_ALIGN = 16
VMEM_LIMIT = 56 * 1024 * 1024


def _dot(a, b):
    return jnp.dot(a.astype(MXU_DTYPE), b.astype(MXU_DTYPE), preferred_element_type=F32)


def _dot_nt(a, b):
    return lax.dot_general(a.astype(MXU_DTYPE), b.astype(MXU_DTYPE),
                           (((1,), (1,)), ((), ())), preferred_element_type=F32)


def _split_dot(x, w):
    hi = x.astype(MXU_DTYPE)
    lo = (x - hi.astype(F32)).astype(MXU_DTYPE)
    return (jnp.dot(hi, w, preferred_element_type=F32)
            + jnp.dot(lo, w, preferred_element_type=F32))


def _layernorm_rows(x, g, b):
    mu = jnp.mean(x, axis=-1, keepdims=True)
    d = x - mu
    var = jnp.mean(d * d, axis=-1, keepdims=True)
    return d * lax.rsqrt(var + LN_EPS) * g + b


def _ada_kernel(c_ref, w_ref, b_ref, o_ref):
    c_act = jax.nn.silu(c_ref[...])
    o_ref[...] = jnp.dot(c_act, w_ref[...], preferred_element_type=F32,
                         precision=lax.Precision.HIGHEST) + b_ref[...]


def _ada_call(c, w_ada, b_ada):
    depth, d, six_d = w_ada.shape
    bsz = c.shape[0]
    n_col = six_d // d
    return pl.pallas_call(
        _ada_kernel,
        grid=(depth, n_col),
        in_specs=[
            pl.BlockSpec((bsz, d), lambda l, j: (0, 0)),
            pl.BlockSpec((None, d, d), lambda l, j: (l, 0, j)),
            pl.BlockSpec((None, 1, d), lambda l, j: (l, 0, j)),
        ],
        out_specs=pl.BlockSpec((None, None, bsz, d), lambda l, j: (l, j, 0, 0)),
        out_shape=jax.ShapeDtypeStruct((depth, n_col, bsz, d), F32),
        compiler_params=pltpu.CompilerParams(
            dimension_semantics=("arbitrary", "arbitrary"), vmem_limit_bytes=VMEM_LIMIT),
        name="ada_ln",
    )(c, w_ada, b_ada.reshape(depth, 1, six_d))


_MIXER_PHASES = 3


def _mixer_kernel(x_ref, ada_ref, cos_ref, sin_ref, *rest):
    params = rest[:-9]
    x1_ref, h2_ref, sel_ref, selt_ref, cnt_ref, before_ref, state_ref, zc_ref, cntacc_ref = rest[-9:]
    b = pl.program_id(0)
    s = pl.program_id(1)

    @pl.when(s == 0)
    def _():
        state_ref[...] = jnp.zeros_like(state_ref)
        zc_ref[...] = jnp.zeros_like(zc_ref)

    @pl.when(jnp.logical_and(b == 0, s == 0))
    def _():
        cntacc_ref[...] = jnp.zeros_like(cntacc_ref)

    n_sub = before_ref.shape[0]
    ts = x_ref.shape[0] // n_sub
    tiles = []
    for u in range(n_sub):
        rows = pl.ds(u * ts, ts)
        tiles.append(_mixer_tile(x_ref.at[rows], ada_ref, cos_ref.at[rows], sin_ref.at[rows], *params,
                                 x1_ref.at[rows], h2_ref.at[rows], sel_ref.at[rows], selt_ref.at[:, rows],
                                 cnt_ref, before_ref.at[u], state_ref, zc_ref, cntacc_ref))
    for _ in range(_MIXER_PHASES):
        for tile in tiles:
            next(tile, None)


def _mixer_tile(x_ref, ada_ref, cos_ref, sin_ref, w_in_ref, lng_ref, lnb_ref, wsp_ref, bsp_ref,
                convw_ref, wbr_ref, wgt_ref, bgt_ref, wout_ref, ln1g_ref, ln1b_ref, wr_ref, br_ref,
                decay_ref, zeta_ref, xi_ref, cdec_ref, bmask_ref, gavg_ref,
                x1_ref, h2_ref, sel_ref, selt_ref, cnt_ref, before_ref,
                state_ref, zc_ref, cntacc_ref):
    ts = x_ref.shape[0]
    x = x_ref[...]
    ada = ada_ref[...]
    sh1, sc1, gt1, sh2, sc2 = ada[0:1], ada[1:2], ada[2:3], ada[3:4], ada[4:5]
    h = (x * (1.0 + sc1) + sh1).astype(MXU_DTYPE)

    def proj(lo, hi):
        return jnp.dot(h, w_in_ref[:, lo:hi], preferred_element_type=F32)

    lane_qk = lax.broadcasted_iota(jnp.int32, (1, RET_QK_WIDTH), 1)
    lane_v = lax.broadcasted_iota(jnp.int32, (1, RET_V_WIDTH), 1)
    qk_masks = [lane_qk // RET_QK_DIM == hd for hd in range(RET_HEADS)]
    v_masks = [lane_v // RET_V_DIM == hd for hd in range(RET_HEADS)]

    qk = proj(_O_QK, _O_VG)
    cos = cos_ref[...]
    sin = sin_ref[...]
    first_half = (lane_qk % RET_QK_DIM) < (RET_QK_DIM // 2)

    def rotary(t):
        swapped = jnp.where(first_half, pltpu.roll(t, RET_QK_WIDTH - RET_QK_DIM // 2, 1),
                            pltpu.roll(t, RET_QK_DIM // 2, 1))
        return t * cos + swapped * sin

    q = rotary(qk[:, :RET_QK_WIDTH])
    k = rotary(qk[:, RET_QK_WIDTH:]) * (RET_QK_DIM ** -0.5)
    vg = proj(_O_VG, _O_GMLP)
    v = vg[:, :RET_V_WIDTH]
    g = vg[:, RET_V_WIDTH:]

    decay = decay_ref[...]
    zeta = zeta_ref[...]
    xi = xi_ref[...]
    cdec = cdec_ref[...]
    bmask = bmask_ref[...]
    gavg = gavg_ref[...]

    o_chunks = []
    state = state_ref[...]
    for c in range(ts // CHUNK):
        rows = slice(c * CHUNK, (c + 1) * CHUNK)
        q_c, k_c, v_c = q[rows], k[rows], v[rows]
        v_m = v_c.astype(MXU_DTYPE)
        q_all = jnp.concatenate([jnp.where(m, q_c, 0.0) for m in qk_masks], axis=0)
        scores = _dot_nt(q_all, k_c) * decay
        o_all = _dot(scores, v_m)
        o = _dot(q_c, state) * xi
        for hd in range(RET_HEADS):
            o = o + jnp.where(v_masks[hd], o_all[hd * CHUNK:(hd + 1) * CHUNK], 0.0)
        kv = _dot((k_c * zeta).T, v_m) * bmask
        state = state * cdec + kv
        o_chunks.append(o)
    state_ref[...] = state
    o = jnp.concatenate(o_chunks, axis=0) if len(o_chunks) > 1 else o_chunks[0]
    mu = _split_dot(o, gavg)
    d = o - mu
    var = _split_dot(d * d, gavg)
    r_br = jax.nn.silu(g) * (d * lax.rsqrt(var + LN_EPS))

    guv = proj(_O_GMLP, _O_CONV)
    u = jax.nn.gelu(guv[:, :GMLP_WIDTH])
    vv = _layernorm_rows(jax.nn.gelu(guv[:, GMLP_WIDTH:]), lng_ref[...], lnb_ref[...]).astype(MXU_DTYPE)
    rr = lax.broadcasted_iota(jnp.int32, (GMLP_GROUPS * CHUNK, CHUNK), 0) % CHUNK
    cc_ = lax.broadcasted_iota(jnp.int32, (GMLP_GROUPS * CHUNK, CHUNK), 1)
    w_sp = jnp.where(cc_ <= rr, wsp_ref[...], 0.0).astype(MXU_DTYPE)
    bsp = bsp_ref[...]
    z_chunks = []
    for c in range(ts // CHUNK):
        z_all = jnp.dot(w_sp, vv[c * CHUNK:(c + 1) * CHUNK], preferred_element_type=F32)
        z = bsp
        for gi in range(GMLP_GROUPS):
            z = z + jnp.where(v_masks[gi], z_all[gi * CHUNK:(gi + 1) * CHUNK], 0.0)
        z_chunks.append(z)
    z = jnp.concatenate(z_chunks, axis=0) if len(z_chunks) > 1 else z_chunks[0]
    s_br = u * z

    cbcx = proj(_O_CONV, _O_CODE)
    gate_b = cbcx[:, :CONV_WIDTH]
    zc = cbcx[:, CONV_WIDTH:2 * CONV_WIDTH] * cbcx[:, 2 * CONV_WIDTH:]
    carry = zc_ref[...]
    prev1 = carry[7:8]
    prev2 = carry[6:7]
    row = lax.broadcasted_iota(jnp.int32, (ts, 1), 0)
    z1 = jnp.where(row == 0, prev1, pltpu.roll(zc, 1, 0))
    z2 = jnp.where(row == 0, prev2, jnp.where(row == 1, prev1, pltpu.roll(zc, 2, 0)))
    zc_ref[...] = zc[ts - 8:ts]
    cw = convw_ref[...]
    k_br = gate_b * (cw[0:1] * z2 + cw[1:2] * z1 + cw[2:3] * zc)

    code = proj(_O_CODE, _IN_WIDTH)
    yield
    merged = None
    for i, br in enumerate((r_br, s_br, k_br)):
        y = jnp.dot(br.astype(MXU_DTYPE), wbr_ref[i], preferred_element_type=F32)
        gl = jnp.dot(code[:, i * GATE_RANK:(i + 1) * GATE_RANK].astype(MXU_DTYPE), wgt_ref[i],
                     preferred_element_type=F32) + bgt_ref[i:i + 1]
        t = jax.nn.sigmoid(gl) * y
        merged = t if merged is None else merged + t
    mix = jnp.dot(merged.astype(MXU_DTYPE), wout_ref[...], preferred_element_type=F32)
    x1 = _layernorm_rows(DEEPNORM_ALPHA * x + (1.0 + gt1) * mix, ln1g_ref[...], ln1b_ref[...])
    x1_ref[...] = x1
    h2 = (x1 * (1.0 + sc2) + sh2).astype(MXU_DTYPE)
    h2_ref[...] = h2

    logits = jnp.dot(h2, wr_ref[...], preferred_element_type=F32) + br_ref[...]
    yield
    lane = lax.broadcasted_iota(jnp.int32, (1, LANES), 1).astype(F32)
    top_v, top_i = [], []
    work = logits
    for _ in range(TOP_K):
        m = jnp.max(work, axis=-1, keepdims=True)
        i_sel = jnp.argmax(work, axis=-1, keepdims=True).astype(F32)
        top_v.append(m)
        top_i.append(i_sel)
        work = jnp.where(lane == i_sel, -jnp.inf, work)
    exps = [jnp.exp(tv - top_v[0]) for tv in top_v]
    denom = exps[0] + exps[1] + exps[2] + exps[3]
    onehots = [lane == ti for ti in top_i]
    member = jnp.zeros((ts, LANES), F32)
    for oh in onehots:
        member = member + jnp.where(oh, 1.0, 0.0)
    tr = lax.broadcasted_iota(jnp.int32, (ts, ts), 0)
    tc = lax.broadcasted_iota(jnp.int32, (ts, ts), 1)
    before = jnp.where(tc < tr, 1.0, 0.0).astype(MXU_DTYPE)
    cnt = cntacc_ref[...]
    local_rank = jnp.dot(before, member.astype(MXU_DTYPE), preferred_element_type=F32)
    sel = jnp.zeros((ts, LANES), F32)
    for kk in range(TOP_K):
        lrank_k = jnp.sum(jnp.where(onehots[kk], local_rank, 0.0), axis=-1, keepdims=True)
        sel = sel + jnp.where(lane == float(kk), top_i[kk], 0.0)
        sel = sel + jnp.where(lane == float(TOP_K + kk), exps[kk] / denom, 0.0)
        sel = sel + jnp.where(lane == float(2 * TOP_K + kk), lrank_k, 0.0)
    sel_ref[...] = sel
    selt_ref[...] = sel.T[:selt_ref.shape[0]]
    before_ref[...] = cnt
    tile_cnt = jnp.sum(member, axis=0, keepdims=True)
    cnt = cnt + jnp.floor((tile_cnt + (RUN_ALIGN - 1.0)) * (1.0 / RUN_ALIGN)) * RUN_ALIGN
    cntacc_ref[...] = cnt
    cnt_ref[...] = cnt


def _const_spec(shape):
    return pl.BlockSpec(shape, lambda b, s: (0,) * len(shape))


def _param_spec(a, layer):
    if layer is None:
        return _const_spec(a.shape)
    return pl.BlockSpec((None,) + a.shape[1:], lambda b, s: (layer,) + (0,) * (a.ndim - 1))


def _mixer_call(x, ada_l, cos_t, sin_t, lw, tables):
    bsz, seq, d = x.shape
    ts = min(SEQ_TILE, seq)
    step = min(MIXER_STEP, seq)
    n_tok = bsz * seq
    row3 = lambda b, s: (b, s, 0)
    in_specs = [
        pl.BlockSpec((None, step, d), row3),
        pl.BlockSpec((None, 6, d), lambda b, s: (b, 0, 0)),
        pl.BlockSpec((None, step, RET_QK_WIDTH), row3),
        pl.BlockSpec((None, step, RET_QK_WIDTH), row3),
    ] + [_param_spec(a, lay) for a, lay in lw] + [_const_spec(a.shape) for a in tables]
    tok_row = lambda b, s: (b * (seq // step) + s, 0)
    out_specs = [
        pl.BlockSpec((step, d), tok_row),
        pl.BlockSpec((step, d), tok_row),
        pl.BlockSpec((step, LANES), tok_row),
        pl.BlockSpec((SELT_ROWS, step), lambda b, s: (0, b * (seq // step) + s)),
        pl.BlockSpec((8, LANES), lambda b, s: (0, 0)),
        pl.BlockSpec((step // ts, 8, LANES), lambda b, s: (b * (seq // step) + s, 0, 0)),
    ]
    out_shape = [
        jax.ShapeDtypeStruct((n_tok, d), F32),
        jax.ShapeDtypeStruct((n_tok, d), MXU_DTYPE),
        jax.ShapeDtypeStruct((n_tok, LANES), F32),
        jax.ShapeDtypeStruct((SELT_ROWS, n_tok), F32),
        jax.ShapeDtypeStruct((8, LANES), F32),
        jax.ShapeDtypeStruct((n_tok // ts, 8, LANES), F32),
    ]
    return pl.pallas_call(
        _mixer_kernel,
        grid=(bsz, seq // step),
        in_specs=in_specs,
        out_specs=out_specs,
        out_shape=out_shape,
        scratch_shapes=[
            pltpu.VMEM((RET_QK_WIDTH, RET_V_WIDTH), F32),
            pltpu.VMEM((8, CONV_WIDTH), F32),
            pltpu.VMEM((8, LANES), F32),
        ],
        compiler_params=pltpu.CompilerParams(
            dimension_semantics=("arbitrary", "arbitrary"), vmem_limit_bytes=VMEM_LIMIT),
        name="mixer",
    )(x, ada_l, cos_t, sin_t, *[a for a, _ in lw], *tables)


def _run_pieces(max_rows):
    units = max_rows // RUN_ALIGN
    pieces = []
    bit = 1
    while bit <= units:
        pieces.append(bit)
        bit *= 2
    return tuple(reversed(pieces))


def _piece_table_width(n_classes):
    return -(-(n_classes * 2 * N_EXPERTS + n_classes) // LANES) * LANES


def _for_each_run_piece(tab_ref, pieces, make_copy, fn):
    for c, bit in enumerate(pieces):
        count = tab_ref[0, 0, len(pieces) * 2 * N_EXPERTS + c]

        def body(j, carry, c=c, bit=bit):
            hbm_row = tab_ref[0, 0, c * 2 * N_EXPERTS + j]
            buf_row = tab_ref[0, 0, (c * 2 + 1) * N_EXPERTS + j]
            fn(make_copy(pl.multiple_of(hbm_row, RUN_ALIGN), pl.multiple_of(buf_row, RUN_ALIGN),
                         bit * RUN_ALIGN))
            return carry

        lax.fori_loop(0, count, body, 0)


SEL_IDX, SEL_WEIGHT, SEL_RANK = 0, TOP_K, 2 * TOP_K
ROWS_PAD = 8


def _dispatch_kernel(pe_ref, ps_ref, tab_cur_ref, tab_prev_ref, rows_ref, h2_ref, xs_hbm,
                     zbuf, sbuf, sem, zsem):
    i = pl.program_id(0)
    n_steps = pl.num_programs(0)
    ts = h2_ref.shape[0]
    blk = zbuf.shape[0]
    n_exp = pe_ref.shape[0]
    run_rows = sbuf.shape[1]
    slot = i % 2
    pieces = _run_pieces(ts)

    @pl.when(i == 0)
    def _():
        zbuf[...] = jnp.zeros_like(zbuf)

        def tail_copy(e):
            start = pl.multiple_of(jnp.maximum(pe_ref[e] - blk, 0), blk)
            return pltpu.make_async_copy(zbuf, xs_hbm.at[pl.ds(start, blk), :], zsem)

        def zstart(e, carry):
            @pl.when(pe_ref[e] > ps_ref[e])
            def _():
                tail_copy(e).start()
            return carry

        def zwait(e, carry):
            @pl.when(pe_ref[e] > ps_ref[e])
            def _():
                tail_copy(e).wait()
            return carry

        lax.fori_loop(0, n_exp, zstart, 0)
        lax.fori_loop(0, n_exp, zwait, 0)

    rows_t = rows_ref[...]
    h2 = h2_ref[...].astype(MXU_DTYPE)
    n_parts = 2
    part = run_rows // n_parts

    def pick_rows(lo):
        buf_row = lax.broadcasted_iota(jnp.int32, (part, ts), 0).astype(F32) + float(lo)
        pick = jnp.zeros((part, ts), F32)
        for kk in range(TOP_K):
            pick = jnp.where(buf_row == rows_t[kk:kk + 1], 1.0, pick)
        return pick.astype(MXU_DTYPE)

    pick = pick_rows(0)
    for p in range(n_parts):
        nxt = pick_rows((p + 1) * part) if p + 1 < n_parts else None
        sbuf[slot, p * part:(p + 1) * part, :] = jnp.dot(
            pick, h2, preferred_element_type=F32).astype(sbuf.dtype)
        pick = nxt

    def push(dst_slot):
        def make(hbm_row, buf_row_, rows):
            return pltpu.make_async_copy(sbuf.at[dst_slot, pl.ds(buf_row_, rows), :],
                                         xs_hbm.at[pl.ds(hbm_row, rows), :], sem.at[dst_slot])
        return make

    _for_each_run_piece(tab_cur_ref, pieces, push(slot), lambda c: c.start())

    @pl.when(i > 0)
    def _():
        _for_each_run_piece(tab_prev_ref, pieces, push(1 - slot), lambda c: c.wait())

    @pl.when(i == n_steps - 1)
    def _():
        _for_each_run_piece(tab_cur_ref, pieces, push(slot), lambda c: c.wait())


def _dispatch_call(pad_end, pad_start, tab3, rows8, h2, n_rows, blk):
    n_tok, d = h2.shape
    n_steps = tab3.shape[0]
    ts = n_tok // n_steps
    run_rows = ts * TOP_K + N_EXPERTS * RUN_ALIGN
    grid_spec = pltpu.PrefetchScalarGridSpec(
        num_scalar_prefetch=2,
        grid=(n_steps,),
        in_specs=[
            pl.BlockSpec((1, 1, tab3.shape[2]), lambda i, pe, ps: (i, 0, 0), memory_space=pltpu.SMEM),
            pl.BlockSpec((1, 1, tab3.shape[2]), lambda i, pe, ps: (jnp.maximum(i - 1, 0), 0, 0),
                         memory_space=pltpu.SMEM),
            pl.BlockSpec((ROWS_PAD, ts), lambda i, pe, ps: (0, i)),
            pl.BlockSpec((ts, d), lambda i, pe, ps: (i, 0)),
        ],
        out_specs=pl.BlockSpec(memory_space=pl.ANY),
        scratch_shapes=[pltpu.VMEM((blk, d), MXU_DTYPE), pltpu.VMEM((2, run_rows, d), MXU_DTYPE),
                        pltpu.SemaphoreType.DMA((2,)), pltpu.SemaphoreType.DMA(())],
    )
    return pl.pallas_call(
        _dispatch_kernel,
        grid_spec=grid_spec,
        out_shape=jax.ShapeDtypeStruct((n_rows, d), MXU_DTYPE),
        compiler_params=pltpu.CompilerParams(
            dimension_semantics=("arbitrary",), vmem_limit_bytes=VMEM_LIMIT),
        name="dispatch",
    )(pad_end, pad_start, tab3, tab3, rows8, h2)


def _expert_up(xs, wgu, bgu_ref):
    return jnp.dot(xs, wgu, preferred_element_type=F32) + bgu_ref[...]


def _expert_down(gu, wdn, bdn_ref):
    gate = jnp.minimum(gu[:, :D_FF], SWIGLU_LIMIT)
    up = jnp.clip(gu[:, D_FF:], -SWIGLU_LIMIT, SWIGLU_LIMIT)
    act = (up + 1.0) * (gate * jax.nn.sigmoid(SWIGLU_ALPHA * gate))
    return jnp.dot(act.astype(MXU_DTYPE), wdn, preferred_element_type=F32) + bdn_ref[...]


def _expert_kernel(be_ref, nu_ref, xs_ref, wgu_ref, bgu_ref, wdn_ref, bdn_ref, ys_ref):
    i = pl.program_id(0)
    n_used = nu_ref[0]

    @pl.when(i < n_used)
    def _():
        part = min(EXPERT_PART, xs_ref.shape[0])
        n_parts = xs_ref.shape[0] // part
        rows = [slice(p * part, (p + 1) * part) for p in range(n_parts)]
        wgu = wgu_ref[...].astype(MXU_DTYPE)
        wdn = wdn_ref[...].astype(MXU_DTYPE)
        gu = _expert_up(xs_ref[rows[0], :], wgu, bgu_ref)
        for p in range(n_parts):
            gu_next = _expert_up(xs_ref[rows[p + 1], :], wgu, bgu_ref) if p + 1 < n_parts else None
            ys_ref[rows[p], :] = _expert_down(gu, wdn, bdn_ref).astype(ys_ref.dtype)
            gu = gu_next

    @pl.when(i >= n_used)
    def _():
        ys_ref[...] = jnp.zeros_like(ys_ref)


def _expert_call(block_e, n_used, xs, w_gu, b_gu, w_down, b_down, blk, layer):
    n_rows, d = xs.shape
    n_blocks = n_rows // blk
    two_f = w_gu.shape[3]
    last = lambda nu: jnp.maximum(nu[0] - 1, 0)
    of_block = lambda i, be, nu: (layer, be[i], 0, 0)
    grid_spec = pltpu.PrefetchScalarGridSpec(
        num_scalar_prefetch=2,
        grid=(n_blocks,),
        in_specs=[
            pl.BlockSpec((blk, d), lambda i, be, nu: (jnp.minimum(i, last(nu)), 0)),
            pl.BlockSpec((None, None, d, two_f), of_block),
            pl.BlockSpec((None, None, 1, two_f), of_block),
            pl.BlockSpec((None, None, two_f // 2, d), of_block),
            pl.BlockSpec((None, None, 1, d), of_block),
        ],
        out_specs=pl.BlockSpec((blk, d), lambda i, be, nu: (i, 0)),
    )
    return pl.pallas_call(
        _expert_kernel,
        grid_spec=grid_spec,
        out_shape=jax.ShapeDtypeStruct((n_rows, d), MXU_DTYPE),
        compiler_params=pltpu.CompilerParams(
            dimension_semantics=("arbitrary",), vmem_limit_bytes=VMEM_LIMIT),
        name="experts",
    )(block_e, n_used, xs, w_gu, b_gu, w_down, b_down)


def _combine_kernel(tab_cur_ref, tab_nxt_ref, ys_hbm, x1_ref, sel_ref, rows_ref, ada_ref, lng_ref, lnb_ref,
                    o_ref, rbuf, sem):
    i = pl.program_id(0)
    n_steps = pl.num_programs(0)
    ts = x1_ref.shape[0]
    slot = i % 2
    pieces = _run_pieces(ts)

    def fetch(dst_slot):
        def make(hbm_row, buf_row, rows):
            return pltpu.make_async_copy(ys_hbm.at[pl.ds(hbm_row, rows), :],
                                         rbuf.at[dst_slot, pl.ds(buf_row, rows), :], sem.at[dst_slot])
        return make

    @pl.when(i == 0)
    def _():
        rbuf[...] = jnp.zeros_like(rbuf)
        _for_each_run_piece(tab_cur_ref, pieces, fetch(0), lambda c: c.start())

    @pl.when(i + 1 < n_steps)
    def _():
        _for_each_run_piece(tab_nxt_ref, pieces, fetch(1 - slot), lambda c: c.start())

    _for_each_run_piece(tab_cur_ref, pieces, fetch(slot), lambda c: c.wait())

    sel = sel_ref[...]
    rows_t = rows_ref[...]
    rows_c = jnp.concatenate([rows_t, jnp.zeros((LANES - ROWS_PAD, ts), F32)], axis=0).T
    col = lax.broadcasted_iota(jnp.int32, (1, rbuf.shape[1]), 1).astype(F32)
    gt2 = ada_ref[...][5:6]
    part = min(COMBINE_PART, ts)

    def weight_pieces(tok):
        weights = jnp.zeros((part, rbuf.shape[1]), F32)
        for kk in range(TOP_K):
            weights = jnp.where(col == rows_c[tok, kk:kk + 1],
                                sel[tok, SEL_WEIGHT + kk:SEL_WEIGHT + kk + 1], weights)
        hi = weights.astype(MXU_DTYPE)
        lo = (weights - hi.astype(F32)).astype(MXU_DTYPE)
        return jnp.concatenate([hi, lo], axis=0)

    toks = [slice(p * part, (p + 1) * part) for p in range(ts // part)]
    pieces_w = weight_pieces(toks[0])
    for p, tok in enumerate(toks):
        nxt = weight_pieces(toks[p + 1]) if p + 1 < len(toks) else None
        both = jnp.dot(pieces_w, rbuf[slot], preferred_element_type=F32)
        ffn = both[:part] + both[part:]
        o_ref[tok, :] = _layernorm_rows(DEEPNORM_ALPHA * x1_ref[tok, :] + (1.0 + gt2) * ffn,
                                        lng_ref[...], lnb_ref[...])
        pieces_w = nxt


def _combine_call(tab3, ys, x1, sel, rows8, ada_l, ln_g, ln_b, seq):
    n_tok, d = x1.shape
    n_steps = tab3.shape[0]
    ts = n_tok // n_steps
    tiles_per_seq = seq // ts
    run_rows = ts * TOP_K + N_EXPERTS * RUN_ALIGN
    return pl.pallas_call(
        _combine_kernel,
        grid=(n_steps,),
        in_specs=[
            pl.BlockSpec((1, 1, tab3.shape[2]), lambda i: (i, 0, 0), memory_space=pltpu.SMEM),
            pl.BlockSpec((1, 1, tab3.shape[2]), lambda i: (jnp.minimum(i + 1, n_steps - 1), 0, 0),
                         memory_space=pltpu.SMEM),
            pl.BlockSpec(memory_space=pl.ANY),
            pl.BlockSpec((ts, d), lambda i: (i, 0)),
            pl.BlockSpec((ts, LANES), lambda i: (i, 0)),
            pl.BlockSpec((ROWS_PAD, ts), lambda i: (0, i)),
            pl.BlockSpec((None, 6, d), lambda i: (i // tiles_per_seq, 0, 0)),
            pl.BlockSpec((1, d), lambda i: (0, 0)),
            pl.BlockSpec((1, d), lambda i: (0, 0)),
        ],
        out_specs=pl.BlockSpec((ts, d), lambda i: (i, 0)),
        out_shape=jax.ShapeDtypeStruct((n_tok, d), F32),
        scratch_shapes=[pltpu.VMEM((2, run_rows, d), MXU_DTYPE), pltpu.SemaphoreType.DMA((2,))],
        compiler_params=pltpu.CompilerParams(
            dimension_semantics=("arbitrary",), vmem_limit_bytes=VMEM_LIMIT),
        name="combine",
    )(tab3, tab3, ys, x1, sel, rows8, ada_l, ln_g.reshape(1, d), ln_b.reshape(1, d))


def _retention_tables():
    log_gamma = jnp.log(1.0 - 2.0 ** (-5.0 - jnp.arange(RET_HEADS, dtype=F32)))
    idx = jnp.arange(CHUNK, dtype=F32)
    diff = idx[:, None] - idx[None, :]
    decay = jnp.where(diff[None] >= 0,
                      jnp.exp(log_gamma[:, None, None] * jnp.maximum(diff, 0.0)[None]), 0.0)
    decay = decay.reshape(RET_HEADS * CHUNK, CHUNK)
    zeta = jnp.exp(log_gamma[:, None] * (CHUNK - 1.0 - idx)[None])
    zeta_t = jnp.repeat(zeta.T, RET_QK_DIM, axis=1)
    xi = jnp.exp(log_gamma[:, None] * (idx + 1.0)[None]).T
    xi_t = jnp.repeat(xi, RET_V_DIM, axis=1)
    cdec = jnp.repeat(jnp.exp(log_gamma * CHUNK), RET_V_DIM)[None, :]
    row_head = jnp.arange(RET_QK_WIDTH) // RET_QK_DIM
    col_head = jnp.arange(RET_V_WIDTH) // RET_V_DIM
    same_head = row_head[:, None] == col_head[None, :]
    bmask = same_head.astype(F32)
    gavg = ((col_head[:, None] == col_head[None, :]).astype(F32) / RET_V_DIM).astype(MXU_DTYPE)
    return decay, zeta_t, xi_t, cdec, bmask, gavg


def _rotary_tables(positions):
    inv_freq = ROPE_BASE ** (-jnp.arange(0, RET_QK_DIM, 2, dtype=F32) / RET_QK_DIM)
    ang = positions.astype(F32)[..., None] * inv_freq
    cos = jnp.cos(ang)
    sin = jnp.sin(ang)
    cos_t = jnp.tile(jnp.concatenate([cos, cos], axis=-1), (1, 1, RET_HEADS))
    sin_t = jnp.tile(jnp.concatenate([-sin, sin], axis=-1), (1, 1, RET_HEADS))
    return cos_t, sin_t


def _moe_rows(n_tok, n_tiles, blk):
    worst = n_tok * TOP_K + n_tiles * N_EXPERTS * (RUN_ALIGN - 1)
    return (-(-worst // blk) + N_EXPERTS) * blk


def _routing_tables(sel_t, counts, before, n_rows, blk, pieces):
    n_blocks = n_rows // blk
    sizes = counts.astype(jnp.int32)
    pad_sizes = (sizes + blk - 1) // blk * blk
    pad_end = jnp.cumsum(pad_sizes)
    pad_start = pad_end - pad_sizes
    n_used = (pad_end[-1] // blk).astype(jnp.int32)
    block_start = jnp.arange(n_blocks, dtype=jnp.int32) * blk
    block_start = jnp.minimum(block_start, (n_used - 1) * blk)
    block_e = jnp.sum((pad_end[None, :] <= block_start[:, None]).astype(jnp.int32), axis=-1)
    block_e = jnp.minimum(block_e, N_EXPERTS - 1)

    before = before.astype(jnp.int32)
    after = jnp.concatenate([before[1:], sizes[None, :]], axis=0)
    run_start = pad_start[None, :] + before
    run_units = (after - before) // RUN_ALIGN
    run_off = (jnp.cumsum(run_units, axis=-1) - run_units) * RUN_ALIGN
    n_tiles = before.shape[0]
    slot_id = jnp.arange(N_EXPERTS, dtype=jnp.int32)
    cols, counts_c = [], []
    for bit in pieces:
        has = jnp.bitwise_and(run_units, bit) != 0
        done = jnp.bitwise_and(run_units, -2 * bit) * RUN_ALIGN
        pos = jnp.cumsum(has.astype(jnp.int32), axis=-1) - 1
        place = jnp.logical_and(has[:, None, :], pos[:, None, :] == slot_id[None, :, None])
        cols.append(jnp.sum(jnp.where(place, (run_start + done)[:, None, :], 0), axis=-1))
        cols.append(jnp.sum(jnp.where(place, (run_off + done)[:, None, :], 0), axis=-1))
        counts_c.append(jnp.sum(has.astype(jnp.int32), axis=-1, keepdims=True))
    width = _piece_table_width(len(pieces))
    used = len(pieces) * (2 * N_EXPERTS + 1)
    tab = jnp.concatenate(cols + counts_c + [jnp.zeros((n_tiles, width - used), jnp.int32)], axis=-1)

    n_tok = sel_t.shape[1]
    idx = sel_t[SEL_IDX:SEL_IDX + TOP_K].astype(jnp.int32)
    off_tok = jnp.repeat(run_off.T, n_tok // n_tiles, axis=1)
    base = jnp.sum(jnp.where(idx[:, None, :] == slot_id[None, :, None], off_tok[None], 0), axis=1)
    rows = base.astype(F32) + sel_t[SEL_RANK:SEL_RANK + TOP_K]
    rows8 = jnp.pad(rows, ((0, ROWS_PAD - TOP_K), (0, 0)))
    return (pad_end.astype(jnp.int32), pad_start.astype(jnp.int32), block_e, n_used.reshape(1),
            tab.astype(jnp.int32).reshape(n_tiles, 1, width), rows8)


def kernel(x, c, positions, w_ada, b_ada, w_in, gmlp_ln_g, gmlp_ln_b, w_spatial, b_spatial, conv_w,
           w_branch, w_gate_up, b_gate, w_out, ln1_g, ln1_b, w_router, b_router, w_gu, b_gu, w_down,
           b_down, ln2_g, ln2_b):
    bsz, seq, d = x.shape
    n_tok = bsz * seq
    depth = w_ada.shape[0]
    blk = min(MOE_BLOCK, n_tok * TOP_K // N_EXPERTS)
    ts = min(SEQ_TILE, seq)

    ada = _ada_call(c, w_ada, b_ada)
    ada = jnp.transpose(ada, (0, 2, 1, 3))
    cos_t, sin_t = _rotary_tables(positions)
    tables = _retention_tables()

    mx = MXU_DTYPE
    w_in_m = w_in.astype(mx)
    w_branch_m = w_branch.astype(mx)
    w_gate_m = w_gate_up.astype(mx)
    w_out_m = w_out.astype(mx)
    b_gu4 = b_gu[:, :, None, :]
    b_down4 = b_down[:, :, None, :]
    n_exp = w_router.shape[-1]
    w_router_m = jnp.pad(w_router, ((0, 0), (0, 0), (0, LANES - n_exp))).astype(mx)
    b_router_p = jnp.pad(b_router, ((0, 0), (0, LANES - n_exp)), constant_values=NEG_BIG)

    for l in range(depth):
        bsp_t = jnp.repeat(b_spatial[l].T, GMLP_WIDTH // GMLP_GROUPS, axis=1)
        lw = (
            (w_in_m, l),
            (gmlp_ln_g[l].reshape(1, -1), None), (gmlp_ln_b[l].reshape(1, -1), None),
            (w_spatial[l].reshape(GMLP_GROUPS * CHUNK, CHUNK), None), (bsp_t, None),
            (jnp.pad(conv_w[l], ((0, 8 - conv_w.shape[1]), (0, 0))), None),
            (w_branch_m, l), (w_gate_m, l), (b_gate, l), (w_out_m, l),
            (ln1_g[l].reshape(1, -1), None), (ln1_b[l].reshape(1, -1), None),
            (w_router_m, l), (b_router_p[l].reshape(1, -1), None),
        )
        x1, h2, sel, sel_t, cnt, before = _mixer_call(x, ada[l], cos_t, sin_t, lw, tables)
        n_rows = _moe_rows(n_tok, n_tok // ts, blk)
        pad_end, pad_start, block_e, n_used, tab3, rows8 = _routing_tables(
            sel_t, cnt[0, :n_exp], before[:, 0, :n_exp], n_rows, blk, _run_pieces(ts))
        xs = _dispatch_call(pad_end, pad_start, tab3, rows8, h2, n_rows, blk)
        ys = _expert_call(block_e, n_used, xs, w_gu, b_gu4, w_down, b_down4, blk, l)
        x = _combine_call(tab3, ys, x1, sel, rows8, ada[l], ln2_g[l], ln2_b[l], seq).reshape(bsz, seq, d)
    return x
```

```python
import jax
import jax.numpy as jnp
from jax import lax
from jax.experimental import pallas as pl
from jax.experimental.pallas import tpu as pltpu

F32 = jnp.float32
MXU_DTYPE = jnp.bfloat16

DEPTH = 4
RET_HEADS = 4
RET_QK_DIM = 32
RET_V_DIM = 64
RET_QK_WIDTH = RET_HEADS * RET_QK_DIM
RET_V_WIDTH = RET_HEADS * RET_V_DIM
CHUNK = 128
ROPE_BASE = 10000.0
GMLP_GROUPS = 4
GMLP_WIDTH = 256
CONV_WIDTH = 256
GATE_RANK = 128
N_BRANCH = 3
N_EXPERTS = 32
TOP_K = 4
D_FF = 256
SWIGLU_LIMIT = 7.0
SWIGLU_ALPHA = 1.702
DEEPNORM_ALPHA = (2.0 * DEPTH) ** 0.25
LN_EPS = 1e-5

_O_QK = 0
_O_VG = 2 * RET_QK_WIDTH
_O_GMLP = _O_VG + 2 * RET_V_WIDTH
_O_CONV = _O_GMLP + 2 * GMLP_WIDTH
_O_CODE = _O_CONV + 3 * CONV_WIDTH
_IN_WIDTH = _O_CODE + N_BRANCH * GATE_RANK

LANES = 128
SUBLANES = 8
NEG_BIG = -1e30

SEQ_TILE = 256
MIXER_STEP = 512
MOE_BLOCK = 2048
EXPERT_PART = 512
COMBINE_PART = 128
SELT_ROWS = 16
RUN_ALIGN = 16
VMEM_LIMIT = 56 * 1024 * 1024


def _dot(a, b):
    return jnp.dot(a.astype(MXU_DTYPE), b.astype(MXU_DTYPE), preferred_element_type=F32)


def _dot_nt(a, b):
    return lax.dot_general(a.astype(MXU_DTYPE), b.astype(MXU_DTYPE),
                           (((1,), (1,)), ((), ())), preferred_element_type=F32)


def _split_dot(x, w):
    hi = x.astype(MXU_DTYPE)
    lo = (x - hi.astype(F32)).astype(MXU_DTYPE)
    return (jnp.dot(hi, w, preferred_element_type=F32)
            + jnp.dot(lo, w, preferred_element_type=F32))


def _layernorm_rows(x, g, b):
    mu = jnp.mean(x, axis=-1, keepdims=True)
    d = x - mu
    var = jnp.mean(d * d, axis=-1, keepdims=True)
    return d * lax.rsqrt(var + LN_EPS) * g + b


def _ada_kernel(c_ref, w_ref, b_ref, o_ref):
    c_act = jax.nn.silu(c_ref[...])
    o_ref[...] = jnp.dot(c_act, w_ref[...], preferred_element_type=F32,
                         precision=lax.Precision.HIGHEST) + b_ref[...]


def _ada_call(c, w_ada, b_ada):
    depth, d, six_d = w_ada.shape
    bsz = c.shape[0]
    n_col = six_d // d
    return pl.pallas_call(
        _ada_kernel,
        grid=(depth, n_col),
        in_specs=[
            pl.BlockSpec((bsz, d), lambda l, j: (0, 0)),
            pl.BlockSpec((None, d, d), lambda l, j: (l, 0, j)),
            pl.BlockSpec((None, 1, d), lambda l, j: (l, 0, j)),
        ],
        out_specs=pl.BlockSpec((None, None, bsz, d), lambda l, j: (l, j, 0, 0)),
        out_shape=jax.ShapeDtypeStruct((depth, n_col, bsz, d), F32),
        compiler_params=pltpu.CompilerParams(
            dimension_semantics=("arbitrary", "arbitrary"), vmem_limit_bytes=VMEM_LIMIT),
        name="ada_ln",
    )(c, w_ada, b_ada.reshape(depth, 1, six_d))


_MIXER_PHASES = 3


def _mixer_kernel(x_ref, ada_ref, cos_ref, sin_ref, *rest):
    params = rest[:-9]
    x1_ref, h2_ref, sel_ref, selt_ref, cnt_ref, before_ref, state_ref, zc_ref, cntacc_ref = rest[-9:]
    b = pl.program_id(0)
    s = pl.program_id(1)

    @pl.when(s == 0)
    def _():
        state_ref[...] = jnp.zeros_like(state_ref)
        zc_ref[...] = jnp.zeros_like(zc_ref)

    @pl.when(jnp.logical_and(b == 0, s == 0))
    def _():
        cntacc_ref[...] = jnp.zeros_like(cntacc_ref)

    n_sub = before_ref.shape[0]
    ts = x_ref.shape[0] // n_sub
    tiles = []
    for u in range(n_sub):
        rows = pl.ds(u * ts, ts)
        tiles.append(_mixer_tile(x_ref.at[rows], ada_ref, cos_ref.at[rows], sin_ref.at[rows], *params,
                                 x1_ref.at[rows], h2_ref.at[rows], sel_ref.at[rows], selt_ref.at[:, rows],
                                 cnt_ref, before_ref.at[u], state_ref, zc_ref, cntacc_ref))
    for _ in range(_MIXER_PHASES):
        for tile in tiles:
            next(tile, None)


def _mixer_tile(x_ref, ada_ref, cos_ref, sin_ref, w_in_ref, lng_ref, lnb_ref, wsp_ref, bsp_ref,
                convw_ref, wbr_ref, wgt_ref, bgt_ref, wout_ref, ln1g_ref, ln1b_ref, wr_ref, br_ref,
                decay_ref, zeta_ref, xi_ref, cdec_ref, bmask_ref, gavg_ref,
                x1_ref, h2_ref, sel_ref, selt_ref, cnt_ref, before_ref,
                state_ref, zc_ref, cntacc_ref):
    ts = x_ref.shape[0]
    x = x_ref[...]
    ada = ada_ref[...]
    sh1, sc1, gt1, sh2, sc2 = ada[0:1], ada[1:2], ada[2:3], ada[3:4], ada[4:5]
    h = (x * (1.0 + sc1) + sh1).astype(MXU_DTYPE)

    def proj(lo, hi):
        return jnp.dot(h, w_in_ref[:, lo:hi], preferred_element_type=F32)

    lane_qk = lax.broadcasted_iota(jnp.int32, (1, RET_QK_WIDTH), 1)
    lane_v = lax.broadcasted_iota(jnp.int32, (1, RET_V_WIDTH), 1)
    qk_masks = [lane_qk // RET_QK_DIM == hd for hd in range(RET_HEADS)]
    v_masks = [lane_v // RET_V_DIM == hd for hd in range(RET_HEADS)]

    qk = proj(_O_QK, _O_VG)
    cos = cos_ref[...]
    sin = sin_ref[...]
    first_half = (lane_qk % RET_QK_DIM) < (RET_QK_DIM // 2)

    def rotary(t):
        swapped = jnp.where(first_half, pltpu.roll(t, RET_QK_WIDTH - RET_QK_DIM // 2, 1),
                            pltpu.roll(t, RET_QK_DIM // 2, 1))
        return t * cos + swapped * sin

    q = rotary(qk[:, :RET_QK_WIDTH])
    k = rotary(qk[:, RET_QK_WIDTH:]) * (RET_QK_DIM ** -0.5)
    vg = proj(_O_VG, _O_GMLP)
    v = vg[:, :RET_V_WIDTH]
    g = vg[:, RET_V_WIDTH:]

    decay = decay_ref[...]
    zeta = zeta_ref[...]
    xi = xi_ref[...]
    cdec = cdec_ref[...]
    bmask = bmask_ref[...]
    gavg = gavg_ref[...]

    o_chunks = []
    state = state_ref[...]
    for c in range(ts // CHUNK):
        rows = slice(c * CHUNK, (c + 1) * CHUNK)
        q_c, k_c, v_c = q[rows], k[rows], v[rows]
        v_m = v_c.astype(MXU_DTYPE)
        q_all = jnp.concatenate([jnp.where(m, q_c, 0.0) for m in qk_masks], axis=0)
        scores = _dot_nt(q_all, k_c) * decay
        o_all = _dot(scores, v_m)
        o = _dot(q_c, state) * xi
        for hd in range(RET_HEADS):
            o = o + jnp.where(v_masks[hd], o_all[hd * CHUNK:(hd + 1) * CHUNK], 0.0)
        kv = _dot((k_c * zeta).T, v_m) * bmask
        state = state * cdec + kv
        o_chunks.append(o)
    state_ref[...] = state
    o = jnp.concatenate(o_chunks, axis=0) if len(o_chunks) > 1 else o_chunks[0]
    mu = _split_dot(o, gavg)
    d = o - mu
    var = _split_dot(d * d, gavg)
    r_br = jax.nn.silu(g) * (d * lax.rsqrt(var + LN_EPS))

    guv = proj(_O_GMLP, _O_CONV)
    u = jax.nn.gelu(guv[:, :GMLP_WIDTH])
    vv = _layernorm_rows(jax.nn.gelu(guv[:, GMLP_WIDTH:]), lng_ref[...], lnb_ref[...]).astype(MXU_DTYPE)
    rr = lax.broadcasted_iota(jnp.int32, (GMLP_GROUPS * CHUNK, CHUNK), 0) % CHUNK
    cc_ = lax.broadcasted_iota(jnp.int32, (GMLP_GROUPS * CHUNK, CHUNK), 1)
    w_sp = jnp.where(cc_ <= rr, wsp_ref[...], 0.0).astype(MXU_DTYPE)
    bsp = bsp_ref[...]
    z_chunks = []
    for c in range(ts // CHUNK):
        z_all = jnp.dot(w_sp, vv[c * CHUNK:(c + 1) * CHUNK], preferred_element_type=F32)
        z = bsp
        for gi in range(GMLP_GROUPS):
            z = z + jnp.where(v_masks[gi], z_all[gi * CHUNK:(gi + 1) * CHUNK], 0.0)
        z_chunks.append(z)
    z = jnp.concatenate(z_chunks, axis=0) if len(z_chunks) > 1 else z_chunks[0]
    s_br = u * z

    cbcx = proj(_O_CONV, _O_CODE)
    gate_b = cbcx[:, :CONV_WIDTH]
    zc = cbcx[:, CONV_WIDTH:2 * CONV_WIDTH] * cbcx[:, 2 * CONV_WIDTH:]
    carry = zc_ref[...]
    prev1 = carry[SUBLANES - 1:SUBLANES]
    prev2 = carry[SUBLANES - 2:SUBLANES - 1]
    row = lax.broadcasted_iota(jnp.int32, (ts, 1), 0)
    z1 = jnp.where(row == 0, prev1, pltpu.roll(zc, 1, 0))
    z2 = jnp.where(row == 0, prev2, jnp.where(row == 1, prev1, pltpu.roll(zc, 2, 0)))
    zc_ref[...] = zc[ts - SUBLANES:ts]
    cw = convw_ref[...]
    k_br = gate_b * (cw[0:1] * z2 + cw[1:2] * z1 + cw[2:3] * zc)

    code = proj(_O_CODE, _IN_WIDTH)
    yield
    merged = None
    for i, br in enumerate((r_br, s_br, k_br)):
        y = jnp.dot(br.astype(MXU_DTYPE), wbr_ref[i], preferred_element_type=F32)
        gl = jnp.dot(code[:, i * GATE_RANK:(i + 1) * GATE_RANK].astype(MXU_DTYPE), wgt_ref[i],
                     preferred_element_type=F32) + bgt_ref[i:i + 1]
        t = jax.nn.sigmoid(gl) * y
        merged = t if merged is None else merged + t
    mix = jnp.dot(merged.astype(MXU_DTYPE), wout_ref[...], preferred_element_type=F32)
    x1 = _layernorm_rows(DEEPNORM_ALPHA * x + (1.0 + gt1) * mix, ln1g_ref[...], ln1b_ref[...])
    x1_ref[...] = x1
    h2 = (x1 * (1.0 + sc2) + sh2).astype(MXU_DTYPE)
    h2_ref[...] = h2

    logits = jnp.dot(h2, wr_ref[...], preferred_element_type=F32) + br_ref[...]
    yield
    lane = lax.broadcasted_iota(jnp.int32, (1, LANES), 1).astype(F32)
    top_v, top_i = [], []
    work = logits
    for _ in range(TOP_K):
        m = jnp.max(work, axis=-1, keepdims=True)
        i_sel = jnp.argmax(work, axis=-1, keepdims=True).astype(F32)
        top_v.append(m)
        top_i.append(i_sel)
        work = jnp.where(lane == i_sel, -jnp.inf, work)
    exps = [jnp.exp(tv - top_v[0]) for tv in top_v]
    denom = exps[0] + exps[1] + exps[2] + exps[3]
    onehots = [lane == ti for ti in top_i]
    member = jnp.zeros((ts, LANES), F32)
    for oh in onehots:
        member = member + jnp.where(oh, 1.0, 0.0)
    tr = lax.broadcasted_iota(jnp.int32, (ts, ts), 0)
    tc = lax.broadcasted_iota(jnp.int32, (ts, ts), 1)
    before = jnp.where(tc < tr, 1.0, 0.0).astype(MXU_DTYPE)
    cnt = cntacc_ref[...]
    local_rank = jnp.dot(before, member.astype(MXU_DTYPE), preferred_element_type=F32)
    sel = jnp.zeros((ts, LANES), F32)
    for kk in range(TOP_K):
        lrank_k = jnp.sum(jnp.where(onehots[kk], local_rank, 0.0), axis=-1, keepdims=True)
        sel = sel + jnp.where(lane == float(kk), top_i[kk], 0.0)
        sel = sel + jnp.where(lane == float(TOP_K + kk), exps[kk] / denom, 0.0)
        sel = sel + jnp.where(lane == float(2 * TOP_K + kk), lrank_k, 0.0)
    sel_ref[...] = sel
    selt_ref[...] = sel.T[:selt_ref.shape[0]]
    before_ref[...] = cnt
    tile_cnt = jnp.sum(member, axis=0, keepdims=True)
    cnt = cnt + jnp.floor((tile_cnt + (RUN_ALIGN - 1.0)) * (1.0 / RUN_ALIGN)) * RUN_ALIGN
    cntacc_ref[...] = cnt
    cnt_ref[...] = cnt


def _const_spec(shape):
    return pl.BlockSpec(shape, lambda b, s: (0,) * len(shape))


def _param_spec(a, layer):
    if layer is None:
        return _const_spec(a.shape)
    return pl.BlockSpec((None,) + a.shape[1:], lambda b, s: (layer,) + (0,) * (a.ndim - 1))


def _mixer_call(x, ada_l, cos_t, sin_t, lw, tables):
    bsz, seq, d = x.shape
    ts = min(SEQ_TILE, seq)
    step = min(MIXER_STEP, seq)
    n_tok = bsz * seq
    row3 = lambda b, s: (b, s, 0)
    in_specs = [
        pl.BlockSpec((None, step, d), row3),
        pl.BlockSpec((None, 6, d), lambda b, s: (b, 0, 0)),
        pl.BlockSpec((None, step, RET_QK_WIDTH), row3),
        pl.BlockSpec((None, step, RET_QK_WIDTH), row3),
    ] + [_param_spec(a, lay) for a, lay in lw] + [_const_spec(a.shape) for a in tables]
    tok_row = lambda b, s: (b * (seq // step) + s, 0)
    out_specs = [
        pl.BlockSpec((step, d), tok_row),
        pl.BlockSpec((step, d), tok_row),
        pl.BlockSpec((step, LANES), tok_row),
        pl.BlockSpec((SELT_ROWS, step), lambda b, s: (0, b * (seq // step) + s)),
        pl.BlockSpec((SUBLANES, LANES), lambda b, s: (0, 0)),
        pl.BlockSpec((step // ts, SUBLANES, LANES), lambda b, s: (b * (seq // step) + s, 0, 0)),
    ]
    out_shape = [
        jax.ShapeDtypeStruct((n_tok, d), F32),
        jax.ShapeDtypeStruct((n_tok, d), MXU_DTYPE),
        jax.ShapeDtypeStruct((n_tok, LANES), F32),
        jax.ShapeDtypeStruct((SELT_ROWS, n_tok), F32),
        jax.ShapeDtypeStruct((SUBLANES, LANES), F32),
        jax.ShapeDtypeStruct((n_tok // ts, SUBLANES, LANES), F32),
    ]
    return pl.pallas_call(
        _mixer_kernel,
        grid=(bsz, seq // step),
        in_specs=in_specs,
        out_specs=out_specs,
        out_shape=out_shape,
        scratch_shapes=[
            pltpu.VMEM((RET_QK_WIDTH, RET_V_WIDTH), F32),
            pltpu.VMEM((SUBLANES, CONV_WIDTH), F32),
            pltpu.VMEM((SUBLANES, LANES), F32),
        ],
        compiler_params=pltpu.CompilerParams(
            dimension_semantics=("arbitrary", "arbitrary"), vmem_limit_bytes=VMEM_LIMIT),
        name="mixer",
    )(x, ada_l, cos_t, sin_t, *[a for a, _ in lw], *tables)


def _run_pieces(max_rows):
    units = max_rows // RUN_ALIGN
    pieces = []
    bit = 1
    while bit <= units:
        pieces.append(bit)
        bit *= 2
    return tuple(reversed(pieces))


def _piece_table_width(n_classes):
    return -(-(n_classes * 2 * N_EXPERTS + n_classes) // LANES) * LANES


def _for_each_run_piece(tab_ref, pieces, make_copy, fn):
    for c, bit in enumerate(pieces):
        count = tab_ref[0, 0, len(pieces) * 2 * N_EXPERTS + c]

        def body(j, carry, c=c, bit=bit):
            hbm_row = tab_ref[0, 0, c * 2 * N_EXPERTS + j]
            buf_row = tab_ref[0, 0, (c * 2 + 1) * N_EXPERTS + j]
            fn(make_copy(pl.multiple_of(hbm_row, RUN_ALIGN), pl.multiple_of(buf_row, RUN_ALIGN),
                         bit * RUN_ALIGN))
            return carry

        lax.fori_loop(0, count, body, 0)


SEL_IDX, SEL_WEIGHT, SEL_RANK = 0, TOP_K, 2 * TOP_K
ROWS_PAD = SUBLANES


def _dispatch_kernel(pe_ref, ps_ref, tab_cur_ref, tab_prev_ref, rows_ref, h2_ref, xs_hbm,
                     zbuf, sbuf, sem, zsem):
    i = pl.program_id(0)
    n_steps = pl.num_programs(0)
    ts = h2_ref.shape[0]
    blk = zbuf.shape[0]
    n_exp = pe_ref.shape[0]
    run_rows = sbuf.shape[1]
    slot = i % 2
    pieces = _run_pieces(ts)

    @pl.when(i == 0)
    def _():
        zbuf[...] = jnp.zeros_like(zbuf)

        def tail_copy(e):
            start = pl.multiple_of(jnp.maximum(pe_ref[e] - blk, 0), blk)
            return pltpu.make_async_copy(zbuf, xs_hbm.at[pl.ds(start, blk), :], zsem)

        def zstart(e, carry):
            @pl.when(pe_ref[e] > ps_ref[e])
            def _():
                tail_copy(e).start()
            return carry

        def zwait(e, carry):
            @pl.when(pe_ref[e] > ps_ref[e])
            def _():
                tail_copy(e).wait()
            return carry

        lax.fori_loop(0, n_exp, zstart, 0)
        lax.fori_loop(0, n_exp, zwait, 0)

    rows_t = rows_ref[...]
    h2 = h2_ref[...].astype(MXU_DTYPE)
    n_parts = 2
    part = run_rows // n_parts

    def pick_rows(lo):
        buf_row = lax.broadcasted_iota(jnp.int32, (part, ts), 0).astype(F32) + float(lo)
        pick = jnp.zeros((part, ts), F32)
        for kk in range(TOP_K):
            pick = jnp.where(buf_row == rows_t[kk:kk + 1], 1.0, pick)
        return pick.astype(MXU_DTYPE)

    pick = pick_rows(0)
    for p in range(n_parts):
        nxt = pick_rows((p + 1) * part) if p + 1 < n_parts else None
        sbuf[slot, p * part:(p + 1) * part, :] = jnp.dot(
            pick, h2, preferred_element_type=F32).astype(sbuf.dtype)
        pick = nxt

    def push(dst_slot):
        def make(hbm_row, buf_row_, rows):
            return pltpu.make_async_copy(sbuf.at[dst_slot, pl.ds(buf_row_, rows), :],
                                         xs_hbm.at[pl.ds(hbm_row, rows), :], sem.at[dst_slot])
        return make

    _for_each_run_piece(tab_cur_ref, pieces, push(slot), lambda c: c.start())

    @pl.when(i > 0)
    def _():
        _for_each_run_piece(tab_prev_ref, pieces, push(1 - slot), lambda c: c.wait())

    @pl.when(i == n_steps - 1)
    def _():
        _for_each_run_piece(tab_cur_ref, pieces, push(slot), lambda c: c.wait())


def _dispatch_call(pad_end, pad_start, tab3, rows8, h2, n_rows, blk):
    n_tok, d = h2.shape
    n_steps = tab3.shape[0]
    ts = n_tok // n_steps
    run_rows = ts * TOP_K + N_EXPERTS * RUN_ALIGN
    grid_spec = pltpu.PrefetchScalarGridSpec(
        num_scalar_prefetch=2,
        grid=(n_steps,),
        in_specs=[
            pl.BlockSpec((1, 1, tab3.shape[2]), lambda i, pe, ps: (i, 0, 0), memory_space=pltpu.SMEM),
            pl.BlockSpec((1, 1, tab3.shape[2]), lambda i, pe, ps: (jnp.maximum(i - 1, 0), 0, 0),
                         memory_space=pltpu.SMEM),
            pl.BlockSpec((ROWS_PAD, ts), lambda i, pe, ps: (0, i)),
            pl.BlockSpec((ts, d), lambda i, pe, ps: (i, 0)),
        ],
        out_specs=pl.BlockSpec(memory_space=pl.ANY),
        scratch_shapes=[pltpu.VMEM((blk, d), MXU_DTYPE), pltpu.VMEM((2, run_rows, d), MXU_DTYPE),
                        pltpu.SemaphoreType.DMA((2,)), pltpu.SemaphoreType.DMA(())],
    )
    return pl.pallas_call(
        _dispatch_kernel,
        grid_spec=grid_spec,
        out_shape=jax.ShapeDtypeStruct((n_rows, d), MXU_DTYPE),
        compiler_params=pltpu.CompilerParams(
            dimension_semantics=("arbitrary",), vmem_limit_bytes=VMEM_LIMIT),
        name="dispatch",
    )(pad_end, pad_start, tab3, tab3, rows8, h2)


def _expert_up(xs, wgu, bgu_ref):
    return jnp.dot(xs, wgu, preferred_element_type=F32) + bgu_ref[...]


def _expert_down(gu, wdn, bdn_ref):
    gate = jnp.minimum(gu[:, :D_FF], SWIGLU_LIMIT)
    up = jnp.clip(gu[:, D_FF:], -SWIGLU_LIMIT, SWIGLU_LIMIT)
    act = (up + 1.0) * (gate * jax.nn.sigmoid(SWIGLU_ALPHA * gate))
    return jnp.dot(act.astype(MXU_DTYPE), wdn, preferred_element_type=F32) + bdn_ref[...]


def _expert_kernel(be_ref, nu_ref, xs_ref, wgu_ref, bgu_ref, wdn_ref, bdn_ref, ys_ref):
    i = pl.program_id(0)
    n_used = nu_ref[0]

    @pl.when(i < n_used)
    def _():
        part = min(EXPERT_PART, xs_ref.shape[0])
        n_parts = xs_ref.shape[0] // part
        rows = [slice(p * part, (p + 1) * part) for p in range(n_parts)]
        wgu = wgu_ref[...].astype(MXU_DTYPE)
        wdn = wdn_ref[...].astype(MXU_DTYPE)
        gu = _expert_up(xs_ref[rows[0], :], wgu, bgu_ref)
        for p in range(n_parts):
            gu_next = _expert_up(xs_ref[rows[p + 1], :], wgu, bgu_ref) if p + 1 < n_parts else None
            ys_ref[rows[p], :] = _expert_down(gu, wdn, bdn_ref).astype(ys_ref.dtype)
            gu = gu_next

    @pl.when(i >= n_used)
    def _():
        ys_ref[...] = jnp.zeros_like(ys_ref)


def _expert_call(block_e, n_used, xs, w_gu, b_gu, w_down, b_down, blk, layer):
    n_rows, d = xs.shape
    n_blocks = n_rows // blk
    two_f = w_gu.shape[3]
    last = lambda nu: jnp.maximum(nu[0] - 1, 0)
    of_block = lambda i, be, nu: (layer, be[i], 0, 0)
    grid_spec = pltpu.PrefetchScalarGridSpec(
        num_scalar_prefetch=2,
        grid=(n_blocks,),
        in_specs=[
            pl.BlockSpec((blk, d), lambda i, be, nu: (jnp.minimum(i, last(nu)), 0)),
            pl.BlockSpec((None, None, d, two_f), of_block),
            pl.BlockSpec((None, None, 1, two_f), of_block),
            pl.BlockSpec((None, None, two_f // 2, d), of_block),
            pl.BlockSpec((None, None, 1, d), of_block),
        ],
        out_specs=pl.BlockSpec((blk, d), lambda i, be, nu: (i, 0)),
    )
    return pl.pallas_call(
        _expert_kernel,
        grid_spec=grid_spec,
        out_shape=jax.ShapeDtypeStruct((n_rows, d), MXU_DTYPE),
        compiler_params=pltpu.CompilerParams(
            dimension_semantics=("arbitrary",), vmem_limit_bytes=VMEM_LIMIT),
        name="experts",
    )(block_e, n_used, xs, w_gu, b_gu, w_down, b_down)


def _combine_kernel(tab_cur_ref, tab_nxt_ref, ys_hbm, x1_ref, sel_ref, rows_ref, ada_ref, lng_ref, lnb_ref,
                    o_ref, rbuf, sem):
    i = pl.program_id(0)
    n_steps = pl.num_programs(0)
    ts = x1_ref.shape[0]
    slot = i % 2
    pieces = _run_pieces(ts)

    def fetch(dst_slot):
        def make(hbm_row, buf_row, rows):
            return pltpu.make_async_copy(ys_hbm.at[pl.ds(hbm_row, rows), :],
                                         rbuf.at[dst_slot, pl.ds(buf_row, rows), :], sem.at[dst_slot])
        return make

    @pl.when(i == 0)
    def _():
        rbuf[...] = jnp.zeros_like(rbuf)
        _for_each_run_piece(tab_cur_ref, pieces, fetch(0), lambda c: c.start())

    @pl.when(i + 1 < n_steps)
    def _():
        _for_each_run_piece(tab_nxt_ref, pieces, fetch(1 - slot), lambda c: c.start())

    _for_each_run_piece(tab_cur_ref, pieces, fetch(slot), lambda c: c.wait())

    sel = sel_ref[...]
    rows_t = rows_ref[...]
    rows_c = jnp.concatenate([rows_t, jnp.zeros((LANES - ROWS_PAD, ts), F32)], axis=0).T
    col = lax.broadcasted_iota(jnp.int32, (1, rbuf.shape[1]), 1).astype(F32)
    gt2 = ada_ref[...][5:6]
    part = min(COMBINE_PART, ts)

    def weight_pieces(tok):
        weights = jnp.zeros((part, rbuf.shape[1]), F32)
        for kk in range(TOP_K):
            weights = jnp.where(col == rows_c[tok, kk:kk + 1],
                                sel[tok, SEL_WEIGHT + kk:SEL_WEIGHT + kk + 1], weights)
        hi = weights.astype(MXU_DTYPE)
        lo = (weights - hi.astype(F32)).astype(MXU_DTYPE)
        return jnp.concatenate([hi, lo], axis=0)

    toks = [slice(p * part, (p + 1) * part) for p in range(ts // part)]
    pieces_w = weight_pieces(toks[0])
    for p, tok in enumerate(toks):
        nxt = weight_pieces(toks[p + 1]) if p + 1 < len(toks) else None
        both = jnp.dot(pieces_w, rbuf[slot], preferred_element_type=F32)
        ffn = both[:part] + both[part:]
        o_ref[tok, :] = _layernorm_rows(DEEPNORM_ALPHA * x1_ref[tok, :] + (1.0 + gt2) * ffn,
                                        lng_ref[...], lnb_ref[...])
        pieces_w = nxt


def _combine_call(tab3, ys, x1, sel, rows8, ada_l, ln_g, ln_b, seq):
    n_tok, d = x1.shape
    n_steps = tab3.shape[0]
    ts = n_tok // n_steps
    tiles_per_seq = seq // ts
    run_rows = ts * TOP_K + N_EXPERTS * RUN_ALIGN
    return pl.pallas_call(
        _combine_kernel,
        grid=(n_steps,),
        in_specs=[
            pl.BlockSpec((1, 1, tab3.shape[2]), lambda i: (i, 0, 0), memory_space=pltpu.SMEM),
            pl.BlockSpec((1, 1, tab3.shape[2]), lambda i: (jnp.minimum(i + 1, n_steps - 1), 0, 0),
                         memory_space=pltpu.SMEM),
            pl.BlockSpec(memory_space=pl.ANY),
            pl.BlockSpec((ts, d), lambda i: (i, 0)),
            pl.BlockSpec((ts, LANES), lambda i: (i, 0)),
            pl.BlockSpec((ROWS_PAD, ts), lambda i: (0, i)),
            pl.BlockSpec((None, 6, d), lambda i: (i // tiles_per_seq, 0, 0)),
            pl.BlockSpec((1, d), lambda i: (0, 0)),
            pl.BlockSpec((1, d), lambda i: (0, 0)),
        ],
        out_specs=pl.BlockSpec((ts, d), lambda i: (i, 0)),
        out_shape=jax.ShapeDtypeStruct((n_tok, d), F32),
        scratch_shapes=[pltpu.VMEM((2, run_rows, d), MXU_DTYPE), pltpu.SemaphoreType.DMA((2,))],
        compiler_params=pltpu.CompilerParams(
            dimension_semantics=("arbitrary",), vmem_limit_bytes=VMEM_LIMIT),
        name="combine",
    )(tab3, tab3, ys, x1, sel, rows8, ada_l, ln_g.reshape(1, d), ln_b.reshape(1, d))


def _retention_tables():
    log_gamma = jnp.log(1.0 - 2.0 ** (-5.0 - jnp.arange(RET_HEADS, dtype=F32)))
    idx = jnp.arange(CHUNK, dtype=F32)
    diff = idx[:, None] - idx[None, :]
    decay = jnp.where(diff[None] >= 0,
                      jnp.exp(log_gamma[:, None, None] * jnp.maximum(diff, 0.0)[None]), 0.0)
    decay = decay.reshape(RET_HEADS * CHUNK, CHUNK)
    zeta = jnp.exp(log_gamma[:, None] * (CHUNK - 1.0 - idx)[None])
    zeta_t = jnp.repeat(zeta.T, RET_QK_DIM, axis=1)
    xi = jnp.exp(log_gamma[:, None] * (idx + 1.0)[None]).T
    xi_t = jnp.repeat(xi, RET_V_DIM, axis=1)
    cdec = jnp.repeat(jnp.exp(log_gamma * CHUNK), RET_V_DIM)[None, :]
    row_head = jnp.arange(RET_QK_WIDTH) // RET_QK_DIM
    col_head = jnp.arange(RET_V_WIDTH) // RET_V_DIM
    same_head = row_head[:, None] == col_head[None, :]
    bmask = same_head.astype(F32)
    gavg = ((col_head[:, None] == col_head[None, :]).astype(F32) / RET_V_DIM).astype(MXU_DTYPE)
    return decay, zeta_t, xi_t, cdec, bmask, gavg


def _rotary_tables(positions):
    inv_freq = ROPE_BASE ** (-jnp.arange(0, RET_QK_DIM, 2, dtype=F32) / RET_QK_DIM)
    ang = positions.astype(F32)[..., None] * inv_freq
    cos = jnp.cos(ang)
    sin = jnp.sin(ang)
    cos_t = jnp.tile(jnp.concatenate([cos, cos], axis=-1), (1, 1, RET_HEADS))
    sin_t = jnp.tile(jnp.concatenate([-sin, sin], axis=-1), (1, 1, RET_HEADS))
    return cos_t, sin_t


def _moe_rows(n_tok, n_tiles, blk):
    worst = n_tok * TOP_K + n_tiles * N_EXPERTS * (RUN_ALIGN - 1)
    return (-(-worst // blk) + N_EXPERTS) * blk


def _routing_tables(sel_t, counts, before, n_rows, blk, pieces):
    n_blocks = n_rows // blk
    sizes = counts.astype(jnp.int32)
    pad_sizes = (sizes + blk - 1) // blk * blk
    pad_end = jnp.cumsum(pad_sizes)
    pad_start = pad_end - pad_sizes
    n_used = (pad_end[-1] // blk).astype(jnp.int32)
    block_start = jnp.arange(n_blocks, dtype=jnp.int32) * blk
    block_start = jnp.minimum(block_start, (n_used - 1) * blk)
    block_e = jnp.sum((pad_end[None, :] <= block_start[:, None]).astype(jnp.int32), axis=-1)
    block_e = jnp.minimum(block_e, N_EXPERTS - 1)

    before = before.astype(jnp.int32)
    after = jnp.concatenate([before[1:], sizes[None, :]], axis=0)
    run_start = pad_start[None, :] + before
    run_units = (after - before) // RUN_ALIGN
    run_off = (jnp.cumsum(run_units, axis=-1) - run_units) * RUN_ALIGN
    n_tiles = before.shape[0]
    slot_id = jnp.arange(N_EXPERTS, dtype=jnp.int32)
    cols, counts_c = [], []
    for bit in pieces:
        has = jnp.bitwise_and(run_units, bit) != 0
        done = jnp.bitwise_and(run_units, -2 * bit) * RUN_ALIGN
        pos = jnp.cumsum(has.astype(jnp.int32), axis=-1) - 1
        place = jnp.logical_and(has[:, None, :], pos[:, None, :] == slot_id[None, :, None])
        cols.append(jnp.sum(jnp.where(place, (run_start + done)[:, None, :], 0), axis=-1))
        cols.append(jnp.sum(jnp.where(place, (run_off + done)[:, None, :], 0), axis=-1))
        counts_c.append(jnp.sum(has.astype(jnp.int32), axis=-1, keepdims=True))
    width = _piece_table_width(len(pieces))
    used = len(pieces) * (2 * N_EXPERTS + 1)
    tab = jnp.concatenate(cols + counts_c + [jnp.zeros((n_tiles, width - used), jnp.int32)], axis=-1)

    n_tok = sel_t.shape[1]
    idx = sel_t[SEL_IDX:SEL_IDX + TOP_K].astype(jnp.int32)
    off_tok = jnp.repeat(run_off.T, n_tok // n_tiles, axis=1)
    base = jnp.sum(jnp.where(idx[:, None, :] == slot_id[None, :, None], off_tok[None], 0), axis=1)
    rows = base.astype(F32) + sel_t[SEL_RANK:SEL_RANK + TOP_K]
    rows8 = jnp.pad(rows, ((0, ROWS_PAD - TOP_K), (0, 0)))
    return (pad_end.astype(jnp.int32), pad_start.astype(jnp.int32), block_e, n_used.reshape(1),
            tab.astype(jnp.int32).reshape(n_tiles, 1, width), rows8)


def kernel(x, c, positions, w_ada, b_ada, w_in, gmlp_ln_g, gmlp_ln_b, w_spatial, b_spatial, conv_w,
           w_branch, w_gate_up, b_gate, w_out, ln1_g, ln1_b, w_router, b_router, w_gu, b_gu, w_down,
           b_down, ln2_g, ln2_b):
    bsz, seq, d = x.shape
    n_tok = bsz * seq
    depth = w_ada.shape[0]
    blk = min(MOE_BLOCK, n_tok * TOP_K // N_EXPERTS)
    ts = min(SEQ_TILE, seq)
    assert depth == DEPTH and w_router.shape[-1] == N_EXPERTS and w_in.shape[-1] == _IN_WIDTH
    assert ts % CHUNK == 0 and seq % min(MIXER_STEP, seq) == 0 and min(MIXER_STEP, seq) % ts == 0
    assert blk % RUN_ALIGN == 0 and d % LANES == 0

    ada = _ada_call(c, w_ada, b_ada)
    ada = jnp.transpose(ada, (0, 2, 1, 3))
    cos_t, sin_t = _rotary_tables(positions)
    tables = _retention_tables()

    mx = MXU_DTYPE
    w_in_m = w_in.astype(mx)
    w_branch_m = w_branch.astype(mx)
    w_gate_m = w_gate_up.astype(mx)
    w_out_m = w_out.astype(mx)
    b_gu4 = b_gu[:, :, None, :]
    b_down4 = b_down[:, :, None, :]
    n_exp = w_router.shape[-1]
    w_router_m = jnp.pad(w_router, ((0, 0), (0, 0), (0, LANES - n_exp))).astype(mx)
    b_router_p = jnp.pad(b_router, ((0, 0), (0, LANES - n_exp)), constant_values=NEG_BIG)

    for l in range(depth):
        bsp_t = jnp.repeat(b_spatial[l].T, GMLP_WIDTH // GMLP_GROUPS, axis=1)
        lw = (
            (w_in_m, l),
            (gmlp_ln_g[l].reshape(1, -1), None), (gmlp_ln_b[l].reshape(1, -1), None),
            (w_spatial[l].reshape(GMLP_GROUPS * CHUNK, CHUNK), None), (bsp_t, None),
            (jnp.pad(conv_w[l], ((0, SUBLANES - conv_w.shape[1]), (0, 0))), None),
            (w_branch_m, l), (w_gate_m, l), (b_gate, l), (w_out_m, l),
            (ln1_g[l].reshape(1, -1), None), (ln1_b[l].reshape(1, -1), None),
            (w_router_m, l), (b_router_p[l].reshape(1, -1), None),
        )
        x1, h2, sel, sel_t, cnt, before = _mixer_call(x, ada[l], cos_t, sin_t, lw, tables)
        n_rows = _moe_rows(n_tok, n_tok // ts, blk)
        pad_end, pad_start, block_e, n_used, tab3, rows8 = _routing_tables(
            sel_t, cnt[0, :n_exp], before[:, 0, :n_exp], n_rows, blk, _run_pieces(ts))
        xs = _dispatch_call(pad_end, pad_start, tab3, rows8, h2, n_rows, blk)
        ys = _expert_call(block_e, n_used, xs, w_gu, b_gu4, w_down, b_down4, blk, l)
        x = _combine_call(tab3, ys, x1, sel, rows8, ada[l], ln2_g[l], ln2_b[l], seq).reshape(bsz, seq, d)
    return x
```

```python
import jax
import jax.numpy as jnp
from jax import lax
from jax.experimental import pallas as pl
from jax.experimental.pallas import tpu as pltpu

F32 = jnp.float32
MXU_DTYPE = jnp.bfloat16

DEPTH = 4
RET_HEADS = 4
RET_QK_DIM = 32
RET_V_DIM = 64
RET_QK_WIDTH = RET_HEADS * RET_QK_DIM
RET_V_WIDTH = RET_HEADS * RET_V_DIM
CHUNK = 128
ROPE_BASE = 10000.0
GMLP_GROUPS = 4
GMLP_WIDTH = 256
CONV_WIDTH = 256
GATE_RANK = 128
N_BRANCH = 3
N_EXPERTS = 32
TOP_K = 4
D_FF = 256
SWIGLU_LIMIT = 7.0
SWIGLU_ALPHA = 1.702
DEEPNORM_ALPHA = (2.0 * DEPTH) ** 0.25
LN_EPS = 1e-5

_O_QK = 0
_O_VG = 2 * RET_QK_WIDTH
_O_GMLP = _O_VG + 2 * RET_V_WIDTH
_O_CONV = _O_GMLP + 2 * GMLP_WIDTH
_O_CODE = _O_CONV + 3 * CONV_WIDTH
_IN_WIDTH = _O_CODE + N_BRANCH * GATE_RANK

LANES = 128
SUBLANES = 8
NEG_BIG = -1e30

SEQ_TILE = 256
MIXER_STEP = 512
MOE_BLOCK = 2048
EXPERT_PART = 512
COMBINE_PART = 128
SELT_ROWS = 16
RUN_ALIGN = 16
VMEM_LIMIT = 56 * 1024 * 1024


def _dot(a, b):
    return jnp.dot(a.astype(MXU_DTYPE), b.astype(MXU_DTYPE), preferred_element_type=F32)


def _dot_nt(a, b):
    return lax.dot_general(a.astype(MXU_DTYPE), b.astype(MXU_DTYPE),
                           (((1,), (1,)), ((), ())), preferred_element_type=F32)


def _split_dot(x, w):
    hi = x.astype(MXU_DTYPE)
    lo = (x - hi.astype(F32)).astype(MXU_DTYPE)
    return (jnp.dot(hi, w, preferred_element_type=F32)
            + jnp.dot(lo, w, preferred_element_type=F32))


def _layernorm_rows(x, g, b):
    mu = jnp.mean(x, axis=-1, keepdims=True)
    d = x - mu
    var = jnp.mean(d * d, axis=-1, keepdims=True)
    return d * lax.rsqrt(var + LN_EPS) * g + b


def _ada_kernel(c_ref, w_ref, b_ref, o_ref):
    c_act = jax.nn.silu(c_ref[...])
    o_ref[...] = jnp.dot(c_act, w_ref[...], preferred_element_type=F32,
                         precision=lax.Precision.HIGHEST) + b_ref[...]


def _ada_call(c, w_ada, b_ada):
    depth, d, six_d = w_ada.shape
    bsz = c.shape[0]
    n_col = six_d // d
    return pl.pallas_call(
        _ada_kernel,
        grid=(depth, n_col),
        in_specs=[
            pl.BlockSpec((bsz, d), lambda l, j: (0, 0)),
            pl.BlockSpec((None, d, d), lambda l, j: (l, 0, j)),
            pl.BlockSpec((None, 1, d), lambda l, j: (l, 0, j)),
        ],
        out_specs=pl.BlockSpec((None, None, bsz, d), lambda l, j: (l, j, 0, 0)),
        out_shape=jax.ShapeDtypeStruct((depth, n_col, bsz, d), F32),
        compiler_params=pltpu.CompilerParams(
            dimension_semantics=("arbitrary", "arbitrary"), vmem_limit_bytes=VMEM_LIMIT),
        name="ada_ln",
    )(c, w_ada, b_ada.reshape(depth, 1, six_d))


_MIXER_PHASES = 3


def _mixer_kernel(x_ref, ada_ref, cos_ref, sin_ref, *rest):
    params = rest[:-9]
    x1_ref, h2_ref, sel_ref, selt_ref, cnt_ref, before_ref, state_ref, zc_ref, cntacc_ref = rest[-9:]
    b = pl.program_id(0)
    s = pl.program_id(1)

    @pl.when(s == 0)
    def _():
        state_ref[...] = jnp.zeros_like(state_ref)
        zc_ref[...] = jnp.zeros_like(zc_ref)

    @pl.when(jnp.logical_and(b == 0, s == 0))
    def _():
        cntacc_ref[...] = jnp.zeros_like(cntacc_ref)

    n_sub = before_ref.shape[0]
    ts = x_ref.shape[0] // n_sub
    tiles = []
    for u in range(n_sub):
        rows = pl.ds(u * ts, ts)
        tiles.append(_mixer_tile(x_ref.at[rows], ada_ref, cos_ref.at[rows], sin_ref.at[rows], *params,
                                 x1_ref.at[rows], h2_ref.at[rows], sel_ref.at[rows], selt_ref.at[:, rows],
                                 cnt_ref, before_ref.at[u], state_ref, zc_ref, cntacc_ref))
    for _ in range(_MIXER_PHASES):
        for tile in tiles:
            next(tile, None)


def _mixer_tile(x_ref, ada_ref, cos_ref, sin_ref, w_in_ref, lng_ref, lnb_ref, wsp_ref, bsp_ref,
                convw_ref, wbr_ref, wgt_ref, bgt_ref, wout_ref, ln1g_ref, ln1b_ref, wr_ref, br_ref,
                decay_ref, zeta_ref, xi_ref, cdec_ref, bmask_ref, gavg_ref,
                x1_ref, h2_ref, sel_ref, selt_ref, cnt_ref, before_ref,
                state_ref, zc_ref, cntacc_ref):
    ts = x_ref.shape[0]
    x = x_ref[...]
    ada = ada_ref[...]
    sh1, sc1, gt1, sh2, sc2 = ada[0:1], ada[1:2], ada[2:3], ada[3:4], ada[4:5]
    h = (x * (1.0 + sc1) + sh1).astype(MXU_DTYPE)

    def proj(lo, hi):
        return jnp.dot(h, w_in_ref[:, lo:hi], preferred_element_type=F32)

    lane_qk = lax.broadcasted_iota(jnp.int32, (1, RET_QK_WIDTH), 1)
    lane_v = lax.broadcasted_iota(jnp.int32, (1, RET_V_WIDTH), 1)
    qk_masks = [lane_qk // RET_QK_DIM == hd for hd in range(RET_HEADS)]
    v_masks = [lane_v // RET_V_DIM == hd for hd in range(RET_HEADS)]

    qk = proj(_O_QK, _O_VG)
    cos = cos_ref[...]
    sin = sin_ref[...]
    first_half = (lane_qk % RET_QK_DIM) < (RET_QK_DIM // 2)

    def rotary(t):
        swapped = jnp.where(first_half, pltpu.roll(t, RET_QK_WIDTH - RET_QK_DIM // 2, 1),
                            pltpu.roll(t, RET_QK_DIM // 2, 1))
        return t * cos + swapped * sin

    q = rotary(qk[:, :RET_QK_WIDTH])
    k = rotary(qk[:, RET_QK_WIDTH:]) * (RET_QK_DIM ** -0.5)
    vg = proj(_O_VG, _O_GMLP)
    v = vg[:, :RET_V_WIDTH]
    g = vg[:, RET_V_WIDTH:]

    decay = decay_ref[...]
    zeta = zeta_ref[...]
    xi = xi_ref[...]
    cdec = cdec_ref[...]
    bmask = bmask_ref[...]
    gavg = gavg_ref[...]

    o_chunks = []
    state = state_ref[...]
    for c in range(ts // CHUNK):
        rows = slice(c * CHUNK, (c + 1) * CHUNK)
        q_c, k_c, v_c = q[rows], k[rows], v[rows]
        v_m = v_c.astype(MXU_DTYPE)
        q_all = jnp.concatenate([jnp.where(m, q_c, 0.0) for m in qk_masks], axis=0)
        scores = _dot_nt(q_all, k_c) * decay
        o_all = _dot(scores, v_m)
        o = _dot(q_c, state) * xi
        for hd in range(RET_HEADS):
            o = o + jnp.where(v_masks[hd], o_all[hd * CHUNK:(hd + 1) * CHUNK], 0.0)
        kv = _dot((k_c * zeta).T, v_m) * bmask
        state = state * cdec + kv
        o_chunks.append(o)
    state_ref[...] = state
    o = jnp.concatenate(o_chunks, axis=0) if len(o_chunks) > 1 else o_chunks[0]
    mu = _split_dot(o, gavg)
    d = o - mu
    var = _split_dot(d * d, gavg)
    r_br = jax.nn.silu(g) * (d * lax.rsqrt(var + LN_EPS))

    guv = proj(_O_GMLP, _O_CONV)
    u = jax.nn.gelu(guv[:, :GMLP_WIDTH])
    vv = _layernorm_rows(jax.nn.gelu(guv[:, GMLP_WIDTH:]), lng_ref[...], lnb_ref[...]).astype(MXU_DTYPE)
    rr = lax.broadcasted_iota(jnp.int32, (GMLP_GROUPS * CHUNK, CHUNK), 0) % CHUNK
    cc_ = lax.broadcasted_iota(jnp.int32, (GMLP_GROUPS * CHUNK, CHUNK), 1)
    w_sp = jnp.where(cc_ <= rr, wsp_ref[...], 0.0).astype(MXU_DTYPE)
    bsp = bsp_ref[...]
    z_chunks = []
    for c in range(ts // CHUNK):
        z_all = jnp.dot(w_sp, vv[c * CHUNK:(c + 1) * CHUNK], preferred_element_type=F32)
        z = bsp
        for gi in range(GMLP_GROUPS):
            z = z + jnp.where(v_masks[gi], z_all[gi * CHUNK:(gi + 1) * CHUNK], 0.0)
        z_chunks.append(z)
    z = jnp.concatenate(z_chunks, axis=0) if len(z_chunks) > 1 else z_chunks[0]
    s_br = u * z

    cbcx = proj(_O_CONV, _O_CODE)
    gate_b = cbcx[:, :CONV_WIDTH]
    zc = cbcx[:, CONV_WIDTH:2 * CONV_WIDTH] * cbcx[:, 2 * CONV_WIDTH:]
    carry = zc_ref[...]
    prev1 = carry[SUBLANES - 1:SUBLANES]
    prev2 = carry[SUBLANES - 2:SUBLANES - 1]
    row = lax.broadcasted_iota(jnp.int32, (ts, 1), 0)
    z1 = jnp.where(row == 0, prev1, pltpu.roll(zc, 1, 0))
    z2 = jnp.where(row == 0, prev2, jnp.where(row == 1, prev1, pltpu.roll(zc, 2, 0)))
    zc_ref[...] = zc[ts - SUBLANES:ts]
    cw = convw_ref[...]
    k_br = gate_b * (cw[0:1] * z2 + cw[1:2] * z1 + cw[2:3] * zc)

    code = proj(_O_CODE, _IN_WIDTH)
    yield
    merged = None
    for i, br in enumerate((r_br, s_br, k_br)):
        y = jnp.dot(br.astype(MXU_DTYPE), wbr_ref[i], preferred_element_type=F32)
        gl = jnp.dot(code[:, i * GATE_RANK:(i + 1) * GATE_RANK].astype(MXU_DTYPE), wgt_ref[i],
                     preferred_element_type=F32) + bgt_ref[i:i + 1]
        t = jax.nn.sigmoid(gl) * y
        merged = t if merged is None else merged + t
    mix = jnp.dot(merged.astype(MXU_DTYPE), wout_ref[...], preferred_element_type=F32)
    x1 = _layernorm_rows(DEEPNORM_ALPHA * x + (1.0 + gt1) * mix, ln1g_ref[...], ln1b_ref[...])
    x1_ref[...] = x1
    h2 = (x1 * (1.0 + sc2) + sh2).astype(MXU_DTYPE)
    h2_ref[...] = h2

    logits = jnp.dot(h2, wr_ref[...], preferred_element_type=F32) + br_ref[...]
    yield
    lane = lax.broadcasted_iota(jnp.int32, (1, LANES), 1).astype(F32)
    top_v, top_i = [], []
    work = logits
    for _ in range(TOP_K):
        m = jnp.max(work, axis=-1, keepdims=True)
        i_sel = jnp.min(jnp.where(work == m, lane, float(LANES)), axis=-1, keepdims=True)
        top_v.append(m)
        top_i.append(i_sel)
        work = jnp.where(lane == i_sel, -jnp.inf, work)
    exps = [jnp.exp(tv - top_v[0]) for tv in top_v]
    denom = exps[0] + exps[1] + exps[2] + exps[3]
    onehots = [lane == ti for ti in top_i]
    member = jnp.zeros((ts, LANES), F32)
    for oh in onehots:
        member = member + jnp.where(oh, 1.0, 0.0)
    tr = lax.broadcasted_iota(jnp.int32, (ts, ts), 0)
    tc = lax.broadcasted_iota(jnp.int32, (ts, ts), 1)
    before = jnp.where(tc < tr, 1.0, 0.0).astype(MXU_DTYPE)
    cnt = cntacc_ref[...]
    local_rank = jnp.dot(before, member.astype(MXU_DTYPE), preferred_element_type=F32)
    sel = jnp.zeros((ts, LANES), F32)
    for kk in range(TOP_K):
        lrank_k = jnp.sum(jnp.where(onehots[kk], local_rank, 0.0), axis=-1, keepdims=True)
        sel = sel + jnp.where(lane == float(kk), top_i[kk], 0.0)
        sel = sel + jnp.where(lane == float(TOP_K + kk), exps[kk] / denom, 0.0)
        sel = sel + jnp.where(lane == float(2 * TOP_K + kk), lrank_k, 0.0)
    sel_ref[...] = sel
    selt_ref[...] = sel.T[:selt_ref.shape[0]]
    before_ref[...] = cnt
    tile_cnt = jnp.sum(member, axis=0, keepdims=True)
    cnt = cnt + jnp.floor((tile_cnt + (RUN_ALIGN - 1.0)) * (1.0 / RUN_ALIGN)) * RUN_ALIGN
    cntacc_ref[...] = cnt
    cnt_ref[...] = cnt


def _const_spec(shape):
    return pl.BlockSpec(shape, lambda b, s: (0,) * len(shape))


def _param_spec(a, layer):
    if layer is None:
        return _const_spec(a.shape)
    return pl.BlockSpec((None,) + a.shape[1:], lambda b, s: (layer,) + (0,) * (a.ndim - 1))


def _mixer_call(x, ada_l, cos_t, sin_t, lw, tables):
    bsz, seq, d = x.shape
    ts = min(SEQ_TILE, seq)
    step = min(MIXER_STEP, seq)
    n_tok = bsz * seq
    row3 = lambda b, s: (b, s, 0)
    in_specs = [
        pl.BlockSpec((None, step, d), row3),
        pl.BlockSpec((None, 6, d), lambda b, s: (b, 0, 0)),
        pl.BlockSpec((None, step, RET_QK_WIDTH), row3),
        pl.BlockSpec((None, step, RET_QK_WIDTH), row3),
    ] + [_param_spec(a, lay) for a, lay in lw] + [_const_spec(a.shape) for a in tables]
    tok_row = lambda b, s: (b * (seq // step) + s, 0)
    out_specs = [
        pl.BlockSpec((step, d), tok_row),
        pl.BlockSpec((step, d), tok_row),
        pl.BlockSpec((step, LANES), tok_row),
        pl.BlockSpec((SELT_ROWS, step), lambda b, s: (0, b * (seq // step) + s)),
        pl.BlockSpec((SUBLANES, LANES), lambda b, s: (0, 0)),
        pl.BlockSpec((step // ts, SUBLANES, LANES), lambda b, s: (b * (seq // step) + s, 0, 0)),
    ]
    out_shape = [
        jax.ShapeDtypeStruct((n_tok, d), F32),
        jax.ShapeDtypeStruct((n_tok, d), MXU_DTYPE),
        jax.ShapeDtypeStruct((n_tok, LANES), F32),
        jax.ShapeDtypeStruct((SELT_ROWS, n_tok), F32),
        jax.ShapeDtypeStruct((SUBLANES, LANES), F32),
        jax.ShapeDtypeStruct((n_tok // ts, SUBLANES, LANES), F32),
    ]
    return pl.pallas_call(
        _mixer_kernel,
        grid=(bsz, seq // step),
        in_specs=in_specs,
        out_specs=out_specs,
        out_shape=out_shape,
        scratch_shapes=[
            pltpu.VMEM((RET_QK_WIDTH, RET_V_WIDTH), F32),
            pltpu.VMEM((SUBLANES, CONV_WIDTH), F32),
            pltpu.VMEM((SUBLANES, LANES), F32),
        ],
        compiler_params=pltpu.CompilerParams(
            dimension_semantics=("arbitrary", "arbitrary"), vmem_limit_bytes=VMEM_LIMIT),
        name="mixer",
    )(x, ada_l, cos_t, sin_t, *[a for a, _ in lw], *tables)


def _run_pieces(max_rows):
    units = max_rows // RUN_ALIGN
    pieces = []
    bit = 1
    while bit <= units:
        pieces.append(bit)
        bit *= 2
    return tuple(reversed(pieces))


def _piece_table_width(n_classes):
    return -(-(n_classes * 2 * N_EXPERTS + n_classes) // LANES) * LANES


def _for_each_run_piece(tab_ref, pieces, make_copy, fn):
    for c, bit in enumerate(pieces):
        count = tab_ref[0, 0, len(pieces) * 2 * N_EXPERTS + c]

        def body(j, carry, c=c, bit=bit):
            hbm_row = tab_ref[0, 0, c * 2 * N_EXPERTS + j]
            buf_row = tab_ref[0, 0, (c * 2 + 1) * N_EXPERTS + j]
            fn(make_copy(pl.multiple_of(hbm_row, RUN_ALIGN), pl.multiple_of(buf_row, RUN_ALIGN),
                         bit * RUN_ALIGN))
            return carry

        lax.fori_loop(0, count, body, 0)


SEL_IDX, SEL_WEIGHT, SEL_RANK = 0, TOP_K, 2 * TOP_K
ROWS_PAD = SUBLANES


def _dispatch_kernel(pe_ref, ps_ref, tab_cur_ref, tab_prev_ref, rows_ref, h2_ref, xs_hbm,
                     zbuf, sbuf, sem, zsem):
    i = pl.program_id(0)
    n_steps = pl.num_programs(0)
    ts = h2_ref.shape[0]
    blk = zbuf.shape[0]
    n_exp = pe_ref.shape[0]
    run_rows = sbuf.shape[1]
    slot = i % 2
    pieces = _run_pieces(ts)

    @pl.when(i == 0)
    def _():
        zbuf[...] = jnp.zeros_like(zbuf)

        def tail_copy(e):
            start = pl.multiple_of(jnp.maximum(pe_ref[e] - blk, 0), blk)
            return pltpu.make_async_copy(zbuf, xs_hbm.at[pl.ds(start, blk), :], zsem)

        def zstart(e, carry):
            @pl.when(pe_ref[e] > ps_ref[e])
            def _():
                tail_copy(e).start()
            return carry

        def zwait(e, carry):
            @pl.when(pe_ref[e] > ps_ref[e])
            def _():
                tail_copy(e).wait()
            return carry

        lax.fori_loop(0, n_exp, zstart, 0)
        lax.fori_loop(0, n_exp, zwait, 0)

    rows_t = rows_ref[...]
    h2 = h2_ref[...].astype(MXU_DTYPE)
    n_parts = 2
    part = run_rows // n_parts

    def pick_rows(lo):
        buf_row = lax.broadcasted_iota(jnp.int32, (part, ts), 0).astype(F32) + float(lo)
        pick = jnp.zeros((part, ts), F32)
        for kk in range(TOP_K):
            pick = jnp.where(buf_row == rows_t[kk:kk + 1], 1.0, pick)
        return pick.astype(MXU_DTYPE)

    pick = pick_rows(0)
    for p in range(n_parts):
        nxt = pick_rows((p + 1) * part) if p + 1 < n_parts else None
        sbuf[slot, p * part:(p + 1) * part, :] = jnp.dot(
            pick, h2, preferred_element_type=F32).astype(sbuf.dtype)
        pick = nxt

    def push(dst_slot):
        def make(hbm_row, buf_row_, rows):
            return pltpu.make_async_copy(sbuf.at[dst_slot, pl.ds(buf_row_, rows), :],
                                         xs_hbm.at[pl.ds(hbm_row, rows), :], sem.at[dst_slot])
        return make

    _for_each_run_piece(tab_cur_ref, pieces, push(slot), lambda c: c.start())

    @pl.when(i > 0)
    def _():
        _for_each_run_piece(tab_prev_ref, pieces, push(1 - slot), lambda c: c.wait())

    @pl.when(i == n_steps - 1)
    def _():
        _for_each_run_piece(tab_cur_ref, pieces, push(slot), lambda c: c.wait())


def _dispatch_call(pad_end, pad_start, tab3, rows8, h2, n_rows, blk):
    n_tok, d = h2.shape
    n_steps = tab3.shape[0]
    ts = n_tok // n_steps
    run_rows = ts * TOP_K + N_EXPERTS * RUN_ALIGN
    grid_spec = pltpu.PrefetchScalarGridSpec(
        num_scalar_prefetch=2,
        grid=(n_steps,),
        in_specs=[
            pl.BlockSpec((1, 1, tab3.shape[2]), lambda i, pe, ps: (i, 0, 0), memory_space=pltpu.SMEM),
            pl.BlockSpec((1, 1, tab3.shape[2]), lambda i, pe, ps: (jnp.maximum(i - 1, 0), 0, 0),
                         memory_space=pltpu.SMEM),
            pl.BlockSpec((ROWS_PAD, ts), lambda i, pe, ps: (0, i)),
            pl.BlockSpec((ts, d), lambda i, pe, ps: (i, 0)),
        ],
        out_specs=pl.BlockSpec(memory_space=pl.ANY),
        scratch_shapes=[pltpu.VMEM((blk, d), MXU_DTYPE), pltpu.VMEM((2, run_rows, d), MXU_DTYPE),
                        pltpu.SemaphoreType.DMA((2,)), pltpu.SemaphoreType.DMA(())],
    )
    return pl.pallas_call(
        _dispatch_kernel,
        grid_spec=grid_spec,
        out_shape=jax.ShapeDtypeStruct((n_rows, d), MXU_DTYPE),
        compiler_params=pltpu.CompilerParams(
            dimension_semantics=("arbitrary",), vmem_limit_bytes=VMEM_LIMIT),
        name="dispatch",
    )(pad_end, pad_start, tab3, tab3, rows8, h2)


def _expert_up(xs, wgu, bgu_ref):
    return jnp.dot(xs, wgu, preferred_element_type=F32) + bgu_ref[...]


def _expert_down(gu, wdn, bdn_ref):
    gate = jnp.minimum(gu[:, :D_FF], SWIGLU_LIMIT)
    up = jnp.clip(gu[:, D_FF:], -SWIGLU_LIMIT, SWIGLU_LIMIT)
    act = (up + 1.0) * (gate * jax.nn.sigmoid(SWIGLU_ALPHA * gate))
    return jnp.dot(act.astype(MXU_DTYPE), wdn, preferred_element_type=F32) + bdn_ref[...]


def _expert_kernel(be_ref, nu_ref, xs_ref, wgu_ref, bgu_ref, wdn_ref, bdn_ref, ys_ref):
    i = pl.program_id(0)
    n_used = nu_ref[0]

    @pl.when(i < n_used)
    def _():
        part = min(EXPERT_PART, xs_ref.shape[0])
        n_parts = xs_ref.shape[0] // part
        rows = [slice(p * part, (p + 1) * part) for p in range(n_parts)]
        wgu = wgu_ref[...].astype(MXU_DTYPE)
        wdn = wdn_ref[...].astype(MXU_DTYPE)
        gu = _expert_up(xs_ref[rows[0], :], wgu, bgu_ref)
        for p in range(n_parts):
            gu_next = _expert_up(xs_ref[rows[p + 1], :], wgu, bgu_ref) if p + 1 < n_parts else None
            ys_ref[rows[p], :] = _expert_down(gu, wdn, bdn_ref).astype(ys_ref.dtype)
            gu = gu_next

    @pl.when(i >= n_used)
    def _():
        ys_ref[...] = jnp.zeros_like(ys_ref)


def _expert_call(block_e, n_used, xs, w_gu, b_gu, w_down, b_down, blk, layer):
    n_rows, d = xs.shape
    n_blocks = n_rows // blk
    two_f = w_gu.shape[3]
    last = lambda nu: jnp.maximum(nu[0] - 1, 0)
    of_block = lambda i, be, nu: (layer, be[i], 0, 0)
    grid_spec = pltpu.PrefetchScalarGridSpec(
        num_scalar_prefetch=2,
        grid=(n_blocks,),
        in_specs=[
            pl.BlockSpec((blk, d), lambda i, be, nu: (jnp.minimum(i, last(nu)), 0)),
            pl.BlockSpec((None, None, d, two_f), of_block),
            pl.BlockSpec((None, None, 1, two_f), of_block),
            pl.BlockSpec((None, None, two_f // 2, d), of_block),
            pl.BlockSpec((None, None, 1, d), of_block),
        ],
        out_specs=pl.BlockSpec((blk, d), lambda i, be, nu: (i, 0)),
    )
    return pl.pallas_call(
        _expert_kernel,
        grid_spec=grid_spec,
        out_shape=jax.ShapeDtypeStruct((n_rows, d), MXU_DTYPE),
        compiler_params=pltpu.CompilerParams(
            dimension_semantics=("arbitrary",), vmem_limit_bytes=VMEM_LIMIT),
        name="experts",
    )(block_e, n_used, xs, w_gu, b_gu, w_down, b_down)


def _combine_kernel(tab_cur_ref, tab_nxt_ref, ys_hbm, x1_ref, sel_ref, rows_ref, ada_ref, lng_ref, lnb_ref,
                    o_ref, rbuf, sem):
    i = pl.program_id(0)
    n_steps = pl.num_programs(0)
    ts = x1_ref.shape[0]
    slot = i % 2
    pieces = _run_pieces(ts)

    def fetch(dst_slot):
        def make(hbm_row, buf_row, rows):
            return pltpu.make_async_copy(ys_hbm.at[pl.ds(hbm_row, rows), :],
                                         rbuf.at[dst_slot, pl.ds(buf_row, rows), :], sem.at[dst_slot])
        return make

    @pl.when(i == 0)
    def _():
        rbuf[...] = jnp.zeros_like(rbuf)
        _for_each_run_piece(tab_cur_ref, pieces, fetch(0), lambda c: c.start())

    @pl.when(i + 1 < n_steps)
    def _():
        _for_each_run_piece(tab_nxt_ref, pieces, fetch(1 - slot), lambda c: c.start())

    _for_each_run_piece(tab_cur_ref, pieces, fetch(slot), lambda c: c.wait())

    sel = sel_ref[...]
    rows_t = rows_ref[...]
    rows_c = jnp.concatenate([rows_t, jnp.zeros((LANES - ROWS_PAD, ts), F32)], axis=0).T
    col = lax.broadcasted_iota(jnp.int32, (1, rbuf.shape[1]), 1).astype(F32)
    gt2 = ada_ref[...][5:6]
    part = min(COMBINE_PART, ts)

    def weight_pieces(tok):
        weights = jnp.zeros((part, rbuf.shape[1]), F32)
        for kk in range(TOP_K):
            weights = jnp.where(col == rows_c[tok, kk:kk + 1],
                                sel[tok, SEL_WEIGHT + kk:SEL_WEIGHT + kk + 1], weights)
        hi = weights.astype(MXU_DTYPE)
        lo = (weights - hi.astype(F32)).astype(MXU_DTYPE)
        return jnp.concatenate([hi, lo], axis=0)

    toks = [slice(p * part, (p + 1) * part) for p in range(ts // part)]
    pieces_w = weight_pieces(toks[0])
    for p, tok in enumerate(toks):
        nxt = weight_pieces(toks[p + 1]) if p + 1 < len(toks) else None
        both = jnp.dot(pieces_w, rbuf[slot], preferred_element_type=F32)
        ffn = both[:part] + both[part:]
        o_ref[tok, :] = _layernorm_rows(DEEPNORM_ALPHA * x1_ref[tok, :] + (1.0 + gt2) * ffn,
                                        lng_ref[...], lnb_ref[...])
        pieces_w = nxt


def _combine_call(tab3, ys, x1, sel, rows8, ada_l, ln_g, ln_b, seq):
    n_tok, d = x1.shape
    n_steps = tab3.shape[0]
    ts = n_tok // n_steps
    tiles_per_seq = seq // ts
    run_rows = ts * TOP_K + N_EXPERTS * RUN_ALIGN
    return pl.pallas_call(
        _combine_kernel,
        grid=(n_steps,),
        in_specs=[
            pl.BlockSpec((1, 1, tab3.shape[2]), lambda i: (i, 0, 0), memory_space=pltpu.SMEM),
            pl.BlockSpec((1, 1, tab3.shape[2]), lambda i: (jnp.minimum(i + 1, n_steps - 1), 0, 0),
                         memory_space=pltpu.SMEM),
            pl.BlockSpec(memory_space=pl.ANY),
            pl.BlockSpec((ts, d), lambda i: (i, 0)),
            pl.BlockSpec((ts, LANES), lambda i: (i, 0)),
            pl.BlockSpec((ROWS_PAD, ts), lambda i: (0, i)),
            pl.BlockSpec((None, 6, d), lambda i: (i // tiles_per_seq, 0, 0)),
            pl.BlockSpec((1, d), lambda i: (0, 0)),
            pl.BlockSpec((1, d), lambda i: (0, 0)),
        ],
        out_specs=pl.BlockSpec((ts, d), lambda i: (i, 0)),
        out_shape=jax.ShapeDtypeStruct((n_tok, d), F32),
        scratch_shapes=[pltpu.VMEM((2, run_rows, d), MXU_DTYPE), pltpu.SemaphoreType.DMA((2,))],
        compiler_params=pltpu.CompilerParams(
            dimension_semantics=("arbitrary",), vmem_limit_bytes=VMEM_LIMIT),
        name="combine",
    )(tab3, tab3, ys, x1, sel, rows8, ada_l, ln_g.reshape(1, d), ln_b.reshape(1, d))


def _retention_tables():
    log_gamma = jnp.log(1.0 - 2.0 ** (-5.0 - jnp.arange(RET_HEADS, dtype=F32)))
    idx = jnp.arange(CHUNK, dtype=F32)
    diff = idx[:, None] - idx[None, :]
    decay = jnp.where(diff[None] >= 0,
                      jnp.exp(log_gamma[:, None, None] * jnp.maximum(diff, 0.0)[None]), 0.0)
    decay = decay.reshape(RET_HEADS * CHUNK, CHUNK)
    zeta = jnp.exp(log_gamma[:, None] * (CHUNK - 1.0 - idx)[None])
    zeta_t = jnp.repeat(zeta.T, RET_QK_DIM, axis=1)
    xi = jnp.exp(log_gamma[:, None] * (idx + 1.0)[None]).T
    xi_t = jnp.repeat(xi, RET_V_DIM, axis=1)
    cdec = jnp.repeat(jnp.exp(log_gamma * CHUNK), RET_V_DIM)[None, :]
    row_head = jnp.arange(RET_QK_WIDTH) // RET_QK_DIM
    col_head = jnp.arange(RET_V_WIDTH) // RET_V_DIM
    same_head = row_head[:, None] == col_head[None, :]
    bmask = same_head.astype(F32)
    gavg = ((col_head[:, None] == col_head[None, :]).astype(F32) / RET_V_DIM).astype(MXU_DTYPE)
    return decay, zeta_t, xi_t, cdec, bmask, gavg


def _rotary_tables(positions):
    inv_freq = ROPE_BASE ** (-jnp.arange(0, RET_QK_DIM, 2, dtype=F32) / RET_QK_DIM)
    ang = positions.astype(F32)[..., None] * inv_freq
    cos = jnp.cos(ang)
    sin = jnp.sin(ang)
    cos_t = jnp.tile(jnp.concatenate([cos, cos], axis=-1), (1, 1, RET_HEADS))
    sin_t = jnp.tile(jnp.concatenate([-sin, sin], axis=-1), (1, 1, RET_HEADS))
    return cos_t, sin_t


def _moe_rows(n_tok, n_tiles, blk):
    worst = n_tok * TOP_K + n_tiles * N_EXPERTS * (RUN_ALIGN - 1)
    return (-(-worst // blk) + N_EXPERTS) * blk


def _routing_tables(sel_t, counts, before, n_rows, blk, pieces):
    n_blocks = n_rows // blk
    sizes = counts.astype(jnp.int32)
    pad_sizes = (sizes + blk - 1) // blk * blk
    pad_end = jnp.cumsum(pad_sizes)
    pad_start = pad_end - pad_sizes
    n_used = (pad_end[-1] // blk).astype(jnp.int32)
    block_start = jnp.arange(n_blocks, dtype=jnp.int32) * blk
    block_start = jnp.minimum(block_start, (n_used - 1) * blk)
    block_e = jnp.sum((pad_end[None, :] <= block_start[:, None]).astype(jnp.int32), axis=-1)
    block_e = jnp.minimum(block_e, N_EXPERTS - 1)

    before = before.astype(jnp.int32)
    after = jnp.concatenate([before[1:], sizes[None, :]], axis=0)
    run_start = pad_start[None, :] + before
    run_units = (after - before) // RUN_ALIGN
    run_off = (jnp.cumsum(run_units, axis=-1) - run_units) * RUN_ALIGN
    n_tiles = before.shape[0]
    slot_id = jnp.arange(N_EXPERTS, dtype=jnp.int32)
    cols, counts_c = [], []
    for bit in pieces:
        has = jnp.bitwise_and(run_units, bit) != 0
        done = jnp.bitwise_and(run_units, -2 * bit) * RUN_ALIGN
        pos = jnp.cumsum(has.astype(jnp.int32), axis=-1) - 1
        place = jnp.logical_and(has[:, None, :], pos[:, None, :] == slot_id[None, :, None])
        cols.append(jnp.sum(jnp.where(place, (run_start + done)[:, None, :], 0), axis=-1))
        cols.append(jnp.sum(jnp.where(place, (run_off + done)[:, None, :], 0), axis=-1))
        counts_c.append(jnp.sum(has.astype(jnp.int32), axis=-1, keepdims=True))
    width = _piece_table_width(len(pieces))
    used = len(pieces) * (2 * N_EXPERTS + 1)
    tab = jnp.concatenate(cols + counts_c + [jnp.zeros((n_tiles, width - used), jnp.int32)], axis=-1)

    n_tok = sel_t.shape[1]
    idx = sel_t[SEL_IDX:SEL_IDX + TOP_K].astype(jnp.int32)
    off_tok = jnp.repeat(run_off.T, n_tok // n_tiles, axis=1)
    base = jnp.sum(jnp.where(idx[:, None, :] == slot_id[None, :, None], off_tok[None], 0), axis=1)
    rows = base.astype(F32) + sel_t[SEL_RANK:SEL_RANK + TOP_K]
    rows8 = jnp.pad(rows, ((0, ROWS_PAD - TOP_K), (0, 0)))
    return (pad_end.astype(jnp.int32), pad_start.astype(jnp.int32), block_e, n_used.reshape(1),
            tab.astype(jnp.int32).reshape(n_tiles, 1, width), rows8)


def kernel(x, c, positions, w_ada, b_ada, w_in, gmlp_ln_g, gmlp_ln_b, w_spatial, b_spatial, conv_w,
           w_branch, w_gate_up, b_gate, w_out, ln1_g, ln1_b, w_router, b_router, w_gu, b_gu, w_down,
           b_down, ln2_g, ln2_b):
    bsz, seq, d = x.shape
    n_tok = bsz * seq
    depth = w_ada.shape[0]
    blk = min(MOE_BLOCK, n_tok * TOP_K // N_EXPERTS)
    ts = min(SEQ_TILE, seq)
    assert depth == DEPTH and w_router.shape[-1] == N_EXPERTS and w_in.shape[-1] == _IN_WIDTH
    assert ts % CHUNK == 0 and seq % min(MIXER_STEP, seq) == 0 and min(MIXER_STEP, seq) % ts == 0
    assert blk % RUN_ALIGN == 0 and d % LANES == 0

    ada = _ada_call(c, w_ada, b_ada)
    ada = jnp.transpose(ada, (0, 2, 1, 3))
    cos_t, sin_t = _rotary_tables(positions)
    tables = _retention_tables()

    mx = MXU_DTYPE
    w_in_m = w_in.astype(mx)
    w_branch_m = w_branch.astype(mx)
    w_gate_m = w_gate_up.astype(mx)
    w_out_m = w_out.astype(mx)
    b_gu4 = b_gu[:, :, None, :]
    b_down4 = b_down[:, :, None, :]
    n_exp = w_router.shape[-1]
    w_router_m = jnp.pad(w_router, ((0, 0), (0, 0), (0, LANES - n_exp))).astype(mx)
    b_router_p = jnp.pad(b_router, ((0, 0), (0, LANES - n_exp)), constant_values=NEG_BIG)

    for l in range(depth):
        bsp_t = jnp.repeat(b_spatial[l].T, GMLP_WIDTH // GMLP_GROUPS, axis=1)
        lw = (
            (w_in_m, l),
            (gmlp_ln_g[l].reshape(1, -1), None), (gmlp_ln_b[l].reshape(1, -1), None),
            (w_spatial[l].reshape(GMLP_GROUPS * CHUNK, CHUNK), None), (bsp_t, None),
            (jnp.pad(conv_w[l], ((0, SUBLANES - conv_w.shape[1]), (0, 0))), None),
            (w_branch_m, l), (w_gate_m, l), (b_gate, l), (w_out_m, l),
            (ln1_g[l].reshape(1, -1), None), (ln1_b[l].reshape(1, -1), None),
            (w_router_m, l), (b_router_p[l].reshape(1, -1), None),
        )
        x1, h2, sel, sel_t, cnt, before = _mixer_call(x, ada[l], cos_t, sin_t, lw, tables)
        n_rows = _moe_rows(n_tok, n_tok // ts, blk)
        pad_end, pad_start, block_e, n_used, tab3, rows8 = _routing_tables(
            sel_t, cnt[0, :n_exp], before[:, 0, :n_exp], n_rows, blk, _run_pieces(ts))
        xs = _dispatch_call(pad_end, pad_start, tab3, rows8, h2, n_rows, blk)
        ys = _expert_call(block_e, n_used, xs, w_gu, b_gu4, w_down, b_down4, blk, l)
        x = _combine_call(tab3, ys, x1, sel, rows8, ada[l], ln2_g[l], ln2_b[l], seq).reshape(bsz, seq, d)
    return x
```

```python
import jax
import jax.numpy as jnp
from jax import lax
from jax.experimental import pallas as pl
from jax.experimental.pallas import tpu as pltpu

F32 = jnp.float32
MXU_DTYPE = jnp.bfloat16

DEPTH = 4
RET_HEADS = 4
RET_QK_DIM = 32
RET_V_DIM = 64
RET_QK_WIDTH = RET_HEADS * RET_QK_DIM
RET_V_WIDTH = RET_HEADS * RET_V_DIM
CHUNK = 128
ROPE_BASE = 10000.0
GMLP_GROUPS = 4
GMLP_WIDTH = 256
CONV_WIDTH = 256
GATE_RANK = 128
N_BRANCH = 3
N_EXPERTS = 32
TOP_K = 4
D_FF = 256
SWIGLU_LIMIT = 7.0
SWIGLU_ALPHA = 1.702
DEEPNORM_ALPHA = (2.0 * DEPTH) ** 0.25
LN_EPS = 1e-5

_O_QK = 0
_O_VG = 2 * RET_QK_WIDTH
_O_GMLP = _O_VG + 2 * RET_V_WIDTH
_O_CONV = _O_GMLP + 2 * GMLP_WIDTH
_O_CODE = _O_CONV + 3 * CONV_WIDTH
_IN_WIDTH = _O_CODE + N_BRANCH * GATE_RANK

LANES = 128
SUBLANES = 8
NEG_BIG = -1e30

SEQ_TILE = 256
MIXER_STEP = 512
MOE_BLOCK = 2048
EXPERT_PART = 1024
COMBINE_PART = 128
SELT_ROWS = 16
RUN_ALIGN = 16
VMEM_LIMIT = 56 * 1024 * 1024


def _dot(a, b):
    return jnp.dot(a.astype(MXU_DTYPE), b.astype(MXU_DTYPE), preferred_element_type=F32)


def _dot_nt(a, b):
    return lax.dot_general(a.astype(MXU_DTYPE), b.astype(MXU_DTYPE),
                           (((1,), (1,)), ((), ())), preferred_element_type=F32)


def _split_dot(x, w):
    hi = x.astype(MXU_DTYPE)
    lo = (x - hi.astype(F32)).astype(MXU_DTYPE)
    return (jnp.dot(hi, w, preferred_element_type=F32)
            + jnp.dot(lo, w, preferred_element_type=F32))


def _layernorm_rows(x, g, b):
    mu = jnp.mean(x, axis=-1, keepdims=True)
    d = x - mu
    var = jnp.mean(d * d, axis=-1, keepdims=True)
    return d * lax.rsqrt(var + LN_EPS) * g + b


def _ada_kernel(c_ref, w_ref, b_ref, o_ref):
    c_act = jax.nn.silu(c_ref[...])
    o_ref[...] = jnp.dot(c_act, w_ref[...], preferred_element_type=F32,
                         precision=lax.Precision.HIGHEST) + b_ref[...]


def _ada_call(c, w_ada, b_ada):
    depth, d, six_d = w_ada.shape
    bsz = c.shape[0]
    n_col = six_d // d
    return pl.pallas_call(
        _ada_kernel,
        grid=(depth, n_col),
        in_specs=[
            pl.BlockSpec((bsz, d), lambda l, j: (0, 0)),
            pl.BlockSpec((None, d, d), lambda l, j: (l, 0, j)),
            pl.BlockSpec((None, 1, d), lambda l, j: (l, 0, j)),
        ],
        out_specs=pl.BlockSpec((None, None, bsz, d), lambda l, j: (l, j, 0, 0)),
        out_shape=jax.ShapeDtypeStruct((depth, n_col, bsz, d), F32),
        compiler_params=pltpu.CompilerParams(
            dimension_semantics=("arbitrary", "arbitrary"), vmem_limit_bytes=VMEM_LIMIT),
        name="ada_ln",
    )(c, w_ada, b_ada.reshape(depth, 1, six_d))


_MIXER_PHASES = 3


def _mixer_kernel(x_ref, ada_ref, cos_ref, sin_ref, *rest):
    params = rest[:-9]
    x1_ref, h2_ref, sel_ref, selt_ref, cnt_ref, before_ref, state_ref, zc_ref, cntacc_ref = rest[-9:]
    b = pl.program_id(0)
    s = pl.program_id(1)

    @pl.when(s == 0)
    def _():
        state_ref[...] = jnp.zeros_like(state_ref)
        zc_ref[...] = jnp.zeros_like(zc_ref)

    @pl.when(jnp.logical_and(b == 0, s == 0))
    def _():
        cntacc_ref[...] = jnp.zeros_like(cntacc_ref)

    n_sub = before_ref.shape[0]
    ts = x_ref.shape[0] // n_sub
    tiles = []
    for u in range(n_sub):
        rows = pl.ds(u * ts, ts)
        tiles.append(_mixer_tile(x_ref.at[rows], ada_ref, cos_ref.at[rows], sin_ref.at[rows], *params,
                                 x1_ref.at[rows], h2_ref.at[rows], sel_ref.at[rows], selt_ref.at[:, rows],
                                 cnt_ref, before_ref.at[u], state_ref, zc_ref, cntacc_ref))
    for _ in range(_MIXER_PHASES):
        for tile in tiles:
            next(tile, None)


def _mixer_tile(x_ref, ada_ref, cos_ref, sin_ref, w_in_ref, lng_ref, lnb_ref, wsp_ref, bsp_ref,
                convw_ref, wbr_ref, wgt_ref, bgt_ref, wout_ref, ln1g_ref, ln1b_ref, wr_ref, br_ref,
                decay_ref, zeta_ref, xi_ref, cdec_ref, bmask_ref, gavg_ref,
                x1_ref, h2_ref, sel_ref, selt_ref, cnt_ref, before_ref,
                state_ref, zc_ref, cntacc_ref):
    ts = x_ref.shape[0]
    x = x_ref[...]
    ada = ada_ref[...]
    sh1, sc1, gt1, sh2, sc2 = ada[0:1], ada[1:2], ada[2:3], ada[3:4], ada[4:5]
    h = (x * (1.0 + sc1) + sh1).astype(MXU_DTYPE)

    def proj(lo, hi):
        return jnp.dot(h, w_in_ref[:, lo:hi], preferred_element_type=F32)

    lane_qk = lax.broadcasted_iota(jnp.int32, (1, RET_QK_WIDTH), 1)
    lane_v = lax.broadcasted_iota(jnp.int32, (1, RET_V_WIDTH), 1)
    qk_masks = [lane_qk // RET_QK_DIM == hd for hd in range(RET_HEADS)]
    v_masks = [lane_v // RET_V_DIM == hd for hd in range(RET_HEADS)]

    qk = proj(_O_QK, _O_VG)
    cos = cos_ref[...]
    sin = sin_ref[...]
    first_half = (lane_qk % RET_QK_DIM) < (RET_QK_DIM // 2)

    def rotary(t):
        swapped = jnp.where(first_half, pltpu.roll(t, RET_QK_WIDTH - RET_QK_DIM // 2, 1),
                            pltpu.roll(t, RET_QK_DIM // 2, 1))
        return t * cos + swapped * sin

    q = rotary(qk[:, :RET_QK_WIDTH])
    k = rotary(qk[:, RET_QK_WIDTH:]) * (RET_QK_DIM ** -0.5)
    vg = proj(_O_VG, _O_GMLP)
    v = vg[:, :RET_V_WIDTH]
    g = vg[:, RET_V_WIDTH:]

    decay = decay_ref[...]
    zeta = zeta_ref[...]
    xi = xi_ref[...]
    cdec = cdec_ref[...]
    bmask = bmask_ref[...]
    gavg = gavg_ref[...]

    o_chunks = []
    state = state_ref[...]
    for c in range(ts // CHUNK):
        rows = slice(c * CHUNK, (c + 1) * CHUNK)
        q_c, k_c, v_c = q[rows], k[rows], v[rows]
        v_m = v_c.astype(MXU_DTYPE)
        q_all = jnp.concatenate([jnp.where(m, q_c, 0.0) for m in qk_masks], axis=0)
        scores = _dot_nt(q_all, k_c) * decay
        o_all = _dot(scores, v_m)
        o = _dot(q_c, state) * xi
        for hd in range(RET_HEADS):
            o = o + jnp.where(v_masks[hd], o_all[hd * CHUNK:(hd + 1) * CHUNK], 0.0)
        kv = _dot((k_c * zeta).T, v_m) * bmask
        state = state * cdec + kv
        o_chunks.append(o)
    state_ref[...] = state
    o = jnp.concatenate(o_chunks, axis=0) if len(o_chunks) > 1 else o_chunks[0]
    mu = _split_dot(o, gavg)
    d = o - mu
    var = _split_dot(d * d, gavg)
    r_br = jax.nn.silu(g) * (d * lax.rsqrt(var + LN_EPS))

    guv = proj(_O_GMLP, _O_CONV)
    u = jax.nn.gelu(guv[:, :GMLP_WIDTH])
    vv = _layernorm_rows(jax.nn.gelu(guv[:, GMLP_WIDTH:]), lng_ref[...], lnb_ref[...]).astype(MXU_DTYPE)
    rr = lax.broadcasted_iota(jnp.int32, (GMLP_GROUPS * CHUNK, CHUNK), 0) % CHUNK
    cc_ = lax.broadcasted_iota(jnp.int32, (GMLP_GROUPS * CHUNK, CHUNK), 1)
    w_sp = jnp.where(cc_ <= rr, wsp_ref[...], 0.0).astype(MXU_DTYPE)
    bsp = bsp_ref[...]
    z_chunks = []
    for c in range(ts // CHUNK):
        z_all = jnp.dot(w_sp, vv[c * CHUNK:(c + 1) * CHUNK], preferred_element_type=F32)
        z = bsp
        for gi in range(GMLP_GROUPS):
            z = z + jnp.where(v_masks[gi], z_all[gi * CHUNK:(gi + 1) * CHUNK], 0.0)
        z_chunks.append(z)
    z = jnp.concatenate(z_chunks, axis=0) if len(z_chunks) > 1 else z_chunks[0]
    s_br = u * z

    cbcx = proj(_O_CONV, _O_CODE)
    gate_b = cbcx[:, :CONV_WIDTH]
    zc = cbcx[:, CONV_WIDTH:2 * CONV_WIDTH] * cbcx[:, 2 * CONV_WIDTH:]
    carry = zc_ref[...]
    prev1 = carry[SUBLANES - 1:SUBLANES]
    prev2 = carry[SUBLANES - 2:SUBLANES - 1]
    row = lax.broadcasted_iota(jnp.int32, (ts, 1), 0)
    z1 = jnp.where(row == 0, prev1, pltpu.roll(zc, 1, 0))
    z2 = jnp.where(row == 0, prev2, jnp.where(row == 1, prev1, pltpu.roll(zc, 2, 0)))
    zc_ref[...] = zc[ts - SUBLANES:ts]
    cw = convw_ref[...]
    k_br = gate_b * (cw[0:1] * z2 + cw[1:2] * z1 + cw[2:3] * zc)

    code = proj(_O_CODE, _IN_WIDTH)
    yield
    merged = None
    for i, br in enumerate((r_br, s_br, k_br)):
        y = jnp.dot(br.astype(MXU_DTYPE), wbr_ref[i], preferred_element_type=F32)
        gl = jnp.dot(code[:, i * GATE_RANK:(i + 1) * GATE_RANK].astype(MXU_DTYPE), wgt_ref[i],
                     preferred_element_type=F32) + bgt_ref[i:i + 1]
        t = jax.nn.sigmoid(gl) * y
        merged = t if merged is None else merged + t
    mix = jnp.dot(merged.astype(MXU_DTYPE), wout_ref[...], preferred_element_type=F32)
    x1 = _layernorm_rows(DEEPNORM_ALPHA * x + (1.0 + gt1) * mix, ln1g_ref[...], ln1b_ref[...])
    x1_ref[...] = x1
    h2 = (x1 * (1.0 + sc2) + sh2).astype(MXU_DTYPE)
    h2_ref[...] = h2

    logits = jnp.dot(h2, wr_ref[...], preferred_element_type=F32) + br_ref[...]
    yield
    lane = lax.broadcasted_iota(jnp.int32, (1, LANES), 1).astype(F32)
    top_v, top_i = [], []
    work = logits
    for _ in range(TOP_K):
        m = jnp.max(work, axis=-1, keepdims=True)
        i_sel = jnp.min(jnp.where(work == m, lane, float(LANES)), axis=-1, keepdims=True)
        top_v.append(m)
        top_i.append(i_sel)
        work = jnp.where(lane == i_sel, -jnp.inf, work)
    exps = [jnp.exp(tv - top_v[0]) for tv in top_v]
    denom = exps[0] + exps[1] + exps[2] + exps[3]
    onehots = [lane == ti for ti in top_i]
    member = jnp.zeros((ts, LANES), F32)
    for oh in onehots:
        member = member + jnp.where(oh, 1.0, 0.0)
    tr = lax.broadcasted_iota(jnp.int32, (ts, ts), 0)
    tc = lax.broadcasted_iota(jnp.int32, (ts, ts), 1)
    before = jnp.where(tc < tr, 1.0, 0.0).astype(MXU_DTYPE)
    cnt = cntacc_ref[...]
    local_rank = jnp.dot(before, member.astype(MXU_DTYPE), preferred_element_type=F32)
    sel = jnp.zeros((ts, LANES), F32)
    for kk in range(TOP_K):
        lrank_k = jnp.sum(jnp.where(onehots[kk], local_rank, 0.0), axis=-1, keepdims=True)
        sel = sel + jnp.where(lane == float(kk), top_i[kk], 0.0)
        sel = sel + jnp.where(lane == float(TOP_K + kk), exps[kk] / denom, 0.0)
        sel = sel + jnp.where(lane == float(2 * TOP_K + kk), lrank_k, 0.0)
    sel_ref[...] = sel
    selt_ref[...] = sel.T[:selt_ref.shape[0]]
    before_ref[...] = cnt
    tile_cnt = jnp.sum(member, axis=0, keepdims=True)
    cnt = cnt + jnp.floor((tile_cnt + (RUN_ALIGN - 1.0)) * (1.0 / RUN_ALIGN)) * RUN_ALIGN
    cntacc_ref[...] = cnt
    cnt_ref[...] = cnt


def _const_spec(shape):
    return pl.BlockSpec(shape, lambda b, s: (0,) * len(shape))


def _param_spec(a, layer):
    if layer is None:
        return _const_spec(a.shape)
    return pl.BlockSpec((None,) + a.shape[1:], lambda b, s: (layer,) + (0,) * (a.ndim - 1))


def _mixer_call(x, ada_l, cos_t, sin_t, lw, tables):
    bsz, seq, d = x.shape
    ts = min(SEQ_TILE, seq)
    step = min(MIXER_STEP, seq)
    n_tok = bsz * seq
    row3 = lambda b, s: (b, s, 0)
    in_specs = [
        pl.BlockSpec((None, step, d), row3),
        pl.BlockSpec((None, 6, d), lambda b, s: (b, 0, 0)),
        pl.BlockSpec((None, step, RET_QK_WIDTH), row3),
        pl.BlockSpec((None, step, RET_QK_WIDTH), row3),
    ] + [_param_spec(a, lay) for a, lay in lw] + [_const_spec(a.shape) for a in tables]
    tok_row = lambda b, s: (b * (seq // step) + s, 0)
    out_specs = [
        pl.BlockSpec((step, d), tok_row),
        pl.BlockSpec((step, d), tok_row),
        pl.BlockSpec((step, LANES), tok_row),
        pl.BlockSpec((SELT_ROWS, step), lambda b, s: (0, b * (seq // step) + s)),
        pl.BlockSpec((SUBLANES, LANES), lambda b, s: (0, 0)),
        pl.BlockSpec((step // ts, SUBLANES, LANES), lambda b, s: (b * (seq // step) + s, 0, 0)),
    ]
    out_shape = [
        jax.ShapeDtypeStruct((n_tok, d), F32),
        jax.ShapeDtypeStruct((n_tok, d), MXU_DTYPE),
        jax.ShapeDtypeStruct((n_tok, LANES), F32),
        jax.ShapeDtypeStruct((SELT_ROWS, n_tok), F32),
        jax.ShapeDtypeStruct((SUBLANES, LANES), F32),
        jax.ShapeDtypeStruct((n_tok // ts, SUBLANES, LANES), F32),
    ]
    return pl.pallas_call(
        _mixer_kernel,
        grid=(bsz, seq // step),
        in_specs=in_specs,
        out_specs=out_specs,
        out_shape=out_shape,
        scratch_shapes=[
            pltpu.VMEM((RET_QK_WIDTH, RET_V_WIDTH), F32),
            pltpu.VMEM((SUBLANES, CONV_WIDTH), F32),
            pltpu.VMEM((SUBLANES, LANES), F32),
        ],
        compiler_params=pltpu.CompilerParams(
            dimension_semantics=("arbitrary", "arbitrary"), vmem_limit_bytes=VMEM_LIMIT),
        name="mixer",
    )(x, ada_l, cos_t, sin_t, *[a for a, _ in lw], *tables)


def _run_pieces(max_rows):
    units = max_rows // RUN_ALIGN
    pieces = []
    bit = 1
    while bit <= units:
        pieces.append(bit)
        bit *= 2
    return tuple(reversed(pieces))


def _piece_table_width(n_classes):
    return -(-(n_classes * 2 * N_EXPERTS + n_classes) // LANES) * LANES


def _for_each_run_piece(tab_ref, pieces, make_copy, fn):
    for c, bit in enumerate(pieces):
        count = tab_ref[0, 0, len(pieces) * 2 * N_EXPERTS + c]

        def body(j, carry, c=c, bit=bit):
            hbm_row = tab_ref[0, 0, c * 2 * N_EXPERTS + j]
            buf_row = tab_ref[0, 0, (c * 2 + 1) * N_EXPERTS + j]
            fn(make_copy(pl.multiple_of(hbm_row, RUN_ALIGN), pl.multiple_of(buf_row, RUN_ALIGN),
                         bit * RUN_ALIGN))
            return carry

        lax.fori_loop(0, count, body, 0)


SEL_IDX, SEL_WEIGHT, SEL_RANK = 0, TOP_K, 2 * TOP_K
ROWS_PAD = SUBLANES


def _dispatch_kernel(pe_ref, ps_ref, tab_cur_ref, tab_prev_ref, rows_ref, h2_ref, xs_hbm,
                     zbuf, sbuf, sem, zsem):
    i = pl.program_id(0)
    n_steps = pl.num_programs(0)
    ts = h2_ref.shape[0]
    blk = zbuf.shape[0]
    n_exp = pe_ref.shape[0]
    run_rows = sbuf.shape[1]
    slot = i % 2
    pieces = _run_pieces(ts)

    @pl.when(i == 0)
    def _():
        zbuf[...] = jnp.zeros_like(zbuf)

        def tail_copy(e):
            start = pl.multiple_of(jnp.maximum(pe_ref[e] - blk, 0), blk)
            return pltpu.make_async_copy(zbuf, xs_hbm.at[pl.ds(start, blk), :], zsem)

        def zstart(e, carry):
            @pl.when(pe_ref[e] > ps_ref[e])
            def _():
                tail_copy(e).start()
            return carry

        def zwait(e, carry):
            @pl.when(pe_ref[e] > ps_ref[e])
            def _():
                tail_copy(e).wait()
            return carry

        lax.fori_loop(0, n_exp, zstart, 0)
        lax.fori_loop(0, n_exp, zwait, 0)

    rows_t = rows_ref[...]
    h2 = h2_ref[...].astype(MXU_DTYPE)
    n_parts = 2
    part = run_rows // n_parts

    def pick_rows(lo):
        buf_row = lax.broadcasted_iota(jnp.int32, (part, ts), 0).astype(F32) + float(lo)
        pick = jnp.zeros((part, ts), F32)
        for kk in range(TOP_K):
            pick = jnp.where(buf_row == rows_t[kk:kk + 1], 1.0, pick)
        return pick.astype(MXU_DTYPE)

    pick = pick_rows(0)
    for p in range(n_parts):
        nxt = pick_rows((p + 1) * part) if p + 1 < n_parts else None
        sbuf[slot, p * part:(p + 1) * part, :] = jnp.dot(
            pick, h2, preferred_element_type=F32).astype(sbuf.dtype)
        pick = nxt

    def push(dst_slot):
        def make(hbm_row, buf_row_, rows):
            return pltpu.make_async_copy(sbuf.at[dst_slot, pl.ds(buf_row_, rows), :],
                                         xs_hbm.at[pl.ds(hbm_row, rows), :], sem.at[dst_slot])
        return make

    _for_each_run_piece(tab_cur_ref, pieces, push(slot), lambda c: c.start())

    @pl.when(i > 0)
    def _():
        _for_each_run_piece(tab_prev_ref, pieces, push(1 - slot), lambda c: c.wait())

    @pl.when(i == n_steps - 1)
    def _():
        _for_each_run_piece(tab_cur_ref, pieces, push(slot), lambda c: c.wait())


def _dispatch_call(pad_end, pad_start, tab3, rows8, h2, n_rows, blk):
    n_tok, d = h2.shape
    n_steps = tab3.shape[0]
    ts = n_tok // n_steps
    run_rows = ts * TOP_K + N_EXPERTS * RUN_ALIGN
    grid_spec = pltpu.PrefetchScalarGridSpec(
        num_scalar_prefetch=2,
        grid=(n_steps,),
        in_specs=[
            pl.BlockSpec((1, 1, tab3.shape[2]), lambda i, pe, ps: (i, 0, 0), memory_space=pltpu.SMEM),
            pl.BlockSpec((1, 1, tab3.shape[2]), lambda i, pe, ps: (jnp.maximum(i - 1, 0), 0, 0),
                         memory_space=pltpu.SMEM),
            pl.BlockSpec((ROWS_PAD, ts), lambda i, pe, ps: (0, i)),
            pl.BlockSpec((ts, d), lambda i, pe, ps: (i, 0)),
        ],
        out_specs=pl.BlockSpec(memory_space=pl.ANY),
        scratch_shapes=[pltpu.VMEM((blk, d), MXU_DTYPE), pltpu.VMEM((2, run_rows, d), MXU_DTYPE),
                        pltpu.SemaphoreType.DMA((2,)), pltpu.SemaphoreType.DMA(())],
    )
    return pl.pallas_call(
        _dispatch_kernel,
        grid_spec=grid_spec,
        out_shape=jax.ShapeDtypeStruct((n_rows, d), MXU_DTYPE),
        compiler_params=pltpu.CompilerParams(
            dimension_semantics=("arbitrary",), vmem_limit_bytes=VMEM_LIMIT),
        name="dispatch",
    )(pad_end, pad_start, tab3, tab3, rows8, h2)


def _expert_up(xs, wgu, bgu_ref):
    return jnp.dot(xs, wgu, preferred_element_type=F32) + bgu_ref[...]


def _expert_down(gu, wdn, bdn_ref):
    gate = jnp.minimum(gu[:, :D_FF], SWIGLU_LIMIT)
    up = jnp.clip(gu[:, D_FF:], -SWIGLU_LIMIT, SWIGLU_LIMIT)
    act = (up + 1.0) * (gate * jax.nn.sigmoid(SWIGLU_ALPHA * gate))
    return jnp.dot(act.astype(MXU_DTYPE), wdn, preferred_element_type=F32) + bdn_ref[...]


def _expert_kernel(be_ref, nu_ref, xs_ref, wgu_ref, bgu_ref, wdn_ref, bdn_ref, ys_ref):
    i = pl.program_id(0)
    n_used = nu_ref[0]

    @pl.when(i < n_used)
    def _():
        part = min(EXPERT_PART, xs_ref.shape[0])
        n_parts = xs_ref.shape[0] // part
        rows = [slice(p * part, (p + 1) * part) for p in range(n_parts)]
        wgu = wgu_ref[...].astype(MXU_DTYPE)
        wdn = wdn_ref[...].astype(MXU_DTYPE)
        gu = _expert_up(xs_ref[rows[0], :], wgu, bgu_ref)
        for p in range(n_parts):
            gu_next = _expert_up(xs_ref[rows[p + 1], :], wgu, bgu_ref) if p + 1 < n_parts else None
            ys_ref[rows[p], :] = _expert_down(gu, wdn, bdn_ref).astype(ys_ref.dtype)
            gu = gu_next

    @pl.when(i >= n_used)
    def _():
        ys_ref[...] = jnp.zeros_like(ys_ref)


def _expert_call(block_e, n_used, xs, w_gu, b_gu, w_down, b_down, blk, layer):
    n_rows, d = xs.shape
    n_blocks = n_rows // blk
    two_f = w_gu.shape[3]
    last = lambda nu: jnp.maximum(nu[0] - 1, 0)
    of_block = lambda i, be, nu: (layer, be[i], 0, 0)
    grid_spec = pltpu.PrefetchScalarGridSpec(
        num_scalar_prefetch=2,
        grid=(n_blocks,),
        in_specs=[
            pl.BlockSpec((blk, d), lambda i, be, nu: (jnp.minimum(i, last(nu)), 0)),
            pl.BlockSpec((None, None, d, two_f), of_block),
            pl.BlockSpec((None, None, 1, two_f), of_block),
            pl.BlockSpec((None, None, two_f // 2, d), of_block),
            pl.BlockSpec((None, None, 1, d), of_block),
        ],
        out_specs=pl.BlockSpec((blk, d), lambda i, be, nu: (i, 0)),
    )
    return pl.pallas_call(
        _expert_kernel,
        grid_spec=grid_spec,
        out_shape=jax.ShapeDtypeStruct((n_rows, d), MXU_DTYPE),
        compiler_params=pltpu.CompilerParams(
            dimension_semantics=("arbitrary",), vmem_limit_bytes=VMEM_LIMIT),
        name="experts",
    )(block_e, n_used, xs, w_gu, b_gu, w_down, b_down)


def _combine_kernel(tab_cur_ref, tab_nxt_ref, ys_hbm, x1_ref, sel_ref, rows_ref, ada_ref, lng_ref, lnb_ref,
                    o_ref, rbuf, sem):
    i = pl.program_id(0)
    n_steps = pl.num_programs(0)
    ts = x1_ref.shape[0]
    slot = i % 2
    pieces = _run_pieces(ts)

    def fetch(dst_slot):
        def make(hbm_row, buf_row, rows):
            return pltpu.make_async_copy(ys_hbm.at[pl.ds(hbm_row, rows), :],
                                         rbuf.at[dst_slot, pl.ds(buf_row, rows), :], sem.at[dst_slot])
        return make

    @pl.when(i == 0)
    def _():
        rbuf[...] = jnp.zeros_like(rbuf)
        _for_each_run_piece(tab_cur_ref, pieces, fetch(0), lambda c: c.start())

    @pl.when(i + 1 < n_steps)
    def _():
        _for_each_run_piece(tab_nxt_ref, pieces, fetch(1 - slot), lambda c: c.start())

    _for_each_run_piece(tab_cur_ref, pieces, fetch(slot), lambda c: c.wait())

    sel = sel_ref[...]
    rows_t = rows_ref[...]
    rows_c = jnp.concatenate([rows_t, jnp.zeros((LANES - ROWS_PAD, ts), F32)], axis=0).T
    col = lax.broadcasted_iota(jnp.int32, (1, rbuf.shape[1]), 1).astype(F32)
    gt2 = ada_ref[...][5:6]
    part = min(COMBINE_PART, ts)

    def weight_pieces(tok):
        weights = jnp.zeros((part, rbuf.shape[1]), F32)
        for kk in range(TOP_K):
            weights = jnp.where(col == rows_c[tok, kk:kk + 1],
                                sel[tok, SEL_WEIGHT + kk:SEL_WEIGHT + kk + 1], weights)
        hi = weights.astype(MXU_DTYPE)
        lo = (weights - hi.astype(F32)).astype(MXU_DTYPE)
        return jnp.concatenate([hi, lo], axis=0)

    toks = [slice(p * part, (p + 1) * part) for p in range(ts // part)]
    pieces_w = weight_pieces(toks[0])
    for p, tok in enumerate(toks):
        nxt = weight_pieces(toks[p + 1]) if p + 1 < len(toks) else None
        both = jnp.dot(pieces_w, rbuf[slot], preferred_element_type=F32)
        ffn = both[:part] + both[part:]
        o_ref[tok, :] = _layernorm_rows(DEEPNORM_ALPHA * x1_ref[tok, :] + (1.0 + gt2) * ffn,
                                        lng_ref[...], lnb_ref[...])
        pieces_w = nxt


def _combine_call(tab3, ys, x1, sel, rows8, ada_l, ln_g, ln_b, seq):
    n_tok, d = x1.shape
    n_steps = tab3.shape[0]
    ts = n_tok // n_steps
    tiles_per_seq = seq // ts
    run_rows = ts * TOP_K + N_EXPERTS * RUN_ALIGN
    return pl.pallas_call(
        _combine_kernel,
        grid=(n_steps,),
        in_specs=[
            pl.BlockSpec((1, 1, tab3.shape[2]), lambda i: (i, 0, 0), memory_space=pltpu.SMEM),
            pl.BlockSpec((1, 1, tab3.shape[2]), lambda i: (jnp.minimum(i + 1, n_steps - 1), 0, 0),
                         memory_space=pltpu.SMEM),
            pl.BlockSpec(memory_space=pl.ANY),
            pl.BlockSpec((ts, d), lambda i: (i, 0)),
            pl.BlockSpec((ts, LANES), lambda i: (i, 0)),
            pl.BlockSpec((ROWS_PAD, ts), lambda i: (0, i)),
            pl.BlockSpec((None, 6, d), lambda i: (i // tiles_per_seq, 0, 0)),
            pl.BlockSpec((1, d), lambda i: (0, 0)),
            pl.BlockSpec((1, d), lambda i: (0, 0)),
        ],
        out_specs=pl.BlockSpec((ts, d), lambda i: (i, 0)),
        out_shape=jax.ShapeDtypeStruct((n_tok, d), F32),
        scratch_shapes=[pltpu.VMEM((2, run_rows, d), MXU_DTYPE), pltpu.SemaphoreType.DMA((2,))],
        compiler_params=pltpu.CompilerParams(
            dimension_semantics=("arbitrary",), vmem_limit_bytes=VMEM_LIMIT),
        name="combine",
    )(tab3, tab3, ys, x1, sel, rows8, ada_l, ln_g.reshape(1, d), ln_b.reshape(1, d))


def _retention_tables():
    log_gamma = jnp.log(1.0 - 2.0 ** (-5.0 - jnp.arange(RET_HEADS, dtype=F32)))
    idx = jnp.arange(CHUNK, dtype=F32)
    diff = idx[:, None] - idx[None, :]
    decay = jnp.where(diff[None] >= 0,
                      jnp.exp(log_gamma[:, None, None] * jnp.maximum(diff, 0.0)[None]), 0.0)
    decay = decay.reshape(RET_HEADS * CHUNK, CHUNK)
    zeta = jnp.exp(log_gamma[:, None] * (CHUNK - 1.0 - idx)[None])
    zeta_t = jnp.repeat(zeta.T, RET_QK_DIM, axis=1)
    xi = jnp.exp(log_gamma[:, None] * (idx + 1.0)[None]).T
    xi_t = jnp.repeat(xi, RET_V_DIM, axis=1)
    cdec = jnp.repeat(jnp.exp(log_gamma * CHUNK), RET_V_DIM)[None, :]
    row_head = jnp.arange(RET_QK_WIDTH) // RET_QK_DIM
    col_head = jnp.arange(RET_V_WIDTH) // RET_V_DIM
    same_head = row_head[:, None] == col_head[None, :]
    bmask = same_head.astype(F32)
    gavg = ((col_head[:, None] == col_head[None, :]).astype(F32) / RET_V_DIM).astype(MXU_DTYPE)
    return decay, zeta_t, xi_t, cdec, bmask, gavg


def _rotary_tables(positions):
    inv_freq = ROPE_BASE ** (-jnp.arange(0, RET_QK_DIM, 2, dtype=F32) / RET_QK_DIM)
    ang = positions.astype(F32)[..., None] * inv_freq
    cos = jnp.cos(ang)
    sin = jnp.sin(ang)
    cos_t = jnp.tile(jnp.concatenate([cos, cos], axis=-1), (1, 1, RET_HEADS))
    sin_t = jnp.tile(jnp.concatenate([-sin, sin], axis=-1), (1, 1, RET_HEADS))
    return cos_t, sin_t


def _moe_rows(n_tok, n_tiles, blk):
    worst = n_tok * TOP_K + n_tiles * N_EXPERTS * (RUN_ALIGN - 1)
    return (-(-worst // blk) + N_EXPERTS) * blk


def _routing_tables(sel_t, counts, before, n_rows, blk, pieces):
    n_blocks = n_rows // blk
    sizes = counts.astype(jnp.int32)
    pad_sizes = (sizes + blk - 1) // blk * blk
    pad_end = jnp.cumsum(pad_sizes)
    pad_start = pad_end - pad_sizes
    n_used = (pad_end[-1] // blk).astype(jnp.int32)
    block_start = jnp.arange(n_blocks, dtype=jnp.int32) * blk
    block_start = jnp.minimum(block_start, (n_used - 1) * blk)
    block_e = jnp.sum((pad_end[None, :] <= block_start[:, None]).astype(jnp.int32), axis=-1)
    block_e = jnp.minimum(block_e, N_EXPERTS - 1)

    before = before.astype(jnp.int32)
    after = jnp.concatenate([before[1:], sizes[None, :]], axis=0)
    run_start = pad_start[None, :] + before
    run_units = (after - before) // RUN_ALIGN
    run_off = (jnp.cumsum(run_units, axis=-1) - run_units) * RUN_ALIGN
    n_tiles = before.shape[0]
    slot_id = jnp.arange(N_EXPERTS, dtype=jnp.int32)
    cols, counts_c = [], []
    for bit in pieces:
        has = jnp.bitwise_and(run_units, bit) != 0
        done = jnp.bitwise_and(run_units, -2 * bit) * RUN_ALIGN
        pos = jnp.cumsum(has.astype(jnp.int32), axis=-1) - 1
        place = jnp.logical_and(has[:, None, :], pos[:, None, :] == slot_id[None, :, None])
        cols.append(jnp.sum(jnp.where(place, (run_start + done)[:, None, :], 0), axis=-1))
        cols.append(jnp.sum(jnp.where(place, (run_off + done)[:, None, :], 0), axis=-1))
        counts_c.append(jnp.sum(has.astype(jnp.int32), axis=-1, keepdims=True))
    width = _piece_table_width(len(pieces))
    used = len(pieces) * (2 * N_EXPERTS + 1)
    tab = jnp.concatenate(cols + counts_c + [jnp.zeros((n_tiles, width - used), jnp.int32)], axis=-1)

    n_tok = sel_t.shape[1]
    idx = sel_t[SEL_IDX:SEL_IDX + TOP_K].astype(jnp.int32)
    off_tok = jnp.repeat(run_off.T, n_tok // n_tiles, axis=1)
    base = jnp.sum(jnp.where(idx[:, None, :] == slot_id[None, :, None], off_tok[None], 0), axis=1)
    rows = base.astype(F32) + sel_t[SEL_RANK:SEL_RANK + TOP_K]
    rows8 = jnp.pad(rows, ((0, ROWS_PAD - TOP_K), (0, 0)))
    return (pad_end.astype(jnp.int32), pad_start.astype(jnp.int32), block_e, n_used.reshape(1),
            tab.astype(jnp.int32).reshape(n_tiles, 1, width), rows8)


def kernel(x, c, positions, w_ada, b_ada, w_in, gmlp_ln_g, gmlp_ln_b, w_spatial, b_spatial, conv_w,
           w_branch, w_gate_up, b_gate, w_out, ln1_g, ln1_b, w_router, b_router, w_gu, b_gu, w_down,
           b_down, ln2_g, ln2_b):
    bsz, seq, d = x.shape
    n_tok = bsz * seq
    depth = w_ada.shape[0]
    blk = min(MOE_BLOCK, n_tok * TOP_K // N_EXPERTS)
    ts = min(SEQ_TILE, seq)
    assert depth == DEPTH and w_router.shape[-1] == N_EXPERTS and w_in.shape[-1] == _IN_WIDTH
    assert ts % CHUNK == 0 and seq % min(MIXER_STEP, seq) == 0 and min(MIXER_STEP, seq) % ts == 0
    assert blk % RUN_ALIGN == 0 and d % LANES == 0

    ada = _ada_call(c, w_ada, b_ada)
    ada = jnp.transpose(ada, (0, 2, 1, 3))
    cos_t, sin_t = _rotary_tables(positions)
    tables = _retention_tables()

    mx = MXU_DTYPE
    w_in_m = w_in.astype(mx)
    w_branch_m = w_branch.astype(mx)
    w_gate_m = w_gate_up.astype(mx)
    w_out_m = w_out.astype(mx)
    b_gu4 = b_gu[:, :, None, :]
    b_down4 = b_down[:, :, None, :]
    n_exp = w_router.shape[-1]
    w_router_m = jnp.pad(w_router, ((0, 0), (0, 0), (0, LANES - n_exp))).astype(mx)
    b_router_p = jnp.pad(b_router, ((0, 0), (0, LANES - n_exp)), constant_values=NEG_BIG)

    for l in range(depth):
        bsp_t = jnp.repeat(b_spatial[l].T, GMLP_WIDTH // GMLP_GROUPS, axis=1)
        lw = (
            (w_in_m, l),
            (gmlp_ln_g[l].reshape(1, -1), None), (gmlp_ln_b[l].reshape(1, -1), None),
            (w_spatial[l].reshape(GMLP_GROUPS * CHUNK, CHUNK), None), (bsp_t, None),
            (jnp.pad(conv_w[l], ((0, SUBLANES - conv_w.shape[1]), (0, 0))), None),
            (w_branch_m, l), (w_gate_m, l), (b_gate, l), (w_out_m, l),
            (ln1_g[l].reshape(1, -1), None), (ln1_b[l].reshape(1, -1), None),
            (w_router_m, l), (b_router_p[l].reshape(1, -1), None),
        )
        x1, h2, sel, sel_t, cnt, before = _mixer_call(x, ada[l], cos_t, sin_t, lw, tables)
        n_rows = _moe_rows(n_tok, n_tok // ts, blk)
        pad_end, pad_start, block_e, n_used, tab3, rows8 = _routing_tables(
            sel_t, cnt[0, :n_exp], before[:, 0, :n_exp], n_rows, blk, _run_pieces(ts))
        xs = _dispatch_call(pad_end, pad_start, tab3, rows8, h2, n_rows, blk)
        ys = _expert_call(block_e, n_used, xs, w_gu, b_gu4, w_down, b_down4, blk, l)
        x = _combine_call(tab3, ys, x1, sel, rows8, ada[l], ln2_g[l], ln2_b[l], seq).reshape(bsz, seq, d)
    return x
```

```python
import jax
import jax.numpy as jnp
from jax import lax
from jax.experimental import pallas as pl
from jax.experimental.pallas import tpu as pltpu

F32 = jnp.float32
MXU_DTYPE = jnp.bfloat16

DEPTH = 4
RET_HEADS = 4
RET_QK_DIM = 32
RET_V_DIM = 64
RET_QK_WIDTH = RET_HEADS * RET_QK_DIM
RET_V_WIDTH = RET_HEADS * RET_V_DIM
CHUNK = 128
ROPE_BASE = 10000.0
GMLP_GROUPS = 4
GMLP_WIDTH = 256
CONV_WIDTH = 256
GATE_RANK = 128
N_BRANCH = 3
N_EXPERTS = 32
TOP_K = 4
D_FF = 256
SWIGLU_LIMIT = 7.0
SWIGLU_ALPHA = 1.702
DEEPNORM_ALPHA = (2.0 * DEPTH) ** 0.25
LN_EPS = 1e-5

_O_QK = 0
_O_VG = 2 * RET_QK_WIDTH
_O_GMLP = _O_VG + 2 * RET_V_WIDTH
_O_CONV = _O_GMLP + 2 * GMLP_WIDTH
_O_CODE = _O_CONV + 3 * CONV_WIDTH
_IN_WIDTH = _O_CODE + N_BRANCH * GATE_RANK

LANES = 128
SUBLANES = 8
NEG_BIG = -1e30

SEQ_TILE = 256
MIXER_STEP = 512
MOE_BLOCK = 2048
EXPERT_PART = 1024
COMBINE_PART = 128
SELT_ROWS = 16
RUN_ALIGN = 16
VMEM_LIMIT = 56 * 1024 * 1024


def _dot(a, b):
    return jnp.dot(a.astype(MXU_DTYPE), b.astype(MXU_DTYPE), preferred_element_type=F32)


def _dot_nt(a, b):
    return lax.dot_general(a.astype(MXU_DTYPE), b.astype(MXU_DTYPE),
                           (((1,), (1,)), ((), ())), preferred_element_type=F32)


def _split_dot(x, w):
    hi = x.astype(MXU_DTYPE)
    lo = (x - hi.astype(F32)).astype(MXU_DTYPE)
    return (jnp.dot(hi, w, preferred_element_type=F32)
            + jnp.dot(lo, w, preferred_element_type=F32))


def _layernorm_rows(x, g, b):
    mu = jnp.mean(x, axis=-1, keepdims=True)
    d = x - mu
    var = jnp.mean(d * d, axis=-1, keepdims=True)
    return d * lax.rsqrt(var + LN_EPS) * g + b


def _ada_kernel(c_ref, w_ref, b_ref, o_ref):
    c_act = jax.nn.silu(c_ref[...])
    o_ref[...] = jnp.dot(c_act, w_ref[...], preferred_element_type=F32,
                         precision=lax.Precision.HIGHEST) + b_ref[...]


def _ada_call(c, w_ada, b_ada):
    depth, d, six_d = w_ada.shape
    bsz = c.shape[0]
    n_col = six_d // d
    return pl.pallas_call(
        _ada_kernel,
        grid=(depth, n_col),
        in_specs=[
            pl.BlockSpec((bsz, d), lambda l, j: (0, 0)),
            pl.BlockSpec((None, d, d), lambda l, j: (l, 0, j)),
            pl.BlockSpec((None, 1, d), lambda l, j: (l, 0, j)),
        ],
        out_specs=pl.BlockSpec((None, None, bsz, d), lambda l, j: (l, j, 0, 0)),
        out_shape=jax.ShapeDtypeStruct((depth, n_col, bsz, d), F32),
        compiler_params=pltpu.CompilerParams(
            dimension_semantics=("arbitrary", "arbitrary"), vmem_limit_bytes=VMEM_LIMIT),
        name="ada_ln",
    )(c, w_ada, b_ada.reshape(depth, 1, six_d))


_MIXER_PHASES = 3


def _mixer_kernel(x_ref, ada_ref, cos_ref, sin_ref, *rest):
    params = rest[:-9]
    x1_ref, h2_ref, sel_ref, selt_ref, cnt_ref, before_ref, state_ref, zc_ref, cntacc_ref = rest[-9:]
    b = pl.program_id(0)
    s = pl.program_id(1)

    @pl.when(s == 0)
    def _():
        state_ref[...] = jnp.zeros_like(state_ref)
        zc_ref[...] = jnp.zeros_like(zc_ref)

    @pl.when(jnp.logical_and(b == 0, s == 0))
    def _():
        cntacc_ref[...] = jnp.zeros_like(cntacc_ref)

    n_sub = before_ref.shape[0]
    ts = x_ref.shape[0] // n_sub
    tiles = []
    for u in range(n_sub):
        rows = pl.ds(u * ts, ts)
        tiles.append(_mixer_tile(x_ref.at[rows], ada_ref, cos_ref.at[rows], sin_ref.at[rows], *params,
                                 x1_ref.at[rows], h2_ref.at[rows], sel_ref.at[rows], selt_ref.at[:, rows],
                                 cnt_ref, before_ref.at[u], state_ref, zc_ref, cntacc_ref))
    for tile in tiles:
        next(tile)
    for tile in tiles:
        for _ in range(_MIXER_PHASES - 1):
            next(tile, None)


def _mixer_tile(x_ref, ada_ref, cos_ref, sin_ref, w_in_ref, lng_ref, lnb_ref, wsp_ref, bsp_ref,
                convw_ref, wbr_ref, wgt_ref, bgt_ref, wout_ref, ln1g_ref, ln1b_ref, wr_ref, br_ref,
                decay_ref, zeta_ref, xi_ref, cdec_ref, bmask_ref, gavg_ref,
                x1_ref, h2_ref, sel_ref, selt_ref, cnt_ref, before_ref,
                state_ref, zc_ref, cntacc_ref):
    ts = x_ref.shape[0]
    x = x_ref[...]
    ada = ada_ref[...]
    sh1, sc1, gt1, sh2, sc2 = ada[0:1], ada[1:2], ada[2:3], ada[3:4], ada[4:5]
    h = (x * (1.0 + sc1) + sh1).astype(MXU_DTYPE)

    def proj(lo, hi):
        return jnp.dot(h, w_in_ref[:, lo:hi], preferred_element_type=F32)

    lane_qk = lax.broadcasted_iota(jnp.int32, (1, RET_QK_WIDTH), 1)
    lane_v = lax.broadcasted_iota(jnp.int32, (1, RET_V_WIDTH), 1)
    qk_masks = [lane_qk // RET_QK_DIM == hd for hd in range(RET_HEADS)]
    v_masks = [lane_v // RET_V_DIM == hd for hd in range(RET_HEADS)]

    qk = proj(_O_QK, _O_VG)
    cos = cos_ref[...]
    sin = sin_ref[...]
    first_half = (lane_qk % RET_QK_DIM) < (RET_QK_DIM // 2)

    def rotary(t):
        swapped = jnp.where(first_half, pltpu.roll(t, RET_QK_WIDTH - RET_QK_DIM // 2, 1),
                            pltpu.roll(t, RET_QK_DIM // 2, 1))
        return t * cos + swapped * sin

    q = rotary(qk[:, :RET_QK_WIDTH])
    k = rotary(qk[:, RET_QK_WIDTH:]) * (RET_QK_DIM ** -0.5)
    vg = proj(_O_VG, _O_GMLP)
    v = vg[:, :RET_V_WIDTH]
    g = vg[:, RET_V_WIDTH:]

    decay = decay_ref[...]
    zeta = zeta_ref[...]
    xi = xi_ref[...]
    cdec = cdec_ref[...]
    bmask = bmask_ref[...]
    gavg = gavg_ref[...]

    o_chunks = []
    state = state_ref[...]
    for c in range(ts // CHUNK):
        rows = slice(c * CHUNK, (c + 1) * CHUNK)
        q_c, k_c, v_c = q[rows], k[rows], v[rows]
        v_m = v_c.astype(MXU_DTYPE)
        q_all = jnp.concatenate([jnp.where(m, q_c, 0.0) for m in qk_masks], axis=0)
        scores = _dot_nt(q_all, k_c) * decay
        o_all = _dot(scores, v_m)
        o = _dot(q_c, state) * xi
        for hd in range(RET_HEADS):
            o = o + jnp.where(v_masks[hd], o_all[hd * CHUNK:(hd + 1) * CHUNK], 0.0)
        kv = _dot((k_c * zeta).T, v_m) * bmask
        state = state * cdec + kv
        o_chunks.append(o)
    state_ref[...] = state
    o = jnp.concatenate(o_chunks, axis=0) if len(o_chunks) > 1 else o_chunks[0]
    mu = _split_dot(o, gavg)
    d = o - mu
    var = _split_dot(d * d, gavg)
    r_br = jax.nn.silu(g) * (d * lax.rsqrt(var + LN_EPS))

    guv = proj(_O_GMLP, _O_CONV)
    u = jax.nn.gelu(guv[:, :GMLP_WIDTH])
    vv = _layernorm_rows(jax.nn.gelu(guv[:, GMLP_WIDTH:]), lng_ref[...], lnb_ref[...]).astype(MXU_DTYPE)
    rr = lax.broadcasted_iota(jnp.int32, (GMLP_GROUPS * CHUNK, CHUNK), 0) % CHUNK
    cc_ = lax.broadcasted_iota(jnp.int32, (GMLP_GROUPS * CHUNK, CHUNK), 1)
    w_sp = jnp.where(cc_ <= rr, wsp_ref[...], 0.0).astype(MXU_DTYPE)
    bsp = bsp_ref[...]
    z_chunks = []
    for c in range(ts // CHUNK):
        z_all = jnp.dot(w_sp, vv[c * CHUNK:(c + 1) * CHUNK], preferred_element_type=F32)
        z = bsp
        for gi in range(GMLP_GROUPS):
            z = z + jnp.where(v_masks[gi], z_all[gi * CHUNK:(gi + 1) * CHUNK], 0.0)
        z_chunks.append(z)
    z = jnp.concatenate(z_chunks, axis=0) if len(z_chunks) > 1 else z_chunks[0]
    s_br = u * z

    cbcx = proj(_O_CONV, _O_CODE)
    gate_b = cbcx[:, :CONV_WIDTH]
    zc = cbcx[:, CONV_WIDTH:2 * CONV_WIDTH] * cbcx[:, 2 * CONV_WIDTH:]
    carry = zc_ref[...]
    prev1 = carry[SUBLANES - 1:SUBLANES]
    prev2 = carry[SUBLANES - 2:SUBLANES - 1]
    row = lax.broadcasted_iota(jnp.int32, (ts, 1), 0)
    z1 = jnp.where(row == 0, prev1, pltpu.roll(zc, 1, 0))
    z2 = jnp.where(row == 0, prev2, jnp.where(row == 1, prev1, pltpu.roll(zc, 2, 0)))
    zc_ref[...] = zc[ts - SUBLANES:ts]
    cw = convw_ref[...]
    k_br = gate_b * (cw[0:1] * z2 + cw[1:2] * z1 + cw[2:3] * zc)

    code = proj(_O_CODE, _IN_WIDTH)
    yield
    merged = None
    for i, br in enumerate((r_br, s_br, k_br)):
        y = jnp.dot(br.astype(MXU_DTYPE), wbr_ref[i], preferred_element_type=F32)
        gl = jnp.dot(code[:, i * GATE_RANK:(i + 1) * GATE_RANK].astype(MXU_DTYPE), wgt_ref[i],
                     preferred_element_type=F32) + bgt_ref[i:i + 1]
        t = jax.nn.sigmoid(gl) * y
        merged = t if merged is None else merged + t
    mix = jnp.dot(merged.astype(MXU_DTYPE), wout_ref[...], preferred_element_type=F32)
    x1 = _layernorm_rows(DEEPNORM_ALPHA * x + (1.0 + gt1) * mix, ln1g_ref[...], ln1b_ref[...])
    x1_ref[...] = x1
    h2 = (x1 * (1.0 + sc2) + sh2).astype(MXU_DTYPE)
    h2_ref[...] = h2

    logits = jnp.dot(h2, wr_ref[...], preferred_element_type=F32) + br_ref[...]
    yield
    lane = lax.broadcasted_iota(jnp.int32, (1, LANES), 1).astype(F32)
    top_v, top_i = [], []
    work = logits
    for _ in range(TOP_K):
        m = jnp.max(work, axis=-1, keepdims=True)
        i_sel = jnp.min(jnp.where(work == m, lane, float(LANES)), axis=-1, keepdims=True)
        top_v.append(m)
        top_i.append(i_sel)
        work = jnp.where(lane == i_sel, -jnp.inf, work)
    exps = [jnp.exp(tv - top_v[0]) for tv in top_v]
    denom = exps[0] + exps[1] + exps[2] + exps[3]
    onehots = [lane == ti for ti in top_i]
    member = jnp.zeros((ts, LANES), F32)
    for oh in onehots:
        member = member + jnp.where(oh, 1.0, 0.0)
    tr = lax.broadcasted_iota(jnp.int32, (ts, ts), 0)
    tc = lax.broadcasted_iota(jnp.int32, (ts, ts), 1)
    before = jnp.where(tc < tr, 1.0, 0.0).astype(MXU_DTYPE)
    cnt = cntacc_ref[...]
    local_rank = jnp.dot(before, member.astype(MXU_DTYPE), preferred_element_type=F32)
    sel = jnp.zeros((ts, LANES), F32)
    for kk in range(TOP_K):
        lrank_k = jnp.sum(jnp.where(onehots[kk], local_rank, 0.0), axis=-1, keepdims=True)
        sel = sel + jnp.where(lane == float(kk), top_i[kk], 0.0)
        sel = sel + jnp.where(lane == float(TOP_K + kk), exps[kk] / denom, 0.0)
        sel = sel + jnp.where(lane == float(2 * TOP_K + kk), lrank_k, 0.0)
    sel_ref[...] = sel
    selt_ref[...] = sel.T[:selt_ref.shape[0]]
    before_ref[...] = cnt
    tile_cnt = jnp.sum(member, axis=0, keepdims=True)
    cnt = cnt + jnp.floor((tile_cnt + (RUN_ALIGN - 1.0)) * (1.0 / RUN_ALIGN)) * RUN_ALIGN
    cntacc_ref[...] = cnt
    cnt_ref[...] = cnt


def _const_spec(shape):
    return pl.BlockSpec(shape, lambda b, s: (0,) * len(shape))


def _param_spec(a, layer):
    if layer is None:
        return _const_spec(a.shape)
    return pl.BlockSpec((None,) + a.shape[1:], lambda b, s: (layer,) + (0,) * (a.ndim - 1))


def _mixer_call(x, ada_l, cos_t, sin_t, lw, tables):
    bsz, seq, d = x.shape
    ts = min(SEQ_TILE, seq)
    step = min(MIXER_STEP, seq)
    n_tok = bsz * seq
    row3 = lambda b, s: (b, s, 0)
    in_specs = [
        pl.BlockSpec((None, step, d), row3),
        pl.BlockSpec((None, 6, d), lambda b, s: (b, 0, 0)),
        pl.BlockSpec((None, step, RET_QK_WIDTH), row3),
        pl.BlockSpec((None, step, RET_QK_WIDTH), row3),
    ] + [_param_spec(a, lay) for a, lay in lw] + [_const_spec(a.shape) for a in tables]
    tok_row = lambda b, s: (b * (seq // step) + s, 0)
    out_specs = [
        pl.BlockSpec((step, d), tok_row),
        pl.BlockSpec((step, d), tok_row),
        pl.BlockSpec((step, LANES), tok_row),
        pl.BlockSpec((SELT_ROWS, step), lambda b, s: (0, b * (seq // step) + s)),
        pl.BlockSpec((SUBLANES, LANES), lambda b, s: (0, 0)),
        pl.BlockSpec((step // ts, SUBLANES, LANES), lambda b, s: (b * (seq // step) + s, 0, 0)),
    ]
    out_shape = [
        jax.ShapeDtypeStruct((n_tok, d), F32),
        jax.ShapeDtypeStruct((n_tok, d), MXU_DTYPE),
        jax.ShapeDtypeStruct((n_tok, LANES), F32),
        jax.ShapeDtypeStruct((SELT_ROWS, n_tok), F32),
        jax.ShapeDtypeStruct((SUBLANES, LANES), F32),
        jax.ShapeDtypeStruct((n_tok // ts, SUBLANES, LANES), F32),
    ]
    return pl.pallas_call(
        _mixer_kernel,
        grid=(bsz, seq // step),
        in_specs=in_specs,
        out_specs=out_specs,
        out_shape=out_shape,
        scratch_shapes=[
            pltpu.VMEM((RET_QK_WIDTH, RET_V_WIDTH), F32),
            pltpu.VMEM((SUBLANES, CONV_WIDTH), F32),
            pltpu.VMEM((SUBLANES, LANES), F32),
        ],
        compiler_params=pltpu.CompilerParams(
            dimension_semantics=("arbitrary", "arbitrary"), vmem_limit_bytes=VMEM_LIMIT),
        name="mixer",
    )(x, ada_l, cos_t, sin_t, *[a for a, _ in lw], *tables)


def _run_pieces(max_rows):
    units = max_rows // RUN_ALIGN
    pieces = []
    bit = 1
    while bit <= units:
        pieces.append(bit)
        bit *= 2
    return tuple(reversed(pieces))


def _piece_table_width(n_classes):
    return -(-(n_classes * 2 * N_EXPERTS + n_classes) // LANES) * LANES


def _for_each_run_piece(tab_ref, pieces, make_copy, fn):
    for c, bit in enumerate(pieces):
        count = tab_ref[0, 0, len(pieces) * 2 * N_EXPERTS + c]

        def body(j, carry, c=c, bit=bit):
            hbm_row = tab_ref[0, 0, c * 2 * N_EXPERTS + j]
            buf_row = tab_ref[0, 0, (c * 2 + 1) * N_EXPERTS + j]
            fn(make_copy(pl.multiple_of(hbm_row, RUN_ALIGN), pl.multiple_of(buf_row, RUN_ALIGN),
                         bit * RUN_ALIGN))
            return carry

        lax.fori_loop(0, count, body, 0)


SEL_IDX, SEL_WEIGHT, SEL_RANK = 0, TOP_K, 2 * TOP_K
ROWS_PAD = SUBLANES


def _dispatch_kernel(pe_ref, ps_ref, tab_cur_ref, tab_prev_ref, rows_ref, h2_ref, xs_hbm,
                     zbuf, sbuf, sem, zsem):
    i = pl.program_id(0)
    n_steps = pl.num_programs(0)
    ts = h2_ref.shape[0]
    blk = zbuf.shape[0]
    n_exp = pe_ref.shape[0]
    run_rows = sbuf.shape[1]
    slot = i % 2
    pieces = _run_pieces(ts)

    @pl.when(i == 0)
    def _():
        zbuf[...] = jnp.zeros_like(zbuf)

        def tail_copy(e):
            start = pl.multiple_of(jnp.maximum(pe_ref[e] - blk, 0), blk)
            return pltpu.make_async_copy(zbuf, xs_hbm.at[pl.ds(start, blk), :], zsem)

        def zstart(e, carry):
            @pl.when(pe_ref[e] > ps_ref[e])
            def _():
                tail_copy(e).start()
            return carry

        def zwait(e, carry):
            @pl.when(pe_ref[e] > ps_ref[e])
            def _():
                tail_copy(e).wait()
            return carry

        lax.fori_loop(0, n_exp, zstart, 0)
        lax.fori_loop(0, n_exp, zwait, 0)

    rows_t = rows_ref[...]
    h2 = h2_ref[...].astype(MXU_DTYPE)
    n_parts = 2
    part = run_rows // n_parts

    def pick_rows(lo):
        buf_row = lax.broadcasted_iota(jnp.int32, (part, ts), 0).astype(F32) + float(lo)
        pick = jnp.zeros((part, ts), F32)
        for kk in range(TOP_K):
            pick = jnp.where(buf_row == rows_t[kk:kk + 1], 1.0, pick)
        return pick.astype(MXU_DTYPE)

    pick = pick_rows(0)
    for p in range(n_parts):
        nxt = pick_rows((p + 1) * part) if p + 1 < n_parts else None
        sbuf[slot, p * part:(p + 1) * part, :] = jnp.dot(
            pick, h2, preferred_element_type=F32).astype(sbuf.dtype)
        pick = nxt

    def push(dst_slot):
        def make(hbm_row, buf_row_, rows):
            return pltpu.make_async_copy(sbuf.at[dst_slot, pl.ds(buf_row_, rows), :],
                                         xs_hbm.at[pl.ds(hbm_row, rows), :], sem.at[dst_slot])
        return make

    _for_each_run_piece(tab_cur_ref, pieces, push(slot), lambda c: c.start())

    @pl.when(i > 0)
    def _():
        _for_each_run_piece(tab_prev_ref, pieces, push(1 - slot), lambda c: c.wait())

    @pl.when(i == n_steps - 1)
    def _():
        _for_each_run_piece(tab_cur_ref, pieces, push(slot), lambda c: c.wait())


def _dispatch_call(pad_end, pad_start, tab3, rows8, h2, n_rows, blk):
    n_tok, d = h2.shape
    n_steps = tab3.shape[0]
    ts = n_tok // n_steps
    run_rows = ts * TOP_K + N_EXPERTS * RUN_ALIGN
    grid_spec = pltpu.PrefetchScalarGridSpec(
        num_scalar_prefetch=2,
        grid=(n_steps,),
        in_specs=[
            pl.BlockSpec((1, 1, tab3.shape[2]), lambda i, pe, ps: (i, 0, 0), memory_space=pltpu.SMEM),
            pl.BlockSpec((1, 1, tab3.shape[2]), lambda i, pe, ps: (jnp.maximum(i - 1, 0), 0, 0),
                         memory_space=pltpu.SMEM),
            pl.BlockSpec((ROWS_PAD, ts), lambda i, pe, ps: (0, i)),
            pl.BlockSpec((ts, d), lambda i, pe, ps: (i, 0)),
        ],
        out_specs=pl.BlockSpec(memory_space=pl.ANY),
        scratch_shapes=[pltpu.VMEM((blk, d), MXU_DTYPE), pltpu.VMEM((2, run_rows, d), MXU_DTYPE),
                        pltpu.SemaphoreType.DMA((2,)), pltpu.SemaphoreType.DMA(())],
    )
    return pl.pallas_call(
        _dispatch_kernel,
        grid_spec=grid_spec,
        out_shape=jax.ShapeDtypeStruct((n_rows, d), MXU_DTYPE),
        compiler_params=pltpu.CompilerParams(
            dimension_semantics=("arbitrary",), vmem_limit_bytes=VMEM_LIMIT),
        name="dispatch",
    )(pad_end, pad_start, tab3, tab3, rows8, h2)


def _expert_up(xs, wgu, bgu_ref):
    return jnp.dot(xs, wgu, preferred_element_type=F32) + bgu_ref[...]


def _expert_down(gu, wdn, bdn_ref):
    gate = jnp.minimum(gu[:, :D_FF], SWIGLU_LIMIT)
    up = jnp.clip(gu[:, D_FF:], -SWIGLU_LIMIT, SWIGLU_LIMIT)
    act = (up + 1.0) * (gate * jax.nn.sigmoid(SWIGLU_ALPHA * gate))
    return jnp.dot(act.astype(MXU_DTYPE), wdn, preferred_element_type=F32) + bdn_ref[...]


def _expert_kernel(be_ref, nu_ref, xs_ref, wgu_ref, bgu_ref, wdn_ref, bdn_ref, ys_ref):
    i = pl.program_id(0)
    n_used = nu_ref[0]

    @pl.when(i < n_used)
    def _():
        part = min(EXPERT_PART, xs_ref.shape[0])
        n_parts = xs_ref.shape[0] // part
        rows = [slice(p * part, (p + 1) * part) for p in range(n_parts)]
        wgu = wgu_ref[...].astype(MXU_DTYPE)
        wdn = wdn_ref[...].astype(MXU_DTYPE)
        gu = _expert_up(xs_ref[rows[0], :], wgu, bgu_ref)
        for p in range(n_parts):
            gu_next = _expert_up(xs_ref[rows[p + 1], :], wgu, bgu_ref) if p + 1 < n_parts else None
            ys_ref[rows[p], :] = _expert_down(gu, wdn, bdn_ref).astype(ys_ref.dtype)
            gu = gu_next

    @pl.when(i >= n_used)
    def _():
        ys_ref[...] = jnp.zeros_like(ys_ref)


def _expert_call(block_e, n_used, xs, w_gu, b_gu, w_down, b_down, blk, layer):
    n_rows, d = xs.shape
    n_blocks = n_rows // blk
    two_f = w_gu.shape[3]
    last = lambda nu: jnp.maximum(nu[0] - 1, 0)
    of_block = lambda i, be, nu: (layer, be[i], 0, 0)
    grid_spec = pltpu.PrefetchScalarGridSpec(
        num_scalar_prefetch=2,
        grid=(n_blocks,),
        in_specs=[
            pl.BlockSpec((blk, d), lambda i, be, nu: (jnp.minimum(i, last(nu)), 0)),
            pl.BlockSpec((None, None, d, two_f), of_block),
            pl.BlockSpec((None, None, 1, two_f), of_block),
            pl.BlockSpec((None, None, two_f // 2, d), of_block),
            pl.BlockSpec((None, None, 1, d), of_block),
        ],
        out_specs=pl.BlockSpec((blk, d), lambda i, be, nu: (i, 0)),
    )
    return pl.pallas_call(
        _expert_kernel,
        grid_spec=grid_spec,
        out_shape=jax.ShapeDtypeStruct((n_rows, d), MXU_DTYPE),
        compiler_params=pltpu.CompilerParams(
            dimension_semantics=("arbitrary",), vmem_limit_bytes=VMEM_LIMIT),
        name="experts",
    )(block_e, n_used, xs, w_gu, b_gu, w_down, b_down)


def _combine_kernel(tab_cur_ref, tab_nxt_ref, ys_hbm, x1_ref, sel_ref, rows_ref, ada_ref, lng_ref, lnb_ref,
                    o_ref, rbuf, sem):
    i = pl.program_id(0)
    n_steps = pl.num_programs(0)
    ts = x1_ref.shape[0]
    slot = i % 2
    pieces = _run_pieces(ts)

    def fetch(dst_slot):
        def make(hbm_row, buf_row, rows):
            return pltpu.make_async_copy(ys_hbm.at[pl.ds(hbm_row, rows), :],
                                         rbuf.at[dst_slot, pl.ds(buf_row, rows), :], sem.at[dst_slot])
        return make

    @pl.when(i == 0)
    def _():
        rbuf[...] = jnp.zeros_like(rbuf)
        _for_each_run_piece(tab_cur_ref, pieces, fetch(0), lambda c: c.start())

    @pl.when(i + 1 < n_steps)
    def _():
        _for_each_run_piece(tab_nxt_ref, pieces, fetch(1 - slot), lambda c: c.start())

    _for_each_run_piece(tab_cur_ref, pieces, fetch(slot), lambda c: c.wait())

    sel = sel_ref[...]
    rows_t = rows_ref[...]
    rows_c = jnp.concatenate([rows_t, jnp.zeros((LANES - ROWS_PAD, ts), F32)], axis=0).T
    col = lax.broadcasted_iota(jnp.int32, (1, rbuf.shape[1]), 1).astype(F32)
    gt2 = ada_ref[...][5:6]
    part = min(COMBINE_PART, ts)

    def weight_pieces(tok):
        weights = jnp.zeros((part, rbuf.shape[1]), F32)
        for kk in range(TOP_K):
            weights = jnp.where(col == rows_c[tok, kk:kk + 1],
                                sel[tok, SEL_WEIGHT + kk:SEL_WEIGHT + kk + 1], weights)
        hi = weights.astype(MXU_DTYPE)
        lo = (weights - hi.astype(F32)).astype(MXU_DTYPE)
        return jnp.concatenate([hi, lo], axis=0)

    toks = [slice(p * part, (p + 1) * part) for p in range(ts // part)]
    pieces_w = weight_pieces(toks[0])
    for p, tok in enumerate(toks):
        nxt = weight_pieces(toks[p + 1]) if p + 1 < len(toks) else None
        both = jnp.dot(pieces_w, rbuf[slot], preferred_element_type=F32)
        ffn = both[:part] + both[part:]
        o_ref[tok, :] = _layernorm_rows(DEEPNORM_ALPHA * x1_ref[tok, :] + (1.0 + gt2) * ffn,
                                        lng_ref[...], lnb_ref[...])
        pieces_w = nxt


def _combine_call(tab3, ys, x1, sel, rows8, ada_l, ln_g, ln_b, seq):
    n_tok, d = x1.shape
    n_steps = tab3.shape[0]
    ts = n_tok // n_steps
    tiles_per_seq = seq // ts
    run_rows = ts * TOP_K + N_EXPERTS * RUN_ALIGN
    return pl.pallas_call(
        _combine_kernel,
        grid=(n_steps,),
        in_specs=[
            pl.BlockSpec((1, 1, tab3.shape[2]), lambda i: (i, 0, 0), memory_space=pltpu.SMEM),
            pl.BlockSpec((1, 1, tab3.shape[2]), lambda i: (jnp.minimum(i + 1, n_steps - 1), 0, 0),
                         memory_space=pltpu.SMEM),
            pl.BlockSpec(memory_space=pl.ANY),
            pl.BlockSpec((ts, d), lambda i: (i, 0)),
            pl.BlockSpec((ts, LANES), lambda i: (i, 0)),
            pl.BlockSpec((ROWS_PAD, ts), lambda i: (0, i)),
            pl.BlockSpec((None, 6, d), lambda i: (i // tiles_per_seq, 0, 0)),
            pl.BlockSpec((1, d), lambda i: (0, 0)),
            pl.BlockSpec((1, d), lambda i: (0, 0)),
        ],
        out_specs=pl.BlockSpec((ts, d), lambda i: (i, 0)),
        out_shape=jax.ShapeDtypeStruct((n_tok, d), F32),
        scratch_shapes=[pltpu.VMEM((2, run_rows, d), MXU_DTYPE), pltpu.SemaphoreType.DMA((2,))],
        compiler_params=pltpu.CompilerParams(
            dimension_semantics=("arbitrary",), vmem_limit_bytes=VMEM_LIMIT),
        name="combine",
    )(tab3, tab3, ys, x1, sel, rows8, ada_l, ln_g.reshape(1, d), ln_b.reshape(1, d))


def _retention_tables():
    log_gamma = jnp.log(1.0 - 2.0 ** (-5.0 - jnp.arange(RET_HEADS, dtype=F32)))
    idx = jnp.arange(CHUNK, dtype=F32)
    diff = idx[:, None] - idx[None, :]
    decay = jnp.where(diff[None] >= 0,
                      jnp.exp(log_gamma[:, None, None] * jnp.maximum(diff, 0.0)[None]), 0.0)
    decay = decay.reshape(RET_HEADS * CHUNK, CHUNK)
    zeta = jnp.exp(log_gamma[:, None] * (CHUNK - 1.0 - idx)[None])
    zeta_t = jnp.repeat(zeta.T, RET_QK_DIM, axis=1)
    xi = jnp.exp(log_gamma[:, None] * (idx + 1.0)[None]).T
    xi_t = jnp.repeat(xi, RET_V_DIM, axis=1)
    cdec = jnp.repeat(jnp.exp(log_gamma * CHUNK), RET_V_DIM)[None, :]
    row_head = jnp.arange(RET_QK_WIDTH) // RET_QK_DIM
    col_head = jnp.arange(RET_V_WIDTH) // RET_V_DIM
    same_head = row_head[:, None] == col_head[None, :]
    bmask = same_head.astype(F32)
    gavg = ((col_head[:, None] == col_head[None, :]).astype(F32) / RET_V_DIM).astype(MXU_DTYPE)
    return decay, zeta_t, xi_t, cdec, bmask, gavg


def _rotary_tables(positions):
    inv_freq = ROPE_BASE ** (-jnp.arange(0, RET_QK_DIM, 2, dtype=F32) / RET_QK_DIM)
    ang = positions.astype(F32)[..., None] * inv_freq
    cos = jnp.cos(ang)
    sin = jnp.sin(ang)
    cos_t = jnp.tile(jnp.concatenate([cos, cos], axis=-1), (1, 1, RET_HEADS))
    sin_t = jnp.tile(jnp.concatenate([-sin, sin], axis=-1), (1, 1, RET_HEADS))
    return cos_t, sin_t


def _moe_rows(n_tok, n_tiles, blk):
    worst = n_tok * TOP_K + n_tiles * N_EXPERTS * (RUN_ALIGN - 1)
    return (-(-worst // blk) + N_EXPERTS) * blk


def _routing_tables(sel_t, counts, before, n_rows, blk, pieces):
    n_blocks = n_rows // blk
    sizes = counts.astype(jnp.int32)
    pad_sizes = (sizes + blk - 1) // blk * blk
    pad_end = jnp.cumsum(pad_sizes)
    pad_start = pad_end - pad_sizes
    n_used = (pad_end[-1] // blk).astype(jnp.int32)
    block_start = jnp.arange(n_blocks, dtype=jnp.int32) * blk
    block_start = jnp.minimum(block_start, (n_used - 1) * blk)
    block_e = jnp.sum((pad_end[None, :] <= block_start[:, None]).astype(jnp.int32), axis=-1)
    block_e = jnp.minimum(block_e, N_EXPERTS - 1)

    before = before.astype(jnp.int32)
    after = jnp.concatenate([before[1:], sizes[None, :]], axis=0)
    run_start = pad_start[None, :] + before
    run_units = (after - before) // RUN_ALIGN
    run_off = (jnp.cumsum(run_units, axis=-1) - run_units) * RUN_ALIGN
    n_tiles = before.shape[0]
    slot_id = jnp.arange(N_EXPERTS, dtype=jnp.int32)
    cols, counts_c = [], []
    for bit in pieces:
        has = jnp.bitwise_and(run_units, bit) != 0
        done = jnp.bitwise_and(run_units, -2 * bit) * RUN_ALIGN
        pos = jnp.cumsum(has.astype(jnp.int32), axis=-1) - 1
        place = jnp.logical_and(has[:, None, :], pos[:, None, :] == slot_id[None, :, None])
        cols.append(jnp.sum(jnp.where(place, (run_start + done)[:, None, :], 0), axis=-1))
        cols.append(jnp.sum(jnp.where(place, (run_off + done)[:, None, :], 0), axis=-1))
        counts_c.append(jnp.sum(has.astype(jnp.int32), axis=-1, keepdims=True))
    width = _piece_table_width(len(pieces))
    used = len(pieces) * (2 * N_EXPERTS + 1)
    tab = jnp.concatenate(cols + counts_c + [jnp.zeros((n_tiles, width - used), jnp.int32)], axis=-1)

    n_tok = sel_t.shape[1]
    idx = sel_t[SEL_IDX:SEL_IDX + TOP_K].astype(jnp.int32)
    off_tok = jnp.repeat(run_off.T, n_tok // n_tiles, axis=1)
    base = jnp.sum(jnp.where(idx[:, None, :] == slot_id[None, :, None], off_tok[None], 0), axis=1)
    rows = base.astype(F32) + sel_t[SEL_RANK:SEL_RANK + TOP_K]
    rows8 = jnp.pad(rows, ((0, ROWS_PAD - TOP_K), (0, 0)))
    return (pad_end.astype(jnp.int32), pad_start.astype(jnp.int32), block_e, n_used.reshape(1),
            tab.astype(jnp.int32).reshape(n_tiles, 1, width), rows8)


def kernel(x, c, positions, w_ada, b_ada, w_in, gmlp_ln_g, gmlp_ln_b, w_spatial, b_spatial, conv_w,
           w_branch, w_gate_up, b_gate, w_out, ln1_g, ln1_b, w_router, b_router, w_gu, b_gu, w_down,
           b_down, ln2_g, ln2_b):
    bsz, seq, d = x.shape
    n_tok = bsz * seq
    depth = w_ada.shape[0]
    blk = min(MOE_BLOCK, n_tok * TOP_K // N_EXPERTS)
    ts = min(SEQ_TILE, seq)
    assert depth == DEPTH and w_router.shape[-1] == N_EXPERTS and w_in.shape[-1] == _IN_WIDTH
    assert ts % CHUNK == 0 and seq % min(MIXER_STEP, seq) == 0 and min(MIXER_STEP, seq) % ts == 0
    assert blk % RUN_ALIGN == 0 and d % LANES == 0

    ada = _ada_call(c, w_ada, b_ada)
    ada = jnp.transpose(ada, (0, 2, 1, 3))
    cos_t, sin_t = _rotary_tables(positions)
    tables = _retention_tables()

    mx = MXU_DTYPE
    w_in_m = w_in.astype(mx)
    w_branch_m = w_branch.astype(mx)
    w_gate_m = w_gate_up.astype(mx)
    w_out_m = w_out.astype(mx)
    b_gu4 = b_gu[:, :, None, :]
    b_down4 = b_down[:, :, None, :]
    n_exp = w_router.shape[-1]
    w_router_m = jnp.pad(w_router, ((0, 0), (0, 0), (0, LANES - n_exp))).astype(mx)
    b_router_p = jnp.pad(b_router, ((0, 0), (0, LANES - n_exp)), constant_values=NEG_BIG)

    for l in range(depth):
        bsp_t = jnp.repeat(b_spatial[l].T, GMLP_WIDTH // GMLP_GROUPS, axis=1)
        lw = (
            (w_in_m, l),
            (gmlp_ln_g[l].reshape(1, -1), None), (gmlp_ln_b[l].reshape(1, -1), None),
            (w_spatial[l].reshape(GMLP_GROUPS * CHUNK, CHUNK), None), (bsp_t, None),
            (jnp.pad(conv_w[l], ((0, SUBLANES - conv_w.shape[1]), (0, 0))), None),
            (w_branch_m, l), (w_gate_m, l), (b_gate, l), (w_out_m, l),
            (ln1_g[l].reshape(1, -1), None), (ln1_b[l].reshape(1, -1), None),
            (w_router_m, l), (b_router_p[l].reshape(1, -1), None),
        )
        x1, h2, sel, sel_t, cnt, before = _mixer_call(x, ada[l], cos_t, sin_t, lw, tables)
        n_rows = _moe_rows(n_tok, n_tok // ts, blk)
        pad_end, pad_start, block_e, n_used, tab3, rows8 = _routing_tables(
            sel_t, cnt[0, :n_exp], before[:, 0, :n_exp], n_rows, blk, _run_pieces(ts))
        xs = _dispatch_call(pad_end, pad_start, tab3, rows8, h2, n_rows, blk)
        ys = _expert_call(block_e, n_used, xs, w_gu, b_gu4, w_down, b_down4, blk, l)
        x = _combine_call(tab3, ys, x1, sel, rows8, ada[l], ln2_g[l], ln2_b[l], seq).reshape(bsz, seq, d)
    return x
```

```python
import jax
import jax.numpy as jnp
from jax import lax
from jax.experimental import pallas as pl
from jax.experimental.pallas import tpu as pltpu

F32 = jnp.float32
MXU_DTYPE = jnp.bfloat16

DEPTH = 4
RET_HEADS = 4
RET_QK_DIM = 32
RET_V_DIM = 64
RET_QK_WIDTH = RET_HEADS * RET_QK_DIM
RET_V_WIDTH = RET_HEADS * RET_V_DIM
CHUNK = 128
ROPE_BASE = 10000.0
GMLP_GROUPS = 4
GMLP_WIDTH = 256
CONV_WIDTH = 256
GATE_RANK = 128
N_BRANCH = 3
N_EXPERTS = 32
TOP_K = 4
D_FF = 256
SWIGLU_LIMIT = 7.0
SWIGLU_ALPHA = 1.702
DEEPNORM_ALPHA = (2.0 * DEPTH) ** 0.25
LN_EPS = 1e-5

_O_QK = 0
_O_VG = 2 * RET_QK_WIDTH
_O_GMLP = _O_VG + 2 * RET_V_WIDTH
_O_CONV = _O_GMLP + 2 * GMLP_WIDTH
_O_CODE = _O_CONV + 3 * CONV_WIDTH
_IN_WIDTH = _O_CODE + N_BRANCH * GATE_RANK

LANES = 128
SUBLANES = 8
NEG_BIG = -1e30

SEQ_TILE = 256
MIXER_STEP = 512
MOE_BLOCK = 2048
EXPERT_PART = 1024
COMBINE_PART = 128
SELT_ROWS = 16
RUN_ALIGN = 16
VMEM_LIMIT = 56 * 1024 * 1024


def _dot(a, b):
    return jnp.dot(a.astype(MXU_DTYPE), b.astype(MXU_DTYPE), preferred_element_type=F32)


def _dot_nt(a, b):
    return lax.dot_general(a.astype(MXU_DTYPE), b.astype(MXU_DTYPE),
                           (((1,), (1,)), ((), ())), preferred_element_type=F32)


def _split_dot(x, w):
    hi = x.astype(MXU_DTYPE)
    lo = (x - hi.astype(F32)).astype(MXU_DTYPE)
    return (jnp.dot(hi, w, preferred_element_type=F32)
            + jnp.dot(lo, w, preferred_element_type=F32))


def _layernorm_rows(x, g, b):
    mu = jnp.mean(x, axis=-1, keepdims=True)
    d = x - mu
    var = jnp.mean(d * d, axis=-1, keepdims=True)
    return d * lax.rsqrt(var + LN_EPS) * g + b


def _ada_kernel(c_ref, w_ref, b_ref, o_ref):
    c_act = jax.nn.silu(c_ref[...])
    o_ref[...] = jnp.dot(c_act, w_ref[...], preferred_element_type=F32,
                         precision=lax.Precision.HIGHEST) + b_ref[...]


def _ada_call(c, w_ada, b_ada):
    depth, d, six_d = w_ada.shape
    bsz = c.shape[0]
    n_col = six_d // d
    return pl.pallas_call(
        _ada_kernel,
        grid=(depth, n_col),
        in_specs=[
            pl.BlockSpec((bsz, d), lambda l, j: (0, 0)),
            pl.BlockSpec((None, d, d), lambda l, j: (l, 0, j)),
            pl.BlockSpec((None, 1, d), lambda l, j: (l, 0, j)),
        ],
        out_specs=pl.BlockSpec((None, None, bsz, d), lambda l, j: (l, j, 0, 0)),
        out_shape=jax.ShapeDtypeStruct((depth, n_col, bsz, d), F32),
        compiler_params=pltpu.CompilerParams(
            dimension_semantics=("arbitrary", "arbitrary"), vmem_limit_bytes=VMEM_LIMIT),
        name="ada_ln",
    )(c, w_ada, b_ada.reshape(depth, 1, six_d))


_MIXER_PHASES = 3


def _mixer_kernel(x_ref, ada_ref, cos_ref, sin_ref, *rest):
    params = rest[:-9]
    x1_ref, h2_ref, sel_ref, selt_ref, cnt_ref, before_ref, state_ref, zc_ref, cntacc_ref = rest[-9:]
    b = pl.program_id(0)
    s = pl.program_id(1)

    @pl.when(s == 0)
    def _():
        state_ref[...] = jnp.zeros_like(state_ref)
        zc_ref[...] = jnp.zeros_like(zc_ref)

    @pl.when(jnp.logical_and(b == 0, s == 0))
    def _():
        cntacc_ref[...] = jnp.zeros_like(cntacc_ref)

    n_sub = before_ref.shape[0]
    ts = x_ref.shape[0] // n_sub
    tiles = []
    for u in range(n_sub):
        rows = pl.ds(u * ts, ts)
        tiles.append(_mixer_tile(x_ref.at[rows], ada_ref, cos_ref.at[rows], sin_ref.at[rows], *params,
                                 x1_ref.at[rows], h2_ref.at[rows], sel_ref.at[rows], selt_ref.at[:, rows],
                                 cnt_ref, before_ref.at[u], state_ref, zc_ref, cntacc_ref))
    for _ in range(_MIXER_PHASES):
        for tile in tiles:
            next(tile, None)


def _mixer_tile(x_ref, ada_ref, cos_ref, sin_ref, w_in_ref, lng_ref, lnb_ref, wsp_ref, bsp_ref,
                convw_ref, wbr_ref, wgt_ref, bgt_ref, wout_ref, ln1g_ref, ln1b_ref, wr_ref, br_ref,
                decay_ref, zeta_ref, xi_ref, cdec_ref, bmask_ref, gavg_ref,
                x1_ref, h2_ref, sel_ref, selt_ref, cnt_ref, before_ref,
                state_ref, zc_ref, cntacc_ref):
    ts = x_ref.shape[0]
    x = x_ref[...]
    ada = ada_ref[...]
    sh1, sc1, gt1, sh2, sc2 = ada[0:1], ada[1:2], ada[2:3], ada[3:4], ada[4:5]
    h = (x * (1.0 + sc1) + sh1).astype(MXU_DTYPE)

    def proj(lo, hi):
        return jnp.dot(h, w_in_ref[:, lo:hi], preferred_element_type=F32)

    lane_qk = lax.broadcasted_iota(jnp.int32, (1, RET_QK_WIDTH), 1)
    lane_v = lax.broadcasted_iota(jnp.int32, (1, RET_V_WIDTH), 1)
    qk_masks = [lane_qk // RET_QK_DIM == hd for hd in range(RET_HEADS)]
    v_masks = [lane_v // RET_V_DIM == hd for hd in range(RET_HEADS)]

    qk = proj(_O_QK, _O_VG)
    cos = cos_ref[...]
    sin = sin_ref[...]
    first_half = (lane_qk % RET_QK_DIM) < (RET_QK_DIM // 2)

    def rotary(t):
        swapped = jnp.where(first_half, pltpu.roll(t, RET_QK_WIDTH - RET_QK_DIM // 2, 1),
                            pltpu.roll(t, RET_QK_DIM // 2, 1))
        return t * cos + swapped * sin

    q = rotary(qk[:, :RET_QK_WIDTH])
    k = rotary(qk[:, RET_QK_WIDTH:]) * (RET_QK_DIM ** -0.5)
    vg = proj(_O_VG, _O_GMLP)
    v = vg[:, :RET_V_WIDTH]
    g = vg[:, RET_V_WIDTH:]

    decay = decay_ref[...]
    zeta = zeta_ref[...]
    xi = xi_ref[...]
    cdec = cdec_ref[...]
    bmask = bmask_ref[...]
    gavg = gavg_ref[...]

    o_chunks = []
    state = state_ref[...]
    for c in range(ts // CHUNK):
        rows = slice(c * CHUNK, (c + 1) * CHUNK)
        q_c, k_c, v_c = q[rows], k[rows], v[rows]
        v_m = v_c.astype(MXU_DTYPE)
        q_all = jnp.concatenate([jnp.where(m, q_c, 0.0) for m in qk_masks], axis=0)
        scores = _dot_nt(q_all, k_c) * decay
        o_all = _dot(scores, v_m)
        o = _dot(q_c, state) * xi
        for hd in range(RET_HEADS):
            o = o + jnp.where(v_masks[hd], o_all[hd * CHUNK:(hd + 1) * CHUNK], 0.0)
        kv = _dot((k_c * zeta).T, v_m) * bmask
        state = state * cdec + kv
        o_chunks.append(o)
    state_ref[...] = state
    o = jnp.concatenate(o_chunks, axis=0) if len(o_chunks) > 1 else o_chunks[0]
    mu = _split_dot(o, gavg)
    d = o - mu
    var = _split_dot(d * d, gavg)
    r_br = jax.nn.silu(g) * (d * lax.rsqrt(var + LN_EPS))

    guv = proj(_O_GMLP, _O_CONV)
    u = jax.nn.gelu(guv[:, :GMLP_WIDTH])
    vv = _layernorm_rows(jax.nn.gelu(guv[:, GMLP_WIDTH:]), lng_ref[...], lnb_ref[...]).astype(MXU_DTYPE)
    rr = lax.broadcasted_iota(jnp.int32, (GMLP_GROUPS * CHUNK, CHUNK), 0) % CHUNK
    cc_ = lax.broadcasted_iota(jnp.int32, (GMLP_GROUPS * CHUNK, CHUNK), 1)
    w_sp = jnp.where(cc_ <= rr, wsp_ref[...], 0.0).astype(MXU_DTYPE)
    bsp = bsp_ref[...]
    z_chunks = []
    for c in range(ts // CHUNK):
        z_all = jnp.dot(w_sp, vv[c * CHUNK:(c + 1) * CHUNK], preferred_element_type=F32)
        z = bsp
        for gi in range(GMLP_GROUPS):
            z = z + jnp.where(v_masks[gi], z_all[gi * CHUNK:(gi + 1) * CHUNK], 0.0)
        z_chunks.append(z)
    z = jnp.concatenate(z_chunks, axis=0) if len(z_chunks) > 1 else z_chunks[0]
    s_br = u * z

    cbcx = proj(_O_CONV, _O_CODE)
    gate_b = cbcx[:, :CONV_WIDTH]
    zc = cbcx[:, CONV_WIDTH:2 * CONV_WIDTH] * cbcx[:, 2 * CONV_WIDTH:]
    carry = zc_ref[...]
    prev1 = carry[SUBLANES - 1:SUBLANES]
    prev2 = carry[SUBLANES - 2:SUBLANES - 1]
    row = lax.broadcasted_iota(jnp.int32, (ts, 1), 0)
    z1 = jnp.where(row == 0, prev1, pltpu.roll(zc, 1, 0))
    z2 = jnp.where(row == 0, prev2, jnp.where(row == 1, prev1, pltpu.roll(zc, 2, 0)))
    zc_ref[...] = zc[ts - SUBLANES:ts]
    cw = convw_ref[...]
    k_br = gate_b * (cw[0:1] * z2 + cw[1:2] * z1 + cw[2:3] * zc)

    code = proj(_O_CODE, _IN_WIDTH)
    yield
    merged = None
    for i, br in enumerate((r_br, s_br, k_br)):
        y = jnp.dot(br.astype(MXU_DTYPE), wbr_ref[i], preferred_element_type=F32)
        gl = jnp.dot(code[:, i * GATE_RANK:(i + 1) * GATE_RANK].astype(MXU_DTYPE), wgt_ref[i],
                     preferred_element_type=F32) + bgt_ref[i:i + 1]
        t = jax.nn.sigmoid(gl) * y
        merged = t if merged is None else merged + t
    mix = jnp.dot(merged.astype(MXU_DTYPE), wout_ref[...], preferred_element_type=F32)
    x1 = _layernorm_rows(DEEPNORM_ALPHA * x + (1.0 + gt1) * mix, ln1g_ref[...], ln1b_ref[...])
    x1_ref[...] = x1
    h2 = (x1 * (1.0 + sc2) + sh2).astype(MXU_DTYPE)
    h2_ref[...] = h2

    logits = jnp.dot(h2, wr_ref[...], preferred_element_type=F32) + br_ref[...]
    yield
    lane = lax.broadcasted_iota(jnp.int32, (1, LANES), 1).astype(F32)
    top_v, top_i = [], []
    work = logits
    for _ in range(TOP_K):
        m = jnp.max(work, axis=-1, keepdims=True)
        i_sel = jnp.min(jnp.where(work == m, lane, float(LANES)), axis=-1, keepdims=True)
        top_v.append(m)
        top_i.append(i_sel)
        work = jnp.where(lane == i_sel, -jnp.inf, work)
    exps = [jnp.exp(tv - top_v[0]) for tv in top_v]
    denom = exps[0] + exps[1] + exps[2] + exps[3]
    onehots = [lane == ti for ti in top_i]
    member = jnp.zeros((ts, LANES), F32)
    for oh in onehots:
        member = member + jnp.where(oh, 1.0, 0.0)
    tr = lax.broadcasted_iota(jnp.int32, (ts, ts), 0)
    tc = lax.broadcasted_iota(jnp.int32, (ts, ts), 1)
    before = jnp.where(tc < tr, 1.0, 0.0).astype(MXU_DTYPE)
    cnt = cntacc_ref[...]
    local_rank = jnp.dot(before, member.astype(MXU_DTYPE), preferred_element_type=F32)
    sel = jnp.zeros((ts, LANES), F32)
    for kk in range(TOP_K):
        lrank_k = jnp.sum(jnp.where(onehots[kk], local_rank, 0.0), axis=-1, keepdims=True)
        sel = sel + jnp.where(lane == float(kk), top_i[kk], 0.0)
        sel = sel + jnp.where(lane == float(TOP_K + kk), exps[kk] / denom, 0.0)
        sel = sel + jnp.where(lane == float(2 * TOP_K + kk), lrank_k, 0.0)
    sel_ref[...] = sel
    selt_ref[...] = sel.T[:selt_ref.shape[0]]
    before_ref[...] = cnt
    tile_cnt = jnp.sum(member, axis=0, keepdims=True)
    cnt = cnt + jnp.floor((tile_cnt + (RUN_ALIGN - 1.0)) * (1.0 / RUN_ALIGN)) * RUN_ALIGN
    cntacc_ref[...] = cnt
    cnt_ref[...] = cnt


def _const_spec(shape):
    return pl.BlockSpec(shape, lambda b, s: (0,) * len(shape))


def _param_spec(a, layer):
    if layer is None:
        return _const_spec(a.shape)
    return pl.BlockSpec((None,) + a.shape[1:], lambda b, s: (layer,) + (0,) * (a.ndim - 1))


def _mixer_call(x, ada_l, cos_t, sin_t, lw, tables):
    bsz, seq, d = x.shape
    ts = min(SEQ_TILE, seq)
    step = min(MIXER_STEP, seq)
    n_tok = bsz * seq
    row3 = lambda b, s: (b, s, 0)
    in_specs = [
        pl.BlockSpec((None, step, d), row3),
        pl.BlockSpec((None, 6, d), lambda b, s: (b, 0, 0)),
        pl.BlockSpec((None, step, RET_QK_WIDTH), row3),
        pl.BlockSpec((None, step, RET_QK_WIDTH), row3),
    ] + [_param_spec(a, lay) for a, lay in lw] + [_const_spec(a.shape) for a in tables]
    tok_row = lambda b, s: (b * (seq // step) + s, 0)
    out_specs = [
        pl.BlockSpec((step, d), tok_row),
        pl.BlockSpec((step, d), tok_row),
        pl.BlockSpec((step, LANES), tok_row),
        pl.BlockSpec((SELT_ROWS, step), lambda b, s: (0, b * (seq // step) + s)),
        pl.BlockSpec((SUBLANES, LANES), lambda b, s: (0, 0)),
        pl.BlockSpec((step // ts, SUBLANES, LANES), lambda b, s: (b * (seq // step) + s, 0, 0)),
    ]
    out_shape = [
        jax.ShapeDtypeStruct((n_tok, d), F32),
        jax.ShapeDtypeStruct((n_tok, d), MXU_DTYPE),
        jax.ShapeDtypeStruct((n_tok, LANES), F32),
        jax.ShapeDtypeStruct((SELT_ROWS, n_tok), F32),
        jax.ShapeDtypeStruct((SUBLANES, LANES), F32),
        jax.ShapeDtypeStruct((n_tok // ts, SUBLANES, LANES), F32),
    ]
    return pl.pallas_call(
        _mixer_kernel,
        grid=(bsz, seq // step),
        in_specs=in_specs,
        out_specs=out_specs,
        out_shape=out_shape,
        scratch_shapes=[
            pltpu.VMEM((RET_QK_WIDTH, RET_V_WIDTH), F32),
            pltpu.VMEM((SUBLANES, CONV_WIDTH), F32),
            pltpu.VMEM((SUBLANES, LANES), F32),
        ],
        compiler_params=pltpu.CompilerParams(
            dimension_semantics=("arbitrary", "arbitrary"), vmem_limit_bytes=VMEM_LIMIT),
        name="mixer",
    )(x, ada_l, cos_t, sin_t, *[a for a, _ in lw], *tables)


def _run_pieces(max_rows):
    units = max_rows // RUN_ALIGN
    pieces = []
    bit = 1
    while bit <= units:
        pieces.append(bit)
        bit *= 2
    return tuple(reversed(pieces))


def _piece_table_width(n_classes):
    return -(-(n_classes * 2 * N_EXPERTS + n_classes) // LANES) * LANES


def _for_each_run_piece(tab_ref, pieces, make_copy, fn):
    for c, bit in enumerate(pieces):
        count = tab_ref[0, 0, len(pieces) * 2 * N_EXPERTS + c]

        def body(j, carry, c=c, bit=bit):
            hbm_row = tab_ref[0, 0, c * 2 * N_EXPERTS + j]
            buf_row = tab_ref[0, 0, (c * 2 + 1) * N_EXPERTS + j]
            fn(make_copy(pl.multiple_of(hbm_row, RUN_ALIGN), pl.multiple_of(buf_row, RUN_ALIGN),
                         bit * RUN_ALIGN))
            return carry

        lax.fori_loop(0, count, body, 0)


SEL_IDX, SEL_WEIGHT, SEL_RANK = 0, TOP_K, 2 * TOP_K
ROWS_PAD = SUBLANES


def _dispatch_kernel(pe_ref, ps_ref, tab_cur_ref, tab_prev_ref, rows_ref, h2_ref, xs_hbm,
                     zbuf, sbuf, sem, zsem):
    i = pl.program_id(0)
    n_steps = pl.num_programs(0)
    ts = h2_ref.shape[0]
    blk = zbuf.shape[0]
    n_exp = pe_ref.shape[0]
    run_rows = sbuf.shape[1]
    slot = i % 2
    pieces = _run_pieces(ts)

    @pl.when(i == 0)
    def _():
        zbuf[...] = jnp.zeros_like(zbuf)

        def tail_copy(e):
            start = pl.multiple_of(jnp.maximum(pe_ref[e] - blk, 0), blk)
            return pltpu.make_async_copy(zbuf, xs_hbm.at[pl.ds(start, blk), :], zsem)

        def zstart(e, carry):
            @pl.when(pe_ref[e] > ps_ref[e])
            def _():
                tail_copy(e).start()
            return carry

        def zwait(e, carry):
            @pl.when(pe_ref[e] > ps_ref[e])
            def _():
                tail_copy(e).wait()
            return carry

        lax.fori_loop(0, n_exp, zstart, 0)
        lax.fori_loop(0, n_exp, zwait, 0)

    rows_t = rows_ref[...]
    h2 = h2_ref[...].astype(MXU_DTYPE)
    n_parts = 2
    part = run_rows // n_parts

    def pick_rows(lo):
        buf_row = lax.broadcasted_iota(jnp.int32, (part, ts), 0).astype(F32) + float(lo)
        pick = jnp.zeros((part, ts), F32)
        for kk in range(TOP_K):
            pick = jnp.where(buf_row == rows_t[kk:kk + 1], 1.0, pick)
        return pick.astype(MXU_DTYPE)

    pick = pick_rows(0)
    for p in range(n_parts):
        nxt = pick_rows((p + 1) * part) if p + 1 < n_parts else None
        sbuf[slot, p * part:(p + 1) * part, :] = jnp.dot(
            pick, h2, preferred_element_type=F32).astype(sbuf.dtype)
        pick = nxt

    def push(dst_slot):
        def make(hbm_row, buf_row_, rows):
            return pltpu.make_async_copy(sbuf.at[dst_slot, pl.ds(buf_row_, rows), :],
                                         xs_hbm.at[pl.ds(hbm_row, rows), :], sem.at[dst_slot])
        return make

    _for_each_run_piece(tab_cur_ref, pieces, push(slot), lambda c: c.start())

    @pl.when(i > 0)
    def _():
        _for_each_run_piece(tab_prev_ref, pieces, push(1 - slot), lambda c: c.wait())

    @pl.when(i == n_steps - 1)
    def _():
        _for_each_run_piece(tab_cur_ref, pieces, push(slot), lambda c: c.wait())


def _dispatch_call(pad_end, pad_start, tab3, rows8, h2, n_rows, blk):
    n_tok, d = h2.shape
    n_steps = tab3.shape[0]
    ts = n_tok // n_steps
    run_rows = ts * TOP_K + N_EXPERTS * RUN_ALIGN
    grid_spec = pltpu.PrefetchScalarGridSpec(
        num_scalar_prefetch=2,
        grid=(n_steps,),
        in_specs=[
            pl.BlockSpec((1, 1, tab3.shape[2]), lambda i, pe, ps: (i, 0, 0), memory_space=pltpu.SMEM),
            pl.BlockSpec((1, 1, tab3.shape[2]), lambda i, pe, ps: (jnp.maximum(i - 1, 0), 0, 0),
                         memory_space=pltpu.SMEM),
            pl.BlockSpec((ROWS_PAD, ts), lambda i, pe, ps: (0, i)),
            pl.BlockSpec((ts, d), lambda i, pe, ps: (i, 0)),
        ],
        out_specs=pl.BlockSpec(memory_space=pl.ANY),
        scratch_shapes=[pltpu.VMEM((blk, d), MXU_DTYPE), pltpu.VMEM((2, run_rows, d), MXU_DTYPE),
                        pltpu.SemaphoreType.DMA((2,)), pltpu.SemaphoreType.DMA(())],
    )
    return pl.pallas_call(
        _dispatch_kernel,
        grid_spec=grid_spec,
        out_shape=jax.ShapeDtypeStruct((n_rows, d), MXU_DTYPE),
        compiler_params=pltpu.CompilerParams(
            dimension_semantics=("arbitrary",), vmem_limit_bytes=VMEM_LIMIT),
        name="dispatch",
    )(pad_end, pad_start, tab3, tab3, rows8, h2)


def _expert_up(xs, wgu, bgu_ref):
    return jnp.dot(xs, wgu, preferred_element_type=F32) + bgu_ref[...]


def _expert_down(gu, wdn, bdn_ref):
    gate = jnp.minimum(gu[:, :D_FF], SWIGLU_LIMIT)
    up = jnp.clip(gu[:, D_FF:], -SWIGLU_LIMIT, SWIGLU_LIMIT)
    act = (up + 1.0) * (gate * jax.nn.sigmoid(SWIGLU_ALPHA * gate))
    return jnp.dot(act.astype(MXU_DTYPE), wdn, preferred_element_type=F32) + bdn_ref[...]


def _expert_kernel(be_ref, nu_ref, xs_ref, wgu_ref, bgu_ref, wdn_ref, bdn_ref, ys_ref, wgu_m, wdn_m):
    i = pl.program_id(0)
    n_used = nu_ref[0]

    @pl.when(jnp.logical_or(i == 0, be_ref[i] != be_ref[jnp.maximum(i - 1, 0)]))
    def _():
        wgu_m[...] = wgu_ref[...].astype(wgu_m.dtype)
        wdn_m[...] = wdn_ref[...].astype(wdn_m.dtype)

    @pl.when(i < n_used)
    def _():
        part = min(EXPERT_PART, xs_ref.shape[0])
        n_parts = xs_ref.shape[0] // part
        rows = [slice(p * part, (p + 1) * part) for p in range(n_parts)]
        wgu = wgu_m[...]
        wdn = wdn_m[...]
        gu = _expert_up(xs_ref[rows[0], :], wgu, bgu_ref)
        for p in range(n_parts):
            gu_next = _expert_up(xs_ref[rows[p + 1], :], wgu, bgu_ref) if p + 1 < n_parts else None
            ys_ref[rows[p], :] = _expert_down(gu, wdn, bdn_ref).astype(ys_ref.dtype)
            gu = gu_next

    @pl.when(i >= n_used)
    def _():
        ys_ref[...] = jnp.zeros_like(ys_ref)


def _expert_call(block_e, n_used, xs, w_gu, b_gu, w_down, b_down, blk, layer):
    n_rows, d = xs.shape
    n_blocks = n_rows // blk
    two_f = w_gu.shape[3]
    last = lambda nu: jnp.maximum(nu[0] - 1, 0)
    of_block = lambda i, be, nu: (layer, be[i], 0, 0)
    grid_spec = pltpu.PrefetchScalarGridSpec(
        num_scalar_prefetch=2,
        grid=(n_blocks,),
        in_specs=[
            pl.BlockSpec((blk, d), lambda i, be, nu: (jnp.minimum(i, last(nu)), 0)),
            pl.BlockSpec((None, None, d, two_f), of_block),
            pl.BlockSpec((None, None, 1, two_f), of_block),
            pl.BlockSpec((None, None, two_f // 2, d), of_block),
            pl.BlockSpec((None, None, 1, d), of_block),
        ],
        out_specs=pl.BlockSpec((blk, d), lambda i, be, nu: (i, 0)),
        scratch_shapes=[pltpu.VMEM((d, two_f), MXU_DTYPE), pltpu.VMEM((two_f // 2, d), MXU_DTYPE)],
    )
    return pl.pallas_call(
        _expert_kernel,
        grid_spec=grid_spec,
        out_shape=jax.ShapeDtypeStruct((n_rows, d), MXU_DTYPE),
        compiler_params=pltpu.CompilerParams(
            dimension_semantics=("arbitrary",), vmem_limit_bytes=VMEM_LIMIT),
        name="experts",
    )(block_e, n_used, xs, w_gu, b_gu, w_down, b_down)


def _combine_kernel(tab_cur_ref, tab_nxt_ref, ys_hbm, x1_ref, sel_ref, rows_ref, ada_ref, lng_ref, lnb_ref,
                    o_ref, rbuf, sem):
    i = pl.program_id(0)
    n_steps = pl.num_programs(0)
    ts = x1_ref.shape[0]
    slot = i % 2
    pieces = _run_pieces(ts)

    def fetch(dst_slot):
        def make(hbm_row, buf_row, rows):
            return pltpu.make_async_copy(ys_hbm.at[pl.ds(hbm_row, rows), :],
                                         rbuf.at[dst_slot, pl.ds(buf_row, rows), :], sem.at[dst_slot])
        return make

    @pl.when(i == 0)
    def _():
        rbuf[...] = jnp.zeros_like(rbuf)
        _for_each_run_piece(tab_cur_ref, pieces, fetch(0), lambda c: c.start())

    @pl.when(i + 1 < n_steps)
    def _():
        _for_each_run_piece(tab_nxt_ref, pieces, fetch(1 - slot), lambda c: c.start())

    _for_each_run_piece(tab_cur_ref, pieces, fetch(slot), lambda c: c.wait())

    sel = sel_ref[...]
    rows_t = rows_ref[...]
    rows_c = jnp.concatenate([rows_t, jnp.zeros((LANES - ROWS_PAD, ts), F32)], axis=0).T
    col = lax.broadcasted_iota(jnp.int32, (1, rbuf.shape[1]), 1).astype(F32)
    gt2 = ada_ref[...][5:6]
    part = min(COMBINE_PART, ts)

    def weight_pieces(tok):
        weights = jnp.zeros((part, rbuf.shape[1]), F32)
        for kk in range(TOP_K):
            weights = jnp.where(col == rows_c[tok, kk:kk + 1],
                                sel[tok, SEL_WEIGHT + kk:SEL_WEIGHT + kk + 1], weights)
        hi = weights.astype(MXU_DTYPE)
        lo = (weights - hi.astype(F32)).astype(MXU_DTYPE)
        return jnp.concatenate([hi, lo], axis=0)

    toks = [slice(p * part, (p + 1) * part) for p in range(ts // part)]
    pieces_w = weight_pieces(toks[0])
    for p, tok in enumerate(toks):
        nxt = weight_pieces(toks[p + 1]) if p + 1 < len(toks) else None
        both = jnp.dot(pieces_w, rbuf[slot], preferred_element_type=F32)
        ffn = both[:part] + both[part:]
        o_ref[tok, :] = _layernorm_rows(DEEPNORM_ALPHA * x1_ref[tok, :] + (1.0 + gt2) * ffn,
                                        lng_ref[...], lnb_ref[...])
        pieces_w = nxt


def _combine_call(tab3, ys, x1, sel, rows8, ada_l, ln_g, ln_b, seq):
    n_tok, d = x1.shape
    n_steps = tab3.shape[0]
    ts = n_tok // n_steps
    tiles_per_seq = seq // ts
    run_rows = ts * TOP_K + N_EXPERTS * RUN_ALIGN
    return pl.pallas_call(
        _combine_kernel,
        grid=(n_steps,),
        in_specs=[
            pl.BlockSpec((1, 1, tab3.shape[2]), lambda i: (i, 0, 0), memory_space=pltpu.SMEM),
            pl.BlockSpec((1, 1, tab3.shape[2]), lambda i: (jnp.minimum(i + 1, n_steps - 1), 0, 0),
                         memory_space=pltpu.SMEM),
            pl.BlockSpec(memory_space=pl.ANY),
            pl.BlockSpec((ts, d), lambda i: (i, 0)),
            pl.BlockSpec((ts, LANES), lambda i: (i, 0)),
            pl.BlockSpec((ROWS_PAD, ts), lambda i: (0, i)),
            pl.BlockSpec((None, 6, d), lambda i: (i // tiles_per_seq, 0, 0)),
            pl.BlockSpec((1, d), lambda i: (0, 0)),
            pl.BlockSpec((1, d), lambda i: (0, 0)),
        ],
        out_specs=pl.BlockSpec((ts, d), lambda i: (i, 0)),
        out_shape=jax.ShapeDtypeStruct((n_tok, d), F32),
        scratch_shapes=[pltpu.VMEM((2, run_rows, d), MXU_DTYPE), pltpu.SemaphoreType.DMA((2,))],
        compiler_params=pltpu.CompilerParams(
            dimension_semantics=("arbitrary",), vmem_limit_bytes=VMEM_LIMIT),
        name="combine",
    )(tab3, tab3, ys, x1, sel, rows8, ada_l, ln_g.reshape(1, d), ln_b.reshape(1, d))


def _retention_tables():
    log_gamma = jnp.log(1.0 - 2.0 ** (-5.0 - jnp.arange(RET_HEADS, dtype=F32)))
    idx = jnp.arange(CHUNK, dtype=F32)
    diff = idx[:, None] - idx[None, :]
    decay = jnp.where(diff[None] >= 0,
                      jnp.exp(log_gamma[:, None, None] * jnp.maximum(diff, 0.0)[None]), 0.0)
    decay = decay.reshape(RET_HEADS * CHUNK, CHUNK)
    zeta = jnp.exp(log_gamma[:, None] * (CHUNK - 1.0 - idx)[None])
    zeta_t = jnp.repeat(zeta.T, RET_QK_DIM, axis=1)
    xi = jnp.exp(log_gamma[:, None] * (idx + 1.0)[None]).T
    xi_t = jnp.repeat(xi, RET_V_DIM, axis=1)
    cdec = jnp.repeat(jnp.exp(log_gamma * CHUNK), RET_V_DIM)[None, :]
    row_head = jnp.arange(RET_QK_WIDTH) // RET_QK_DIM
    col_head = jnp.arange(RET_V_WIDTH) // RET_V_DIM
    same_head = row_head[:, None] == col_head[None, :]
    bmask = same_head.astype(F32)
    gavg = ((col_head[:, None] == col_head[None, :]).astype(F32) / RET_V_DIM).astype(MXU_DTYPE)
    return decay, zeta_t, xi_t, cdec, bmask, gavg


def _rotary_tables(positions):
    inv_freq = ROPE_BASE ** (-jnp.arange(0, RET_QK_DIM, 2, dtype=F32) / RET_QK_DIM)
    ang = positions.astype(F32)[..., None] * inv_freq
    cos = jnp.cos(ang)
    sin = jnp.sin(ang)
    cos_t = jnp.tile(jnp.concatenate([cos, cos], axis=-1), (1, 1, RET_HEADS))
    sin_t = jnp.tile(jnp.concatenate([-sin, sin], axis=-1), (1, 1, RET_HEADS))
    return cos_t, sin_t


def _moe_rows(n_tok, n_tiles, blk):
    worst = n_tok * TOP_K + n_tiles * N_EXPERTS * (RUN_ALIGN - 1)
    return (-(-worst // blk) + N_EXPERTS) * blk


def _routing_tables(sel_t, counts, before, n_rows, blk, pieces):
    n_blocks = n_rows // blk
    sizes = counts.astype(jnp.int32)
    pad_sizes = (sizes + blk - 1) // blk * blk
    pad_end = jnp.cumsum(pad_sizes)
    pad_start = pad_end - pad_sizes
    n_used = (pad_end[-1] // blk).astype(jnp.int32)
    block_start = jnp.arange(n_blocks, dtype=jnp.int32) * blk
    block_start = jnp.minimum(block_start, (n_used - 1) * blk)
    block_e = jnp.sum((pad_end[None, :] <= block_start[:, None]).astype(jnp.int32), axis=-1)
    block_e = jnp.minimum(block_e, N_EXPERTS - 1)

    before = before.astype(jnp.int32)
    after = jnp.concatenate([before[1:], sizes[None, :]], axis=0)
    run_start = pad_start[None, :] + before
    run_units = (after - before) // RUN_ALIGN
    run_off = (jnp.cumsum(run_units, axis=-1) - run_units) * RUN_ALIGN
    n_tiles = before.shape[0]
    slot_id = jnp.arange(N_EXPERTS, dtype=jnp.int32)
    cols, counts_c = [], []
    for bit in pieces:
        has = jnp.bitwise_and(run_units, bit) != 0
        done = jnp.bitwise_and(run_units, -2 * bit) * RUN_ALIGN
        pos = jnp.cumsum(has.astype(jnp.int32), axis=-1) - 1
        place = jnp.logical_and(has[:, None, :], pos[:, None, :] == slot_id[None, :, None])
        cols.append(jnp.sum(jnp.where(place, (run_start + done)[:, None, :], 0), axis=-1))
        cols.append(jnp.sum(jnp.where(place, (run_off + done)[:, None, :], 0), axis=-1))
        counts_c.append(jnp.sum(has.astype(jnp.int32), axis=-1, keepdims=True))
    width = _piece_table_width(len(pieces))
    used = len(pieces) * (2 * N_EXPERTS + 1)
    tab = jnp.concatenate(cols + counts_c + [jnp.zeros((n_tiles, width - used), jnp.int32)], axis=-1)

    n_tok = sel_t.shape[1]
    idx = sel_t[SEL_IDX:SEL_IDX + TOP_K].astype(jnp.int32)
    off_tok = jnp.repeat(run_off.T, n_tok // n_tiles, axis=1)
    base = jnp.sum(jnp.where(idx[:, None, :] == slot_id[None, :, None], off_tok[None], 0), axis=1)
    rows = base.astype(F32) + sel_t[SEL_RANK:SEL_RANK + TOP_K]
    rows8 = jnp.pad(rows, ((0, ROWS_PAD - TOP_K), (0, 0)))
    return (pad_end.astype(jnp.int32), pad_start.astype(jnp.int32), block_e, n_used.reshape(1),
            tab.astype(jnp.int32).reshape(n_tiles, 1, width), rows8)


def kernel(x, c, positions, w_ada, b_ada, w_in, gmlp_ln_g, gmlp_ln_b, w_spatial, b_spatial, conv_w,
           w_branch, w_gate_up, b_gate, w_out, ln1_g, ln1_b, w_router, b_router, w_gu, b_gu, w_down,
           b_down, ln2_g, ln2_b):
    bsz, seq, d = x.shape
    n_tok = bsz * seq
    depth = w_ada.shape[0]
    blk = min(MOE_BLOCK, n_tok * TOP_K // N_EXPERTS)
    ts = min(SEQ_TILE, seq)
    assert depth == DEPTH and w_router.shape[-1] == N_EXPERTS and w_in.shape[-1] == _IN_WIDTH
    assert ts % CHUNK == 0 and seq % min(MIXER_STEP, seq) == 0 and min(MIXER_STEP, seq) % ts == 0
    assert blk % RUN_ALIGN == 0 and d % LANES == 0

    ada = _ada_call(c, w_ada, b_ada)
    ada = jnp.transpose(ada, (0, 2, 1, 3))
    cos_t, sin_t = _rotary_tables(positions)
    tables = _retention_tables()

    mx = MXU_DTYPE
    w_in_m = w_in.astype(mx)
    w_branch_m = w_branch.astype(mx)
    w_gate_m = w_gate_up.astype(mx)
    w_out_m = w_out.astype(mx)
    b_gu4 = b_gu[:, :, None, :]
    b_down4 = b_down[:, :, None, :]
    n_exp = w_router.shape[-1]
    w_router_m = jnp.pad(w_router, ((0, 0), (0, 0), (0, LANES - n_exp))).astype(mx)
    b_router_p = jnp.pad(b_router, ((0, 0), (0, LANES - n_exp)), constant_values=NEG_BIG)

    for l in range(depth):
        bsp_t = jnp.repeat(b_spatial[l].T, GMLP_WIDTH // GMLP_GROUPS, axis=1)
        lw = (
            (w_in_m, l),
            (gmlp_ln_g[l].reshape(1, -1), None), (gmlp_ln_b[l].reshape(1, -1), None),
            (w_spatial[l].reshape(GMLP_GROUPS * CHUNK, CHUNK), None), (bsp_t, None),
            (jnp.pad(conv_w[l], ((0, SUBLANES - conv_w.shape[1]), (0, 0))), None),
            (w_branch_m, l), (w_gate_m, l), (b_gate, l), (w_out_m, l),
            (ln1_g[l].reshape(1, -1), None), (ln1_b[l].reshape(1, -1), None),
            (w_router_m, l), (b_router_p[l].reshape(1, -1), None),
        )
        x1, h2, sel, sel_t, cnt, before = _mixer_call(x, ada[l], cos_t, sin_t, lw, tables)
        n_rows = _moe_rows(n_tok, n_tok // ts, blk)
        pad_end, pad_start, block_e, n_used, tab3, rows8 = _routing_tables(
            sel_t, cnt[0, :n_exp], before[:, 0, :n_exp], n_rows, blk, _run_pieces(ts))
        xs = _dispatch_call(pad_end, pad_start, tab3, rows8, h2, n_rows, blk)
        ys = _expert_call(block_e, n_used, xs, w_gu, b_gu4, w_down, b_down4, blk, l)
        x = _combine_call(tab3, ys, x1, sel, rows8, ada[l], ln2_g[l], ln2_b[l], seq).reshape(bsz, seq, d)
    return x
```

```python
import jax
import jax.numpy as jnp
from jax import lax
from jax.experimental import pallas as pl
from jax.experimental.pallas import tpu as pltpu

F32 = jnp.float32
MXU_DTYPE = jnp.bfloat16

DEPTH = 4
RET_HEADS = 4
RET_QK_DIM = 32
RET_V_DIM = 64
RET_QK_WIDTH = RET_HEADS * RET_QK_DIM
RET_V_WIDTH = RET_HEADS * RET_V_DIM
CHUNK = 128
ROPE_BASE = 10000.0
GMLP_GROUPS = 4
GMLP_WIDTH = 256
CONV_WIDTH = 256
GATE_RANK = 128
N_BRANCH = 3
N_EXPERTS = 32
TOP_K = 4
D_FF = 256
SWIGLU_LIMIT = 7.0
SWIGLU_ALPHA = 1.702
DEEPNORM_ALPHA = (2.0 * DEPTH) ** 0.25
LN_EPS = 1e-5

_O_QK = 0
_O_VG = 2 * RET_QK_WIDTH
_O_GMLP = _O_VG + 2 * RET_V_WIDTH
_O_CONV = _O_GMLP + 2 * GMLP_WIDTH
_O_CODE = _O_CONV + 3 * CONV_WIDTH
_IN_WIDTH = _O_CODE + N_BRANCH * GATE_RANK

LANES = 128
SUBLANES = 8
NEG_BIG = -1e30

SEQ_TILE = 256
MIXER_STEP = 512
MOE_BLOCK = 2048
EXPERT_PART = 1024
COMBINE_PART = 128
SELT_ROWS = 16
RUN_ALIGN = 16
VMEM_LIMIT = 56 * 1024 * 1024


def _dot(a, b):
    return jnp.dot(a.astype(MXU_DTYPE), b.astype(MXU_DTYPE), preferred_element_type=F32)


def _dot_nt(a, b):
    return lax.dot_general(a.astype(MXU_DTYPE), b.astype(MXU_DTYPE),
                           (((1,), (1,)), ((), ())), preferred_element_type=F32)


def _split_dot(x, w):
    hi = x.astype(MXU_DTYPE)
    lo = (x - hi.astype(F32)).astype(MXU_DTYPE)
    return (jnp.dot(hi, w, preferred_element_type=F32)
            + jnp.dot(lo, w, preferred_element_type=F32))


def _layernorm_rows(x, g, b):
    mu = jnp.mean(x, axis=-1, keepdims=True)
    d = x - mu
    var = jnp.mean(d * d, axis=-1, keepdims=True)
    return d * lax.rsqrt(var + LN_EPS) * g + b


def _ada_kernel(c_ref, w_ref, b_ref, o_ref):
    c_act = jax.nn.silu(c_ref[...])
    o_ref[...] = jnp.dot(c_act, w_ref[...], preferred_element_type=F32,
                         precision=lax.Precision.HIGHEST) + b_ref[...]


def _ada_call(c, w_ada, b_ada):
    depth, d, six_d = w_ada.shape
    bsz = c.shape[0]
    n_col = six_d // d
    return pl.pallas_call(
        _ada_kernel,
        grid=(depth, n_col),
        in_specs=[
            pl.BlockSpec((bsz, d), lambda l, j: (0, 0)),
            pl.BlockSpec((None, d, d), lambda l, j: (l, 0, j)),
            pl.BlockSpec((None, 1, d), lambda l, j: (l, 0, j)),
        ],
        out_specs=pl.BlockSpec((None, None, bsz, d), lambda l, j: (l, j, 0, 0)),
        out_shape=jax.ShapeDtypeStruct((depth, n_col, bsz, d), F32),
        compiler_params=pltpu.CompilerParams(
            dimension_semantics=("arbitrary", "arbitrary"), vmem_limit_bytes=VMEM_LIMIT),
        name="ada_ln",
    )(c, w_ada, b_ada.reshape(depth, 1, six_d))


_MIXER_PHASES = 3


def _mixer_kernel(x_ref, ada_ref, cos_ref, sin_ref, *rest):
    params = rest[:-9]
    x1_ref, h2_ref, sel_ref, selt_ref, cnt_ref, before_ref, state_ref, zc_ref, cntacc_ref = rest[-9:]
    b = pl.program_id(0)
    s = pl.program_id(1)

    @pl.when(s == 0)
    def _():
        state_ref[...] = jnp.zeros_like(state_ref)
        zc_ref[...] = jnp.zeros_like(zc_ref)

    @pl.when(jnp.logical_and(b == 0, s == 0))
    def _():
        cntacc_ref[...] = jnp.zeros_like(cntacc_ref)

    n_sub = before_ref.shape[0]
    ts = x_ref.shape[0] // n_sub
    tiles = []
    for u in range(n_sub):
        rows = pl.ds(u * ts, ts)
        tiles.append(_mixer_tile(x_ref.at[rows], ada_ref, cos_ref.at[rows], sin_ref.at[rows], *params,
                                 x1_ref.at[rows], h2_ref.at[rows], sel_ref.at[rows], selt_ref.at[:, rows],
                                 cnt_ref, before_ref.at[u], state_ref, zc_ref, cntacc_ref))
    for _ in range(_MIXER_PHASES):
        for tile in tiles:
            next(tile, None)


def _mixer_tile(x_ref, ada_ref, cos_ref, sin_ref, w_in_ref, lng_ref, lnb_ref, wsp_ref, bsp_ref,
                convw_ref, wbr_ref, wgt_ref, bgt_ref, wout_ref, ln1g_ref, ln1b_ref, wr_ref, br_ref,
                decay_ref, zeta_ref, xi_ref, cdec_ref, bmask_ref, gavg_ref,
                x1_ref, h2_ref, sel_ref, selt_ref, cnt_ref, before_ref,
                state_ref, zc_ref, cntacc_ref):
    ts = x_ref.shape[0]
    x = x_ref[...]
    ada = ada_ref[...]
    sh1, sc1, gt1, sh2, sc2 = ada[0:1], ada[1:2], ada[2:3], ada[3:4], ada[4:5]
    h = (x * (1.0 + sc1) + sh1).astype(MXU_DTYPE)

    def proj(lo, hi):
        return jnp.dot(h, w_in_ref[:, lo:hi], preferred_element_type=F32)

    lane_qk = lax.broadcasted_iota(jnp.int32, (1, RET_QK_WIDTH), 1)
    lane_v = lax.broadcasted_iota(jnp.int32, (1, RET_V_WIDTH), 1)
    qk_masks = [lane_qk // RET_QK_DIM == hd for hd in range(RET_HEADS)]
    v_masks = [lane_v // RET_V_DIM == hd for hd in range(RET_HEADS)]

    qk = proj(_O_QK, _O_VG)
    cos = cos_ref[...]
    sin = sin_ref[...]
    first_half = (lane_qk % RET_QK_DIM) < (RET_QK_DIM // 2)

    def rotary(t):
        swapped = jnp.where(first_half, pltpu.roll(t, RET_QK_WIDTH - RET_QK_DIM // 2, 1),
                            pltpu.roll(t, RET_QK_DIM // 2, 1))
        return t * cos + swapped * sin

    q = rotary(qk[:, :RET_QK_WIDTH])
    k = rotary(qk[:, RET_QK_WIDTH:]) * (RET_QK_DIM ** -0.5)
    vg = proj(_O_VG, _O_GMLP)
    v = vg[:, :RET_V_WIDTH]
    g = vg[:, RET_V_WIDTH:]

    decay = decay_ref[...]
    zeta = zeta_ref[...]
    xi = xi_ref[...]
    cdec = cdec_ref[...]
    bmask = bmask_ref[...]
    gavg = gavg_ref[...]

    o_chunks = []
    state = state_ref[...]
    for c in range(ts // CHUNK):
        rows = slice(c * CHUNK, (c + 1) * CHUNK)
        q_c, k_c, v_c = q[rows], k[rows], v[rows]
        v_m = v_c.astype(MXU_DTYPE)
        q_all = jnp.concatenate([jnp.where(m, q_c, 0.0) for m in qk_masks], axis=0)
        scores = _dot_nt(q_all, k_c) * decay
        o_all = _dot(scores, v_m)
        o = _dot(q_c, state) * xi
        for hd in range(RET_HEADS):
            o = o + jnp.where(v_masks[hd], o_all[hd * CHUNK:(hd + 1) * CHUNK], 0.0)
        kv = _dot((k_c * zeta).T, v_m) * bmask
        state = state * cdec + kv
        o_chunks.append(o)
    state_ref[...] = state
    o = jnp.concatenate(o_chunks, axis=0) if len(o_chunks) > 1 else o_chunks[0]
    mu = _split_dot(o, gavg)
    d = o - mu
    var = _split_dot(d * d, gavg)
    r_br = jax.nn.silu(g) * (d * lax.rsqrt(var + LN_EPS))

    guv = proj(_O_GMLP, _O_CONV)
    u = jax.nn.gelu(guv[:, :GMLP_WIDTH])
    vv = _layernorm_rows(jax.nn.gelu(guv[:, GMLP_WIDTH:]), lng_ref[...], lnb_ref[...]).astype(MXU_DTYPE)
    rr = lax.broadcasted_iota(jnp.int32, (GMLP_GROUPS * CHUNK, CHUNK), 0) % CHUNK
    cc_ = lax.broadcasted_iota(jnp.int32, (GMLP_GROUPS * CHUNK, CHUNK), 1)
    w_sp = jnp.where(cc_ <= rr, wsp_ref[...], 0.0).astype(MXU_DTYPE)
    bsp = bsp_ref[...]
    z_chunks = []
    for c in range(ts // CHUNK):
        z_all = jnp.dot(w_sp, vv[c * CHUNK:(c + 1) * CHUNK], preferred_element_type=F32)
        z = bsp
        for gi in range(GMLP_GROUPS):
            z = z + jnp.where(v_masks[gi], z_all[gi * CHUNK:(gi + 1) * CHUNK], 0.0)
        z_chunks.append(z)
    z = jnp.concatenate(z_chunks, axis=0) if len(z_chunks) > 1 else z_chunks[0]
    s_br = u * z

    cbcx = proj(_O_CONV, _O_CODE)
    gate_b = cbcx[:, :CONV_WIDTH]
    zc = cbcx[:, CONV_WIDTH:2 * CONV_WIDTH] * cbcx[:, 2 * CONV_WIDTH:]
    carry = zc_ref[...]
    prev1 = carry[SUBLANES - 1:SUBLANES]
    prev2 = carry[SUBLANES - 2:SUBLANES - 1]
    row = lax.broadcasted_iota(jnp.int32, (ts, 1), 0)
    z1 = jnp.where(row == 0, prev1, pltpu.roll(zc, 1, 0))
    z2 = jnp.where(row == 0, prev2, jnp.where(row == 1, prev1, pltpu.roll(zc, 2, 0)))
    zc_ref[...] = zc[ts - SUBLANES:ts]
    cw = convw_ref[...]
    k_br = gate_b * (cw[0:1] * z2 + cw[1:2] * z1 + cw[2:3] * zc)

    code = proj(_O_CODE, _IN_WIDTH)
    yield
    merged = None
    for i, br in enumerate((r_br, s_br, k_br)):
        y = jnp.dot(br.astype(MXU_DTYPE), wbr_ref[i], preferred_element_type=F32)
        gl = jnp.dot(code[:, i * GATE_RANK:(i + 1) * GATE_RANK].astype(MXU_DTYPE), wgt_ref[i],
                     preferred_element_type=F32) + bgt_ref[i:i + 1]
        t = jax.nn.sigmoid(gl) * y
        merged = t if merged is None else merged + t
    mix = jnp.dot(merged.astype(MXU_DTYPE), wout_ref[...], preferred_element_type=F32)
    x1 = _layernorm_rows(DEEPNORM_ALPHA * x + (1.0 + gt1) * mix, ln1g_ref[...], ln1b_ref[...])
    x1_ref[...] = x1
    h2 = (x1 * (1.0 + sc2) + sh2).astype(MXU_DTYPE)
    h2_ref[...] = h2

    logits = jnp.dot(h2, wr_ref[...], preferred_element_type=F32) + br_ref[...]
    yield
    lane = lax.broadcasted_iota(jnp.int32, (1, LANES), 1).astype(F32)
    top_v, top_i = [], []
    work = logits
    for _ in range(TOP_K):
        m = jnp.max(work, axis=-1, keepdims=True)
        i_sel = jnp.min(jnp.where(work == m, lane, float(LANES)), axis=-1, keepdims=True)
        top_v.append(m)
        top_i.append(i_sel)
        work = jnp.where(lane == i_sel, -jnp.inf, work)
    exps = [jnp.exp(tv - top_v[0]) for tv in top_v]
    denom = exps[0] + exps[1] + exps[2] + exps[3]
    onehots = [lane == ti for ti in top_i]
    member = jnp.zeros((ts, LANES), F32)
    for oh in onehots:
        member = member + jnp.where(oh, 1.0, 0.0)
    tr = lax.broadcasted_iota(jnp.int32, (ts, ts), 0)
    tc = lax.broadcasted_iota(jnp.int32, (ts, ts), 1)
    before = jnp.where(tc < tr, 1.0, 0.0).astype(MXU_DTYPE)
    cnt = cntacc_ref[...]
    local_rank = jnp.dot(before, member.astype(MXU_DTYPE), preferred_element_type=F32)
    sel = jnp.zeros((ts, LANES), F32)
    for kk in range(TOP_K):
        lrank_k = jnp.sum(jnp.where(onehots[kk], local_rank, 0.0), axis=-1, keepdims=True)
        sel = sel + jnp.where(lane == float(kk), top_i[kk], 0.0)
        sel = sel + jnp.where(lane == float(TOP_K + kk), exps[kk] / denom, 0.0)
        sel = sel + jnp.where(lane == float(2 * TOP_K + kk), lrank_k, 0.0)
    sel_ref[...] = sel
    selt_ref[...] = sel.T[:selt_ref.shape[0]]
    before_ref[...] = cnt
    tile_cnt = jnp.sum(member, axis=0, keepdims=True)
    cnt = cnt + jnp.floor((tile_cnt + (RUN_ALIGN - 1.0)) * (1.0 / RUN_ALIGN)) * RUN_ALIGN
    cntacc_ref[...] = cnt
    cnt_ref[...] = cnt


def _const_spec(shape):
    return pl.BlockSpec(shape, lambda b, s: (0,) * len(shape))


def _param_spec(a, layer):
    if layer is None:
        return _const_spec(a.shape)
    return pl.BlockSpec((None,) + a.shape[1:], lambda b, s: (layer,) + (0,) * (a.ndim - 1))


def _mixer_call(x, ada_l, cos_t, sin_t, lw, tables):
    bsz, seq, d = x.shape
    ts = min(SEQ_TILE, seq)
    step = min(MIXER_STEP, seq)
    n_tok = bsz * seq
    row3 = lambda b, s: (b, s, 0)
    in_specs = [
        pl.BlockSpec((None, step, d), row3),
        pl.BlockSpec((None, 6, d), lambda b, s: (b, 0, 0)),
        pl.BlockSpec((None, step, RET_QK_WIDTH), row3),
        pl.BlockSpec((None, step, RET_QK_WIDTH), row3),
    ] + [_param_spec(a, lay) for a, lay in lw] + [_const_spec(a.shape) for a in tables]
    tok_row = lambda b, s: (b * (seq // step) + s, 0)
    out_specs = [
        pl.BlockSpec((step, d), tok_row),
        pl.BlockSpec((step, d), tok_row),
        pl.BlockSpec((step, LANES), tok_row),
        pl.BlockSpec((SELT_ROWS, step), lambda b, s: (0, b * (seq // step) + s)),
        pl.BlockSpec((SUBLANES, LANES), lambda b, s: (0, 0)),
        pl.BlockSpec((step // ts, SUBLANES, LANES), lambda b, s: (b * (seq // step) + s, 0, 0)),
    ]
    out_shape = [
        jax.ShapeDtypeStruct((n_tok, d), F32),
        jax.ShapeDtypeStruct((n_tok, d), MXU_DTYPE),
        jax.ShapeDtypeStruct((n_tok, LANES), F32),
        jax.ShapeDtypeStruct((SELT_ROWS, n_tok), F32),
        jax.ShapeDtypeStruct((SUBLANES, LANES), F32),
        jax.ShapeDtypeStruct((n_tok // ts, SUBLANES, LANES), F32),
    ]
    return pl.pallas_call(
        _mixer_kernel,
        grid=(bsz, seq // step),
        in_specs=in_specs,
        out_specs=out_specs,
        out_shape=out_shape,
        scratch_shapes=[
            pltpu.VMEM((RET_QK_WIDTH, RET_V_WIDTH), F32),
            pltpu.VMEM((SUBLANES, CONV_WIDTH), F32),
            pltpu.VMEM((SUBLANES, LANES), F32),
        ],
        compiler_params=pltpu.CompilerParams(
            dimension_semantics=("arbitrary", "arbitrary"), vmem_limit_bytes=VMEM_LIMIT),
        name="mixer",
    )(x, ada_l, cos_t, sin_t, *[a for a, _ in lw], *tables)


def _run_pieces(max_rows):
    units = max_rows // RUN_ALIGN
    pieces = []
    bit = 1
    while bit <= units:
        pieces.append(bit)
        bit *= 2
    return tuple(reversed(pieces))


def _piece_table_width(n_classes):
    return -(-(n_classes * 2 * N_EXPERTS + n_classes) // LANES) * LANES


def _for_each_run_piece(tab_ref, pieces, make_copy, fn):
    for c, bit in enumerate(pieces):
        count = tab_ref[0, 0, len(pieces) * 2 * N_EXPERTS + c]

        def body(j, carry, c=c, bit=bit):
            hbm_row = tab_ref[0, 0, c * 2 * N_EXPERTS + j]
            buf_row = tab_ref[0, 0, (c * 2 + 1) * N_EXPERTS + j]
            fn(make_copy(pl.multiple_of(hbm_row, RUN_ALIGN), pl.multiple_of(buf_row, RUN_ALIGN),
                         bit * RUN_ALIGN))
            return carry

        lax.fori_loop(0, count, body, 0)


SEL_IDX, SEL_WEIGHT, SEL_RANK = 0, TOP_K, 2 * TOP_K
ROWS_PAD = SUBLANES


def _dispatch_kernel(pe_ref, ps_ref, tab_cur_ref, tab_prev_ref, rows_ref, h2_ref, xs_hbm,
                     zbuf, sbuf, sem, zsem):
    i = pl.program_id(0)
    n_steps = pl.num_programs(0)
    ts = h2_ref.shape[0]
    blk = zbuf.shape[0]
    n_exp = pe_ref.shape[0]
    run_rows = sbuf.shape[1]
    slot = i % 2
    pieces = _run_pieces(ts)

    @pl.when(i == 0)
    def _():
        zbuf[...] = jnp.zeros_like(zbuf)

        def tail_copy(e):
            start = pl.multiple_of(jnp.maximum(pe_ref[e] - blk, 0), blk)
            return pltpu.make_async_copy(zbuf, xs_hbm.at[pl.ds(start, blk), :], zsem)

        def zstart(e, carry):
            @pl.when(pe_ref[e] > ps_ref[e])
            def _():
                tail_copy(e).start()
            return carry

        def zwait(e, carry):
            @pl.when(pe_ref[e] > ps_ref[e])
            def _():
                tail_copy(e).wait()
            return carry

        lax.fori_loop(0, n_exp, zstart, 0)
        lax.fori_loop(0, n_exp, zwait, 0)

    rows_t = rows_ref[...]
    h2 = h2_ref[...].astype(MXU_DTYPE)
    n_parts = 2
    part = run_rows // n_parts

    def pick_rows(lo):
        buf_row = lax.broadcasted_iota(jnp.int32, (part, ts), 0).astype(F32) + float(lo)
        pick = jnp.zeros((part, ts), F32)
        for kk in range(TOP_K):
            pick = jnp.where(buf_row == rows_t[kk:kk + 1], 1.0, pick)
        return pick.astype(MXU_DTYPE)

    pick = pick_rows(0)
    for p in range(n_parts):
        nxt = pick_rows((p + 1) * part) if p + 1 < n_parts else None
        sbuf[slot, p * part:(p + 1) * part, :] = jnp.dot(
            pick, h2, preferred_element_type=F32).astype(sbuf.dtype)
        pick = nxt

    def push(dst_slot):
        def make(hbm_row, buf_row_, rows):
            return pltpu.make_async_copy(sbuf.at[dst_slot, pl.ds(buf_row_, rows), :],
                                         xs_hbm.at[pl.ds(hbm_row, rows), :], sem.at[dst_slot])
        return make

    _for_each_run_piece(tab_cur_ref, pieces, push(slot), lambda c: c.start())

    @pl.when(i > 0)
    def _():
        _for_each_run_piece(tab_prev_ref, pieces, push(1 - slot), lambda c: c.wait())

    @pl.when(i == n_steps - 1)
    def _():
        _for_each_run_piece(tab_cur_ref, pieces, push(slot), lambda c: c.wait())


def _dispatch_call(pad_end, pad_start, tab3, rows8, h2, n_rows, blk):
    n_tok, d = h2.shape
    n_steps = tab3.shape[0]
    ts = n_tok // n_steps
    run_rows = ts * TOP_K + N_EXPERTS * RUN_ALIGN
    grid_spec = pltpu.PrefetchScalarGridSpec(
        num_scalar_prefetch=2,
        grid=(n_steps,),
        in_specs=[
            pl.BlockSpec((1, 1, tab3.shape[2]), lambda i, pe, ps: (i, 0, 0), memory_space=pltpu.SMEM),
            pl.BlockSpec((1, 1, tab3.shape[2]), lambda i, pe, ps: (jnp.maximum(i - 1, 0), 0, 0),
                         memory_space=pltpu.SMEM),
            pl.BlockSpec((ROWS_PAD, ts), lambda i, pe, ps: (0, i)),
            pl.BlockSpec((ts, d), lambda i, pe, ps: (i, 0)),
        ],
        out_specs=pl.BlockSpec(memory_space=pl.ANY),
        scratch_shapes=[pltpu.VMEM((blk, d), MXU_DTYPE), pltpu.VMEM((2, run_rows, d), MXU_DTYPE),
                        pltpu.SemaphoreType.DMA((2,)), pltpu.SemaphoreType.DMA(())],
    )
    return pl.pallas_call(
        _dispatch_kernel,
        grid_spec=grid_spec,
        out_shape=jax.ShapeDtypeStruct((n_rows, d), MXU_DTYPE),
        compiler_params=pltpu.CompilerParams(
            dimension_semantics=("arbitrary",), vmem_limit_bytes=VMEM_LIMIT),
        name="dispatch",
    )(pad_end, pad_start, tab3, tab3, rows8, h2)


def _expert_up(xs, wgu, bgu_ref):
    return jnp.dot(xs, wgu, preferred_element_type=F32) + bgu_ref[...]


def _expert_down(gu, wdn, bdn_ref):
    gate = jnp.minimum(gu[:, :D_FF], SWIGLU_LIMIT)
    up = jnp.clip(gu[:, D_FF:], -SWIGLU_LIMIT, SWIGLU_LIMIT)
    act = (up + 1.0) * (gate * jax.nn.sigmoid(SWIGLU_ALPHA * gate))
    return jnp.dot(act.astype(MXU_DTYPE), wdn, preferred_element_type=F32) + bdn_ref[...]


def _expert_kernel(be_ref, nu_ref, xs_ref, wgu_ref, bgu_ref, wdn_ref, bdn_ref, ys_ref):
    i = pl.program_id(0)
    n_used = nu_ref[0]

    @pl.when(i < n_used)
    def _():
        part = min(EXPERT_PART, xs_ref.shape[0])
        n_parts = xs_ref.shape[0] // part
        rows = [slice(p * part, (p + 1) * part) for p in range(n_parts)]
        wgu = wgu_ref[...].astype(MXU_DTYPE)
        wdn = wdn_ref[...].astype(MXU_DTYPE)
        gu = _expert_up(xs_ref[rows[0], :], wgu, bgu_ref)
        for p in range(n_parts):
            gu_next = _expert_up(xs_ref[rows[p + 1], :], wgu, bgu_ref) if p + 1 < n_parts else None
            ys_ref[rows[p], :] = _expert_down(gu, wdn, bdn_ref).astype(ys_ref.dtype)
            gu = gu_next


def _expert_call(block_e, n_used, xs, w_gu, b_gu, w_down, b_down, blk, layer):
    n_rows, d = xs.shape
    n_blocks = n_rows // blk
    two_f = w_gu.shape[3]
    last = lambda nu: jnp.maximum(nu[0] - 1, 0)
    of_block = lambda i, be, nu: (layer, be[i], 0, 0)
    grid_spec = pltpu.PrefetchScalarGridSpec(
        num_scalar_prefetch=2,
        grid=(n_blocks,),
        in_specs=[
            pl.BlockSpec((blk, d), lambda i, be, nu: (jnp.minimum(i, last(nu)), 0)),
            pl.BlockSpec((None, None, d, two_f), of_block),
            pl.BlockSpec((None, None, 1, two_f), of_block),
            pl.BlockSpec((None, None, two_f // 2, d), of_block),
            pl.BlockSpec((None, None, 1, d), of_block),
        ],
        out_specs=pl.BlockSpec((blk, d), lambda i, be, nu: (jnp.minimum(i, last(nu)), 0)),
    )
    return pl.pallas_call(
        _expert_kernel,
        grid_spec=grid_spec,
        out_shape=jax.ShapeDtypeStruct((n_rows, d), MXU_DTYPE),
        compiler_params=pltpu.CompilerParams(
            dimension_semantics=("arbitrary",), vmem_limit_bytes=VMEM_LIMIT),
        name="experts",
    )(block_e, n_used, xs, w_gu, b_gu, w_down, b_down)


def _combine_kernel(tab_cur_ref, tab_nxt_ref, ys_hbm, x1_ref, sel_ref, rows_ref, ada_ref, lng_ref, lnb_ref,
                    o_ref, rbuf, sem):
    i = pl.program_id(0)
    n_steps = pl.num_programs(0)
    ts = x1_ref.shape[0]
    slot = i % 2
    pieces = _run_pieces(ts)

    def fetch(dst_slot):
        def make(hbm_row, buf_row, rows):
            return pltpu.make_async_copy(ys_hbm.at[pl.ds(hbm_row, rows), :],
                                         rbuf.at[dst_slot, pl.ds(buf_row, rows), :], sem.at[dst_slot])
        return make

    @pl.when(i == 0)
    def _():
        rbuf[...] = jnp.zeros_like(rbuf)
        _for_each_run_piece(tab_cur_ref, pieces, fetch(0), lambda c: c.start())

    @pl.when(i + 1 < n_steps)
    def _():
        _for_each_run_piece(tab_nxt_ref, pieces, fetch(1 - slot), lambda c: c.start())

    _for_each_run_piece(tab_cur_ref, pieces, fetch(slot), lambda c: c.wait())

    sel = sel_ref[...]
    rows_t = rows_ref[...]
    rows_c = jnp.concatenate([rows_t, jnp.zeros((LANES - ROWS_PAD, ts), F32)], axis=0).T
    col = lax.broadcasted_iota(jnp.int32, (1, rbuf.shape[1]), 1).astype(F32)
    gt2 = ada_ref[...][5:6]
    part = min(COMBINE_PART, ts)

    def weight_pieces(tok):
        weights = jnp.zeros((part, rbuf.shape[1]), F32)
        for kk in range(TOP_K):
            weights = jnp.where(col == rows_c[tok, kk:kk + 1],
                                sel[tok, SEL_WEIGHT + kk:SEL_WEIGHT + kk + 1], weights)
        hi = weights.astype(MXU_DTYPE)
        lo = (weights - hi.astype(F32)).astype(MXU_DTYPE)
        return jnp.concatenate([hi, lo], axis=0)

    toks = [slice(p * part, (p + 1) * part) for p in range(ts // part)]
    pieces_w = weight_pieces(toks[0])
    for p, tok in enumerate(toks):
        nxt = weight_pieces(toks[p + 1]) if p + 1 < len(toks) else None
        both = jnp.dot(pieces_w, rbuf[slot], preferred_element_type=F32)
        ffn = both[:part] + both[part:]
        o_ref[tok, :] = _layernorm_rows(DEEPNORM_ALPHA * x1_ref[tok, :] + (1.0 + gt2) * ffn,
                                        lng_ref[...], lnb_ref[...])
        pieces_w = nxt


def _combine_call(tab3, ys, x1, sel, rows8, ada_l, ln_g, ln_b, seq):
    n_tok, d = x1.shape
    n_steps = tab3.shape[0]
    ts = n_tok // n_steps
    tiles_per_seq = seq // ts
    run_rows = ts * TOP_K + N_EXPERTS * RUN_ALIGN
    return pl.pallas_call(
        _combine_kernel,
        grid=(n_steps,),
        in_specs=[
            pl.BlockSpec((1, 1, tab3.shape[2]), lambda i: (i, 0, 0), memory_space=pltpu.SMEM),
            pl.BlockSpec((1, 1, tab3.shape[2]), lambda i: (jnp.minimum(i + 1, n_steps - 1), 0, 0),
                         memory_space=pltpu.SMEM),
            pl.BlockSpec(memory_space=pl.ANY),
            pl.BlockSpec((ts, d), lambda i: (i, 0)),
            pl.BlockSpec((ts, LANES), lambda i: (i, 0)),
            pl.BlockSpec((ROWS_PAD, ts), lambda i: (0, i)),
            pl.BlockSpec((None, 6, d), lambda i: (i // tiles_per_seq, 0, 0)),
            pl.BlockSpec((1, d), lambda i: (0, 0)),
            pl.BlockSpec((1, d), lambda i: (0, 0)),
        ],
        out_specs=pl.BlockSpec((ts, d), lambda i: (i, 0)),
        out_shape=jax.ShapeDtypeStruct((n_tok, d), F32),
        scratch_shapes=[pltpu.VMEM((2, run_rows, d), MXU_DTYPE), pltpu.SemaphoreType.DMA((2,))],
        compiler_params=pltpu.CompilerParams(
            dimension_semantics=("arbitrary",), vmem_limit_bytes=VMEM_LIMIT),
        name="combine",
    )(tab3, tab3, ys, x1, sel, rows8, ada_l, ln_g.reshape(1, d), ln_b.reshape(1, d))


def _retention_tables():
    log_gamma = jnp.log(1.0 - 2.0 ** (-5.0 - jnp.arange(RET_HEADS, dtype=F32)))
    idx = jnp.arange(CHUNK, dtype=F32)
    diff = idx[:, None] - idx[None, :]
    decay = jnp.where(diff[None] >= 0,
                      jnp.exp(log_gamma[:, None, None] * jnp.maximum(diff, 0.0)[None]), 0.0)
    decay = decay.reshape(RET_HEADS * CHUNK, CHUNK)
    zeta = jnp.exp(log_gamma[:, None] * (CHUNK - 1.0 - idx)[None])
    zeta_t = jnp.repeat(zeta.T, RET_QK_DIM, axis=1)
    xi = jnp.exp(log_gamma[:, None] * (idx + 1.0)[None]).T
    xi_t = jnp.repeat(xi, RET_V_DIM, axis=1)
    cdec = jnp.repeat(jnp.exp(log_gamma * CHUNK), RET_V_DIM)[None, :]
    row_head = jnp.arange(RET_QK_WIDTH) // RET_QK_DIM
    col_head = jnp.arange(RET_V_WIDTH) // RET_V_DIM
    same_head = row_head[:, None] == col_head[None, :]
    bmask = same_head.astype(F32)
    gavg = ((col_head[:, None] == col_head[None, :]).astype(F32) / RET_V_DIM).astype(MXU_DTYPE)
    return decay, zeta_t, xi_t, cdec, bmask, gavg


def _rotary_tables(positions):
    inv_freq = ROPE_BASE ** (-jnp.arange(0, RET_QK_DIM, 2, dtype=F32) / RET_QK_DIM)
    ang = positions.astype(F32)[..., None] * inv_freq
    cos = jnp.cos(ang)
    sin = jnp.sin(ang)
    cos_t = jnp.tile(jnp.concatenate([cos, cos], axis=-1), (1, 1, RET_HEADS))
    sin_t = jnp.tile(jnp.concatenate([-sin, sin], axis=-1), (1, 1, RET_HEADS))
    return cos_t, sin_t


def _moe_rows(n_tok, n_tiles, blk):
    worst = n_tok * TOP_K + n_tiles * N_EXPERTS * (RUN_ALIGN - 1)
    return (-(-worst // blk) + N_EXPERTS) * blk


def _routing_tables(sel_t, counts, before, n_rows, blk, pieces):
    n_blocks = n_rows // blk
    sizes = counts.astype(jnp.int32)
    pad_sizes = (sizes + blk - 1) // blk * blk
    pad_end = jnp.cumsum(pad_sizes)
    pad_start = pad_end - pad_sizes
    n_used = (pad_end[-1] // blk).astype(jnp.int32)
    block_start = jnp.arange(n_blocks, dtype=jnp.int32) * blk
    block_start = jnp.minimum(block_start, (n_used - 1) * blk)
    block_e = jnp.sum((pad_end[None, :] <= block_start[:, None]).astype(jnp.int32), axis=-1)
    block_e = jnp.minimum(block_e, N_EXPERTS - 1)

    before = before.astype(jnp.int32)
    after = jnp.concatenate([before[1:], sizes[None, :]], axis=0)
    run_start = pad_start[None, :] + before
    run_units = (after - before) // RUN_ALIGN
    run_off = (jnp.cumsum(run_units, axis=-1) - run_units) * RUN_ALIGN
    n_tiles = before.shape[0]
    slot_id = jnp.arange(N_EXPERTS, dtype=jnp.int32)
    cols, counts_c = [], []
    for bit in pieces:
        has = jnp.bitwise_and(run_units, bit) != 0
        done = jnp.bitwise_and(run_units, -2 * bit) * RUN_ALIGN
        pos = jnp.cumsum(has.astype(jnp.int32), axis=-1) - 1
        place = jnp.logical_and(has[:, None, :], pos[:, None, :] == slot_id[None, :, None])
        cols.append(jnp.sum(jnp.where(place, (run_start + done)[:, None, :], 0), axis=-1))
        cols.append(jnp.sum(jnp.where(place, (run_off + done)[:, None, :], 0), axis=-1))
        counts_c.append(jnp.sum(has.astype(jnp.int32), axis=-1, keepdims=True))
    width = _piece_table_width(len(pieces))
    used = len(pieces) * (2 * N_EXPERTS + 1)
    tab = jnp.concatenate(cols + counts_c + [jnp.zeros((n_tiles, width - used), jnp.int32)], axis=-1)

    n_tok = sel_t.shape[1]
    idx = sel_t[SEL_IDX:SEL_IDX + TOP_K].astype(jnp.int32)
    off_tok = jnp.repeat(run_off.T, n_tok // n_tiles, axis=1)
    base = jnp.sum(jnp.where(idx[:, None, :] == slot_id[None, :, None], off_tok[None], 0), axis=1)
    rows = base.astype(F32) + sel_t[SEL_RANK:SEL_RANK + TOP_K]
    rows8 = jnp.pad(rows, ((0, ROWS_PAD - TOP_K), (0, 0)))
    return (pad_end.astype(jnp.int32), pad_start.astype(jnp.int32), block_e, n_used.reshape(1),
            tab.astype(jnp.int32).reshape(n_tiles, 1, width), rows8)


def kernel(x, c, positions, w_ada, b_ada, w_in, gmlp_ln_g, gmlp_ln_b, w_spatial, b_spatial, conv_w,
           w_branch, w_gate_up, b_gate, w_out, ln1_g, ln1_b, w_router, b_router, w_gu, b_gu, w_down,
           b_down, ln2_g, ln2_b):
    bsz, seq, d = x.shape
    n_tok = bsz * seq
    depth = w_ada.shape[0]
    blk = min(MOE_BLOCK, n_tok * TOP_K // N_EXPERTS)
    ts = min(SEQ_TILE, seq)
    assert depth == DEPTH and w_router.shape[-1] == N_EXPERTS and w_in.shape[-1] == _IN_WIDTH
    assert ts % CHUNK == 0 and seq % min(MIXER_STEP, seq) == 0 and min(MIXER_STEP, seq) % ts == 0
    assert blk % RUN_ALIGN == 0 and d % LANES == 0

    ada = _ada_call(c, w_ada, b_ada)
    ada = jnp.transpose(ada, (0, 2, 1, 3))
    cos_t, sin_t = _rotary_tables(positions)
    tables = _retention_tables()

    mx = MXU_DTYPE
    w_in_m = w_in.astype(mx)
    w_branch_m = w_branch.astype(mx)
    w_gate_m = w_gate_up.astype(mx)
    w_out_m = w_out.astype(mx)
    b_gu4 = b_gu[:, :, None, :]
    b_down4 = b_down[:, :, None, :]
    n_exp = w_router.shape[-1]
    w_router_m = jnp.pad(w_router, ((0, 0), (0, 0), (0, LANES - n_exp))).astype(mx)
    b_router_p = jnp.pad(b_router, ((0, 0), (0, LANES - n_exp)), constant_values=NEG_BIG)

    for l in range(depth):
        bsp_t = jnp.repeat(b_spatial[l].T, GMLP_WIDTH // GMLP_GROUPS, axis=1)
        lw = (
            (w_in_m, l),
            (gmlp_ln_g[l].reshape(1, -1), None), (gmlp_ln_b[l].reshape(1, -1), None),
            (w_spatial[l].reshape(GMLP_GROUPS * CHUNK, CHUNK), None), (bsp_t, None),
            (jnp.pad(conv_w[l], ((0, SUBLANES - conv_w.shape[1]), (0, 0))), None),
            (w_branch_m, l), (w_gate_m, l), (b_gate, l), (w_out_m, l),
            (ln1_g[l].reshape(1, -1), None), (ln1_b[l].reshape(1, -1), None),
            (w_router_m, l), (b_router_p[l].reshape(1, -1), None),
        )
        x1, h2, sel, sel_t, cnt, before = _mixer_call(x, ada[l], cos_t, sin_t, lw, tables)
        n_rows = _moe_rows(n_tok, n_tok // ts, blk)
        pad_end, pad_start, block_e, n_used, tab3, rows8 = _routing_tables(
            sel_t, cnt[0, :n_exp], before[:, 0, :n_exp], n_rows, blk, _run_pieces(ts))
        xs = _dispatch_call(pad_end, pad_start, tab3, rows8, h2, n_rows, blk)
        ys = _expert_call(block_e, n_used, xs, w_gu, b_gu4, w_down, b_down4, blk, l)
        x = _combine_call(tab3, ys, x1, sel, rows8, ada[l], ln2_g[l], ln2_b[l], seq).reshape(bsz, seq, d)
    return x
```
